```python
import math
import jax, jax.numpy as jnp
from jax import lax
import numpy as np

D_MODEL = 1024
BATCH = 8
SEQ = 2048
DEPTH = 2

ROPE_THETA = 500000.0
NEG_INF = -1e30
LN_EPS = 1e-5
RMS_EPS = 1e-6
DEEPNORM_ALPHA = (2 * DEPTH) ** 0.25
DEEPNORM_BETA = (8 * DEPTH) ** -0.25
POS_OFFSET_MAX = 4096
MLA_HEADS = 8
MLA_NOPE = 64
MLA_ROPE = 32
MLA_V = 64
MLA_Q_RANK = 768
MLA_KV_RANK = 256
ATTN_Q_BLOCK = 128
MOBA_HEADS = 8
MOBA_HEAD_DIM = 64
MOBA_ROT_DIM = MOBA_HEAD_DIM // 4
MOBA_BLOCK = 256
MOBA_TOPK = 3
MOBA_Q_CHUNK = 16
GDN_HEADS = 8
GDN_HEAD_DIM = 64
GDN_CONV = 4
GDN_CHUNK = 64
N_BRANCH = 3
BRANCH_WIDTH = 512
MLA_WIDTH = MLA_HEADS * MLA_V
MOBA_WIDTH = MOBA_HEADS * MOBA_HEAD_DIM
GDN_WIDTH = GDN_HEADS * GDN_HEAD_DIM
IN_SPLITS = (MLA_Q_RANK, MLA_KV_RANK, MLA_ROPE, 3 * MOBA_WIDTH, 3 * GDN_WIDTH, GDN_WIDTH, GDN_HEADS, GDN_HEADS, N_BRANCH * D_MODEL)
IN_COLS = sum(IN_SPLITS)
PEER_HEADS = 8
N_KEYS = 128
N_EXPERTS = N_KEYS * N_KEYS
PEER_TOPK = 16
PEER_QDIM = 256
PEER_HALF = PEER_QDIM // 2
PEER_TOKEN_CHUNK = 128

kernel_name = 'hybrid_mla_moba_gdn_peer_deepnorm'


def layer_norm(x, g, b):
    xf = x.astype(jnp.float32)
    mu = jnp.mean(xf, axis=-1, keepdims=True)
    var = jnp.mean(jnp.square(xf - mu), axis=-1, keepdims=True)
    return ((xf - mu) * lax.rsqrt(var + LN_EPS) * g + b).astype(x.dtype)


def rms_norm(x, g):
    xf = x.astype(jnp.float32)
    return (xf * lax.rsqrt(jnp.mean(xf * xf, axis=-1, keepdims=True) + RMS_EPS) * g).astype(x.dtype)


def l2_normalize(t):
    tf = t.astype(jnp.float32)
    return tf * lax.rsqrt(jnp.sum(tf * tf, axis=-1, keepdims=True) + RMS_EPS)


def rope_tables(positions, rot_dim):
    inv_freq = ROPE_THETA ** (-jnp.arange(0, rot_dim, 2, dtype=jnp.float32) / rot_dim)
    ang = positions.astype(jnp.float32)[..., None] * inv_freq
    return jnp.cos(ang), jnp.sin(ang)


def apply_rope(x, cos, sin):
    r = cos.shape[-1]
    c = cos[:, :, None, :].astype(x.dtype)
    s = sin[:, :, None, :].astype(x.dtype)
    x1, x2, rest = x[..., :r], x[..., r:2 * r], x[..., 2 * r:]
    return jnp.concatenate([x1 * c - x2 * s, x2 * c + x1 * s, rest], axis=-1)


def causal_attention_blocked(q, k, v, scale):
    B, S, H, dk = q.shape
    nq = S // ATTN_Q_BLOCK
    qb = q.reshape(B, nq, ATTN_Q_BLOCK, H, dk).transpose(1, 0, 2, 3, 4)
    kpos = jnp.arange(S)

    def one_block(args):
        i, qi = args
        s = jnp.einsum('bqhd,bkhd->bhqk', qi, k, preferred_element_type=jnp.float32) * scale
        qpos = i * ATTN_Q_BLOCK + jnp.arange(ATTN_Q_BLOCK)
        s = jnp.where(kpos[None, :] <= qpos[:, None], s, NEG_INF)
        p = jax.nn.softmax(s, axis=-1).astype(v.dtype)
        return jnp.einsum('bhqk,bkhd->bqhd', p, v)

    out = lax.map(one_block, (jnp.arange(nq), qb))
    return out.transpose(1, 0, 2, 3, 4).reshape(B, S, H, v.shape[-1])


def mla_branch(c_q, c_kv, k_rope_in, q_norm_g, kv_norm_g, w_uq, w_ukv, cos, sin):
    B, S, _ = c_q.shape
    q = (rms_norm(c_q, q_norm_g) @ w_uq).reshape(B, S, MLA_HEADS, MLA_NOPE + MLA_ROPE)
    q = jnp.concatenate([q[..., :MLA_NOPE], apply_rope(q[..., MLA_NOPE:], cos, sin)], axis=-1)
    kv = (rms_norm(c_kv, kv_norm_g) @ w_ukv).reshape(B, S, MLA_HEADS, MLA_NOPE + MLA_V)
    k_nope, v = kv[..., :MLA_NOPE], kv[..., MLA_NOPE:]
    k_rope = apply_rope(k_rope_in[:, :, None, :], cos, sin)
    k = jnp.concatenate([k_nope, jnp.broadcast_to(k_rope, (B, S, MLA_HEADS, MLA_ROPE))], axis=-1)
    o = causal_attention_blocked(q, k, v, (MLA_NOPE + MLA_ROPE) ** -0.5)
    return o.reshape(B, S, MLA_WIDTH)


def moba_branch(q, k, v, cos, sin):
    B, S, H, dh = q.shape
    q = apply_rope(q, cos, sin)
    k = apply_rope(k, cos, sin)
    nb = -(-S // MOBA_BLOCK)
    n_sel = min(MOBA_TOPK, nb)
    pad = nb * MOBA_BLOCK - S
    kp = jnp.pad(k, ((0, 0), (0, pad), (0, 0), (0, 0)))
    vp = jnp.pad(v, ((0, 0), (0, pad), (0, 0), (0, 0)))
    k_blk = kp.reshape(B, nb, MOBA_BLOCK, H, dh).transpose(0, 3, 1, 2, 4)
    v_blk = vp.reshape(B, nb, MOBA_BLOCK, H, dh).transpose(0, 3, 1, 2, 4)
    k_mean = jnp.mean(k_blk.astype(jnp.float32), axis=3)
    scale = dh ** -0.5
    nqc = S // MOBA_Q_CHUNK
    q_chunks = q.reshape(B, nqc, MOBA_Q_CHUNK, H, dh).transpose(1, 0, 3, 2, 4)
    b_idx = jnp.arange(B)[:, None, None, None]
    h_idx = jnp.arange(H)[None, :, None, None]
    blk_ids = jnp.arange(nb)

    def one_chunk(args):
        c, qc = args
        q0 = c * MOBA_Q_CHUNK
        own = q0 // MOBA_BLOCK
        gate = jnp.einsum('bhqd,bhnd->bhqn', qc.astype(jnp.float32), k_mean)
        gate = jnp.where(blk_ids < own, gate, NEG_INF)
        _, sel = lax.top_k(gate, n_sel)
        valid = sel < own
        kg = k_blk[b_idx, h_idx, sel]
        vg = v_blk[b_idx, h_idx, sel]
        s_past = jnp.einsum('bhqd,bhqjpd->bhqjp', qc, kg, preferred_element_type=jnp.float32) * scale
        s_past = jnp.where(valid[..., None], s_past, NEG_INF).reshape(B, H, MOBA_Q_CHUNK, n_sel * MOBA_BLOCK)
        k_own = lax.dynamic_index_in_dim(k_blk, own, axis=2, keepdims=False)
        v_own = lax.dynamic_index_in_dim(v_blk, own, axis=2, keepdims=False)
        s_own = jnp.einsum('bhqd,bhpd->bhqp', qc, k_own, preferred_element_type=jnp.float32) * scale
        qpos = q0 + jnp.arange(MOBA_Q_CHUNK)
        kpos = own * MOBA_BLOCK + jnp.arange(MOBA_BLOCK)
        s_own = jnp.where(kpos[None, :] <= qpos[:, None], s_own, NEG_INF)
        p = jax.nn.softmax(jnp.concatenate([s_past, s_own], axis=-1), axis=-1).astype(v.dtype)
        p_past = p[..., :n_sel * MOBA_BLOCK].reshape(B, H, MOBA_Q_CHUNK, n_sel, MOBA_BLOCK)
        p_own = p[..., n_sel * MOBA_BLOCK:]
        return jnp.einsum('bhqjp,bhqjpd->bhqd', p_past, vg) + jnp.einsum('bhqp,bhpd->bhqd', p_own, v_own)

    out = lax.map(one_chunk, (jnp.arange(nqc), q_chunks))
    return out.transpose(1, 0, 3, 2, 4).reshape(B, S, MOBA_WIDTH)


def causal_depthwise_conv(x, w):
    K = w.shape[0]
    return lax.conv_general_dilated(x, w[:, None, :].astype(x.dtype), window_strides=(1,), padding=((K - 1, 0),),
                                    dimension_numbers=('NWC', 'WIO', 'NWC'), feature_group_count=x.shape[-1])


def chunk_gated_delta_rule(q, k, v, g, beta):
    B, S, H, dk = q.shape
    dv = v.shape[-1]
    C = GDN_CHUNK
    n = S // C
    f32 = jnp.float32

    def to_chunks(t):
        return t.astype(f32).reshape(B, n, C, H, -1).transpose(0, 3, 1, 2, 4)

    q, k, v = to_chunks(q), to_chunks(k), to_chunks(v)
    g = g.astype(f32).reshape(B, n, C, H).transpose(0, 3, 1, 2)
    beta = beta.astype(f32).reshape(B, n, C, H).transpose(0, 3, 1, 2)
    g_cum = jnp.cumsum(g, axis=-1)
    tril = jnp.tril(jnp.ones((C, C), dtype=bool))
    strict = jnp.tril(jnp.ones((C, C), dtype=bool), -1)
    decay = jnp.exp(jnp.where(tril, g_cum[..., :, None] - g_cum[..., None, :], -jnp.inf))
    k_beta = k * beta[..., None]
    A = jnp.where(strict, jnp.einsum('bhncd,bhnsd->bhncs', k_beta, k) * decay, 0.0)
    lhs = jnp.eye(C, dtype=f32) + A
    rhs = jnp.concatenate([v * beta[..., None], k_beta * jnp.exp(g_cum)[..., None]], axis=-1)
    sol = lax.linalg.triangular_solve(lhs, rhs, left_side=True, lower=True, unit_diagonal=True)
    u, w = sol[..., :dv], sol[..., dv:]
    qk = jnp.where(tril, jnp.einsum('bhncd,bhnsd->bhncs', q, k) * decay, 0.0)

    def step(state, xs):
        q_c, k_c, u_c, w_c, g_c, qk_c = xs
        v_new = u_c - jnp.einsum('bhcd,bhde->bhce', w_c, state)
        o_c = jnp.einsum('bhcd,bhde->bhce', q_c * jnp.exp(g_c)[..., None], state) + jnp.einsum('bhcs,bhse->bhce', qk_c, v_new)
        g_last = g_c[..., -1]
        k_dec = k_c * jnp.exp(g_last[..., None] - g_c)[..., None]
        state = state * jnp.exp(g_last)[..., None, None] + jnp.einsum('bhcd,bhce->bhde', k_dec, v_new)
        return state, o_c

    xs = tuple(jnp.moveaxis(t, 2, 0) for t in (q, k, u, w, g_cum, qk))
    _, o = lax.scan(step, jnp.zeros((B, H, dk, dv), f32), xs)
    return o.transpose(1, 0, 3, 2, 4).reshape(B, S, H, dv)


def gdn_branch(qkv, z, a, b_logit, conv_w, A_log, dt_bias, o_norm_g):
    B, S, _ = qkv.shape
    qkv = jax.nn.silu(causal_depthwise_conv(qkv, conv_w))
    q, k, v = jnp.split(qkv, 3, axis=-1)
    q = l2_normalize(q.reshape(B, S, GDN_HEADS, GDN_HEAD_DIM)) * (GDN_HEAD_DIM ** -0.5)
    k = l2_normalize(k.reshape(B, S, GDN_HEADS, GDN_HEAD_DIM))
    v = v.reshape(B, S, GDN_HEADS, GDN_HEAD_DIM)
    beta = jax.nn.sigmoid(b_logit.astype(jnp.float32))
    g = -jnp.exp(A_log.astype(jnp.float32)) * jax.nn.softplus(a.astype(jnp.float32) + dt_bias)
    o = chunk_gated_delta_rule(q, k, v, g, beta)
    o = rms_norm(o, o_norm_g) * jax.nn.silu(z.reshape(B, S, GDN_HEADS, GDN_HEAD_DIM).astype(jnp.float32))
    return o.reshape(B, S, GDN_WIDTH).astype(z.dtype)


def token_mixer(h, cos_a, sin_a, cos_b, sin_b, w_in, mla_q_norm, mla_kv_norm, mla_w_uq, mla_w_ukv,
                gdn_conv_w, gdn_A_log, gdn_dt_bias, gdn_o_norm, gate_bias, w_branch, w_out):
    B, S, D = h.shape
    offs = np.cumsum(IN_SPLITS)[:-1].tolist()
    c_q, c_kv, k_rope, moba_qkv, gdn_qkv, gdn_z, gdn_a, gdn_b, gate_logits = jnp.split(h @ w_in, offs, axis=-1)
    o_a = mla_branch(c_q, c_kv, k_rope, mla_q_norm, mla_kv_norm, mla_w_uq, mla_w_ukv, cos_a, sin_a)
    mq, mk, mv = [t.reshape(B, S, MOBA_HEADS, MOBA_HEAD_DIM) for t in jnp.split(moba_qkv, 3, axis=-1)]
    o_b = moba_branch(mq, mk, mv, cos_b, sin_b)
    o_c = gdn_branch(gdn_qkv, gdn_z, gdn_a, gdn_b, gdn_conv_w, gdn_A_log, gdn_dt_bias, gdn_o_norm)
    branches = jnp.stack([o_a, o_b, o_c], axis=2)
    gates = jax.nn.sigmoid(gate_logits.reshape(B, S, N_BRANCH, D) + gate_bias)
    merged = jnp.sum(gates * jnp.einsum('bsnw,nwd->bsnd', branches, w_branch), axis=2)
    return merged @ w_out


def peer_ffn(h, w_q, sub_keys, expert_u, expert_v):
    B, S, D = h.shape
    T = B * S
    xt = h.reshape(T, D)
    qry = (xt @ w_q).reshape(T, PEER_HEADS, 2, PEER_HALF)
    sc = jnp.einsum('thpd,hpnd->thpn', qry, sub_keys, preferred_element_type=jnp.float32)
    s1, i1 = lax.top_k(sc[:, :, 0], PEER_TOPK)
    s2, i2 = lax.top_k(sc[:, :, 1], PEER_TOPK)
    cand_s = (s1[..., :, None] + s2[..., None, :]).reshape(T, PEER_HEADS, PEER_TOPK * PEER_TOPK)
    cand_i = (i1[..., :, None] * N_KEYS + i2[..., None, :]).reshape(T, PEER_HEADS, PEER_TOPK * PEER_TOPK)
    top_s, pos = lax.top_k(cand_s, PEER_TOPK)
    idx = jnp.take_along_axis(cand_i, pos, axis=-1)
    gate = jax.nn.softmax(top_s, axis=-1)
    n_chunks = T // PEER_TOKEN_CHUNK

    def one_chunk(args):
        xc, ic, gc = args
        u = expert_u[ic]
        act = jax.nn.gelu(jnp.einsum('td,thkd->thk', xc, u, preferred_element_type=jnp.float32), approximate=False)
        return jnp.einsum('thk,thkd->td', (gc * act).astype(h.dtype), expert_v[ic])

    out = lax.map(one_chunk, (xt.reshape(n_chunks, PEER_TOKEN_CHUNK, D),
                              idx.reshape(n_chunks, PEER_TOKEN_CHUNK, PEER_HEADS, PEER_TOPK),
                              gate.reshape(n_chunks, PEER_TOKEN_CHUNK, PEER_HEADS, PEER_TOPK)))
    return out.reshape(B, S, D)


def setup_inputs(seed: int = 0) -> dict:
    key = jax.random.key(seed)
    ks = jax.random.split(key, 32)
    f32 = jnp.float32
    L, D = DEPTH, D_MODEL

    def nrm(k, shape, scale):
        return jax.random.normal(k, shape, f32) * scale

    def gain(k, shape):
        return 1.0 + 0.05 * jax.random.normal(k, shape, f32)

    x = nrm(ks[0], (BATCH, SEQ, D), 1.0)
    positions = jax.random.randint(ks[1], (BATCH, 1), 0, POS_OFFSET_MAX, dtype=jnp.int32) + jnp.arange(SEQ, dtype=jnp.int32)[None, :]
    dt = jnp.exp(jax.random.uniform(ks[11], (L, GDN_HEADS), f32, math.log(1e-3), math.log(1e-1)))
    return {
        'x': x,
        'positions': positions,
        'ln_in_g': gain(ks[2], (D,)),
        'ln_in_b': nrm(ks[3], (D,), 0.02),
        'w_in': nrm(ks[4], (L, D, IN_COLS), D ** -0.5),
        'mla_q_norm': gain(ks[5], (L, MLA_Q_RANK)),
        'mla_kv_norm': gain(ks[6], (L, MLA_KV_RANK)),
        'mla_w_uq': nrm(ks[7], (L, MLA_Q_RANK, MLA_HEADS * (MLA_NOPE + MLA_ROPE)), MLA_Q_RANK ** -0.5),
        'mla_w_ukv': nrm(ks[8], (L, MLA_KV_RANK, MLA_HEADS * (MLA_NOPE + MLA_V)), MLA_KV_RANK ** -0.5),
        'gdn_conv_w': nrm(ks[9], (L, GDN_CONV, 3 * GDN_WIDTH), GDN_CONV ** -0.5),
        'gdn_A_log': jnp.log(jax.random.uniform(ks[10], (L, GDN_HEADS), f32, 1.0, 16.0)),
        'gdn_dt_bias': dt + jnp.log(-jnp.expm1(-dt)),
        'gdn_o_norm': gain(ks[12], (L, GDN_HEAD_DIM)),
        'gate_bias': nrm(ks[13], (L, N_BRANCH, D), 0.02),
        'w_branch': nrm(ks[14], (L, N_BRANCH, BRANCH_WIDTH, D), DEEPNORM_BETA * BRANCH_WIDTH ** -0.5),
        'w_out': nrm(ks[15], (L, D, D), DEEPNORM_BETA * D ** -0.5),
        'ln1_g': gain(ks[16], (L, D)),
        'ln1_b': nrm(ks[17], (L, D), 0.02),
        'peer_w_q': nrm(ks[18], (L, D, PEER_HEADS * PEER_QDIM), D ** -0.5),
        'peer_sub_keys': nrm(ks[19], (L, PEER_HEADS, 2, N_KEYS, PEER_HALF), PEER_HALF ** -0.5),
        'peer_u': nrm(ks[20], (L, N_EXPERTS, D), D ** -0.5),
        'peer_v': nrm(ks[21], (L, N_EXPERTS, D), DEEPNORM_BETA * PEER_HEADS ** -0.5),
        'ln2_g': gain(ks[22], (L, D)),
        'ln2_b': nrm(ks[23], (L, D), 0.02),
    }


def reference(x, positions, ln_in_g, ln_in_b, w_in, mla_q_norm, mla_kv_norm, mla_w_uq, mla_w_ukv,
              gdn_conv_w, gdn_A_log, gdn_dt_bias, gdn_o_norm, gate_bias, w_branch, w_out, ln1_g, ln1_b,
              peer_w_q, peer_sub_keys, peer_u, peer_v, ln2_g, ln2_b):
    cos_a, sin_a = rope_tables(positions, MLA_ROPE)
    cos_b, sin_b = rope_tables(positions, MOBA_ROT_DIM)
    h = layer_norm(x, ln_in_g, ln_in_b)
    for l in range(DEPTH):
        mix = token_mixer(h, cos_a, sin_a, cos_b, sin_b, w_in[l], mla_q_norm[l], mla_kv_norm[l], mla_w_uq[l],
                          mla_w_ukv[l], gdn_conv_w[l], gdn_A_log[l], gdn_dt_bias[l], gdn_o_norm[l],
                          gate_bias[l], w_branch[l], w_out[l])
        h = layer_norm(DEEPNORM_ALPHA * h + mix, ln1_g[l], ln1_b[l])
        ffn = peer_ffn(h, peer_w_q[l], peer_sub_keys[l], peer_u[l], peer_v[l])
        h = layer_norm(DEEPNORM_ALPHA * h + ffn, ln2_g[l], ln2_b[l])
    return h
```

```python
import functools
import math

import numpy as np
import jax
import jax.numpy as jnp
from jax import lax
from jax.experimental import pallas as pl
from jax.experimental.pallas import tpu as pltpu

F32 = jnp.float32
BF16 = jnp.bfloat16
HIGHEST = lax.Precision.HIGHEST

DEPTH = 2
ROPE_THETA = 500000.0
NEG_INF = -1e30
LN_EPS = 1e-5
RMS_EPS = 1e-6
DEEPNORM_ALPHA = (2 * DEPTH) ** 0.25
HEADS = 8
MLA_NOPE, MLA_ROPE, MLA_V = 64, 32, 64
MLA_Q_RANK, MLA_KV_RANK = 768, 256
MOBA_HEAD_DIM, MOBA_ROT_DIM, MOBA_BLOCK, MOBA_TOPK = 64, 16, 256, 3
GDN_HEAD_DIM, GDN_CONV = 64, 4
BRANCH_WIDTH = 512
N_KEYS, PEER_TOPK, PEER_HALF = 128, 16, 128
N_EXPERTS = N_KEYS * N_KEYS

LANES = 128
SUBLANES = 8
VMEM_LIMIT = 56 * 1024 * 1024

GDN_KCHUNK = 128
ATT_TQ = 256
PEER_TT = 512
PEER_EB = 1024
PEER_SCORE_TT = 256

C_GATE, C_CQ, C_CKV, C_MQ, C_MK, C_MV, C_GQKV, C_GZ, C_SMALL, C_TOTAL = (
    0, 3072, 3840, 4096, 5120, 6144, 6656, 8192, 8704, 8832)


def _cp(*sem):
    return pltpu.CompilerParams(dimension_semantics=sem, vmem_limit_bytes=VMEM_LIMIT)


def _dot(a, b):
    return jnp.dot(a.astype(BF16), b.astype(BF16), preferred_element_type=F32)


def _dot_nt(a, b):
    return lax.dot_general(a.astype(BF16), b.astype(BF16), (((1,), (1,)), ((), ())),
                           preferred_element_type=F32)


def _split(a):
    hi = a.astype(BF16)
    return hi, (a - hi.astype(F32)).astype(BF16)


def _dot3(a, b):
    ah, al = _split(a)
    bh, bl = _split(b)
    d = functools.partial(jnp.dot, preferred_element_type=F32)
    return d(ah, bh) + (d(al, bh) + d(ah, bl))


def _ln_rows(y, g, b):
    mu = jnp.mean(y, axis=-1, keepdims=True)
    d = y - mu
    var = jnp.mean(d * d, axis=-1, keepdims=True)
    return d * lax.rsqrt(var + LN_EPS) * g + b


def _sigmoid(x):
    return 1.0 / (1.0 + jnp.exp(-x))


def _ln_kernel(x_ref, g_ref, b_ref, o_ref):
    o_ref[...] = _ln_rows(x_ref[...], g_ref[...], b_ref[...])


def _layer_norm(x, g, b, tm=512):
    T, D = x.shape
    return pl.pallas_call(
        _ln_kernel, grid=(T // tm,),
        in_specs=[pl.BlockSpec((tm, D), lambda i: (i, 0)),
                  pl.BlockSpec((1, D), lambda i: (0, 0)),
                  pl.BlockSpec((1, D), lambda i: (0, 0))],
        out_specs=pl.BlockSpec((tm, D), lambda i: (i, 0)),
        out_shape=jax.ShapeDtypeStruct((T, D), F32),
        compiler_params=_cp("parallel"))(x, g.reshape(1, D), b.reshape(1, D))


def _mm_kernel(x_ref, w_ref, o_ref):
    o_ref[...] = _dot(x_ref[...], w_ref[...]).astype(o_ref.dtype)


def _mm(x, w, tm, tn, out_dtype=F32):
    M, K = x.shape
    N = w.shape[1]
    return pl.pallas_call(
        _mm_kernel, grid=(M // tm, N // tn),
        in_specs=[pl.BlockSpec((tm, K), lambda i, j: (i, 0)),
                  pl.BlockSpec((K, tn), lambda i, j: (0, j))],
        out_specs=pl.BlockSpec((tm, tn), lambda i, j: (i, j)),
        out_shape=jax.ShapeDtypeStruct((M, N), out_dtype),
        compiler_params=_cp("parallel", "parallel"))(x, w)


def _rmsmm_kernel(x_ref, g_ref, w_ref, o_ref):
    x = x_ref[...]
    xn = x * lax.rsqrt(jnp.mean(x * x, axis=-1, keepdims=True) + RMS_EPS) * g_ref[...]
    o_ref[...] = _dot(xn, w_ref[...])


def _rmsnorm_mm(proj, col_block, g, w, tm, tn):
    T = proj.shape[0]
    K, N = w.shape
    return pl.pallas_call(
        _rmsmm_kernel, grid=(T // tm, N // tn),
        in_specs=[pl.BlockSpec((tm, K), lambda i, j: (i, col_block)),
                  pl.BlockSpec((1, K), lambda i, j: (0, 0)),
                  pl.BlockSpec((K, tn), lambda i, j: (0, j))],
        out_specs=pl.BlockSpec((tm, tn), lambda i, j: (i, j)),
        out_shape=jax.ShapeDtypeStruct((T, N), F32),
        compiler_params=_cp("parallel", "parallel"))(proj, g.reshape(1, K), w)


def _head_slabs(x):
    return [x[:, h * LANES:(h + 1) * LANES] for h in range(HEADS)]


def _mla_prep_kernel(q_ref, k_ref, sm_ref, cq_ref, sq_ref, ck_ref, sk_ref, qo_ref, ko_ref):
    cq, sq = cq_ref[...], sq_ref[...]
    sm = sm_ref[...]
    kr = sm * ck_ref[...] + pltpu.roll(sm, LANES - MLA_ROPE, 1) * sk_ref[...]
    q, k = q_ref[...], k_ref[...]
    for h, (qs, ks) in enumerate(zip(_head_slabs(q), _head_slabs(k))):
        sl = slice(h * LANES, (h + 1) * LANES)
        qo_ref[:, sl] = (qs * cq + pltpu.roll(qs, LANES - MLA_ROPE, 1) * sq).astype(BF16)
        ko_ref[:, sl] = (ks + kr).astype(BF16)


def _mla_prep(q_raw, kv_raw, proj, tabs, tm=512):
    T = q_raw.shape[0]
    W = HEADS * LANES
    row = lambda i: (i, 0)
    tab = pl.BlockSpec((tm, LANES), row)
    return pl.pallas_call(
        _mla_prep_kernel, grid=(T // tm,),
        in_specs=[pl.BlockSpec((tm, W), row), pl.BlockSpec((tm, W), row),
                  pl.BlockSpec((tm, LANES), lambda i: (i, C_SMALL // LANES)), tab, tab, tab, tab],
        out_specs=[pl.BlockSpec((tm, W), row), pl.BlockSpec((tm, W), row)],
        out_shape=[jax.ShapeDtypeStruct((T, W), BF16)] * 2,
        compiler_params=_cp("parallel"))(q_raw, kv_raw, proj, *tabs)


def _moba_prep_kernel(q_ref, k_ref, cq_ref, sq_ref, ck_ref, sk_ref, qo_ref, ko_ref, km_ref):
    cq, sq, ck, sk = cq_ref[...], sq_ref[...], ck_ref[...], sk_ref[...]
    q, k = q_ref[...], k_ref[...]
    for h, (qs, ks) in enumerate(zip(_head_slabs(q), _head_slabs(k))):
        sl = slice(h * LANES, (h + 1) * LANES)
        qo_ref[:, sl] = qs * cq + pltpu.roll(qs, LANES // 2, 1) * sq
        kk = ks * ck + pltpu.roll(ks, LANES // 2, 1) * sk
        ko_ref[:, sl] = kk.astype(BF16)
        km_ref[0, :, sl] = jnp.mean(kk, axis=0, keepdims=True)


def _moba_prep(proj, tabs):
    T = proj.shape[0]
    W = HEADS * LANES
    tm = MOBA_BLOCK
    row = lambda i: (i, 0)
    tab = pl.BlockSpec((tm, LANES), row)
    return pl.pallas_call(
        _moba_prep_kernel, grid=(T // tm,),
        in_specs=[pl.BlockSpec((tm, W), lambda i: (i, C_MQ // W)),
                  pl.BlockSpec((tm, W), lambda i: (i, C_MK // W)), tab, tab, tab, tab],
        out_specs=[pl.BlockSpec((tm, W), row), pl.BlockSpec((tm, W), row),
                   pl.BlockSpec((1, 1, W), lambda i: (i, 0, 0))],
        out_shape=[jax.ShapeDtypeStruct((T, W), F32), jax.ShapeDtypeStruct((T, W), BF16),
                   jax.ShapeDtypeStruct((T // tm, 1, W), F32)],
        compiler_params=_cp("parallel"))(proj, proj, *tabs)


def _attn_kernel(*refs, moba, nk):
    if moba:
        q_ref, k_ref, v_ref, km_ref, o_ref, m_sc, l_sc, acc_sc, bias_sc = refs
    else:
        q_ref, k_ref, v_ref, o_ref, m_sc, l_sc, acc_sc = refs
    tq = tk = ATT_TQ
    qi = pl.program_id(2)
    lane = lax.broadcasted_iota(jnp.int32, (tq, LANES), 1)
    rowi = lax.broadcasted_iota(jnp.int32, (tq, tk), 0)
    coli = lax.broadcasted_iota(jnp.int32, (tq, tk), 1)
    outs = []
    for hh in range(2):
        hs = slice(hh * LANES, (hh + 1) * LANES)
        qf = q_ref[:, hs]
        q = qf.astype(BF16)
        if moba:
            km = km_ref[0, :, hs]
            km = jnp.concatenate([km, jnp.zeros((LANES - nk, LANES), F32)], axis=0)
            gate = lax.dot_general(qf, km, (((1,), (1,)), ((), ())), precision=HIGHEST,
                                   preferred_element_type=F32)
            gate = jnp.where(lane < qi, gate, -jnp.inf)
            for n in range(nk):
                gn = gate[:, n:n + 1]
                beats = jnp.where(gate > gn, 1.0, jnp.where((gate == gn) & (lane < n), 1.0, 0.0))
                cnt = jnp.sum(beats, axis=1, keepdims=True)
                admit = ((cnt < MOBA_TOPK) & (n < qi)) | (n == qi)
                bias_sc[n] = jnp.where(admit, 0.0, NEG_INF)
        m_sc[...] = jnp.full(m_sc.shape, NEG_INF, F32)
        l_sc[...] = jnp.zeros(l_sc.shape, F32)
        acc_sc[...] = jnp.zeros(acc_sc.shape, F32)
        for j in range(nk):
            @pl.when(j <= qi)
            def _():
                k = k_ref[j * tk:(j + 1) * tk, hs]
                s = _dot_nt(q, k)
                s = jnp.where(coli + j * tk <= rowi + qi * tq, s, NEG_INF)
                if moba:
                    s = s + bias_sc[j]
                m_prev = m_sc[...]
                m_new = jnp.maximum(m_prev, jnp.max(s, axis=1, keepdims=True))
                alpha = jnp.exp(m_prev - m_new)
                p = jnp.exp(s - m_new)
                l_sc[...] = alpha * l_sc[...] + jnp.sum(p, axis=1, keepdims=True)
                acc_sc[...] = alpha * acc_sc[...] + _dot(p, v_ref[j * tk:(j + 1) * tk, :])
                m_sc[...] = m_new
        outs.append(acc_sc[...] / l_sc[...])
    o_ref[...] = jnp.where(lane < LANES // 2, outs[0], outs[1])


def _attention(q, k, v, v_col0, B, S, kmean=None):
    T = B * S
    tq = ATT_TQ
    nq = S // tq
    moba = kmean is not None
    in_specs = [pl.BlockSpec((tq, 2 * LANES), lambda b, p, i: (b * nq + i, p)),
                pl.BlockSpec((S, 2 * LANES), lambda b, p, i: (b, p)),
                pl.BlockSpec((S, LANES), lambda b, p, i: (b, v_col0 + p))]
    scratch = [pltpu.VMEM((tq, 1), F32), pltpu.VMEM((tq, 1), F32), pltpu.VMEM((tq, LANES), F32)]
    args = [q, k, v]
    if moba:
        in_specs.append(pl.BlockSpec((1, nq, 2 * LANES), lambda b, p, i: (b, 0, p)))
        scratch.append(pltpu.VMEM((nq, tq, 1), F32))
        args.append(kmean)
    return pl.pallas_call(
        functools.partial(_attn_kernel, moba=moba, nk=nq),
        grid=(B, HEADS // 2, nq), in_specs=in_specs,
        out_specs=pl.BlockSpec((tq, LANES), lambda b, p, i: (b * nq + i, p)),
        out_shape=jax.ShapeDtypeStruct((T, BRANCH_WIDTH), F32),
        scratch_shapes=scratch,
        compiler_params=_cp("parallel", "parallel", "arbitrary"))(*args)


def _gdn_conv_kernel(x_ref, w_ref, o_ref):
    c = pl.program_id(1)
    x = x_ref[...]
    w = w_ref[...]
    S = x.shape[0]
    row = lax.broadcasted_iota(jnp.int32, (S, LANES), 0)
    lane = lax.broadcasted_iota(jnp.int32, (S, LANES), 1)
    y = x * w[GDN_CONV - 1:GDN_CONV, :]
    for d in range(1, GDN_CONV):
        xs = jnp.where(row >= d, pltpu.roll(x, d, 0), 0.0)
        y = y + xs * w[GDN_CONV - 1 - d:GDN_CONV - d, :]
    y = y * _sigmoid(y)
    sq = y * y
    lo = lane < GDN_HEAD_DIM
    ss0 = jnp.sum(jnp.where(lo, sq, 0.0), axis=1, keepdims=True)
    ss1 = jnp.sum(jnp.where(lo, 0.0, sq), axis=1, keepdims=True)
    inv = lax.rsqrt(jnp.where(lo, ss0, ss1) + RMS_EPS)
    nqb = BRANCH_WIDTH // LANES
    scale = jnp.where(c < nqb, GDN_HEAD_DIM ** -0.5, 1.0)
    o_ref[...] = jnp.where(c < 2 * nqb, y * inv * scale, y)


def _gdn_conv(proj, conv_w, B, S):
    T = B * S
    nb = 3 * BRANCH_WIDTH // LANES
    return pl.pallas_call(
        _gdn_conv_kernel, grid=(B, nb),
        in_specs=[pl.BlockSpec((S, LANES), lambda b, c: (b, C_GQKV // LANES + c)),
                  pl.BlockSpec((GDN_CONV, LANES), lambda b, c: (0, c))],
        out_specs=pl.BlockSpec((S, LANES), lambda b, c: (b, c)),
        out_shape=jax.ShapeDtypeStruct((T, 3 * BRANCH_WIDTH), F32),
        compiler_params=_cp("parallel", "parallel"))(proj, conv_w)


def _gdn_chunk_kernel(q_ref, k_ref, v_ref, sm_ref, alog_ref, dtb_ref,
                      u_ref, w_ref, qg_ref, qk_ref, kdt_ref, dl_ref):
    C = GDN_KCHUNK
    h = pl.program_id(1)
    odd = (h % 2) == 1
    lane = lax.broadcasted_iota(jnp.int32, (C, LANES), 1)
    row = lax.broadcasted_iota(jnp.int32, (C, LANES), 0)
    lo = lane < GDN_HEAD_DIM

    def pick(ref):
        x = ref[...]
        return jnp.where(lo, jnp.where(odd, pltpu.roll(x, LANES // 2, 1), x), 0.0)

    qh, kh, vh = pick(q_ref), pick(k_ref), pick(v_ref)
    sm = sm_ref[...]
    a = sm + dtb_ref[...]
    softplus = jnp.maximum(a, 0.0) + jnp.log1p(jnp.exp(-jnp.abs(a)))
    garr = -jnp.exp(alog_ref[...]) * softplus
    g_col = jnp.sum(jnp.where(lane == h, garr, 0.0), axis=1, keepdims=True)
    beta = jnp.sum(jnp.where(lane == HEADS + h, _sigmoid(sm), 0.0), axis=1, keepdims=True)
    tril = row >= lane
    strict = row > lane
    gc = jnp.dot(tril.astype(F32), jnp.broadcast_to(g_col, (C, LANES)), precision=HIGHEST,
                 preferred_element_type=F32)
    gr = gc.T
    decay = jnp.where(tril, jnp.exp(jnp.where(tril, gc - gr, 0.0)), 0.0)
    kb = kh * beta
    A = jnp.where(strict, _dot_nt(kb, kh) * decay, 0.0)
    eye = jnp.where(row == lane, 1.0, 0.0)
    P = eye - A
    X = A
    for _ in range(int(math.log2(C)) - 1):
        X = _dot3(X, X)
        P = P + _dot3(P, X)
    eg = jnp.exp(gc)
    u_ref[...] = _dot3(P, vh * beta)
    w_ref[...] = _dot3(P, kb * eg)
    qk_ref[...] = jnp.where(tril, _dot_nt(qh, kh) * decay, 0.0)
    qg_ref[...] = qh * eg
    glast = gc[C - 1:C, :]
    kdt_ref[...] = (kh * jnp.exp(glast - gc)).T
    dl_ref[...] = jnp.broadcast_to(jnp.exp(glast), (SUBLANES, LANES))


def _gdn_chunks(qkv, proj, alog_row, dtb_row, B, S):
    C = GDN_KCHUNK
    n = S // C
    nqb = BRANCH_WIDTH // LANES
    rows = B * HEADS * n
    big = jax.ShapeDtypeStruct((rows * C, LANES), F32)
    ospec = pl.BlockSpec((C, LANES), lambda b, h, c: ((b * HEADS + h) * n + c, 0))
    par = pl.BlockSpec((1, LANES), lambda b, h, c: (0, 0))
    return pl.pallas_call(
        _gdn_chunk_kernel, grid=(B, HEADS, n),
        in_specs=[pl.BlockSpec((C, LANES), lambda b, h, c: (b * n + c, h // 2)),
                  pl.BlockSpec((C, LANES), lambda b, h, c: (b * n + c, nqb + h // 2)),
                  pl.BlockSpec((C, LANES), lambda b, h, c: (b * n + c, 2 * nqb + h // 2)),
                  pl.BlockSpec((C, LANES), lambda b, h, c: (b * n + c, C_SMALL // LANES)), par, par],
        out_specs=[ospec] * 5 + [pl.BlockSpec((SUBLANES, LANES), lambda b, h, c: ((b * HEADS + h) * n + c, 0))],
        out_shape=[big] * 5 + [jax.ShapeDtypeStruct((rows * SUBLANES, LANES), F32)],
        compiler_params=_cp("parallel", "parallel", "parallel"))(qkv, qkv, qkv, proj, alog_row, dtb_row)


def _gdn_scan_kernel(u_ref, w_ref, qg_ref, qk_ref, kdt_ref, dl_ref, o_ref, *, n):
    C = GDN_KCHUNK

    def step(c, state):
        r = pl.multiple_of(c * C, C)
        sb = state.astype(BF16)
        v_new = u_ref[pl.ds(r, C), :] - _dot(w_ref[pl.ds(r, C), :], sb)
        vb = v_new.astype(BF16)
        o_ref[pl.ds(r, C), :] = _dot(qg_ref[pl.ds(r, C), :], sb) + _dot(qk_ref[pl.ds(r, C), :], vb)
        dl = dl_ref[pl.ds(pl.multiple_of(c * SUBLANES, SUBLANES), 1), :]
        return state * dl + _dot(kdt_ref[pl.ds(r, C), :], vb)

    lax.fori_loop(0, n, step, jnp.zeros((LANES, LANES), F32))


def _gdn_scan(parts, B, S):
    n = S // GDN_KCHUNK
    seq = pl.BlockSpec((S, LANES), lambda i: (i, 0))
    return pl.pallas_call(
        functools.partial(_gdn_scan_kernel, n=n), grid=(B * HEADS,),
        in_specs=[seq] * 5 + [pl.BlockSpec((n * SUBLANES, LANES), lambda i: (i, 0))],
        out_specs=seq,
        out_shape=jax.ShapeDtypeStruct((B * HEADS * S, LANES), F32),
        compiler_params=_cp("parallel"))(*parts)


def _gdn_out_kernel(o0_ref, o1_ref, z_ref, g_ref, o_ref):
    lane = lax.broadcasted_iota(jnp.int32, o_ref.shape, 1)

    def nrm(o):
        ms = jnp.sum(o * o, axis=1, keepdims=True) * (1.0 / GDN_HEAD_DIM)
        return o * lax.rsqrt(ms + RMS_EPS)

    nn = jnp.where(lane < GDN_HEAD_DIM, nrm(o0_ref[...]), pltpu.roll(nrm(o1_ref[...]), LANES // 2, 1))
    z = z_ref[...]
    o_ref[...] = nn * g_ref[...] * (z * _sigmoid(z))


def _gdn_out(o, proj, g_row, B, S, tm=512):
    ns = S // tm
    return pl.pallas_call(
        _gdn_out_kernel, grid=(B, ns, HEADS // 2),
        in_specs=[pl.BlockSpec((tm, LANES), lambda b, s, p: ((b * HEADS + 2 * p) * ns + s, 0)),
                  pl.BlockSpec((tm, LANES), lambda b, s, p: ((b * HEADS + 2 * p + 1) * ns + s, 0)),
                  pl.BlockSpec((tm, LANES), lambda b, s, p: (b * ns + s, C_GZ // LANES + p)),
                  pl.BlockSpec((1, LANES), lambda b, s, p: (0, 0))],
        out_specs=pl.BlockSpec((tm, LANES), lambda b, s, p: (b * ns + s, p)),
        out_shape=jax.ShapeDtypeStruct((B * S, BRANCH_WIDTH), F32),
        compiler_params=_cp("parallel", "parallel", "parallel"))(o, o, proj, g_row)


def _merge_kernel(g0_ref, g1_ref, g2_ref, gb_ref, a_ref, b_ref, c_ref, wb_ref, o_ref):
    acc = None
    for n, (gl, br) in enumerate(((g0_ref, a_ref), (g1_ref, b_ref), (g2_ref, c_ref))):
        y = _sigmoid(gl[...] + gb_ref[n:n + 1, :]) * _dot(br[...], wb_ref[n])
        acc = y if acc is None else acc + y
    o_ref[...] = acc.astype(BF16)


def _merge(proj, gate_bias, o_a, o_b, o_c, w_branch, tm=512, tn=512):
    T = o_a.shape[0]
    D = w_branch.shape[2]
    nj = D // tn
    gspec = lambda n: pl.BlockSpec((tm, tn), lambda i, j: (i, n * nj + j))
    bspec = pl.BlockSpec((tm, BRANCH_WIDTH), lambda i, j: (i, 0))
    return pl.pallas_call(
        _merge_kernel, grid=(T // tm, nj),
        in_specs=[gspec(0), gspec(1), gspec(2), pl.BlockSpec((3, tn), lambda i, j: (0, j)),
                  bspec, bspec, bspec, pl.BlockSpec((3, BRANCH_WIDTH, tn), lambda i, j: (0, 0, j))],
        out_specs=pl.BlockSpec((tm, tn), lambda i, j: (i, j)),
        out_shape=jax.ShapeDtypeStruct((T, D), BF16),
        compiler_params=_cp("parallel", "parallel"))(proj, proj, proj, gate_bias, o_a, o_b, o_c, w_branch)


def _outproj_kernel(m_ref, w_ref, h_ref, g_ref, b_ref, o_ref, ot_ref):
    y = DEEPNORM_ALPHA * h_ref[...] + _dot(m_ref[...], w_ref[...])
    y = _ln_rows(y, g_ref[...], b_ref[...])
    o_ref[...] = y
    ot_ref[...] = y.T.astype(BF16)


def _outproj_ln(merged, w_out, h, g, b, tm=256):
    T, D = h.shape
    return pl.pallas_call(
        _outproj_kernel, grid=(T // tm,),
        in_specs=[pl.BlockSpec((tm, D), lambda i: (i, 0)), pl.BlockSpec((D, D), lambda i: (0, 0)),
                  pl.BlockSpec((tm, D), lambda i: (i, 0)),
                  pl.BlockSpec((1, D), lambda i: (0, 0)), pl.BlockSpec((1, D), lambda i: (0, 0))],
        out_specs=[pl.BlockSpec((tm, D), lambda i: (i, 0)), pl.BlockSpec((D, tm), lambda i: (0, i))],
        out_shape=[jax.ShapeDtypeStruct((T, D), F32), jax.ShapeDtypeStruct((D, T), BF16)],
        compiler_params=_cp("parallel"))(merged, w_out, h, g.reshape(1, D), b.reshape(1, D))


def _extract_top(cur, n, on_max):
    for k in range(n):
        m = jnp.max(cur, axis=0, keepdims=True)
        on_max(k, m)
        if k + 1 < n:
            cur = jnp.where(cur >= m, -jnp.inf, cur)


def _peer_score_kernel(qt_ref, keys_ref, s2_ref, e2_ref, c_ref, e1_ref, top_sc, cand_sc):
    K = PEER_TOPK
    tt = qt_ref.shape[1]

    def head(h, carry):
        s = []
        for p in range(2):
            r = pl.multiple_of(h * 2 * PEER_HALF + p * PEER_HALF, PEER_HALF)
            sp = _dot(keys_ref[2 * h + p], qt_ref[pl.ds(r, PEER_HALF), :])
            s.append(sp)

            def put(k, m, p=p):
                top_sc[p, k:k + 1, :] = m
            _extract_top(sp, K + 1, put)
        v1a, v1x = top_sc[0, 0:K, :], top_sc[0, K:K + 1, :]
        v2a, v2x = top_sc[1, 0:K, :], top_sc[1, K:K + 1, :]
        cand_sc[0:K, :] = v1a + v2a[0:1]
        for b in range(1, SUBLANES):
            cand_sc[K + SUBLANES * (b - 1):K + SUBLANES * b, :] = v1a[0:SUBLANES] + v2a[b:b + 1]
        base = K + SUBLANES * (SUBLANES - 1)
        cand_sc[base:base + SUBLANES, :] = v1a[0:1] + v2a[SUBLANES:K]
        r8 = lax.broadcasted_iota(jnp.int32, (SUBLANES, tt), 0)
        cand_sc[base + SUBLANES:base + 2 * SUBLANES, :] = jnp.where(
            r8 == 0, v1x + v2a[0:1], jnp.where(r8 == 1, v1a[0:1] + v2x, -jnp.inf))
        st = {"z": jnp.zeros((1, tt), F32)}

        def acc(k, m):
            if k == 0:
                st["top"] = m
            if k < K:
                st["z"] = st["z"] + jnp.exp(m - st["top"])
            if k == K - 1:
                st["t16"] = m
            if k == K:
                st["t17"] = m
        _extract_top(cand_sc[...], K + 1, acc)
        tau = 0.5 * (st["t16"] + st["t17"])
        ro = pl.ds(pl.multiple_of(h * N_KEYS, N_KEYS), N_KEYS)
        s2_ref[ro, :] = s[1]
        e2_ref[ro, :] = jnp.exp(s[1] - v2a[0:1]) / st["z"]
        c_ref[ro, :] = tau - s[0]
        e1_ref[ro, :] = jnp.exp(s[0] - v1a[0:1])
        return carry

    lax.fori_loop(0, HEADS, head, 0)


def _peer_scores(qt, keys):
    T = qt.shape[1]
    tt = PEER_SCORE_TT
    R = HEADS * N_KEYS
    ncand = PEER_TOPK + SUBLANES * (SUBLANES + 1)
    ospec = pl.BlockSpec((R, tt), lambda i: (0, i))
    return pl.pallas_call(
        _peer_score_kernel, grid=(T // tt,),
        in_specs=[pl.BlockSpec((2 * R, tt), lambda i: (0, i)),
                  pl.BlockSpec((2 * HEADS, N_KEYS, PEER_HALF), lambda i: (0, 0, 0))],
        out_specs=[ospec] * 4,
        out_shape=[jax.ShapeDtypeStruct((R, T), F32)] * 4,
        scratch_shapes=[pltpu.VMEM((2, 3 * SUBLANES, tt), F32), pltpu.VMEM((ncand, tt), F32)],
        compiler_params=_cp("parallel"))(qt, keys)


def _peer_expert_kernel(ht_ref, u_ref, vt_ref, s2_ref, e2_ref, c_ref, e1_ref, h_ref, g_ref, b_ref,
                        o_ref, acc_sc, act_sc, hw_sc):
    j = pl.program_id(1)
    eb, tt = act_sc.shape

    @pl.when(j == 0)
    def _():
        acc_sc[...] = jnp.zeros(acc_sc.shape, F32)

    a = jnp.dot(u_ref[...], ht_ref[...], preferred_element_type=F32)
    act_sc[...] = 0.5 * a * (1.0 + lax.erf(a * (2.0 ** -0.5)))
    ng = eb // N_KEYS
    assert ng == SUBLANES
    for lc in range(tt // LANES):
        ls = slice(lc * LANES, (lc + 1) * LANES)
        rows = [pl.ds(pl.multiple_of(h * N_KEYS + j * ng, ng), ng) for h in range(HEADS)]
        thr8 = [c_ref[r, ls] for r in rows]
        e18 = [e1_ref[r, ls] for r in rows]
        for g in range(ng):
            wsum = jnp.zeros((N_KEYS, LANES), F32)
            for h in range(HEADS):
                hr = slice(h * N_KEYS, (h + 1) * N_KEYS)
                wsum = wsum + jnp.where(s2_ref[hr, ls] >= thr8[h][g:g + 1], e2_ref[hr, ls] * e18[h][g:g + 1], 0.0)
            gs = slice(g * N_KEYS, (g + 1) * N_KEYS)
            hw_sc[gs, ls] = (wsum * act_sc[gs, ls]).astype(BF16)
    acc_sc[...] += jnp.dot(vt_ref[...], hw_sc[...], preferred_element_type=F32)

    @pl.when(j == pl.num_programs(1) - 1)
    def _():
        y = DEEPNORM_ALPHA * h_ref[...] + acc_sc[...].T
        o_ref[...] = _ln_rows(y, g_ref[...], b_ref[...])


def _peer_experts(ht, u, vt, s2, e2, c, e1, h, g, b):
    T, D = h.shape
    tt, eb = PEER_TT, PEER_EB
    R = HEADS * N_KEYS
    sspec = pl.BlockSpec((R, tt), lambda i, j: (0, i))
    vec = pl.BlockSpec((1, D), lambda i, j: (0, 0))
    return pl.pallas_call(
        _peer_expert_kernel, grid=(T // tt, N_EXPERTS // eb),
        in_specs=[pl.BlockSpec((D, tt), lambda i, j: (0, i)),
                  pl.BlockSpec((eb, D), lambda i, j: (j, 0)),
                  pl.BlockSpec((D, eb), lambda i, j: (0, j)),
                  sspec, sspec, sspec, sspec,
                  pl.BlockSpec((tt, D), lambda i, j: (i, 0)), vec, vec],
        out_specs=pl.BlockSpec((tt, D), lambda i, j: (i, 0)),
        out_shape=jax.ShapeDtypeStruct((T, D), F32),
        scratch_shapes=[pltpu.VMEM((D, tt), F32), pltpu.VMEM((eb, tt), F32), pltpu.VMEM((eb, tt), BF16)],
        compiler_params=_cp("parallel", "arbitrary"))(ht, u, vt, s2, e2, c, e1, h, g.reshape(1, D), b.reshape(1, D))


def _rot_half(x, r):
    return jnp.concatenate([-x[..., r:2 * r], x[..., :r]], axis=-1)


def _prep_w_in(w):
    D = w.shape[0]
    splits = (MLA_Q_RANK, MLA_KV_RANK, MLA_ROPE, 3 * BRANCH_WIDTH, 3 * BRANCH_WIDTH, BRANCH_WIDTH, HEADS, HEADS,
              3 * D)
    o = np.cumsum((0,) + splits)
    cq, ckv, kr, mqkv, gqkv, gz, ga, gb, gl = [w[:, o[i]:o[i + 1]] for i in range(9)]
    mq, mk, mv = jnp.split(mqkv, 3, axis=1)

    def slab(m):
        m = m.reshape(D, HEADS, MOBA_HEAD_DIM)
        pad = jnp.zeros((D, HEADS, LANES - MOBA_HEAD_DIM - MOBA_ROT_DIM), w.dtype)
        return jnp.concatenate([m, _rot_half(m, MOBA_ROT_DIM // 2), pad], axis=-1).reshape(D, HEADS * LANES)

    small = jnp.concatenate([ga, gb, jnp.zeros((D, LANES // 2 - 2 * HEADS), w.dtype), kr,
                             _rot_half(kr, MLA_ROPE // 2)], axis=1)
    out = jnp.concatenate([gl, cq, ckv, slab(mq), slab(mk), mv, gqkv, gz, small], axis=1)
    assert out.shape[1] == C_TOTAL
    return out.astype(BF16)


def _prep_w_uq(w):
    R = w.shape[0]
    w = w.reshape(R, HEADS, MLA_NOPE + MLA_ROPE)
    rope = w[..., MLA_NOPE:]
    return jnp.concatenate([w, _rot_half(rope, MLA_ROPE // 2)], axis=-1).reshape(R, HEADS * LANES).astype(BF16)


def _prep_w_ukv(w):
    R = w.shape[0]
    w = w.reshape(R, HEADS, MLA_NOPE + MLA_V)
    k = jnp.concatenate([w[..., :MLA_NOPE], jnp.zeros((R, HEADS, LANES - MLA_NOPE), w.dtype)], axis=-1)
    return jnp.concatenate([k.reshape(R, HEADS * LANES), w[..., MLA_NOPE:].reshape(R, HEADS * MLA_V)],
                           axis=1).astype(BF16)


def _rope_tables(positions):
    pos = positions.reshape(-1).astype(F32)[:, None]
    T = pos.shape[0]

    def cs(rot):
        inv = ROPE_THETA ** (-jnp.arange(0, rot, 2, dtype=F32) / rot)
        ang = pos * inv
        return jnp.cos(ang), jnp.sin(ang)

    ca, sa = cs(MLA_ROPE)
    cb, sb = cs(MOBA_ROT_DIM)
    one = lambda n: jnp.ones((T, n), F32)
    zero = lambda n: jnp.zeros((T, n), F32)
    sc_a = (MLA_NOPE + MLA_ROPE) ** -0.5
    sc_b = MOBA_HEAD_DIM ** -0.5
    cat = lambda *xs: jnp.concatenate(xs, axis=1)
    mla = (cat(one(MLA_NOPE), ca, ca, zero(32)) * sc_a, cat(zero(MLA_NOPE), sa, sa, zero(32)) * sc_a,
           cat(zero(MLA_NOPE), ca, ca, zero(32)), cat(zero(MLA_NOPE), sa, sa, zero(32)))
    cm = cat(cb, cb, one(MOBA_HEAD_DIM - MOBA_ROT_DIM), zero(LANES - MOBA_HEAD_DIM))
    sm = cat(sb, sb, zero(LANES - MOBA_ROT_DIM))
    moba = (cm * sc_b, sm * sc_b, cm, sm)
    return mla, moba


def _lane_row(v):
    return jnp.concatenate([v.astype(F32), jnp.zeros((LANES - v.shape[0],), F32)]).reshape(1, LANES)


def kernel(x, positions, ln_in_g, ln_in_b, w_in, mla_q_norm, mla_kv_norm, mla_w_uq, mla_w_ukv, gdn_conv_w, gdn_A_log, gdn_dt_bias, gdn_o_norm, gate_bias, w_branch, w_out, ln1_g, ln1_b, peer_w_q, peer_sub_keys, peer_u, peer_v, ln2_g, ln2_b):
    B, S, D = x.shape
    T = B * S
    assert S % ATT_TQ == 0 and S % GDN_KCHUNK == 0 and T % PEER_TT == 0 and ATT_TQ == MOBA_BLOCK
    mla_tabs, moba_tabs = _rope_tables(positions)
    h = _layer_norm(x.reshape(T, D), ln_in_g, ln_in_b)
    for l in range(DEPTH):
        proj = _mm(h, _prep_w_in(w_in[l]), tm=1024 if T % 1024 == 0 else 512, tn=384)
        q_raw = _rmsnorm_mm(proj, C_CQ // MLA_Q_RANK, mla_q_norm[l], _prep_w_uq(mla_w_uq[l]), 512, 512)
        kv_raw = _rmsnorm_mm(proj, C_CKV // MLA_KV_RANK, mla_kv_norm[l], _prep_w_ukv(mla_w_ukv[l]), 512, 512)
        qa, ka = _mla_prep(q_raw, kv_raw, proj, mla_tabs)
        o_a = _attention(qa, ka, kv_raw, HEADS, B, S)
        qm, km, kmean = _moba_prep(proj, moba_tabs)
        o_b = _attention(qm, km, proj, C_MV // LANES, B, S, kmean=kmean.reshape(B, S // MOBA_BLOCK, HEADS * LANES))
        qkv = _gdn_conv(proj, gdn_conv_w[l], B, S)
        parts = _gdn_chunks(qkv, proj, _lane_row(gdn_A_log[l]), _lane_row(gdn_dt_bias[l]), B, S)
        o_c = _gdn_out(_gdn_scan(parts, B, S), proj, jnp.tile(gdn_o_norm[l], 2).reshape(1, LANES), B, S)
        merged = _merge(proj, gate_bias[l], o_a, o_b, o_c, w_branch[l].astype(BF16))
        h, ht = _outproj_ln(merged, w_out[l].astype(BF16), h, ln1_g[l], ln1_b[l])
        qt = _mm(peer_w_q[l].T.astype(BF16), ht, tm=512, tn=512)
        keys = peer_sub_keys[l].reshape(2 * HEADS, N_KEYS, PEER_HALF).astype(BF16)
        s2, e2, c, e1 = _peer_scores(qt, keys)
        h = _peer_experts(ht, peer_u[l].astype(BF16), peer_v[l].T.astype(BF16), s2, e2, c, e1, h, ln2_g[l], ln2_b[l])
    return h.reshape(B, S, D)
```

```python
import functools
import math

import numpy as np
import jax
import jax.numpy as jnp
from jax import lax
from jax.experimental import pallas as pl
from jax.experimental.pallas import tpu as pltpu

F32 = jnp.float32
BF16 = jnp.bfloat16
HIGHEST = lax.Precision.HIGHEST

DEPTH = 2
ROPE_THETA = 500000.0
NEG_INF = -1e30
LN_EPS = 1e-5
RMS_EPS = 1e-6
DEEPNORM_ALPHA = (2 * DEPTH) ** 0.25
HEADS = 8
MLA_NOPE, MLA_ROPE, MLA_V = 64, 32, 64
MLA_Q_RANK, MLA_KV_RANK = 768, 256
MOBA_HEAD_DIM, MOBA_ROT_DIM, MOBA_BLOCK, MOBA_TOPK = 64, 16, 256, 3
GDN_HEAD_DIM, GDN_CONV = 64, 4
BRANCH_WIDTH = 512
N_KEYS, PEER_TOPK, PEER_HALF = 128, 16, 128
N_EXPERTS = N_KEYS * N_KEYS

LANES = 128
SUBLANES = 8
VMEM_LIMIT = 56 * 1024 * 1024

GDN_KCHUNK = 128
ATT_TQ = 256
PEER_TT = 512
PEER_EB = 1024
PEER_SCORE_TT = 256

C_GATE, C_CQ, C_CKV, C_MQ, C_MK, C_MV, C_GQKV, C_GZ, C_SMALL, C_TOTAL = (
    0, 3072, 3840, 4096, 5120, 6144, 6656, 8192, 8704, 8832)


def _cp(*sem):
    return pltpu.CompilerParams(dimension_semantics=sem, vmem_limit_bytes=VMEM_LIMIT)


def _dot(a, b):
    return jnp.dot(a.astype(BF16), b.astype(BF16), preferred_element_type=F32)


def _dot_nt(a, b):
    return lax.dot_general(a.astype(BF16), b.astype(BF16), (((1,), (1,)), ((), ())),
                           preferred_element_type=F32)


def _split(a):
    hi = a.astype(BF16)
    return hi, (a - hi.astype(F32)).astype(BF16)


def _dot3(a, b):
    ah, al = _split(a)
    bh, bl = _split(b)
    d = functools.partial(jnp.dot, preferred_element_type=F32)
    return d(ah, bh) + (d(al, bh) + d(ah, bl))


def _ln_rows(y, g, b):
    mu = jnp.mean(y, axis=-1, keepdims=True)
    d = y - mu
    var = jnp.mean(d * d, axis=-1, keepdims=True)
    return d * lax.rsqrt(var + LN_EPS) * g + b


def _sigmoid(x):
    return 1.0 / (1.0 + jnp.exp(-x))


def _ln_kernel(x_ref, g_ref, b_ref, o_ref):
    o_ref[...] = _ln_rows(x_ref[...], g_ref[...], b_ref[...])


def _layer_norm(x, g, b, tm=512):
    T, D = x.shape
    return pl.pallas_call(
        _ln_kernel, grid=(T // tm,),
        in_specs=[pl.BlockSpec((tm, D), lambda i: (i, 0)),
                  pl.BlockSpec((1, D), lambda i: (0, 0)),
                  pl.BlockSpec((1, D), lambda i: (0, 0))],
        out_specs=pl.BlockSpec((tm, D), lambda i: (i, 0)),
        out_shape=jax.ShapeDtypeStruct((T, D), F32),
        compiler_params=_cp("parallel"))(x, g.reshape(1, D), b.reshape(1, D))


def _mm_kernel(x_ref, w_ref, o_ref):
    o_ref[...] = _dot(x_ref[...], w_ref[...]).astype(o_ref.dtype)


def _mm(x, w, tm, tn, out_dtype=F32):
    M, K = x.shape
    N = w.shape[1]
    return pl.pallas_call(
        _mm_kernel, grid=(M // tm, N // tn),
        in_specs=[pl.BlockSpec((tm, K), lambda i, j: (i, 0)),
                  pl.BlockSpec((K, tn), lambda i, j: (0, j))],
        out_specs=pl.BlockSpec((tm, tn), lambda i, j: (i, j)),
        out_shape=jax.ShapeDtypeStruct((M, N), out_dtype),
        compiler_params=_cp("parallel", "parallel"))(x, w)


def _rmsmm_kernel(x_ref, g_ref, w_ref, o_ref):
    x = x_ref[...]
    xn = x * lax.rsqrt(jnp.mean(x * x, axis=-1, keepdims=True) + RMS_EPS) * g_ref[...]
    o_ref[...] = _dot(xn, w_ref[...])


def _rmsnorm_mm(proj, col_block, g, w, tm, tn):
    T = proj.shape[0]
    K, N = w.shape
    return pl.pallas_call(
        _rmsmm_kernel, grid=(T // tm, N // tn),
        in_specs=[pl.BlockSpec((tm, K), lambda i, j: (i, col_block)),
                  pl.BlockSpec((1, K), lambda i, j: (0, 0)),
                  pl.BlockSpec((K, tn), lambda i, j: (0, j))],
        out_specs=pl.BlockSpec((tm, tn), lambda i, j: (i, j)),
        out_shape=jax.ShapeDtypeStruct((T, N), F32),
        compiler_params=_cp("parallel", "parallel"))(proj, g.reshape(1, K), w)


def _head_slabs(x):
    return [x[:, h * LANES:(h + 1) * LANES] for h in range(HEADS)]


def _mla_prep_kernel(q_ref, k_ref, sm_ref, cq_ref, sq_ref, ck_ref, sk_ref, qo_ref, ko_ref):
    cq, sq = cq_ref[...], sq_ref[...]
    sm = sm_ref[...]
    kr = sm * ck_ref[...] + pltpu.roll(sm, LANES - MLA_ROPE, 1) * sk_ref[...]
    q, k = q_ref[...], k_ref[...]
    for h, (qs, ks) in enumerate(zip(_head_slabs(q), _head_slabs(k))):
        sl = slice(h * LANES, (h + 1) * LANES)
        qo_ref[:, sl] = (qs * cq + pltpu.roll(qs, LANES - MLA_ROPE, 1) * sq).astype(BF16)
        ko_ref[:, sl] = (ks + kr).astype(BF16)


def _mla_prep(q_raw, kv_raw, proj, tabs, tm=512):
    T = q_raw.shape[0]
    W = HEADS * LANES
    row = lambda i: (i, 0)
    tab = pl.BlockSpec((tm, LANES), row)
    return pl.pallas_call(
        _mla_prep_kernel, grid=(T // tm,),
        in_specs=[pl.BlockSpec((tm, W), row), pl.BlockSpec((tm, W), row),
                  pl.BlockSpec((tm, LANES), lambda i: (i, C_SMALL // LANES)), tab, tab, tab, tab],
        out_specs=[pl.BlockSpec((tm, W), row), pl.BlockSpec((tm, W), row)],
        out_shape=[jax.ShapeDtypeStruct((T, W), BF16)] * 2,
        compiler_params=_cp("parallel"))(q_raw, kv_raw, proj, *tabs)


def _moba_prep_kernel(q_ref, k_ref, cq_ref, sq_ref, ck_ref, sk_ref, qo_ref, ko_ref, km_ref):
    cq, sq, ck, sk = cq_ref[...], sq_ref[...], ck_ref[...], sk_ref[...]
    q, k = q_ref[...], k_ref[...]
    for h, (qs, ks) in enumerate(zip(_head_slabs(q), _head_slabs(k))):
        sl = slice(h * LANES, (h + 1) * LANES)
        qo_ref[:, sl] = qs * cq + pltpu.roll(qs, LANES // 2, 1) * sq
        kk = ks * ck + pltpu.roll(ks, LANES // 2, 1) * sk
        ko_ref[:, sl] = kk.astype(BF16)
        km_ref[0, :, sl] = jnp.mean(kk, axis=0, keepdims=True)


def _moba_prep(proj, tabs):
    T = proj.shape[0]
    W = HEADS * LANES
    tm = MOBA_BLOCK
    row = lambda i: (i, 0)
    tab = pl.BlockSpec((tm, LANES), row)
    return pl.pallas_call(
        _moba_prep_kernel, grid=(T // tm,),
        in_specs=[pl.BlockSpec((tm, W), lambda i: (i, C_MQ // W)),
                  pl.BlockSpec((tm, W), lambda i: (i, C_MK // W)), tab, tab, tab, tab],
        out_specs=[pl.BlockSpec((tm, W), row), pl.BlockSpec((tm, W), row),
                   pl.BlockSpec((1, 1, W), lambda i: (i, 0, 0))],
        out_shape=[jax.ShapeDtypeStruct((T, W), F32), jax.ShapeDtypeStruct((T, W), BF16),
                   jax.ShapeDtypeStruct((T // tm, 1, W), F32)],
        compiler_params=_cp("parallel"))(proj, proj, *tabs)


def _attn_kernel(*refs, moba, nk):
    if moba:
        q_ref, k_ref, v_ref, km_ref, o_ref, vt_sc, qb_sc, m_sc, l_sc, al_sc, acc_sc, s_sc, p_sc, bias_sc = refs
    else:
        q_ref, k_ref, v_ref, o_ref, vt_sc, m_sc, l_sc, al_sc, acc_sc, s_sc, p_sc = refs
    tq = tk = ATT_TQ
    dv = BRANCH_WIDTH // HEADS
    qi = pl.program_id(1)
    hslab = [slice(h * LANES, (h + 1) * LANES) for h in range(HEADS)]

    @pl.when(qi == 0)
    def _():
        for j in range(nk):
            for c in range(tk // LANES):
                for g in range(BRANCH_WIDTH // LANES):
                    vt_sc[j, g * LANES:(g + 1) * LANES, c * LANES:(c + 1) * LANES] = (
                        v_ref[j * tk + c * LANES:j * tk + (c + 1) * LANES, g * LANES:(g + 1) * LANES].T.astype(BF16))

    if moba:
        blk = lax.broadcasted_iota(jnp.int32, (SUBLANES, tq), 0)
        for h in range(HEADS):
            qf = q_ref[:, hslab[h]]
            qb_sc[:, hslab[h]] = qf.astype(BF16)
            km = km_ref[0, :, hslab[h]]
            if nk < SUBLANES:
                km = jnp.concatenate([km, jnp.zeros((SUBLANES - nk, LANES), F32)], axis=0)
            gate = lax.dot_general(km, qf, (((1,), (1,)), ((), ())), precision=HIGHEST,
                                   preferred_element_type=F32)
            gate = jnp.where(blk < qi, gate, -jnp.inf)
            for n in range(nk):
                gn = gate[n:n + 1, :]
                beats = jnp.where(gate > gn, 1.0, jnp.where((gate == gn) & (blk < n), 1.0, 0.0))
                cnt = jnp.sum(beats, axis=0, keepdims=True)
                bias_sc[h, n] = jnp.where((cnt < MOBA_TOPK) & (n < qi), 0.0, NEG_INF)
    qsrc = qb_sc if moba else q_ref
    m_sc[...] = jnp.full(m_sc.shape, NEG_INF, F32)
    l_sc[...] = jnp.zeros(l_sc.shape, F32)
    acc_sc[...] = jnp.zeros(acc_sc.shape, F32)

    def block(j, diag):
        rows = pl.ds(pl.multiple_of(j * tk, tk), tk)
        for h in range(HEADS):
            s = _dot_nt(k_ref[rows, hslab[h]], qsrc[:, hslab[h]])
            if diag:
                keyi = lax.broadcasted_iota(jnp.int32, (tk, tq), 0)
                qryi = lax.broadcasted_iota(jnp.int32, (tk, tq), 1)
                s = jnp.where(keyi <= qryi, s, NEG_INF)
            elif moba:
                s = s + bias_sc[h, j]
            s_sc[h] = s
        for h in range(HEADS):
            s = s_sc[h]
            m_prev = m_sc[h]
            m_new = jnp.maximum(m_prev, jnp.max(s, axis=0, keepdims=True))
            alpha = jnp.exp(m_prev - m_new)
            p = jnp.exp(s - m_new)
            l_sc[h] = alpha * l_sc[h] + jnp.sum(p, axis=0, keepdims=True)
            p_sc[h] = p.astype(BF16)
            al_sc[h] = alpha
            m_sc[h] = m_new
        for h in range(HEADS):
            acc_sc[h] = al_sc[h] * acc_sc[h] + jnp.dot(vt_sc[j, h * dv:(h + 1) * dv, :], p_sc[h],
                                                       preferred_element_type=F32)

    def past(j, carry):
        block(j, False)
        return carry
    lax.fori_loop(0, qi, past, 0)
    block(qi, True)
    ot = jnp.concatenate([acc_sc[h] * (1.0 / l_sc[h]) for h in range(HEADS)], axis=0)
    o_ref[...] = ot.T


def _attention(q, k, v, v_col, B, S, kmean=None):
    T = B * S
    tq = ATT_TQ
    nq = S // tq
    W = HEADS * LANES
    dv = BRANCH_WIDTH // HEADS
    moba = kmean is not None
    assert nq <= SUBLANES and v_col % BRANCH_WIDTH == 0
    in_specs = [pl.BlockSpec((tq, W), lambda b, i: (b * nq + i, 0)),
                pl.BlockSpec((S, W), lambda b, i: (b, 0)),
                pl.BlockSpec((S, BRANCH_WIDTH), lambda b, i: (b, v_col // BRANCH_WIDTH))]
    scratch = [pltpu.VMEM((nq, BRANCH_WIDTH, tq), BF16)]
    if moba:
        scratch.append(pltpu.VMEM((tq, W), BF16))
    row = pltpu.VMEM((HEADS, 1, tq), F32)
    scratch += [row, row, row, pltpu.VMEM((HEADS, dv, tq), F32),
                pltpu.VMEM((HEADS, tq, tq), F32), pltpu.VMEM((HEADS, tq, tq), BF16)]
    args = [q, k, v]
    if moba:
        in_specs.append(pl.BlockSpec((1, nq, W), lambda b, i: (b, 0, 0)))
        scratch.append(pltpu.VMEM((HEADS, nq, 1, tq), F32))
        args.append(kmean)
    return pl.pallas_call(
        functools.partial(_attn_kernel, moba=moba, nk=nq),
        grid=(B, nq), in_specs=in_specs,
        out_specs=pl.BlockSpec((tq, BRANCH_WIDTH), lambda b, i: (b * nq + i, 0)),
        out_shape=jax.ShapeDtypeStruct((T, BRANCH_WIDTH), F32),
        scratch_shapes=scratch,
        compiler_params=_cp("parallel", "arbitrary"))(*args)


def _gdn_conv_kernel(x_ref, w_ref, o_ref):
    c = pl.program_id(1)
    x = x_ref[...]
    w = w_ref[...]
    S = x.shape[0]
    row = lax.broadcasted_iota(jnp.int32, (S, LANES), 0)
    lane = lax.broadcasted_iota(jnp.int32, (S, LANES), 1)
    y = x * w[GDN_CONV - 1:GDN_CONV, :]
    for d in range(1, GDN_CONV):
        xs = jnp.where(row >= d, pltpu.roll(x, d, 0), 0.0)
        y = y + xs * w[GDN_CONV - 1 - d:GDN_CONV - d, :]
    y = y * _sigmoid(y)
    sq = y * y
    lo = lane < GDN_HEAD_DIM
    ss0 = jnp.sum(jnp.where(lo, sq, 0.0), axis=1, keepdims=True)
    ss1 = jnp.sum(jnp.where(lo, 0.0, sq), axis=1, keepdims=True)
    inv = lax.rsqrt(jnp.where(lo, ss0, ss1) + RMS_EPS)
    nqb = BRANCH_WIDTH // LANES
    scale = jnp.where(c < nqb, GDN_HEAD_DIM ** -0.5, 1.0)
    o_ref[...] = jnp.where(c < 2 * nqb, y * inv * scale, y)


def _gdn_conv(proj, conv_w, B, S):
    T = B * S
    nb = 3 * BRANCH_WIDTH // LANES
    return pl.pallas_call(
        _gdn_conv_kernel, grid=(B, nb),
        in_specs=[pl.BlockSpec((S, LANES), lambda b, c: (b, C_GQKV // LANES + c)),
                  pl.BlockSpec((GDN_CONV, LANES), lambda b, c: (0, c))],
        out_specs=pl.BlockSpec((S, LANES), lambda b, c: (b, c)),
        out_shape=jax.ShapeDtypeStruct((T, 3 * BRANCH_WIDTH), F32),
        compiler_params=_cp("parallel", "parallel"))(proj, conv_w)


def _gdn_chunk_kernel(q_ref, k_ref, v_ref, sm_ref, alog_ref, dtb_ref,
                      u_ref, w_ref, qg_ref, qk_ref, kdt_ref, dl_ref):
    C = GDN_KCHUNK
    h = pl.program_id(1)
    odd = (h % 2) == 1
    lane = lax.broadcasted_iota(jnp.int32, (C, LANES), 1)
    row = lax.broadcasted_iota(jnp.int32, (C, LANES), 0)
    lo = lane < GDN_HEAD_DIM

    def pick(ref):
        x = ref[...]
        return jnp.where(lo, jnp.where(odd, pltpu.roll(x, LANES // 2, 1), x), 0.0)

    qh, kh, vh = pick(q_ref), pick(k_ref), pick(v_ref)
    sm = sm_ref[...]
    a = sm + dtb_ref[...]
    softplus = jnp.maximum(a, 0.0) + jnp.log1p(jnp.exp(-jnp.abs(a)))
    garr = -jnp.exp(alog_ref[...]) * softplus
    g_col = jnp.sum(jnp.where(lane == h, garr, 0.0), axis=1, keepdims=True)
    beta = jnp.sum(jnp.where(lane == HEADS + h, _sigmoid(sm), 0.0), axis=1, keepdims=True)
    tril = row >= lane
    strict = row > lane
    gc = jnp.dot(tril.astype(F32), jnp.broadcast_to(g_col, (C, LANES)), precision=HIGHEST,
                 preferred_element_type=F32)
    gr = gc.T
    decay = jnp.where(tril, jnp.exp(jnp.where(tril, gc - gr, 0.0)), 0.0)
    kb = kh * beta
    A = jnp.where(strict, _dot_nt(kb, kh) * decay, 0.0)
    eye = jnp.where(row == lane, 1.0, 0.0)
    P = eye - A
    X = A
    for _ in range(int(math.log2(C)) - 1):
        X = _dot3(X, X)
        P = P + _dot3(P, X)
    eg = jnp.exp(gc)
    u_ref[...] = _dot3(P, vh * beta)
    w_ref[...] = _dot3(P, kb * eg)
    qk_ref[...] = jnp.where(tril, _dot_nt(qh, kh) * decay, 0.0)
    qg_ref[...] = qh * eg
    glast = gc[C - 1:C, :]
    kdt_ref[...] = (kh * jnp.exp(glast - gc)).T
    dl_ref[...] = jnp.broadcast_to(jnp.exp(glast), (SUBLANES, LANES))


def _gdn_chunks(qkv, proj, alog_row, dtb_row, B, S):
    C = GDN_KCHUNK
    n = S // C
    nqb = BRANCH_WIDTH // LANES
    rows = B * HEADS * n
    big = jax.ShapeDtypeStruct((rows * C, LANES), F32)
    ospec = pl.BlockSpec((C, LANES), lambda b, h, c: ((b * HEADS + h) * n + c, 0))
    par = pl.BlockSpec((1, LANES), lambda b, h, c: (0, 0))
    return pl.pallas_call(
        _gdn_chunk_kernel, grid=(B, HEADS, n),
        in_specs=[pl.BlockSpec((C, LANES), lambda b, h, c: (b * n + c, h // 2)),
                  pl.BlockSpec((C, LANES), lambda b, h, c: (b * n + c, nqb + h // 2)),
                  pl.BlockSpec((C, LANES), lambda b, h, c: (b * n + c, 2 * nqb + h // 2)),
                  pl.BlockSpec((C, LANES), lambda b, h, c: (b * n + c, C_SMALL // LANES)), par, par],
        out_specs=[ospec] * 5 + [pl.BlockSpec((SUBLANES, LANES), lambda b, h, c: ((b * HEADS + h) * n + c, 0))],
        out_shape=[big] * 5 + [jax.ShapeDtypeStruct((rows * SUBLANES, LANES), F32)],
        compiler_params=_cp("parallel", "parallel", "parallel"))(qkv, qkv, qkv, proj, alog_row, dtb_row)


def _gdn_scan_kernel(u_ref, w_ref, qg_ref, qk_ref, kdt_ref, dl_ref, o_ref, *, n):
    C = GDN_KCHUNK

    def step(c, state):
        r = pl.multiple_of(c * C, C)
        sb = state.astype(BF16)
        v_new = u_ref[pl.ds(r, C), :] - _dot(w_ref[pl.ds(r, C), :], sb)
        vb = v_new.astype(BF16)
        o_ref[pl.ds(r, C), :] = _dot(qg_ref[pl.ds(r, C), :], sb) + _dot(qk_ref[pl.ds(r, C), :], vb)
        dl = dl_ref[pl.ds(pl.multiple_of(c * SUBLANES, SUBLANES), 1), :]
        return state * dl + _dot(kdt_ref[pl.ds(r, C), :], vb)

    lax.fori_loop(0, n, step, jnp.zeros((LANES, LANES), F32))


def _gdn_scan(parts, B, S):
    n = S // GDN_KCHUNK
    seq = pl.BlockSpec((S, LANES), lambda i: (i, 0))
    return pl.pallas_call(
        functools.partial(_gdn_scan_kernel, n=n), grid=(B * HEADS,),
        in_specs=[seq] * 5 + [pl.BlockSpec((n * SUBLANES, LANES), lambda i: (i, 0))],
        out_specs=seq,
        out_shape=jax.ShapeDtypeStruct((B * HEADS * S, LANES), F32),
        compiler_params=_cp("parallel"))(*parts)


def _gdn_out_kernel(o0_ref, o1_ref, z_ref, g_ref, o_ref):
    lane = lax.broadcasted_iota(jnp.int32, o_ref.shape, 1)

    def nrm(o):
        ms = jnp.sum(o * o, axis=1, keepdims=True) * (1.0 / GDN_HEAD_DIM)
        return o * lax.rsqrt(ms + RMS_EPS)

    nn = jnp.where(lane < GDN_HEAD_DIM, nrm(o0_ref[...]), pltpu.roll(nrm(o1_ref[...]), LANES // 2, 1))
    z = z_ref[...]
    o_ref[...] = nn * g_ref[...] * (z * _sigmoid(z))


def _gdn_out(o, proj, g_row, B, S, tm=512):
    ns = S // tm
    return pl.pallas_call(
        _gdn_out_kernel, grid=(B, ns, HEADS // 2),
        in_specs=[pl.BlockSpec((tm, LANES), lambda b, s, p: ((b * HEADS + 2 * p) * ns + s, 0)),
                  pl.BlockSpec((tm, LANES), lambda b, s, p: ((b * HEADS + 2 * p + 1) * ns + s, 0)),
                  pl.BlockSpec((tm, LANES), lambda b, s, p: (b * ns + s, C_GZ // LANES + p)),
                  pl.BlockSpec((1, LANES), lambda b, s, p: (0, 0))],
        out_specs=pl.BlockSpec((tm, LANES), lambda b, s, p: (b * ns + s, p)),
        out_shape=jax.ShapeDtypeStruct((B * S, BRANCH_WIDTH), F32),
        compiler_params=_cp("parallel", "parallel", "parallel"))(o, o, proj, g_row)


def _merge_kernel(g0_ref, g1_ref, g2_ref, gb_ref, a_ref, b_ref, c_ref, wb_ref, o_ref):
    acc = None
    for n, (gl, br) in enumerate(((g0_ref, a_ref), (g1_ref, b_ref), (g2_ref, c_ref))):
        y = _sigmoid(gl[...] + gb_ref[n:n + 1, :]) * _dot(br[...], wb_ref[n])
        acc = y if acc is None else acc + y
    o_ref[...] = acc.astype(BF16)


def _merge(proj, gate_bias, o_a, o_b, o_c, w_branch, tm=512, tn=512):
    T = o_a.shape[0]
    D = w_branch.shape[2]
    nj = D // tn
    gspec = lambda n: pl.BlockSpec((tm, tn), lambda i, j: (i, n * nj + j))
    bspec = pl.BlockSpec((tm, BRANCH_WIDTH), lambda i, j: (i, 0))
    return pl.pallas_call(
        _merge_kernel, grid=(T // tm, nj),
        in_specs=[gspec(0), gspec(1), gspec(2), pl.BlockSpec((3, tn), lambda i, j: (0, j)),
                  bspec, bspec, bspec, pl.BlockSpec((3, BRANCH_WIDTH, tn), lambda i, j: (0, 0, j))],
        out_specs=pl.BlockSpec((tm, tn), lambda i, j: (i, j)),
        out_shape=jax.ShapeDtypeStruct((T, D), BF16),
        compiler_params=_cp("parallel", "parallel"))(proj, proj, proj, gate_bias, o_a, o_b, o_c, w_branch)


def _outproj_kernel(m_ref, w_ref, h_ref, g_ref, b_ref, o_ref, ot_ref):
    y = DEEPNORM_ALPHA * h_ref[...] + _dot(m_ref[...], w_ref[...])
    y = _ln_rows(y, g_ref[...], b_ref[...])
    o_ref[...] = y
    ot_ref[...] = y.T.astype(BF16)


def _outproj_ln(merged, w_out, h, g, b, tm=256):
    T, D = h.shape
    return pl.pallas_call(
        _outproj_kernel, grid=(T // tm,),
        in_specs=[pl.BlockSpec((tm, D), lambda i: (i, 0)), pl.BlockSpec((D, D), lambda i: (0, 0)),
                  pl.BlockSpec((tm, D), lambda i: (i, 0)),
                  pl.BlockSpec((1, D), lambda i: (0, 0)), pl.BlockSpec((1, D), lambda i: (0, 0))],
        out_specs=[pl.BlockSpec((tm, D), lambda i: (i, 0)), pl.BlockSpec((D, tm), lambda i: (0, i))],
        out_shape=[jax.ShapeDtypeStruct((T, D), F32), jax.ShapeDtypeStruct((D, T), BF16)],
        compiler_params=_cp("parallel"))(merged, w_out, h, g.reshape(1, D), b.reshape(1, D))


def _extract_top(cur, n, on_max):
    for k in range(n):
        m = jnp.max(cur, axis=0, keepdims=True)
        on_max(k, m)
        if k + 1 < n:
            cur = jnp.where(cur >= m, -jnp.inf, cur)


def _peer_score_kernel(qt_ref, keys_ref, s2_ref, e2_ref, c_ref, e1_ref, top_sc, cand_sc):
    K = PEER_TOPK
    tt = qt_ref.shape[1]

    def head(h, carry):
        s = []
        for p in range(2):
            r = pl.multiple_of(h * 2 * PEER_HALF + p * PEER_HALF, PEER_HALF)
            sp = _dot(keys_ref[2 * h + p], qt_ref[pl.ds(r, PEER_HALF), :])
            s.append(sp)

            def put(k, m, p=p):
                top_sc[p, k:k + 1, :] = m
            _extract_top(sp, K + 1, put)
        v1a, v1x = top_sc[0, 0:K, :], top_sc[0, K:K + 1, :]
        v2a, v2x = top_sc[1, 0:K, :], top_sc[1, K:K + 1, :]
        cand_sc[0:K, :] = v1a + v2a[0:1]
        for b in range(1, SUBLANES):
            cand_sc[K + SUBLANES * (b - 1):K + SUBLANES * b, :] = v1a[0:SUBLANES] + v2a[b:b + 1]
        base = K + SUBLANES * (SUBLANES - 1)
        cand_sc[base:base + SUBLANES, :] = v1a[0:1] + v2a[SUBLANES:K]
        r8 = lax.broadcasted_iota(jnp.int32, (SUBLANES, tt), 0)
        cand_sc[base + SUBLANES:base + 2 * SUBLANES, :] = jnp.where(
            r8 == 0, v1x + v2a[0:1], jnp.where(r8 == 1, v1a[0:1] + v2x, -jnp.inf))
        st = {"z": jnp.zeros((1, tt), F32)}

        def acc(k, m):
            if k == 0:
                st["top"] = m
            if k < K:
                st["z"] = st["z"] + jnp.exp(m - st["top"])
            if k == K - 1:
                st["t16"] = m
            if k == K:
                st["t17"] = m
        _extract_top(cand_sc[...], K + 1, acc)
        tau = 0.5 * (st["t16"] + st["t17"])
        ro = pl.ds(pl.multiple_of(h * N_KEYS, N_KEYS), N_KEYS)
        s2_ref[ro, :] = s[1]
        e2_ref[ro, :] = jnp.exp(s[1] - v2a[0:1]) / st["z"]
        c_ref[ro, :] = tau - s[0]
        e1_ref[ro, :] = jnp.exp(s[0] - v1a[0:1])
        return carry

    lax.fori_loop(0, HEADS, head, 0)


def _peer_scores(qt, keys):
    T = qt.shape[1]
    tt = PEER_SCORE_TT
    R = HEADS * N_KEYS
    ncand = PEER_TOPK + SUBLANES * (SUBLANES + 1)
    ospec = pl.BlockSpec((R, tt), lambda i: (0, i))
    return pl.pallas_call(
        _peer_score_kernel, grid=(T // tt,),
        in_specs=[pl.BlockSpec((2 * R, tt), lambda i: (0, i)),
                  pl.BlockSpec((2 * HEADS, N_KEYS, PEER_HALF), lambda i: (0, 0, 0))],
        out_specs=[ospec] * 4,
        out_shape=[jax.ShapeDtypeStruct((R, T), F32)] * 4,
        scratch_shapes=[pltpu.VMEM((2, 3 * SUBLANES, tt), F32), pltpu.VMEM((ncand, tt), F32)],
        compiler_params=_cp("parallel"))(qt, keys)


def _peer_expert_kernel(ht_ref, u_ref, vt_ref, s2_ref, e2_ref, c_ref, e1_ref, h_ref, g_ref, b_ref,
                        o_ref, acc_sc, act_sc, hw_sc):
    j = pl.program_id(1)
    eb, tt = act_sc.shape

    @pl.when(j == 0)
    def _():
        acc_sc[...] = jnp.zeros(acc_sc.shape, F32)

    a = jnp.dot(u_ref[...], ht_ref[...], preferred_element_type=F32)
    act_sc[...] = 0.5 * a * (1.0 + lax.erf(a * (2.0 ** -0.5)))
    ng = eb // N_KEYS
    assert ng == SUBLANES
    for lc in range(tt // LANES):
        ls = slice(lc * LANES, (lc + 1) * LANES)
        rows = [pl.ds(pl.multiple_of(h * N_KEYS + j * ng, ng), ng) for h in range(HEADS)]
        thr8 = [c_ref[r, ls] for r in rows]
        e18 = [e1_ref[r, ls] for r in rows]
        for g in range(ng):
            wsum = jnp.zeros((N_KEYS, LANES), F32)
            for h in range(HEADS):
                hr = slice(h * N_KEYS, (h + 1) * N_KEYS)
                wsum = wsum + jnp.where(s2_ref[hr, ls] >= thr8[h][g:g + 1], e2_ref[hr, ls] * e18[h][g:g + 1], 0.0)
            gs = slice(g * N_KEYS, (g + 1) * N_KEYS)
            hw_sc[gs, ls] = (wsum * act_sc[gs, ls]).astype(BF16)
    acc_sc[...] += jnp.dot(vt_ref[...], hw_sc[...], preferred_element_type=F32)

    @pl.when(j == pl.num_programs(1) - 1)
    def _():
        y = DEEPNORM_ALPHA * h_ref[...] + acc_sc[...].T
        o_ref[...] = _ln_rows(y, g_ref[...], b_ref[...])


def _peer_experts(ht, u, vt, s2, e2, c, e1, h, g, b):
    T, D = h.shape
    tt, eb = PEER_TT, PEER_EB
    R = HEADS * N_KEYS
    sspec = pl.BlockSpec((R, tt), lambda i, j: (0, i))
    vec = pl.BlockSpec((1, D), lambda i, j: (0, 0))
    return pl.pallas_call(
        _peer_expert_kernel, grid=(T // tt, N_EXPERTS // eb),
        in_specs=[pl.BlockSpec((D, tt), lambda i, j: (0, i)),
                  pl.BlockSpec((eb, D), lambda i, j: (j, 0)),
                  pl.BlockSpec((D, eb), lambda i, j: (0, j)),
                  sspec, sspec, sspec, sspec,
                  pl.BlockSpec((tt, D), lambda i, j: (i, 0)), vec, vec],
        out_specs=pl.BlockSpec((tt, D), lambda i, j: (i, 0)),
        out_shape=jax.ShapeDtypeStruct((T, D), F32),
        scratch_shapes=[pltpu.VMEM((D, tt), F32), pltpu.VMEM((eb, tt), F32), pltpu.VMEM((eb, tt), BF16)],
        compiler_params=_cp("parallel", "arbitrary"))(ht, u, vt, s2, e2, c, e1, h, g.reshape(1, D), b.reshape(1, D))


def _rot_half(x, r):
    return jnp.concatenate([-x[..., r:2 * r], x[..., :r]], axis=-1)


def _prep_w_in(w):
    D = w.shape[0]
    splits = (MLA_Q_RANK, MLA_KV_RANK, MLA_ROPE, 3 * BRANCH_WIDTH, 3 * BRANCH_WIDTH, BRANCH_WIDTH, HEADS, HEADS,
              3 * D)
    o = np.cumsum((0,) + splits)
    cq, ckv, kr, mqkv, gqkv, gz, ga, gb, gl = [w[:, o[i]:o[i + 1]] for i in range(9)]
    mq, mk, mv = jnp.split(mqkv, 3, axis=1)

    def slab(m):
        m = m.reshape(D, HEADS, MOBA_HEAD_DIM)
        pad = jnp.zeros((D, HEADS, LANES - MOBA_HEAD_DIM - MOBA_ROT_DIM), w.dtype)
        return jnp.concatenate([m, _rot_half(m, MOBA_ROT_DIM // 2), pad], axis=-1).reshape(D, HEADS * LANES)

    small = jnp.concatenate([ga, gb, jnp.zeros((D, LANES // 2 - 2 * HEADS), w.dtype), kr,
                             _rot_half(kr, MLA_ROPE // 2)], axis=1)
    out = jnp.concatenate([gl, cq, ckv, slab(mq), slab(mk), mv, gqkv, gz, small], axis=1)
    assert out.shape[1] == C_TOTAL
    return out.astype(BF16)


def _prep_w_uq(w):
    R = w.shape[0]
    w = w.reshape(R, HEADS, MLA_NOPE + MLA_ROPE)
    rope = w[..., MLA_NOPE:]
    return jnp.concatenate([w, _rot_half(rope, MLA_ROPE // 2)], axis=-1).reshape(R, HEADS * LANES).astype(BF16)


def _prep_w_ukv(w):
    R = w.shape[0]
    w = w.reshape(R, HEADS, MLA_NOPE + MLA_V)
    k = jnp.concatenate([w[..., :MLA_NOPE], jnp.zeros((R, HEADS, LANES - MLA_NOPE), w.dtype)], axis=-1)
    return jnp.concatenate([k.reshape(R, HEADS * LANES), w[..., MLA_NOPE:].reshape(R, HEADS * MLA_V)],
                           axis=1).astype(BF16)


def _rope_tables(positions):
    pos = positions.reshape(-1).astype(F32)[:, None]
    T = pos.shape[0]

    def cs(rot):
        inv = ROPE_THETA ** (-jnp.arange(0, rot, 2, dtype=F32) / rot)
        ang = pos * inv
        return jnp.cos(ang), jnp.sin(ang)

    ca, sa = cs(MLA_ROPE)
    cb, sb = cs(MOBA_ROT_DIM)
    one = lambda n: jnp.ones((T, n), F32)
    zero = lambda n: jnp.zeros((T, n), F32)
    sc_a = (MLA_NOPE + MLA_ROPE) ** -0.5
    sc_b = MOBA_HEAD_DIM ** -0.5
    cat = lambda *xs: jnp.concatenate(xs, axis=1)
    mla = (cat(one(MLA_NOPE), ca, ca, zero(32)) * sc_a, cat(zero(MLA_NOPE), sa, sa, zero(32)) * sc_a,
           cat(zero(MLA_NOPE), ca, ca, zero(32)), cat(zero(MLA_NOPE), sa, sa, zero(32)))
    cm = cat(cb, cb, one(MOBA_HEAD_DIM - MOBA_ROT_DIM), zero(LANES - MOBA_HEAD_DIM))
    sm = cat(sb, sb, zero(LANES - MOBA_ROT_DIM))
    moba = (cm * sc_b, sm * sc_b, cm, sm)
    return mla, moba


def _lane_row(v):
    return jnp.concatenate([v.astype(F32), jnp.zeros((LANES - v.shape[0],), F32)]).reshape(1, LANES)


def kernel(x, positions, ln_in_g, ln_in_b, w_in, mla_q_norm, mla_kv_norm, mla_w_uq, mla_w_ukv, gdn_conv_w, gdn_A_log, gdn_dt_bias, gdn_o_norm, gate_bias, w_branch, w_out, ln1_g, ln1_b, peer_w_q, peer_sub_keys, peer_u, peer_v, ln2_g, ln2_b):
    B, S, D = x.shape
    T = B * S
    assert S % ATT_TQ == 0 and S % GDN_KCHUNK == 0 and T % PEER_TT == 0 and ATT_TQ == MOBA_BLOCK
    mla_tabs, moba_tabs = _rope_tables(positions)
    h = _layer_norm(x.reshape(T, D), ln_in_g, ln_in_b)
    for l in range(DEPTH):
        proj = _mm(h, _prep_w_in(w_in[l]), tm=1024 if T % 1024 == 0 else 512, tn=384)
        q_raw = _rmsnorm_mm(proj, C_CQ // MLA_Q_RANK, mla_q_norm[l], _prep_w_uq(mla_w_uq[l]), 512, 512)
        kv_raw = _rmsnorm_mm(proj, C_CKV // MLA_KV_RANK, mla_kv_norm[l], _prep_w_ukv(mla_w_ukv[l]), 512, 512)
        qa, ka = _mla_prep(q_raw, kv_raw, proj, mla_tabs)
        o_a = _attention(qa, ka, kv_raw, HEADS * LANES, B, S)
        qm, km, kmean = _moba_prep(proj, moba_tabs)
        o_b = _attention(qm, km, proj, C_MV, B, S, kmean=kmean.reshape(B, S // MOBA_BLOCK, HEADS * LANES))
        qkv = _gdn_conv(proj, gdn_conv_w[l], B, S)
        parts = _gdn_chunks(qkv, proj, _lane_row(gdn_A_log[l]), _lane_row(gdn_dt_bias[l]), B, S)
        o_c = _gdn_out(_gdn_scan(parts, B, S), proj, jnp.tile(gdn_o_norm[l], 2).reshape(1, LANES), B, S)
        merged = _merge(proj, gate_bias[l], o_a, o_b, o_c, w_branch[l].astype(BF16))
        h, ht = _outproj_ln(merged, w_out[l].astype(BF16), h, ln1_g[l], ln1_b[l])
        qt = _mm(peer_w_q[l].T.astype(BF16), ht, tm=512, tn=512)
        keys = peer_sub_keys[l].reshape(2 * HEADS, N_KEYS, PEER_HALF).astype(BF16)
        s2, e2, c, e1 = _peer_scores(qt, keys)
        h = _peer_experts(ht, peer_u[l].astype(BF16), peer_v[l].T.astype(BF16), s2, e2, c, e1, h, ln2_g[l], ln2_b[l])
    return h.reshape(B, S, D)
```

```python
import functools
import math

import numpy as np
import jax
import jax.numpy as jnp
from jax import lax
from jax.experimental import pallas as pl
from jax.experimental.pallas import tpu as pltpu

F32 = jnp.float32
BF16 = jnp.bfloat16
HIGHEST = lax.Precision.HIGHEST

DEPTH = 2
ROPE_THETA = 500000.0
NEG_INF = -1e30
LN_EPS = 1e-5
RMS_EPS = 1e-6
DEEPNORM_ALPHA = (2 * DEPTH) ** 0.25
HEADS = 8
MLA_NOPE, MLA_ROPE, MLA_V = 64, 32, 64
MLA_Q_RANK, MLA_KV_RANK = 768, 256
MOBA_HEAD_DIM, MOBA_ROT_DIM, MOBA_BLOCK, MOBA_TOPK = 64, 16, 256, 3
GDN_HEAD_DIM, GDN_CONV = 64, 4
BRANCH_WIDTH = 512
N_KEYS, PEER_TOPK, PEER_HALF = 128, 16, 128
N_EXPERTS = N_KEYS * N_KEYS

LANES = 128
SUBLANES = 8
VMEM_LIMIT = 56 * 1024 * 1024

GDN_KCHUNK = 128
GDN_HEADS_PER_STEP = 4
ATT_TQ = 256
PEER_TT = 512
PEER_EB = 1024
PEER_SCORE_TT = 256
PEER_SCORE_HEADS_PER_TRIP = 2

C_GATE, C_CQ, C_CKV, C_MQ, C_MK, C_MV, C_GQKV, C_GZ, C_SMALL, C_TOTAL = (
    0, 3072, 3840, 4096, 5120, 6144, 6656, 8192, 8704, 8960)
PROJ_TN = 1792


def _cp(*sem):
    return pltpu.CompilerParams(dimension_semantics=sem, vmem_limit_bytes=VMEM_LIMIT)


def _dot(a, b):
    return jnp.dot(a.astype(BF16), b.astype(BF16), preferred_element_type=F32)


def _dot_nt(a, b):
    return lax.dot_general(a.astype(BF16), b.astype(BF16), (((1,), (1,)), ((), ())),
                           preferred_element_type=F32)


def _split(a):
    hi = a.astype(BF16)
    return hi, (a - hi.astype(F32)).astype(BF16)


def _dot3(a, b):
    ah, al = _split(a)
    bh, bl = _split(b)
    d = functools.partial(jnp.dot, preferred_element_type=F32)
    return d(ah, bh) + (d(al, bh) + d(ah, bl))


def _ln_rows(y, g, b):
    mu = jnp.mean(y, axis=-1, keepdims=True)
    d = y - mu
    var = jnp.mean(d * d, axis=-1, keepdims=True)
    return d * lax.rsqrt(var + LN_EPS) * g + b


def _sigmoid(x):
    return 1.0 / (1.0 + jnp.exp(-x))


def _ln_kernel(x_ref, g_ref, b_ref, o_ref):
    o_ref[...] = _ln_rows(x_ref[...], g_ref[...], b_ref[...])


def _layer_norm(x, g, b, tm=512):
    T, D = x.shape
    return pl.pallas_call(
        _ln_kernel, grid=(T // tm,),
        in_specs=[pl.BlockSpec((tm, D), lambda i: (i, 0)),
                  pl.BlockSpec((1, D), lambda i: (0, 0)),
                  pl.BlockSpec((1, D), lambda i: (0, 0))],
        out_specs=pl.BlockSpec((tm, D), lambda i: (i, 0)),
        out_shape=jax.ShapeDtypeStruct((T, D), F32),
        compiler_params=_cp("parallel"))(x, g.reshape(1, D), b.reshape(1, D))


def _mm_kernel(x_ref, w_ref, o_ref):
    o_ref[...] = _dot(x_ref[...], w_ref[...]).astype(o_ref.dtype)


def _mm(x, w, tm, tn, out_dtype=F32):
    M, K = x.shape
    N = w.shape[1]
    return pl.pallas_call(
        _mm_kernel, grid=(M // tm, N // tn),
        in_specs=[pl.BlockSpec((tm, K), lambda i, j: (i, 0)),
                  pl.BlockSpec((K, tn), lambda i, j: (0, j))],
        out_specs=pl.BlockSpec((tm, tn), lambda i, j: (i, j)),
        out_shape=jax.ShapeDtypeStruct((M, N), out_dtype),
        compiler_params=_cp("parallel", "parallel"))(x, w)


def _rmsmm_kernel(x_ref, g_ref, w_ref, o_ref):
    x = x_ref[...]
    xn = x * lax.rsqrt(jnp.mean(x * x, axis=-1, keepdims=True) + RMS_EPS) * g_ref[...]
    o_ref[...] = _dot(xn, w_ref[...])


def _rmsnorm_mm(proj, col_block, g, w, tm, tn):
    T = proj.shape[0]
    K, N = w.shape
    return pl.pallas_call(
        _rmsmm_kernel, grid=(T // tm, N // tn),
        in_specs=[pl.BlockSpec((tm, K), lambda i, j: (i, col_block)),
                  pl.BlockSpec((1, K), lambda i, j: (0, 0)),
                  pl.BlockSpec((K, tn), lambda i, j: (0, j))],
        out_specs=pl.BlockSpec((tm, tn), lambda i, j: (i, j)),
        out_shape=jax.ShapeDtypeStruct((T, N), F32),
        compiler_params=_cp("parallel", "parallel"))(proj, g.reshape(1, K), w)


def _head_slabs(x):
    return [x[:, h * LANES:(h + 1) * LANES] for h in range(HEADS)]


def _mla_prep_kernel(q_ref, k_ref, sm_ref, cq_ref, sq_ref, ck_ref, sk_ref, qo_ref, ko_ref):
    cq, sq = cq_ref[...], sq_ref[...]
    sm = sm_ref[...]
    kr = sm * ck_ref[...] + pltpu.roll(sm, LANES - MLA_ROPE, 1) * sk_ref[...]
    q, k = q_ref[...], k_ref[...]
    for h, (qs, ks) in enumerate(zip(_head_slabs(q), _head_slabs(k))):
        sl = slice(h * LANES, (h + 1) * LANES)
        qo_ref[:, sl] = (qs * cq + pltpu.roll(qs, LANES - MLA_ROPE, 1) * sq).astype(BF16)
        ko_ref[:, sl] = (ks + kr).astype(BF16)


def _mla_prep(q_raw, kv_raw, proj, tabs, tm=512):
    T = q_raw.shape[0]
    W = HEADS * LANES
    row = lambda i: (i, 0)
    tab = pl.BlockSpec((tm, LANES), row)
    return pl.pallas_call(
        _mla_prep_kernel, grid=(T // tm,),
        in_specs=[pl.BlockSpec((tm, W), row), pl.BlockSpec((tm, W), row),
                  pl.BlockSpec((tm, LANES), lambda i: (i, C_SMALL // LANES)), tab, tab, tab, tab],
        out_specs=[pl.BlockSpec((tm, W), row), pl.BlockSpec((tm, W), row)],
        out_shape=[jax.ShapeDtypeStruct((T, W), BF16)] * 2,
        compiler_params=_cp("parallel"))(q_raw, kv_raw, proj, *tabs)


def _moba_prep_kernel(q_ref, k_ref, cq_ref, sq_ref, ck_ref, sk_ref, qo_ref, ko_ref, km_ref):
    cq, sq, ck, sk = cq_ref[...], sq_ref[...], ck_ref[...], sk_ref[...]
    q, k = q_ref[...], k_ref[...]
    for h, (qs, ks) in enumerate(zip(_head_slabs(q), _head_slabs(k))):
        sl = slice(h * LANES, (h + 1) * LANES)
        qo_ref[:, sl] = qs * cq + pltpu.roll(qs, LANES // 2, 1) * sq
        kk = ks * ck + pltpu.roll(ks, LANES // 2, 1) * sk
        ko_ref[:, sl] = kk.astype(BF16)
        km_ref[0, :, sl] = jnp.mean(kk, axis=0, keepdims=True)


def _moba_prep(proj, tabs):
    T = proj.shape[0]
    W = HEADS * LANES
    tm = MOBA_BLOCK
    row = lambda i: (i, 0)
    tab = pl.BlockSpec((tm, LANES), row)
    return pl.pallas_call(
        _moba_prep_kernel, grid=(T // tm,),
        in_specs=[pl.BlockSpec((tm, W), lambda i: (i, C_MQ // W)),
                  pl.BlockSpec((tm, W), lambda i: (i, C_MK // W)), tab, tab, tab, tab],
        out_specs=[pl.BlockSpec((tm, W), row), pl.BlockSpec((tm, W), row),
                   pl.BlockSpec((1, 1, W), lambda i: (i, 0, 0))],
        out_shape=[jax.ShapeDtypeStruct((T, W), F32), jax.ShapeDtypeStruct((T, W), BF16),
                   jax.ShapeDtypeStruct((T // tm, 1, W), F32)],
        compiler_params=_cp("parallel"))(proj, proj, *tabs)


def _attn_kernel(*refs, moba, nk):
    if moba:
        q_ref, k_ref, v_ref, km_ref, o_ref, vt_sc, qb_sc, m_sc, l_sc, al_sc, acc_sc, s_sc, p_sc, bias_sc = refs
    else:
        q_ref, k_ref, v_ref, o_ref, vt_sc, m_sc, l_sc, al_sc, acc_sc, s_sc, p_sc = refs
    tq = tk = ATT_TQ
    dv = BRANCH_WIDTH // HEADS
    qi = pl.program_id(1)
    hslab = [slice(h * LANES, (h + 1) * LANES) for h in range(HEADS)]

    @pl.when(qi == 0)
    def _():
        for j in range(nk):
            for c in range(tk // LANES):
                for g in range(BRANCH_WIDTH // LANES):
                    vt_sc[j, g * LANES:(g + 1) * LANES, c * LANES:(c + 1) * LANES] = (
                        v_ref[j * tk + c * LANES:j * tk + (c + 1) * LANES, g * LANES:(g + 1) * LANES].T.astype(BF16))

    if moba:
        blk = lax.broadcasted_iota(jnp.int32, (SUBLANES, tq), 0)
        for h in range(HEADS):
            qf = q_ref[:, hslab[h]]
            qb_sc[:, hslab[h]] = qf.astype(BF16)
            km = km_ref[0, :, hslab[h]]
            if nk < SUBLANES:
                km = jnp.concatenate([km, jnp.zeros((SUBLANES - nk, LANES), F32)], axis=0)
            gate = lax.dot_general(km, qf, (((1,), (1,)), ((), ())), precision=HIGHEST,
                                   preferred_element_type=F32)
            gate = jnp.where(blk < qi, gate, -jnp.inf)
            for n in range(nk):
                gn = gate[n:n + 1, :]
                beats = jnp.where(gate > gn, 1.0, jnp.where((gate == gn) & (blk < n), 1.0, 0.0))
                cnt = jnp.sum(beats, axis=0, keepdims=True)
                bias_sc[h, n] = jnp.where((cnt < MOBA_TOPK) & (n < qi), 0.0, NEG_INF)
    qsrc = qb_sc if moba else q_ref
    m_sc[...] = jnp.full(m_sc.shape, NEG_INF, F32)
    l_sc[...] = jnp.zeros(l_sc.shape, F32)
    acc_sc[...] = jnp.zeros(acc_sc.shape, F32)

    def block(j, diag):
        rows = pl.ds(pl.multiple_of(j * tk, tk), tk)
        for h in range(HEADS):
            s = _dot_nt(k_ref[rows, hslab[h]], qsrc[:, hslab[h]])
            if diag:
                keyi = lax.broadcasted_iota(jnp.int32, (tk, tq), 0)
                qryi = lax.broadcasted_iota(jnp.int32, (tk, tq), 1)
                s = jnp.where(keyi <= qryi, s, NEG_INF)
            elif moba:
                s = s + bias_sc[h, j]
            s_sc[h] = s
        for h in range(HEADS):
            s = s_sc[h]
            m_prev = m_sc[h]
            m_new = jnp.maximum(m_prev, jnp.max(s, axis=0, keepdims=True))
            alpha = jnp.exp(m_prev - m_new)
            p = jnp.exp(s - m_new)
            l_sc[h] = alpha * l_sc[h] + jnp.sum(p, axis=0, keepdims=True)
            p_sc[h] = p.astype(BF16)
            al_sc[h] = alpha
            m_sc[h] = m_new
        for h in range(HEADS):
            acc_sc[h] = al_sc[h] * acc_sc[h] + jnp.dot(vt_sc[j, h * dv:(h + 1) * dv, :], p_sc[h],
                                                       preferred_element_type=F32)

    def past(j, carry):
        block(j, False)
        return carry
    lax.fori_loop(0, qi, past, 0)
    block(qi, True)
    ot = jnp.concatenate([acc_sc[h] * (1.0 / l_sc[h]) for h in range(HEADS)], axis=0)
    o_ref[...] = ot.T


def _attention(q, k, v, v_col, B, S, kmean=None):
    T = B * S
    tq = ATT_TQ
    nq = S // tq
    W = HEADS * LANES
    dv = BRANCH_WIDTH // HEADS
    moba = kmean is not None
    assert nq <= SUBLANES and v_col % BRANCH_WIDTH == 0
    in_specs = [pl.BlockSpec((tq, W), lambda b, i: (b * nq + i, 0)),
                pl.BlockSpec((S, W), lambda b, i: (b, 0)),
                pl.BlockSpec((S, BRANCH_WIDTH), lambda b, i: (b, v_col // BRANCH_WIDTH))]
    scratch = [pltpu.VMEM((nq, BRANCH_WIDTH, tq), BF16)]
    if moba:
        scratch.append(pltpu.VMEM((tq, W), BF16))
    row = pltpu.VMEM((HEADS, 1, tq), F32)
    scratch += [row, row, row, pltpu.VMEM((HEADS, dv, tq), F32),
                pltpu.VMEM((HEADS, tq, tq), F32), pltpu.VMEM((HEADS, tq, tq), BF16)]
    args = [q, k, v]
    if moba:
        in_specs.append(pl.BlockSpec((1, nq, W), lambda b, i: (b, 0, 0)))
        scratch.append(pltpu.VMEM((HEADS, nq, 1, tq), F32))
        args.append(kmean)
    return pl.pallas_call(
        functools.partial(_attn_kernel, moba=moba, nk=nq),
        grid=(B, nq), in_specs=in_specs,
        out_specs=pl.BlockSpec((tq, BRANCH_WIDTH), lambda b, i: (b * nq + i, 0)),
        out_shape=jax.ShapeDtypeStruct((T, BRANCH_WIDTH), F32),
        scratch_shapes=scratch,
        compiler_params=_cp("parallel", "arbitrary"))(*args)


def _gdn_conv_kernel(x_ref, w_ref, o_ref):
    c = pl.program_id(1)
    x = x_ref[...]
    w = w_ref[...]
    S = x.shape[0]
    row = lax.broadcasted_iota(jnp.int32, (S, LANES), 0)
    lane = lax.broadcasted_iota(jnp.int32, (S, LANES), 1)
    y = x * w[GDN_CONV - 1:GDN_CONV, :]
    for d in range(1, GDN_CONV):
        xs = jnp.where(row >= d, pltpu.roll(x, d, 0), 0.0)
        y = y + xs * w[GDN_CONV - 1 - d:GDN_CONV - d, :]
    y = y * _sigmoid(y)
    sq = y * y
    lo = lane < GDN_HEAD_DIM
    ss0 = jnp.sum(jnp.where(lo, sq, 0.0), axis=1, keepdims=True)
    ss1 = jnp.sum(jnp.where(lo, 0.0, sq), axis=1, keepdims=True)
    inv = lax.rsqrt(jnp.where(lo, ss0, ss1) + RMS_EPS)
    nqb = BRANCH_WIDTH // LANES
    scale = jnp.where(c < nqb, GDN_HEAD_DIM ** -0.5, 1.0)
    o_ref[...] = jnp.where(c < 2 * nqb, y * inv * scale, y)


def _gdn_conv(proj, conv_w, B, S):
    T = B * S
    nb = 3 * BRANCH_WIDTH // LANES
    return pl.pallas_call(
        _gdn_conv_kernel, grid=(B, nb),
        in_specs=[pl.BlockSpec((S, LANES), lambda b, c: (b, C_GQKV // LANES + c)),
                  pl.BlockSpec((GDN_CONV, LANES), lambda b, c: (0, c))],
        out_specs=pl.BlockSpec((S, LANES), lambda b, c: (b, c)),
        out_shape=jax.ShapeDtypeStruct((T, 3 * BRANCH_WIDTH), F32),
        compiler_params=_cp("parallel", "parallel"))(proj, conv_w)


def _gdn_chunk_kernel(q_ref, k_ref, v_ref, sm_ref, alog_ref, dtb_ref,
                      u_ref, w_ref, qg_ref, qk_ref, kdt_ref, dl_ref):
    C = GDN_KCHUNK
    nh = GDN_HEADS_PER_STEP
    h0 = pl.program_id(1) * nh
    lane = lax.broadcasted_iota(jnp.int32, (C, LANES), 1)
    row = lax.broadcasted_iota(jnp.int32, (C, LANES), 0)
    lo = lane < GDN_HEAD_DIM
    tril = row >= lane
    strict = row > lane
    eye = jnp.where(row == lane, 1.0, 0.0)
    ltri = tril.astype(BF16)

    def pick(ref, k):
        x = ref[:, (k // 2) * LANES:(k // 2 + 1) * LANES]
        return jnp.where(lo, pltpu.roll(x, LANES // 2, 1) if k % 2 else x, 0.0)

    sm = sm_ref[...]
    a = sm + dtb_ref[...]
    softplus = jnp.maximum(a, 0.0) + jnp.log1p(jnp.exp(-jnp.abs(a)))
    garr = -jnp.exp(alog_ref[...]) * softplus
    sig = _sigmoid(sm)
    kh, kb, gc, decay, A = [], [], [], [], []
    for k in range(nh):
        g_col = jnp.sum(jnp.where(lane == h0 + k, garr, 0.0), axis=1, keepdims=True)
        beta = jnp.sum(jnp.where(lane == HEADS + h0 + k, sig, 0.0), axis=1, keepdims=True)
        g1 = jnp.broadcast_to(g_col, (C, LANES))
        gsum = None
        for _ in range(3):
            gb = g1.astype(BF16)
            part = jnp.dot(ltri, gb, preferred_element_type=F32)
            gsum = part if gsum is None else gsum + part
            g1 = g1 - gb.astype(F32)
        gc.append(gsum)
        decay.append(jnp.where(tril, jnp.exp(jnp.where(tril, gsum - gsum.T, 0.0)), 0.0))
        kh.append(pick(k_ref, k))
        kb.append(kh[k] * beta)
        vb = pick(v_ref, k) * beta
        A.append(jnp.where(strict, _dot_nt(kb[k], kh[k]) * decay[k], 0.0))
        eg = jnp.exp(gsum)
        qh = pick(q_ref, k)
        qk_ref[k, 0] = jnp.where(tril, _dot_nt(qh, kh[k]) * decay[k], 0.0)
        qg_ref[k, 0] = qh * eg
        glast = gsum[C - 1:C, :]
        kdt_ref[k, 0] = (kh[k] * jnp.exp(glast - gsum)).T
        dl_ref[k, 0] = jnp.broadcast_to(jnp.exp(glast), (SUBLANES, LANES))
        u_ref[k, 0] = vb
        w_ref[k, 0] = kb[k] * eg
    P = [eye - A[k] for k in range(nh)]
    X = A
    for _ in range(int(math.log2(C)) - 1):
        X = [_dot3(X[k], X[k]) for k in range(nh)]
        P = [P[k] + _dot3(P[k], X[k]) for k in range(nh)]
    for k in range(nh):
        u_ref[k, 0] = _dot3(P[k], u_ref[k, 0])
        w_ref[k, 0] = _dot3(P[k], w_ref[k, 0])


def _gdn_chunks(qkv, proj, alog_row, dtb_row, B, S):
    C = GDN_KCHUNK
    n = S // C
    nh = GDN_HEADS_PER_STEP
    ng = HEADS // nh
    wq = nh // 2 * LANES
    nqb = BRANCH_WIDTH // wq
    big = jax.ShapeDtypeStruct((B * HEADS, n, C, LANES), F32)
    ospec = pl.BlockSpec((nh, 1, C, LANES), lambda b, g, c: (b * ng + g, c, 0, 0))
    par = pl.BlockSpec((1, LANES), lambda b, g, c: (0, 0))
    outs = pl.pallas_call(
        _gdn_chunk_kernel, grid=(B, ng, n),
        in_specs=[pl.BlockSpec((C, wq), lambda b, g, c: (b * n + c, g)),
                  pl.BlockSpec((C, wq), lambda b, g, c: (b * n + c, nqb + g)),
                  pl.BlockSpec((C, wq), lambda b, g, c: (b * n + c, 2 * nqb + g)),
                  pl.BlockSpec((C, LANES), lambda b, g, c: (b * n + c, C_SMALL // LANES)), par, par],
        out_specs=[ospec] * 5 + [pl.BlockSpec((nh, 1, SUBLANES, LANES), lambda b, g, c: (b * ng + g, c, 0, 0))],
        out_shape=[big] * 5 + [jax.ShapeDtypeStruct((B * HEADS, n, SUBLANES, LANES), F32)],
        compiler_params=_cp("parallel", "parallel", "parallel"))(qkv, qkv, qkv, proj, alog_row, dtb_row)
    return [o.reshape(-1, LANES) for o in outs]


def _gdn_scan_kernel(u_ref, w_ref, qg_ref, qk_ref, kdt_ref, dl_ref, o_ref, *, n):
    C = GDN_KCHUNK

    def step(c, state):
        r = pl.multiple_of(c * C, C)
        sb = state.astype(BF16)
        v_new = u_ref[pl.ds(r, C), :] - _dot(w_ref[pl.ds(r, C), :], sb)
        vb = v_new.astype(BF16)
        o_ref[pl.ds(r, C), :] = _dot(qg_ref[pl.ds(r, C), :], sb) + _dot(qk_ref[pl.ds(r, C), :], vb)
        dl = dl_ref[pl.ds(pl.multiple_of(c * SUBLANES, SUBLANES), 1), :]
        return state * dl + _dot(kdt_ref[pl.ds(r, C), :], vb)

    lax.fori_loop(0, n, step, jnp.zeros((LANES, LANES), F32))


def _gdn_scan(parts, B, S):
    n = S // GDN_KCHUNK
    seq = pl.BlockSpec((S, LANES), lambda i: (i, 0))
    return pl.pallas_call(
        functools.partial(_gdn_scan_kernel, n=n), grid=(B * HEADS,),
        in_specs=[seq] * 5 + [pl.BlockSpec((n * SUBLANES, LANES), lambda i: (i, 0))],
        out_specs=seq,
        out_shape=jax.ShapeDtypeStruct((B * HEADS * S, LANES), F32),
        compiler_params=_cp("parallel"))(*parts)


def _gdn_out_kernel(o0_ref, o1_ref, z_ref, g_ref, o_ref):
    lane = lax.broadcasted_iota(jnp.int32, o_ref.shape, 1)

    def nrm(o):
        ms = jnp.sum(o * o, axis=1, keepdims=True) * (1.0 / GDN_HEAD_DIM)
        return o * lax.rsqrt(ms + RMS_EPS)

    nn = jnp.where(lane < GDN_HEAD_DIM, nrm(o0_ref[...]), pltpu.roll(nrm(o1_ref[...]), LANES // 2, 1))
    z = z_ref[...]
    o_ref[...] = nn * g_ref[...] * (z * _sigmoid(z))


def _gdn_out(o, proj, g_row, B, S, tm=512):
    ns = S // tm
    return pl.pallas_call(
        _gdn_out_kernel, grid=(B, ns, HEADS // 2),
        in_specs=[pl.BlockSpec((tm, LANES), lambda b, s, p: ((b * HEADS + 2 * p) * ns + s, 0)),
                  pl.BlockSpec((tm, LANES), lambda b, s, p: ((b * HEADS + 2 * p + 1) * ns + s, 0)),
                  pl.BlockSpec((tm, LANES), lambda b, s, p: (b * ns + s, C_GZ // LANES + p)),
                  pl.BlockSpec((1, LANES), lambda b, s, p: (0, 0))],
        out_specs=pl.BlockSpec((tm, LANES), lambda b, s, p: (b * ns + s, p)),
        out_shape=jax.ShapeDtypeStruct((B * S, BRANCH_WIDTH), F32),
        compiler_params=_cp("parallel", "parallel", "parallel"))(o, o, proj, g_row)


def _merge_kernel(g0_ref, g1_ref, g2_ref, gb_ref, a_ref, b_ref, c_ref, wb_ref, o_ref):
    acc = None
    for n, (gl, br) in enumerate(((g0_ref, a_ref), (g1_ref, b_ref), (g2_ref, c_ref))):
        y = _sigmoid(gl[...] + gb_ref[n:n + 1, :]) * _dot(br[...], wb_ref[n])
        acc = y if acc is None else acc + y
    o_ref[...] = acc.astype(BF16)


def _merge(proj, gate_bias, o_a, o_b, o_c, w_branch, tm=512, tn=512):
    T = o_a.shape[0]
    D = w_branch.shape[2]
    nj = D // tn
    gspec = lambda n: pl.BlockSpec((tm, tn), lambda i, j: (i, n * nj + j))
    bspec = pl.BlockSpec((tm, BRANCH_WIDTH), lambda i, j: (i, 0))
    return pl.pallas_call(
        _merge_kernel, grid=(T // tm, nj),
        in_specs=[gspec(0), gspec(1), gspec(2), pl.BlockSpec((3, tn), lambda i, j: (0, j)),
                  bspec, bspec, bspec, pl.BlockSpec((3, BRANCH_WIDTH, tn), lambda i, j: (0, 0, j))],
        out_specs=pl.BlockSpec((tm, tn), lambda i, j: (i, j)),
        out_shape=jax.ShapeDtypeStruct((T, D), BF16),
        compiler_params=_cp("parallel", "parallel"))(proj, proj, proj, gate_bias, o_a, o_b, o_c, w_branch)


def _outproj_kernel(m_ref, w_ref, h_ref, g_ref, b_ref, o_ref, ot_ref):
    y = DEEPNORM_ALPHA * h_ref[...] + _dot(m_ref[...], w_ref[...])
    y = _ln_rows(y, g_ref[...], b_ref[...])
    o_ref[...] = y
    ot_ref[...] = y.T.astype(BF16)


def _outproj_ln(merged, w_out, h, g, b, tm=256):
    T, D = h.shape
    return pl.pallas_call(
        _outproj_kernel, grid=(T // tm,),
        in_specs=[pl.BlockSpec((tm, D), lambda i: (i, 0)), pl.BlockSpec((D, D), lambda i: (0, 0)),
                  pl.BlockSpec((tm, D), lambda i: (i, 0)),
                  pl.BlockSpec((1, D), lambda i: (0, 0)), pl.BlockSpec((1, D), lambda i: (0, 0))],
        out_specs=[pl.BlockSpec((tm, D), lambda i: (i, 0)), pl.BlockSpec((D, tm), lambda i: (0, i))],
        out_shape=[jax.ShapeDtypeStruct((T, D), F32), jax.ShapeDtypeStruct((D, T), BF16)],
        compiler_params=_cp("parallel"))(merged, w_out, h, g.reshape(1, D), b.reshape(1, D))


def _extract_top(curs, n, on_max):
    curs = list(curs)
    for k in range(n):
        for i, cur in enumerate(curs):
            m = jnp.max(cur, axis=0, keepdims=True)
            on_max(i, k, m)
            if k + 1 < n:
                curs[i] = jnp.where(cur >= m, -jnp.inf, cur)


def _peer_score_kernel(qt_ref, keys_ref, s2_ref, e2_ref, c_ref, e1_ref, top_sc, cand_sc):
    K = PEER_TOPK
    tt = qt_ref.shape[1]
    nh = PEER_SCORE_HEADS_PER_TRIP

    def heads(hp, carry):
        s = []
        for i in range(2 * nh):
            r = pl.multiple_of((hp * nh * 2 + i) * PEER_HALF, PEER_HALF)
            s.append(_dot(keys_ref[hp * nh * 2 + i], qt_ref[pl.ds(r, PEER_HALF), :]))

        def put(i, k, m):
            top_sc[i, k:k + 1, :] = m
        _extract_top(s, K + 1, put)
        r8 = lax.broadcasted_iota(jnp.int32, (SUBLANES, tt), 0)
        v1max, v2max = [], []
        for d in range(nh):
            v1a, v1x = top_sc[2 * d, 0:K, :], top_sc[2 * d, K:K + 1, :]
            v2a, v2x = top_sc[2 * d + 1, 0:K, :], top_sc[2 * d + 1, K:K + 1, :]
            v1max.append(v1a[0:1])
            v2max.append(v2a[0:1])
            cand_sc[d, 0:K, :] = v1a + v2a[0:1]
            for b in range(1, SUBLANES):
                cand_sc[d, K + SUBLANES * (b - 1):K + SUBLANES * b, :] = v1a[0:SUBLANES] + v2a[b:b + 1]
            base = K + SUBLANES * (SUBLANES - 1)
            cand_sc[d, base:base + SUBLANES, :] = v1a[0:1] + v2a[SUBLANES:K]
            cand_sc[d, base + SUBLANES:base + 2 * SUBLANES, :] = jnp.where(
                r8 == 0, v1x + v2a[0:1], jnp.where(r8 == 1, v1a[0:1] + v2x, -jnp.inf))
        st = [{"z": jnp.zeros((1, tt), F32)} for _ in range(nh)]

        def acc(d, k, m):
            if k == 0:
                st[d]["top"] = m
            if k < K:
                st[d]["z"] = st[d]["z"] + jnp.exp(m - st[d]["top"])
            if k == K - 1:
                st[d]["t16"] = m
            if k == K:
                st[d]["t17"] = m
        _extract_top([cand_sc[d] for d in range(nh)], K + 1, acc)
        for d in range(nh):
            tau = 0.5 * (st[d]["t16"] + st[d]["t17"])
            ro = pl.ds(pl.multiple_of((hp * nh + d) * N_KEYS, N_KEYS), N_KEYS)
            s2_ref[ro, :] = s[2 * d + 1]
            e2_ref[ro, :] = jnp.exp(s[2 * d + 1] - v2max[d]) / st[d]["z"]
            c_ref[ro, :] = tau - s[2 * d]
            e1_ref[ro, :] = jnp.exp(s[2 * d] - v1max[d])
        return carry

    lax.fori_loop(0, HEADS // nh, heads, 0)


def _peer_scores(qt, keys):
    T = qt.shape[1]
    tt = PEER_SCORE_TT
    R = HEADS * N_KEYS
    ncand = PEER_TOPK + SUBLANES * (SUBLANES + 1)
    ospec = pl.BlockSpec((R, tt), lambda i: (0, i))
    return pl.pallas_call(
        _peer_score_kernel, grid=(T // tt,),
        in_specs=[pl.BlockSpec((2 * R, tt), lambda i: (0, i)),
                  pl.BlockSpec((2 * HEADS, N_KEYS, PEER_HALF), lambda i: (0, 0, 0))],
        out_specs=[ospec] * 4,
        out_shape=[jax.ShapeDtypeStruct((R, T), F32)] * 4,
        scratch_shapes=[pltpu.VMEM((2 * PEER_SCORE_HEADS_PER_TRIP, 3 * SUBLANES, tt), F32),
                        pltpu.VMEM((PEER_SCORE_HEADS_PER_TRIP, ncand, tt), F32)],
        compiler_params=_cp("parallel"))(qt, keys)


def _peer_expert_kernel(ht_ref, u_ref, vtp_ref, vtl_ref, s2_ref, e2_ref, c_ref, e1_ref, h_ref, g_ref, b_ref,
                        o_ref, acc_sc, act_sc, hw_sc, bc_sc):
    j = pl.program_id(1)
    _, eb, tt = hw_sc.shape
    ng = eb // N_KEYS
    assert ng == SUBLANES
    slot = j % 2

    @pl.when(j == 0)
    def _():
        acc_sc[...] = jnp.zeros(acc_sc.shape, F32)
        hw_sc[1] = jnp.zeros(hw_sc.shape[1:], BF16)

    mxu_w = 2 * LANES
    for half in range(tt // mxu_w):
        hl = slice(half * mxu_w, (half + 1) * mxu_w)
        a = jnp.dot(u_ref[...], ht_ref[:, hl], preferred_element_type=F32)
        act_sc[:, hl] = 0.5 * a * (1.0 + lax.erf(a * (2.0 ** -0.5)))
        acc_sc[:, hl] += jnp.dot(vtp_ref[...], hw_sc[1 - slot, :, hl], preferred_element_type=F32)
        for lc in range(half * (mxu_w // LANES), (half + 1) * (mxu_w // LANES)):
            ls = slice(lc * LANES, (lc + 1) * LANES)
            for h in range(HEADS):
                r8 = pl.ds(pl.multiple_of(h * N_KEYS + j * ng, ng), ng)
                thr8, e18 = c_ref[r8, ls], e1_ref[r8, ls]
                for g in range(ng):
                    bc_sc[0, h * ng + g] = jnp.broadcast_to(thr8[g:g + 1], (SUBLANES, LANES))
                    bc_sc[1, h * ng + g] = jnp.broadcast_to(e18[g:g + 1], (SUBLANES, LANES))
            for g in range(ng):
                wsum = jnp.zeros((N_KEYS // SUBLANES, SUBLANES, LANES), F32)
                for h in range(HEADS):
                    hr = slice(h * N_KEYS, (h + 1) * N_KEYS)
                    s2 = s2_ref[hr, ls].reshape(N_KEYS // SUBLANES, SUBLANES, LANES)
                    e2 = e2_ref[hr, ls].reshape(N_KEYS // SUBLANES, SUBLANES, LANES)
                    wsum = wsum + jnp.where(s2 >= bc_sc[0, h * ng + g][None], e2 * bc_sc[1, h * ng + g][None], 0.0)
                gs = slice(g * N_KEYS, (g + 1) * N_KEYS)
                hw_sc[slot, gs, ls] = (wsum.reshape(N_KEYS, LANES) * act_sc[gs, ls]).astype(BF16)

    @pl.when(j == pl.num_programs(1) - 1)
    def _():
        acc = acc_sc[...] + jnp.dot(vtl_ref[...], hw_sc[slot], preferred_element_type=F32)
        y = DEEPNORM_ALPHA * h_ref[...] + acc.T
        o_ref[...] = _ln_rows(y, g_ref[...], b_ref[...])


def _peer_experts(ht, u, vt, s2, e2, c, e1, h, g, b):
    T, D = h.shape
    tt, eb = PEER_TT, PEER_EB
    R = HEADS * N_KEYS
    nb = N_EXPERTS // eb
    assert nb % 2 == 0
    sspec = pl.BlockSpec((R, tt), lambda i, j: (0, i))
    vec = pl.BlockSpec((1, D), lambda i, j: (0, 0))
    return pl.pallas_call(
        _peer_expert_kernel, grid=(T // tt, nb),
        in_specs=[pl.BlockSpec((D, tt), lambda i, j: (0, i)),
                  pl.BlockSpec((eb, D), lambda i, j: (j, 0)),
                  pl.BlockSpec((D, eb), lambda i, j: (0, jnp.maximum(j - 1, 0))),
                  pl.BlockSpec((D, eb), lambda i, j: (0, nb - 1)),
                  sspec, sspec, sspec, sspec,
                  pl.BlockSpec((tt, D), lambda i, j: (i, 0)), vec, vec],
        out_specs=pl.BlockSpec((tt, D), lambda i, j: (i, 0)),
        out_shape=jax.ShapeDtypeStruct((T, D), F32),
        scratch_shapes=[pltpu.VMEM((D, tt), F32), pltpu.VMEM((eb, tt), F32), pltpu.VMEM((2, eb, tt), BF16),
                        pltpu.VMEM((2, HEADS * SUBLANES, SUBLANES, LANES), F32)],
        compiler_params=_cp("parallel", "arbitrary"))(ht, u, vt, vt, s2, e2, c, e1, h, g.reshape(1, D),
                                                      b.reshape(1, D))


def _rot_half(x, r):
    return jnp.concatenate([-x[..., r:2 * r], x[..., :r]], axis=-1)


def _prep_w_in(w):
    D = w.shape[0]
    splits = (MLA_Q_RANK, MLA_KV_RANK, MLA_ROPE, 3 * BRANCH_WIDTH, 3 * BRANCH_WIDTH, BRANCH_WIDTH, HEADS, HEADS,
              3 * D)
    o = np.cumsum((0,) + splits)
    cq, ckv, kr, mqkv, gqkv, gz, ga, gb, gl = [w[:, o[i]:o[i + 1]] for i in range(9)]
    mq, mk, mv = jnp.split(mqkv, 3, axis=1)

    def slab(m):
        m = m.reshape(D, HEADS, MOBA_HEAD_DIM)
        pad = jnp.zeros((D, HEADS, LANES - MOBA_HEAD_DIM - MOBA_ROT_DIM), w.dtype)
        return jnp.concatenate([m, _rot_half(m, MOBA_ROT_DIM // 2), pad], axis=-1).reshape(D, HEADS * LANES)

    small = jnp.concatenate([ga, gb, jnp.zeros((D, LANES // 2 - 2 * HEADS), w.dtype), kr,
                             _rot_half(kr, MLA_ROPE // 2)], axis=1)
    out = jnp.concatenate([gl, cq, ckv, slab(mq), slab(mk), mv, gqkv, gz, small, jnp.zeros((D, LANES), w.dtype)],
                          axis=1)
    assert out.shape[1] == C_TOTAL and C_TOTAL % PROJ_TN == 0
    return out.astype(BF16)


def _prep_w_uq(w):
    R = w.shape[0]
    w = w.reshape(R, HEADS, MLA_NOPE + MLA_ROPE)
    rope = w[..., MLA_NOPE:]
    return jnp.concatenate([w, _rot_half(rope, MLA_ROPE // 2)], axis=-1).reshape(R, HEADS * LANES).astype(BF16)


def _prep_w_ukv(w):
    R = w.shape[0]
    w = w.reshape(R, HEADS, MLA_NOPE + MLA_V)
    k = jnp.concatenate([w[..., :MLA_NOPE], jnp.zeros((R, HEADS, LANES - MLA_NOPE), w.dtype)], axis=-1)
    return jnp.concatenate([k.reshape(R, HEADS * LANES), w[..., MLA_NOPE:].reshape(R, HEADS * MLA_V)],
                           axis=1).astype(BF16)


def _rope_tables(positions):
    pos = positions.reshape(-1).astype(F32)[:, None]
    T = pos.shape[0]

    def cs(rot):
        inv = ROPE_THETA ** (-jnp.arange(0, rot, 2, dtype=F32) / rot)
        ang = pos * inv
        return jnp.cos(ang), jnp.sin(ang)

    ca, sa = cs(MLA_ROPE)
    cb, sb = cs(MOBA_ROT_DIM)
    one = lambda n: jnp.ones((T, n), F32)
    zero = lambda n: jnp.zeros((T, n), F32)
    sc_a = (MLA_NOPE + MLA_ROPE) ** -0.5
    sc_b = MOBA_HEAD_DIM ** -0.5
    cat = lambda *xs: jnp.concatenate(xs, axis=1)
    mla = (cat(one(MLA_NOPE), ca, ca, zero(32)) * sc_a, cat(zero(MLA_NOPE), sa, sa, zero(32)) * sc_a,
           cat(zero(MLA_NOPE), ca, ca, zero(32)), cat(zero(MLA_NOPE), sa, sa, zero(32)))
    cm = cat(cb, cb, one(MOBA_HEAD_DIM - MOBA_ROT_DIM), zero(LANES - MOBA_HEAD_DIM))
    sm = cat(sb, sb, zero(LANES - MOBA_ROT_DIM))
    moba = (cm * sc_b, sm * sc_b, cm, sm)
    return mla, moba


def _lane_row(v):
    return jnp.concatenate([v.astype(F32), jnp.zeros((LANES - v.shape[0],), F32)]).reshape(1, LANES)


def kernel(x, positions, ln_in_g, ln_in_b, w_in, mla_q_norm, mla_kv_norm, mla_w_uq, mla_w_ukv, gdn_conv_w, gdn_A_log, gdn_dt_bias, gdn_o_norm, gate_bias, w_branch, w_out, ln1_g, ln1_b, peer_w_q, peer_sub_keys, peer_u, peer_v, ln2_g, ln2_b):
    B, S, D = x.shape
    T = B * S
    assert S % ATT_TQ == 0 and S % GDN_KCHUNK == 0 and T % PEER_TT == 0 and ATT_TQ == MOBA_BLOCK
    mla_tabs, moba_tabs = _rope_tables(positions)
    h = _layer_norm(x.reshape(T, D), ln_in_g, ln_in_b)
    for l in range(DEPTH):
        proj = _mm(h, _prep_w_in(w_in[l]), tm=512, tn=PROJ_TN)
        q_raw = _rmsnorm_mm(proj, C_CQ // MLA_Q_RANK, mla_q_norm[l], _prep_w_uq(mla_w_uq[l]), 512, 512)
        kv_raw = _rmsnorm_mm(proj, C_CKV // MLA_KV_RANK, mla_kv_norm[l], _prep_w_ukv(mla_w_ukv[l]), 512, 512)
        qa, ka = _mla_prep(q_raw, kv_raw, proj, mla_tabs)
        o_a = _attention(qa, ka, kv_raw, HEADS * LANES, B, S)
        qm, km, kmean = _moba_prep(proj, moba_tabs)
        o_b = _attention(qm, km, proj, C_MV, B, S, kmean=kmean.reshape(B, S // MOBA_BLOCK, HEADS * LANES))
        qkv = _gdn_conv(proj, gdn_conv_w[l], B, S)
        parts = _gdn_chunks(qkv, proj, _lane_row(gdn_A_log[l]), _lane_row(gdn_dt_bias[l]), B, S)
        o_c = _gdn_out(_gdn_scan(parts, B, S), proj, jnp.tile(gdn_o_norm[l], 2).reshape(1, LANES), B, S)
        merged = _merge(proj, gate_bias[l], o_a, o_b, o_c, w_branch[l].astype(BF16))
        h, ht = _outproj_ln(merged, w_out[l].astype(BF16), h, ln1_g[l], ln1_b[l])
        qt = _mm(peer_w_q[l].T.astype(BF16), ht, tm=512, tn=512)
        keys = peer_sub_keys[l].reshape(2 * HEADS, N_KEYS, PEER_HALF).astype(BF16)
        s2, e2, c, e1 = _peer_scores(qt, keys)
        h = _peer_experts(ht, peer_u[l].astype(BF16), peer_v[l].T.astype(BF16), s2, e2, c, e1, h, ln2_g[l], ln2_b[l])
    return h.reshape(B, S, D)
```

```python
import functools
import math

import numpy as np
import jax
import jax.numpy as jnp
from jax import lax
from jax.experimental import pallas as pl
from jax.experimental.pallas import tpu as pltpu

F32 = jnp.float32
BF16 = jnp.bfloat16
HIGHEST = lax.Precision.HIGHEST

DEPTH = 2
ROPE_THETA = 500000.0
NEG_INF = -1e30
LN_EPS = 1e-5
RMS_EPS = 1e-6
DEEPNORM_ALPHA = (2 * DEPTH) ** 0.25
HEADS = 8
MLA_NOPE, MLA_ROPE, MLA_V = 64, 32, 64
MLA_Q_RANK, MLA_KV_RANK = 768, 256
MOBA_HEAD_DIM, MOBA_ROT_DIM, MOBA_BLOCK, MOBA_TOPK = 64, 16, 256, 3
GDN_HEAD_DIM, GDN_CONV = 64, 4
BRANCH_WIDTH = 512
N_KEYS, PEER_TOPK, PEER_HALF = 128, 16, 128
N_EXPERTS = N_KEYS * N_KEYS

LANES = 128
SUBLANES = 8
VMEM_LIMIT = 56 * 1024 * 1024

GDN_KCHUNK = 128
GDN_HEADS_PER_STEP = 4
GDN_SCAN_HEADS = 2
ATT_TQ = 256
PEER_TT = 512
PEER_EB = 1024
PEER_SCORE_TT = 256
PEER_SCORE_HEADS_PER_TRIP = 2

C_GATE, C_CQ, C_CKV, C_MQ, C_MK, C_MV, C_GQKV, C_GZ, C_TOTAL = (
    0, 3072, 3840, 4096, 5120, 6144, 6656, 8192, 8704)
PROJ_TN = 4352


def _cp(*sem):
    return pltpu.CompilerParams(dimension_semantics=sem, vmem_limit_bytes=VMEM_LIMIT)


def _dot(a, b):
    return jnp.dot(a.astype(BF16), b.astype(BF16), preferred_element_type=F32)


def _dot_nt(a, b):
    return lax.dot_general(a.astype(BF16), b.astype(BF16), (((1,), (1,)), ((), ())),
                           preferred_element_type=F32)


def _split(a):
    hi = a.astype(BF16)
    return hi, (a - hi.astype(F32)).astype(BF16)


def _dot3(a, b):
    ah, al = _split(a)
    bh, bl = _split(b)
    d = functools.partial(jnp.dot, preferred_element_type=F32)
    return d(ah, bh) + (d(al, bh) + d(ah, bl))


def _ln_rows(y, g, b):
    mu = jnp.mean(y, axis=-1, keepdims=True)
    d = y - mu
    var = jnp.mean(d * d, axis=-1, keepdims=True)
    return d * lax.rsqrt(var + LN_EPS) * g + b


def _sigmoid(x):
    return 1.0 / (1.0 + jnp.exp(-x))


def _ln_kernel(x_ref, g_ref, b_ref, o_ref):
    o_ref[...] = _ln_rows(x_ref[...], g_ref[...], b_ref[...])


def _layer_norm(x, g, b, tm=512):
    T, D = x.shape
    return pl.pallas_call(
        _ln_kernel, grid=(T // tm,),
        in_specs=[pl.BlockSpec((tm, D), lambda i: (i, 0)),
                  pl.BlockSpec((1, D), lambda i: (0, 0)),
                  pl.BlockSpec((1, D), lambda i: (0, 0))],
        out_specs=pl.BlockSpec((tm, D), lambda i: (i, 0)),
        out_shape=jax.ShapeDtypeStruct((T, D), F32),
        compiler_params=_cp("parallel"))(x, g.reshape(1, D), b.reshape(1, D))


def _mm_kernel(x_ref, w_ref, o_ref):
    o_ref[...] = _dot(x_ref[...], w_ref[...]).astype(o_ref.dtype)


def _mm(x, w, tm, tn, out_dtype=F32):
    M, K = x.shape
    N = w.shape[1]
    return pl.pallas_call(
        _mm_kernel, grid=(M // tm, N // tn),
        in_specs=[pl.BlockSpec((tm, K), lambda i, j: (i, 0)),
                  pl.BlockSpec((K, tn), lambda i, j: (0, j))],
        out_specs=pl.BlockSpec((tm, tn), lambda i, j: (i, j)),
        out_shape=jax.ShapeDtypeStruct((M, N), out_dtype),
        compiler_params=_cp("parallel", "parallel"))(x, w)


def _rmsmm_kernel(x_ref, g_ref, w_ref, o_ref):
    x = x_ref[...].astype(F32)
    xn = x * lax.rsqrt(jnp.mean(x * x, axis=-1, keepdims=True) + RMS_EPS) * g_ref[...]
    o_ref[...] = _dot(xn, w_ref[...])


def _rmsnorm_mm(proj, col_block, g, w, tm, tn):
    T = proj.shape[0]
    K, N = w.shape
    return pl.pallas_call(
        _rmsmm_kernel, grid=(T // tm, N // tn),
        in_specs=[pl.BlockSpec((tm, K), lambda i, j: (i, col_block)),
                  pl.BlockSpec((1, K), lambda i, j: (0, 0)),
                  pl.BlockSpec((K, tn), lambda i, j: (0, j))],
        out_specs=pl.BlockSpec((tm, tn), lambda i, j: (i, j)),
        out_shape=jax.ShapeDtypeStruct((T, N), F32),
        compiler_params=_cp("parallel", "parallel"))(proj, g.reshape(1, K), w)


def _head_slabs(x):
    return [x[:, h * LANES:(h + 1) * LANES] for h in range(HEADS)]


def _mla_prep_kernel(q_ref, k_ref, sm_ref, cq_ref, sq_ref, ck_ref, sk_ref, qo_ref, ko_ref):
    cq, sq = cq_ref[...], sq_ref[...]
    sm = sm_ref[...]
    kr = sm * ck_ref[...] + pltpu.roll(sm, LANES - MLA_ROPE, 1) * sk_ref[...]
    q, k = q_ref[...], k_ref[...]
    for h, (qs, ks) in enumerate(zip(_head_slabs(q), _head_slabs(k))):
        sl = slice(h * LANES, (h + 1) * LANES)
        qo_ref[:, sl] = (qs * cq + pltpu.roll(qs, LANES - MLA_ROPE, 1) * sq).astype(BF16)
        ko_ref[:, sl] = (ks + kr).astype(BF16)


def _mla_prep(q_raw, kv_raw, small, tabs, tm=512):
    T = q_raw.shape[0]
    W = HEADS * LANES
    row = lambda i: (i, 0)
    tab = pl.BlockSpec((tm, LANES), row)
    return pl.pallas_call(
        _mla_prep_kernel, grid=(T // tm,),
        in_specs=[pl.BlockSpec((tm, W), row), pl.BlockSpec((tm, W), row),
                  tab, tab, tab, tab, tab],
        out_specs=[pl.BlockSpec((tm, W), row), pl.BlockSpec((tm, W), row)],
        out_shape=[jax.ShapeDtypeStruct((T, W), BF16)] * 2,
        compiler_params=_cp("parallel"))(q_raw, kv_raw, small, *tabs)


def _moba_prep_kernel(q_ref, k_ref, cq_ref, sq_ref, ck_ref, sk_ref, qo_ref, ko_ref, km_ref):
    cq, sq, ck, sk = cq_ref[...], sq_ref[...], ck_ref[...], sk_ref[...]
    q, k = q_ref[...].astype(F32), k_ref[...].astype(F32)
    for h, (qs, ks) in enumerate(zip(_head_slabs(q), _head_slabs(k))):
        sl = slice(h * LANES, (h + 1) * LANES)
        qo_ref[:, sl] = qs * cq + pltpu.roll(qs, LANES // 2, 1) * sq
        kk = ks * ck + pltpu.roll(ks, LANES // 2, 1) * sk
        ko_ref[:, sl] = kk.astype(BF16)
        km_ref[0, :, sl] = jnp.mean(kk, axis=0, keepdims=True)


def _moba_prep(proj, tabs):
    T = proj.shape[0]
    W = HEADS * LANES
    tm = MOBA_BLOCK
    row = lambda i: (i, 0)
    tab = pl.BlockSpec((tm, LANES), row)
    return pl.pallas_call(
        _moba_prep_kernel, grid=(T // tm,),
        in_specs=[pl.BlockSpec((tm, W), lambda i: (i, C_MQ // W)),
                  pl.BlockSpec((tm, W), lambda i: (i, C_MK // W)), tab, tab, tab, tab],
        out_specs=[pl.BlockSpec((tm, W), row), pl.BlockSpec((tm, W), row),
                   pl.BlockSpec((1, 1, W), lambda i: (i, 0, 0))],
        out_shape=[jax.ShapeDtypeStruct((T, W), F32), jax.ShapeDtypeStruct((T, W), BF16),
                   jax.ShapeDtypeStruct((T // tm, 1, W), F32)],
        compiler_params=_cp("parallel"))(proj, proj, *tabs)


def _attn_kernel(*refs, moba, nk):
    if moba:
        q_ref, k_ref, v_ref, km_ref, o_ref, vt_sc, qb_sc, m_sc, l_sc, al_sc, acc_sc, s_sc, p_sc, bias_sc = refs
    else:
        q_ref, k_ref, v_ref, o_ref, vt_sc, m_sc, l_sc, al_sc, acc_sc, s_sc, p_sc = refs
    tq = tk = ATT_TQ
    dv = BRANCH_WIDTH // HEADS
    qi = pl.program_id(1)
    hslab = [slice(h * LANES, (h + 1) * LANES) for h in range(HEADS)]

    @pl.when(qi == 0)
    def _():
        for j in range(nk):
            for c in range(tk // LANES):
                for g in range(BRANCH_WIDTH // LANES):
                    vt_sc[j, g * LANES:(g + 1) * LANES, c * LANES:(c + 1) * LANES] = (
                        v_ref[j * tk + c * LANES:j * tk + (c + 1) * LANES,
                              g * LANES:(g + 1) * LANES].astype(F32).T.astype(BF16))

    if moba:
        blk = lax.broadcasted_iota(jnp.int32, (SUBLANES, tq), 0)
        for h in range(HEADS):
            qf = q_ref[:, hslab[h]]
            qb_sc[:, hslab[h]] = qf.astype(BF16)
            km = km_ref[0, :, hslab[h]]
            if nk < SUBLANES:
                km = jnp.concatenate([km, jnp.zeros((SUBLANES - nk, LANES), F32)], axis=0)
            gate = lax.dot_general(km, qf, (((1,), (1,)), ((), ())), precision=HIGHEST,
                                   preferred_element_type=F32)
            gate = jnp.where(blk < qi, gate, -jnp.inf)
            for n in range(nk):
                gn = gate[n:n + 1, :]
                beats = jnp.where(gate > gn, 1.0, jnp.where((gate == gn) & (blk < n), 1.0, 0.0))
                cnt = jnp.sum(beats, axis=0, keepdims=True)
                bias_sc[h, n] = jnp.where((cnt < MOBA_TOPK) & (n < qi), 0.0, NEG_INF)
    qsrc = qb_sc if moba else q_ref
    m_sc[...] = jnp.full(m_sc.shape, NEG_INF, F32)
    l_sc[...] = jnp.zeros(l_sc.shape, F32)
    acc_sc[...] = jnp.zeros(acc_sc.shape, F32)

    def block(j, diag):
        rows = pl.ds(pl.multiple_of(j * tk, tk), tk)
        for h in range(HEADS):
            s = _dot_nt(k_ref[rows, hslab[h]], qsrc[:, hslab[h]])
            if diag:
                keyi = lax.broadcasted_iota(jnp.int32, (tk, tq), 0)
                qryi = lax.broadcasted_iota(jnp.int32, (tk, tq), 1)
                s = jnp.where(keyi <= qryi, s, NEG_INF)
            elif moba:
                s = s + bias_sc[h, j]
            s_sc[h] = s
        for h in range(HEADS):
            for lh in range(tq // LANES):
                ls = slice(lh * LANES, (lh + 1) * LANES)
                s = s_sc[h, :, ls]
                m_prev = m_sc[h, :, ls]
                m_new = jnp.maximum(m_prev, jnp.max(s, axis=0, keepdims=True))
                alpha = jnp.exp(m_prev - m_new)
                p = jnp.exp(s - m_new)
                l_sc[h, :, ls] = alpha * l_sc[h, :, ls] + jnp.sum(p, axis=0, keepdims=True)
                p_sc[h, :, ls] = p.astype(BF16)
                al_sc[h, :, ls] = alpha
                m_sc[h, :, ls] = m_new
        for h in range(HEADS):
            acc_sc[h] = al_sc[h] * acc_sc[h] + jnp.dot(vt_sc[j, h * dv:(h + 1) * dv, :], p_sc[h],
                                                       preferred_element_type=F32)

    def past(j, carry):
        block(j, False)
        return carry
    lax.fori_loop(0, qi, past, 0)
    block(qi, True)
    ot = jnp.concatenate([acc_sc[h] * (1.0 / l_sc[h]) for h in range(HEADS)], axis=0)
    o_ref[...] = ot.T


def _attention(q, k, v, v_col, B, S, kmean=None):
    T = B * S
    tq = ATT_TQ
    nq = S // tq
    W = HEADS * LANES
    dv = BRANCH_WIDTH // HEADS
    moba = kmean is not None
    assert nq <= SUBLANES and v_col % BRANCH_WIDTH == 0
    in_specs = [pl.BlockSpec((tq, W), lambda b, i: (b * nq + i, 0)),
                pl.BlockSpec((S, W), lambda b, i: (b, 0)),
                pl.BlockSpec((S, BRANCH_WIDTH), lambda b, i: (b, v_col // BRANCH_WIDTH))]
    scratch = [pltpu.VMEM((nq, BRANCH_WIDTH, tq), BF16)]
    if moba:
        scratch.append(pltpu.VMEM((tq, W), BF16))
    row = pltpu.VMEM((HEADS, 1, tq), F32)
    scratch += [row, row, row, pltpu.VMEM((HEADS, dv, tq), F32),
                pltpu.VMEM((HEADS, tq, tq), F32), pltpu.VMEM((HEADS, tq, tq), BF16)]
    args = [q, k, v]
    if moba:
        in_specs.append(pl.BlockSpec((1, nq, W), lambda b, i: (b, 0, 0)))
        scratch.append(pltpu.VMEM((HEADS, nq, 1, tq), F32))
        args.append(kmean)
    return pl.pallas_call(
        functools.partial(_attn_kernel, moba=moba, nk=nq),
        grid=(B, nq), in_specs=in_specs,
        out_specs=pl.BlockSpec((tq, BRANCH_WIDTH), lambda b, i: (b * nq + i, 0)),
        out_shape=jax.ShapeDtypeStruct((T, BRANCH_WIDTH), F32),
        scratch_shapes=scratch,
        compiler_params=_cp("parallel", "arbitrary"))(*args)


def _gdn_conv_kernel(x_ref, w_ref, o_ref):
    c = pl.program_id(1)
    x = x_ref[...].astype(F32)
    w = w_ref[...]
    S = x.shape[0]
    row = lax.broadcasted_iota(jnp.int32, (S, LANES), 0)
    lane = lax.broadcasted_iota(jnp.int32, (S, LANES), 1)
    y = x * w[GDN_CONV - 1:GDN_CONV, :]
    for d in range(1, GDN_CONV):
        xs = jnp.where(row >= d, pltpu.roll(x, d, 0), 0.0)
        y = y + xs * w[GDN_CONV - 1 - d:GDN_CONV - d, :]
    y = y * _sigmoid(y)
    sq = y * y
    lo = lane < GDN_HEAD_DIM
    ss0 = jnp.sum(jnp.where(lo, sq, 0.0), axis=1, keepdims=True)
    ss1 = jnp.sum(jnp.where(lo, 0.0, sq), axis=1, keepdims=True)
    inv = lax.rsqrt(jnp.where(lo, ss0, ss1) + RMS_EPS)
    nqb = BRANCH_WIDTH // LANES
    scale = jnp.where(c < nqb, GDN_HEAD_DIM ** -0.5, 1.0)
    o_ref[...] = jnp.where(c < 2 * nqb, y * inv * scale, y)


def _gdn_conv(proj, conv_w, B, S):
    T = B * S
    nb = 3 * BRANCH_WIDTH // LANES
    return pl.pallas_call(
        _gdn_conv_kernel, grid=(B, nb),
        in_specs=[pl.BlockSpec((S, LANES), lambda b, c: (b, C_GQKV // LANES + c)),
                  pl.BlockSpec((GDN_CONV, LANES), lambda b, c: (0, c))],
        out_specs=pl.BlockSpec((S, LANES), lambda b, c: (b, c)),
        out_shape=jax.ShapeDtypeStruct((T, 3 * BRANCH_WIDTH), F32),
        compiler_params=_cp("parallel", "parallel"))(proj, conv_w)


def _gdn_chunk_kernel(q_ref, k_ref, v_ref, sm_ref, alog_ref, dtb_ref,
                      u_ref, w_ref, qg_ref, qk_ref, kdt_ref, dl_ref):
    C = GDN_KCHUNK
    nh = GDN_HEADS_PER_STEP
    h0 = pl.program_id(1) * nh
    lane = lax.broadcasted_iota(jnp.int32, (C, LANES), 1)
    row = lax.broadcasted_iota(jnp.int32, (C, LANES), 0)
    lo = lane < GDN_HEAD_DIM
    tril = row >= lane
    strict = row > lane
    eye = jnp.where(row == lane, 1.0, 0.0)
    ltri = tril.astype(BF16)

    def pick(ref, k):
        x = ref[:, (k // 2) * LANES:(k // 2 + 1) * LANES]
        return jnp.where(lo, pltpu.roll(x, LANES // 2, 1) if k % 2 else x, 0.0)

    sm = sm_ref[...]
    a = sm + dtb_ref[...]
    softplus = jnp.maximum(a, 0.0) + jnp.log1p(jnp.exp(-jnp.abs(a)))
    garr = -jnp.exp(alog_ref[...]) * softplus
    sig = _sigmoid(sm)
    kh, kb, gc, decay, A = [], [], [], [], []
    for k in range(nh):
        g_col = jnp.sum(jnp.where(lane == h0 + k, garr, 0.0), axis=1, keepdims=True)
        beta = jnp.sum(jnp.where(lane == HEADS + h0 + k, sig, 0.0), axis=1, keepdims=True)
        g1 = jnp.broadcast_to(g_col, (C, LANES))
        gsum = None
        for _ in range(3):
            gb = g1.astype(BF16)
            part = jnp.dot(ltri, gb, preferred_element_type=F32)
            gsum = part if gsum is None else gsum + part
            g1 = g1 - gb.astype(F32)
        gc.append(gsum)
        decay.append(jnp.where(tril, jnp.exp(jnp.where(tril, gsum - gsum.T, 0.0)), 0.0))
        kh.append(pick(k_ref, k))
        kb.append(kh[k] * beta)
        vb = pick(v_ref, k) * beta
        A.append(jnp.where(strict, _dot_nt(kb[k], kh[k]) * decay[k], 0.0))
        eg = jnp.exp(gsum)
        qh = pick(q_ref, k)
        qk_ref[k, 0] = jnp.where(tril, _dot_nt(qh, kh[k]) * decay[k], 0.0)
        qg_ref[k, 0] = qh * eg
        glast = gsum[C - 1:C, :]
        kdt_ref[k, 0] = (kh[k] * jnp.exp(glast - gsum)).T
        dl_ref[k, 0] = jnp.broadcast_to(jnp.exp(glast), (SUBLANES, LANES))
        u_ref[k, 0] = vb
        w_ref[k, 0] = kb[k] * eg
    P = [eye - A[k] for k in range(nh)]
    X = A
    for _ in range(int(math.log2(C)) - 1):
        X = [_dot3(X[k], X[k]) for k in range(nh)]
        P = [P[k] + _dot3(P[k], X[k]) for k in range(nh)]
    for k in range(nh):
        u_ref[k, 0] = _dot3(P[k], u_ref[k, 0])
        w_ref[k, 0] = _dot3(P[k], w_ref[k, 0])


def _gdn_chunks(qkv, small, alog_row, dtb_row, B, S):
    C = GDN_KCHUNK
    n = S // C
    nh = GDN_HEADS_PER_STEP
    ng = HEADS // nh
    wq = nh // 2 * LANES
    nqb = BRANCH_WIDTH // wq
    big = jax.ShapeDtypeStruct((B * HEADS, n, C, LANES), F32)
    ospec = pl.BlockSpec((nh, 1, C, LANES), lambda b, g, c: (b * ng + g, c, 0, 0))
    par = pl.BlockSpec((1, LANES), lambda b, g, c: (0, 0))
    outs = pl.pallas_call(
        _gdn_chunk_kernel, grid=(B, ng, n),
        in_specs=[pl.BlockSpec((C, wq), lambda b, g, c: (b * n + c, g)),
                  pl.BlockSpec((C, wq), lambda b, g, c: (b * n + c, nqb + g)),
                  pl.BlockSpec((C, wq), lambda b, g, c: (b * n + c, 2 * nqb + g)),
                  pl.BlockSpec((C, LANES), lambda b, g, c: (b * n + c, 0)), par, par],
        out_specs=[ospec] * 5 + [pl.BlockSpec((nh, 1, SUBLANES, LANES), lambda b, g, c: (b * ng + g, c, 0, 0))],
        out_shape=[big] * 5 + [jax.ShapeDtypeStruct((B * HEADS, n, SUBLANES, LANES), F32)],
        compiler_params=_cp("parallel", "parallel", "parallel"))(qkv, qkv, qkv, small, alog_row, dtb_row)
    return [o.reshape(-1, LANES) for o in outs]


def _gdn_scan_kernel(u_ref, w_ref, qg_ref, qk_ref, kdt_ref, dl_ref, o_ref, *, n):
    C = GDN_KCHUNK
    nh = GDN_SCAN_HEADS

    def step(c, states):
        out = []
        for k, state in enumerate(states):
            r = pl.ds(pl.multiple_of(k * n * C + c * C, C), C)
            sb = state.astype(BF16)
            v_new = u_ref[r, :] - _dot(w_ref[r, :], sb)
            vb = v_new.astype(BF16)
            o_ref[r, :] = _dot(qg_ref[r, :], sb) + _dot(qk_ref[r, :], vb)
            dl = dl_ref[pl.ds(pl.multiple_of((k * n + c) * SUBLANES, SUBLANES), 1), :]
            out.append(state * dl + _dot(kdt_ref[r, :], vb))
        return tuple(out)

    lax.fori_loop(0, n, step, tuple(jnp.zeros((LANES, LANES), F32) for _ in range(nh)))


def _gdn_scan(parts, B, S):
    n = S // GDN_KCHUNK
    nh = GDN_SCAN_HEADS
    seq = pl.BlockSpec((nh * S, LANES), lambda i: (i, 0))
    return pl.pallas_call(
        functools.partial(_gdn_scan_kernel, n=n), grid=(B * HEADS // nh,),
        in_specs=[seq] * 5 + [pl.BlockSpec((nh * n * SUBLANES, LANES), lambda i: (i, 0))],
        out_specs=seq,
        out_shape=jax.ShapeDtypeStruct((B * HEADS * S, LANES), F32),
        compiler_params=_cp("parallel"))(*parts)


def _gdn_out_kernel(o0_ref, o1_ref, z_ref, g_ref, o_ref):
    lane = lax.broadcasted_iota(jnp.int32, o_ref.shape, 1)

    def nrm(o):
        ms = jnp.sum(o * o, axis=1, keepdims=True) * (1.0 / GDN_HEAD_DIM)
        return o * lax.rsqrt(ms + RMS_EPS)

    nn = jnp.where(lane < GDN_HEAD_DIM, nrm(o0_ref[...]), pltpu.roll(nrm(o1_ref[...]), LANES // 2, 1))
    z = z_ref[...].astype(F32)
    o_ref[...] = nn * g_ref[...] * (z * _sigmoid(z))


def _gdn_out(o, proj, g_row, B, S, tm=512):
    ns = S // tm
    return pl.pallas_call(
        _gdn_out_kernel, grid=(B, ns, HEADS // 2),
        in_specs=[pl.BlockSpec((tm, LANES), lambda b, s, p: ((b * HEADS + 2 * p) * ns + s, 0)),
                  pl.BlockSpec((tm, LANES), lambda b, s, p: ((b * HEADS + 2 * p + 1) * ns + s, 0)),
                  pl.BlockSpec((tm, LANES), lambda b, s, p: (b * ns + s, C_GZ // LANES + p)),
                  pl.BlockSpec((1, LANES), lambda b, s, p: (0, 0))],
        out_specs=pl.BlockSpec((tm, LANES), lambda b, s, p: (b * ns + s, p)),
        out_shape=jax.ShapeDtypeStruct((B * S, BRANCH_WIDTH), F32),
        compiler_params=_cp("parallel", "parallel", "parallel"))(o, o, proj, g_row)


def _merge_kernel(g0_ref, g1_ref, g2_ref, gb_ref, a_ref, b_ref, c_ref, wb_ref, o_ref):
    acc = None
    for n, (gl, br) in enumerate(((g0_ref, a_ref), (g1_ref, b_ref), (g2_ref, c_ref))):
        y = _sigmoid(gl[...].astype(F32) + gb_ref[n:n + 1, :]) * _dot(br[...], wb_ref[n])
        acc = y if acc is None else acc + y
    o_ref[...] = acc.astype(BF16)


def _merge(proj, gate_bias, o_a, o_b, o_c, w_branch, tm=512, tn=512):
    T = o_a.shape[0]
    D = w_branch.shape[2]
    nj = D // tn
    gspec = lambda n: pl.BlockSpec((tm, tn), lambda i, j: (i, n * nj + j))
    bspec = pl.BlockSpec((tm, BRANCH_WIDTH), lambda i, j: (i, 0))
    return pl.pallas_call(
        _merge_kernel, grid=(T // tm, nj),
        in_specs=[gspec(0), gspec(1), gspec(2), pl.BlockSpec((3, tn), lambda i, j: (0, j)),
                  bspec, bspec, bspec, pl.BlockSpec((3, BRANCH_WIDTH, tn), lambda i, j: (0, 0, j))],
        out_specs=pl.BlockSpec((tm, tn), lambda i, j: (i, j)),
        out_shape=jax.ShapeDtypeStruct((T, D), BF16),
        compiler_params=_cp("parallel", "parallel"))(proj, proj, proj, gate_bias, o_a, o_b, o_c, w_branch)


def _outproj_kernel(m_ref, w_ref, h_ref, g_ref, b_ref, o_ref, ot_ref):
    y = DEEPNORM_ALPHA * h_ref[...] + _dot(m_ref[...], w_ref[...])
    y = _ln_rows(y, g_ref[...], b_ref[...])
    o_ref[...] = y
    ot_ref[...] = y.T.astype(BF16)


def _outproj_ln(merged, w_out, h, g, b, tm=256):
    T, D = h.shape
    return pl.pallas_call(
        _outproj_kernel, grid=(T // tm,),
        in_specs=[pl.BlockSpec((tm, D), lambda i: (i, 0)), pl.BlockSpec((D, D), lambda i: (0, 0)),
                  pl.BlockSpec((tm, D), lambda i: (i, 0)),
                  pl.BlockSpec((1, D), lambda i: (0, 0)), pl.BlockSpec((1, D), lambda i: (0, 0))],
        out_specs=[pl.BlockSpec((tm, D), lambda i: (i, 0)), pl.BlockSpec((D, tm), lambda i: (0, i))],
        out_shape=[jax.ShapeDtypeStruct((T, D), F32), jax.ShapeDtypeStruct((D, T), BF16)],
        compiler_params=_cp("parallel"))(merged, w_out, h, g.reshape(1, D), b.reshape(1, D))


def _extract_top(curs, n, on_max):
    curs = list(curs)
    for k in range(n):
        for i, cur in enumerate(curs):
            m = jnp.max(cur, axis=0, keepdims=True)
            on_max(i, k, m)
            if k + 1 < n:
                curs[i] = jnp.where(cur >= m, -jnp.inf, cur)


def _peer_score_kernel(qt_ref, keys_ref, s2_ref, e2_ref, c_ref, e1_ref, top_sc, cand_sc):
    K = PEER_TOPK
    tt = qt_ref.shape[1]
    nh = PEER_SCORE_HEADS_PER_TRIP

    def heads(hp, carry):
        s = []
        for i in range(2 * nh):
            r = pl.multiple_of((hp * nh * 2 + i) * PEER_HALF, PEER_HALF)
            s.append(_dot(keys_ref[hp * nh * 2 + i], qt_ref[pl.ds(r, PEER_HALF), :]))

        def put(i, k, m):
            top_sc[i, k:k + 1, :] = m
        _extract_top(s, K + 1, put)
        r8 = lax.broadcasted_iota(jnp.int32, (SUBLANES, tt), 0)
        v1max, v2max = [], []
        for d in range(nh):
            v1a, v1x = top_sc[2 * d, 0:K, :], top_sc[2 * d, K:K + 1, :]
            v2a, v2x = top_sc[2 * d + 1, 0:K, :], top_sc[2 * d + 1, K:K + 1, :]
            v1max.append(v1a[0:1])
            v2max.append(v2a[0:1])
            cand_sc[d, 0:K, :] = v1a + v2a[0:1]
            for b in range(1, SUBLANES):
                cand_sc[d, K + SUBLANES * (b - 1):K + SUBLANES * b, :] = v1a[0:SUBLANES] + v2a[b:b + 1]
            base = K + SUBLANES * (SUBLANES - 1)
            cand_sc[d, base:base + SUBLANES, :] = v1a[0:1] + v2a[SUBLANES:K]
            cand_sc[d, base + SUBLANES:base + 2 * SUBLANES, :] = jnp.where(
                r8 == 0, v1x + v2a[0:1], jnp.where(r8 == 1, v1a[0:1] + v2x, -jnp.inf))
        st = [{"z": jnp.zeros((1, tt), F32)} for _ in range(nh)]

        def acc(d, k, m):
            if k == 0:
                st[d]["top"] = m
            if k < K:
                st[d]["z"] = st[d]["z"] + jnp.exp(m - st[d]["top"])
            if k == K - 1:
                st[d]["t16"] = m
            if k == K:
                st[d]["t17"] = m
        _extract_top([cand_sc[d] for d in range(nh)], K + 1, acc)
        for d in range(nh):
            tau = 0.5 * (st[d]["t16"] + st[d]["t17"])
            ro = pl.ds(pl.multiple_of((hp * nh + d) * N_KEYS, N_KEYS), N_KEYS)
            s2_ref[ro, :] = s[2 * d + 1]
            e2_ref[ro, :] = jnp.exp(s[2 * d + 1] - v2max[d]) / st[d]["z"]
            c_ref[ro, :] = tau - s[2 * d]
            e1_ref[ro, :] = jnp.exp(s[2 * d] - v1max[d])
        return carry

    lax.fori_loop(0, HEADS // nh, heads, 0)


def _peer_scores(qt, keys):
    T = qt.shape[1]
    tt = PEER_SCORE_TT
    R = HEADS * N_KEYS
    ncand = PEER_TOPK + SUBLANES * (SUBLANES + 1)
    ospec = pl.BlockSpec((R, tt), lambda i: (0, i))
    return pl.pallas_call(
        _peer_score_kernel, grid=(T // tt,),
        in_specs=[pl.BlockSpec((2 * R, tt), lambda i: (0, i)),
                  pl.BlockSpec((2 * HEADS, N_KEYS, PEER_HALF), lambda i: (0, 0, 0))],
        out_specs=[ospec] * 4,
        out_shape=[jax.ShapeDtypeStruct((R, T), F32)] * 4,
        scratch_shapes=[pltpu.VMEM((2 * PEER_SCORE_HEADS_PER_TRIP, 3 * SUBLANES, tt), F32),
                        pltpu.VMEM((PEER_SCORE_HEADS_PER_TRIP, ncand, tt), F32)],
        compiler_params=_cp("parallel"))(qt, keys)


def _peer_expert_kernel(ht_ref, u_ref, vtp_ref, vtl_ref, s2_ref, e2_ref, c_ref, e1_ref, h_ref, g_ref, b_ref,
                        o_ref, acc_sc, act_sc, hw_sc, bc_sc):
    j = pl.program_id(1)
    _, eb, tt = hw_sc.shape
    ng = eb // N_KEYS
    assert ng == SUBLANES
    slot = j % 2

    @pl.when(j == 0)
    def _():
        acc_sc[...] = jnp.zeros(acc_sc.shape, F32)
        hw_sc[1] = jnp.zeros(hw_sc.shape[1:], BF16)

    mxu_w = 2 * LANES
    for half in range(tt // mxu_w):
        hl = slice(half * mxu_w, (half + 1) * mxu_w)
        a = jnp.dot(u_ref[...], ht_ref[:, hl], preferred_element_type=F32)
        act_sc[:, hl] = 0.5 * a * (1.0 + lax.erf(a * (2.0 ** -0.5)))
        acc_sc[:, hl] += jnp.dot(vtp_ref[...], hw_sc[1 - slot, :, hl], preferred_element_type=F32)
        for lc in range(half * (mxu_w // LANES), (half + 1) * (mxu_w // LANES)):
            ls = slice(lc * LANES, (lc + 1) * LANES)
            for h in range(HEADS):
                r8 = pl.ds(pl.multiple_of(h * N_KEYS + j * ng, ng), ng)
                thr8, e18 = c_ref[r8, ls], e1_ref[r8, ls]
                for g in range(ng):
                    bc_sc[0, h * ng + g] = jnp.broadcast_to(thr8[g:g + 1], (SUBLANES, LANES))
                    bc_sc[1, h * ng + g] = jnp.broadcast_to(e18[g:g + 1], (SUBLANES, LANES))
            for g in range(ng):
                wsum = jnp.zeros((N_KEYS // SUBLANES, SUBLANES, LANES), F32)
                for h in range(HEADS):
                    hr = slice(h * N_KEYS, (h + 1) * N_KEYS)
                    s2 = s2_ref[hr, ls].reshape(N_KEYS // SUBLANES, SUBLANES, LANES)
                    e2 = e2_ref[hr, ls].reshape(N_KEYS // SUBLANES, SUBLANES, LANES)
                    wsum = wsum + jnp.where(s2 >= bc_sc[0, h * ng + g][None], e2 * bc_sc[1, h * ng + g][None], 0.0)
                gs = slice(g * N_KEYS, (g + 1) * N_KEYS)
                hw_sc[slot, gs, ls] = (wsum.reshape(N_KEYS, LANES) * act_sc[gs, ls]).astype(BF16)

    @pl.when(j == pl.num_programs(1) - 1)
    def _():
        acc = acc_sc[...] + jnp.dot(vtl_ref[...], hw_sc[slot], preferred_element_type=F32)
        y = DEEPNORM_ALPHA * h_ref[...] + acc.T
        o_ref[...] = _ln_rows(y, g_ref[...], b_ref[...])


def _peer_experts(ht, u, vt, s2, e2, c, e1, h, g, b):
    T, D = h.shape
    tt, eb = PEER_TT, PEER_EB
    R = HEADS * N_KEYS
    nb = N_EXPERTS // eb
    assert nb % 2 == 0
    sspec = pl.BlockSpec((R, tt), lambda i, j: (0, i))
    vec = pl.BlockSpec((1, D), lambda i, j: (0, 0))
    return pl.pallas_call(
        _peer_expert_kernel, grid=(T // tt, nb),
        in_specs=[pl.BlockSpec((D, tt), lambda i, j: (0, i)),
                  pl.BlockSpec((eb, D), lambda i, j: (j, 0)),
                  pl.BlockSpec((D, eb), lambda i, j: (0, jnp.maximum(j - 1, 0))),
                  pl.BlockSpec((D, eb), lambda i, j: (0, nb - 1)),
                  sspec, sspec, sspec, sspec,
                  pl.BlockSpec((tt, D), lambda i, j: (i, 0)), vec, vec],
        out_specs=pl.BlockSpec((tt, D), lambda i, j: (i, 0)),
        out_shape=jax.ShapeDtypeStruct((T, D), F32),
        scratch_shapes=[pltpu.VMEM((D, tt), F32), pltpu.VMEM((eb, tt), F32), pltpu.VMEM((2, eb, tt), BF16),
                        pltpu.VMEM((2, HEADS * SUBLANES, SUBLANES, LANES), F32)],
        compiler_params=_cp("parallel", "arbitrary"))(ht, u, vt, vt, s2, e2, c, e1, h, g.reshape(1, D),
                                                      b.reshape(1, D))


def _rot_half(x, r):
    return jnp.concatenate([-x[..., r:2 * r], x[..., :r]], axis=-1)


def _prep_w_in(w):
    D = w.shape[0]
    splits = (MLA_Q_RANK, MLA_KV_RANK, MLA_ROPE, 3 * BRANCH_WIDTH, 3 * BRANCH_WIDTH, BRANCH_WIDTH, HEADS, HEADS,
              3 * D)
    o = np.cumsum((0,) + splits)
    cq, ckv, kr, mqkv, gqkv, gz, ga, gb, gl = [w[:, o[i]:o[i + 1]] for i in range(9)]
    mq, mk, mv = jnp.split(mqkv, 3, axis=1)

    def slab(m):
        m = m.reshape(D, HEADS, MOBA_HEAD_DIM)
        pad = jnp.zeros((D, HEADS, LANES - MOBA_HEAD_DIM - MOBA_ROT_DIM), w.dtype)
        return jnp.concatenate([m, _rot_half(m, MOBA_ROT_DIM // 2), pad], axis=-1).reshape(D, HEADS * LANES)

    small = jnp.concatenate([ga, gb, jnp.zeros((D, LANES // 2 - 2 * HEADS), w.dtype), kr,
                             _rot_half(kr, MLA_ROPE // 2)], axis=1)
    out = jnp.concatenate([gl, cq, ckv, slab(mq), slab(mk), mv, gqkv, gz], axis=1)
    assert out.shape[1] == C_TOTAL and C_TOTAL % PROJ_TN == 0
    return out.astype(BF16), small.astype(BF16)


def _prep_w_uq(w):
    R = w.shape[0]
    w = w.reshape(R, HEADS, MLA_NOPE + MLA_ROPE)
    rope = w[..., MLA_NOPE:]
    return jnp.concatenate([w, _rot_half(rope, MLA_ROPE // 2)], axis=-1).reshape(R, HEADS * LANES).astype(BF16)


def _prep_w_ukv(w):
    R = w.shape[0]
    w = w.reshape(R, HEADS, MLA_NOPE + MLA_V)
    k = jnp.concatenate([w[..., :MLA_NOPE], jnp.zeros((R, HEADS, LANES - MLA_NOPE), w.dtype)], axis=-1)
    return jnp.concatenate([k.reshape(R, HEADS * LANES), w[..., MLA_NOPE:].reshape(R, HEADS * MLA_V)],
                           axis=1).astype(BF16)


def _rope_tables(positions):
    pos = positions.reshape(-1).astype(F32)[:, None]
    T = pos.shape[0]

    def cs(rot):
        inv = ROPE_THETA ** (-jnp.arange(0, rot, 2, dtype=F32) / rot)
        ang = pos * inv
        return jnp.cos(ang), jnp.sin(ang)

    ca, sa = cs(MLA_ROPE)
    cb, sb = cs(MOBA_ROT_DIM)
    one = lambda n: jnp.ones((T, n), F32)
    zero = lambda n: jnp.zeros((T, n), F32)
    sc_a = (MLA_NOPE + MLA_ROPE) ** -0.5
    sc_b = MOBA_HEAD_DIM ** -0.5
    cat = lambda *xs: jnp.concatenate(xs, axis=1)
    mla = (cat(one(MLA_NOPE), ca, ca, zero(32)) * sc_a, cat(zero(MLA_NOPE), sa, sa, zero(32)) * sc_a,
           cat(zero(MLA_NOPE), ca, ca, zero(32)), cat(zero(MLA_NOPE), sa, sa, zero(32)))
    cm = cat(cb, cb, one(MOBA_HEAD_DIM - MOBA_ROT_DIM), zero(LANES - MOBA_HEAD_DIM))
    sm = cat(sb, sb, zero(LANES - MOBA_ROT_DIM))
    moba = (cm * sc_b, sm * sc_b, cm, sm)
    return mla, moba


def _lane_row(v):
    return jnp.concatenate([v.astype(F32), jnp.zeros((LANES - v.shape[0],), F32)]).reshape(1, LANES)


def kernel(x, positions, ln_in_g, ln_in_b, w_in, mla_q_norm, mla_kv_norm, mla_w_uq, mla_w_ukv, gdn_conv_w, gdn_A_log, gdn_dt_bias, gdn_o_norm, gate_bias, w_branch, w_out, ln1_g, ln1_b, peer_w_q, peer_sub_keys, peer_u, peer_v, ln2_g, ln2_b):
    B, S, D = x.shape
    T = B * S
    assert S % ATT_TQ == 0 and S % GDN_KCHUNK == 0 and T % PEER_TT == 0 and ATT_TQ == MOBA_BLOCK
    mla_tabs, moba_tabs = _rope_tables(positions)
    h = _layer_norm(x.reshape(T, D), ln_in_g, ln_in_b)
    for l in range(DEPTH):
        w_main, w_small = _prep_w_in(w_in[l])
        proj = _mm(h, w_main, tm=512, tn=PROJ_TN, out_dtype=BF16)
        small = _mm(h, w_small, tm=512, tn=LANES)
        q_raw = _rmsnorm_mm(proj, C_CQ // MLA_Q_RANK, mla_q_norm[l], _prep_w_uq(mla_w_uq[l]), 512, 512)
        kv_raw = _rmsnorm_mm(proj, C_CKV // MLA_KV_RANK, mla_kv_norm[l], _prep_w_ukv(mla_w_ukv[l]), 512, 512)
        qa, ka = _mla_prep(q_raw, kv_raw, small, mla_tabs)
        o_a = _attention(qa, ka, kv_raw, HEADS * LANES, B, S)
        qm, km, kmean = _moba_prep(proj, moba_tabs)
        o_b = _attention(qm, km, proj, C_MV, B, S, kmean=kmean.reshape(B, S // MOBA_BLOCK, HEADS * LANES))
        qkv = _gdn_conv(proj, gdn_conv_w[l], B, S)
        parts = _gdn_chunks(qkv, small, _lane_row(gdn_A_log[l]), _lane_row(gdn_dt_bias[l]), B, S)
        o_c = _gdn_out(_gdn_scan(parts, B, S), proj, jnp.tile(gdn_o_norm[l], 2).reshape(1, LANES), B, S)
        merged = _merge(proj, gate_bias[l], o_a, o_b, o_c, w_branch[l].astype(BF16))
        h, ht = _outproj_ln(merged, w_out[l].astype(BF16), h, ln1_g[l], ln1_b[l])
        qt = _mm(peer_w_q[l].T.astype(BF16), ht, tm=512, tn=512)
        keys = peer_sub_keys[l].reshape(2 * HEADS, N_KEYS, PEER_HALF).astype(BF16)
        s2, e2, c, e1 = _peer_scores(qt, keys)
        h = _peer_experts(ht, peer_u[l].astype(BF16), peer_v[l].T.astype(BF16), s2, e2, c, e1, h, ln2_g[l], ln2_b[l])
    return h.reshape(B, S, D)
```

```python
import functools
import math

import numpy as np
import jax
import jax.numpy as jnp
from jax import lax
from jax.experimental import pallas as pl
from jax.experimental.pallas import tpu as pltpu

F32 = jnp.float32
BF16 = jnp.bfloat16
HIGHEST = lax.Precision.HIGHEST

DEPTH = 2
ROPE_THETA = 500000.0
NEG_INF = -1e30
LN_EPS = 1e-5
RMS_EPS = 1e-6
DEEPNORM_ALPHA = (2 * DEPTH) ** 0.25
HEADS = 8
MLA_NOPE, MLA_ROPE, MLA_V = 64, 32, 64
MLA_Q_RANK, MLA_KV_RANK = 768, 256
MOBA_HEAD_DIM, MOBA_ROT_DIM, MOBA_BLOCK, MOBA_TOPK = 64, 16, 256, 3
GDN_HEAD_DIM, GDN_CONV = 64, 4
BRANCH_WIDTH = 512
N_KEYS, PEER_TOPK, PEER_HALF = 128, 16, 128
N_EXPERTS = N_KEYS * N_KEYS

LANES = 128
SUBLANES = 8
VMEM_LIMIT = 56 * 1024 * 1024

GDN_KCHUNK = 128
GDN_HEADS_PER_STEP = 4
ATT_TQ = 256
PEER_TT = 512
PEER_EB = 1024
PEER_SCORE_TT = 256
PEER_SCORE_HEADS_PER_TRIP = 2

C_GATE, C_CQ, C_CKV, C_MQ, C_MK, C_MV, C_GQKV, C_GZ, C_TOTAL = (
    0, 3072, 3840, 4096, 5120, 6144, 6656, 8192, 8704)
PROJ_TN = 4352


def _cp(*sem):
    return pltpu.CompilerParams(dimension_semantics=sem, vmem_limit_bytes=VMEM_LIMIT)


def _dot(a, b):
    return jnp.dot(a.astype(BF16), b.astype(BF16), preferred_element_type=F32)


def _dot_nt(a, b):
    return lax.dot_general(a.astype(BF16), b.astype(BF16), (((1,), (1,)), ((), ())),
                           preferred_element_type=F32)


def _split(a):
    hi = a.astype(BF16)
    return hi, (a - hi.astype(F32)).astype(BF16)


def _dot3(a, b):
    ah, al = _split(a)
    bh, bl = _split(b)
    d = functools.partial(jnp.dot, preferred_element_type=F32)
    return d(ah, bh) + (d(al, bh) + d(ah, bl))


def _ln_rows(y, g, b):
    mu = jnp.mean(y, axis=-1, keepdims=True)
    d = y - mu
    var = jnp.mean(d * d, axis=-1, keepdims=True)
    return d * lax.rsqrt(var + LN_EPS) * g + b


def _sigmoid(x):
    return 1.0 / (1.0 + jnp.exp(-x))


def _ln_kernel(x_ref, g_ref, b_ref, o_ref):
    o_ref[...] = _ln_rows(x_ref[...], g_ref[...], b_ref[...])


def _layer_norm(x, g, b, tm=512):
    T, D = x.shape
    return pl.pallas_call(
        _ln_kernel, grid=(T // tm,),
        in_specs=[pl.BlockSpec((tm, D), lambda i: (i, 0)),
                  pl.BlockSpec((1, D), lambda i: (0, 0)),
                  pl.BlockSpec((1, D), lambda i: (0, 0))],
        out_specs=pl.BlockSpec((tm, D), lambda i: (i, 0)),
        out_shape=jax.ShapeDtypeStruct((T, D), F32),
        compiler_params=_cp("parallel"))(x, g.reshape(1, D), b.reshape(1, D))


def _mm_kernel(x_ref, w_ref, o_ref):
    o_ref[...] = _dot(x_ref[...], w_ref[...]).astype(o_ref.dtype)


def _mm(x, w, tm, tn, out_dtype=F32):
    M, K = x.shape
    N = w.shape[1]
    return pl.pallas_call(
        _mm_kernel, grid=(M // tm, N // tn),
        in_specs=[pl.BlockSpec((tm, K), lambda i, j: (i, 0)),
                  pl.BlockSpec((K, tn), lambda i, j: (0, j))],
        out_specs=pl.BlockSpec((tm, tn), lambda i, j: (i, j)),
        out_shape=jax.ShapeDtypeStruct((M, N), out_dtype),
        compiler_params=_cp("parallel", "parallel"))(x, w)


def _rmsmm_kernel(x_ref, g_ref, w_ref, o_ref):
    x = x_ref[...].astype(F32)
    xn = x * lax.rsqrt(jnp.mean(x * x, axis=-1, keepdims=True) + RMS_EPS) * g_ref[...]
    o_ref[...] = _dot(xn, w_ref[...])


def _rmsnorm_mm(proj, col_block, g, w, tm, tn):
    T = proj.shape[0]
    K, N = w.shape
    return pl.pallas_call(
        _rmsmm_kernel, grid=(T // tm, N // tn),
        in_specs=[pl.BlockSpec((tm, K), lambda i, j: (i, col_block)),
                  pl.BlockSpec((1, K), lambda i, j: (0, 0)),
                  pl.BlockSpec((K, tn), lambda i, j: (0, j))],
        out_specs=pl.BlockSpec((tm, tn), lambda i, j: (i, j)),
        out_shape=jax.ShapeDtypeStruct((T, N), F32),
        compiler_params=_cp("parallel", "parallel"))(proj, g.reshape(1, K), w)


def _head_slabs(x):
    return [x[:, h * LANES:(h + 1) * LANES] for h in range(HEADS)]


def _mla_prep_kernel(q_ref, k_ref, sm_ref, cq_ref, sq_ref, ck_ref, sk_ref, qo_ref, ko_ref):
    cq, sq = cq_ref[...], sq_ref[...]
    sm = sm_ref[...]
    kr = sm * ck_ref[...] + pltpu.roll(sm, LANES - MLA_ROPE, 1) * sk_ref[...]
    q, k = q_ref[...], k_ref[...]
    for h, (qs, ks) in enumerate(zip(_head_slabs(q), _head_slabs(k))):
        sl = slice(h * LANES, (h + 1) * LANES)
        qo_ref[:, sl] = (qs * cq + pltpu.roll(qs, LANES - MLA_ROPE, 1) * sq).astype(BF16)
        ko_ref[:, sl] = (ks + kr).astype(BF16)


def _mla_prep(q_raw, kv_raw, small, tabs, tm=512):
    T = q_raw.shape[0]
    W = HEADS * LANES
    row = lambda i: (i, 0)
    tab = pl.BlockSpec((tm, LANES), row)
    return pl.pallas_call(
        _mla_prep_kernel, grid=(T // tm,),
        in_specs=[pl.BlockSpec((tm, W), row), pl.BlockSpec((tm, W), row),
                  tab, tab, tab, tab, tab],
        out_specs=[pl.BlockSpec((tm, W), row), pl.BlockSpec((tm, W), row)],
        out_shape=[jax.ShapeDtypeStruct((T, W), BF16)] * 2,
        compiler_params=_cp("parallel"))(q_raw, kv_raw, small, *tabs)


def _moba_prep_kernel(q_ref, k_ref, cq_ref, sq_ref, ck_ref, sk_ref, qo_ref, ko_ref, km_ref):
    cq, sq, ck, sk = cq_ref[...], sq_ref[...], ck_ref[...], sk_ref[...]
    q, k = q_ref[...].astype(F32), k_ref[...].astype(F32)
    for h, (qs, ks) in enumerate(zip(_head_slabs(q), _head_slabs(k))):
        sl = slice(h * LANES, (h + 1) * LANES)
        qo_ref[:, sl] = qs * cq + pltpu.roll(qs, LANES // 2, 1) * sq
        kk = ks * ck + pltpu.roll(ks, LANES // 2, 1) * sk
        ko_ref[:, sl] = kk.astype(BF16)
        km_ref[0, :, sl] = jnp.mean(kk, axis=0, keepdims=True)


def _moba_prep(proj, tabs):
    T = proj.shape[0]
    W = HEADS * LANES
    tm = MOBA_BLOCK
    row = lambda i: (i, 0)
    tab = pl.BlockSpec((tm, LANES), row)
    return pl.pallas_call(
        _moba_prep_kernel, grid=(T // tm,),
        in_specs=[pl.BlockSpec((tm, W), lambda i: (i, C_MQ // W)),
                  pl.BlockSpec((tm, W), lambda i: (i, C_MK // W)), tab, tab, tab, tab],
        out_specs=[pl.BlockSpec((tm, W), row), pl.BlockSpec((tm, W), row),
                   pl.BlockSpec((1, 1, W), lambda i: (i, 0, 0))],
        out_shape=[jax.ShapeDtypeStruct((T, W), F32), jax.ShapeDtypeStruct((T, W), BF16),
                   jax.ShapeDtypeStruct((T // tm, 1, W), F32)],
        compiler_params=_cp("parallel"))(proj, proj, *tabs)


def _attn_kernel(*refs, moba, nk):
    if moba:
        q_ref, k_ref, v_ref, km_ref, o_ref, vt_sc, qb_sc, m_sc, l_sc, al_sc, acc_sc, s_sc, p_sc, bias_sc = refs
    else:
        q_ref, k_ref, v_ref, o_ref, vt_sc, m_sc, l_sc, al_sc, acc_sc, s_sc, p_sc = refs
    tq = tk = ATT_TQ
    dv = BRANCH_WIDTH // HEADS
    qi = pl.program_id(1)
    hslab = [slice(h * LANES, (h + 1) * LANES) for h in range(HEADS)]

    @pl.when(qi == 0)
    def _():
        for j in range(nk):
            for c in range(tk // LANES):
                for g in range(BRANCH_WIDTH // LANES):
                    vt_sc[j, g * LANES:(g + 1) * LANES, c * LANES:(c + 1) * LANES] = (
                        v_ref[j * tk + c * LANES:j * tk + (c + 1) * LANES,
                              g * LANES:(g + 1) * LANES].astype(F32).T.astype(BF16))

    if moba:
        blk = lax.broadcasted_iota(jnp.int32, (SUBLANES, tq), 0)
        for h in range(HEADS):
            qf = q_ref[:, hslab[h]]
            qb_sc[:, hslab[h]] = qf.astype(BF16)
            km = km_ref[0, :, hslab[h]]
            if nk < SUBLANES:
                km = jnp.concatenate([km, jnp.zeros((SUBLANES - nk, LANES), F32)], axis=0)
            gate = lax.dot_general(km, qf, (((1,), (1,)), ((), ())), precision=HIGHEST,
                                   preferred_element_type=F32)
            gate = jnp.where(blk < qi, gate, -jnp.inf)
            for n in range(nk):
                gn = gate[n:n + 1, :]
                beats = jnp.where(gate > gn, 1.0, jnp.where((gate == gn) & (blk < n), 1.0, 0.0))
                cnt = jnp.sum(beats, axis=0, keepdims=True)
                bias_sc[h, n] = jnp.where((cnt < MOBA_TOPK) & (n < qi), 0.0, NEG_INF)
    qsrc = qb_sc if moba else q_ref
    m_sc[...] = jnp.full(m_sc.shape, NEG_INF, F32)
    l_sc[...] = jnp.zeros(l_sc.shape, F32)
    acc_sc[...] = jnp.zeros(acc_sc.shape, F32)

    def block(j, diag):
        rows = pl.ds(pl.multiple_of(j * tk, tk), tk)
        for h in range(HEADS):
            s = _dot_nt(k_ref[rows, hslab[h]], qsrc[:, hslab[h]])
            if diag:
                keyi = lax.broadcasted_iota(jnp.int32, (tk, tq), 0)
                qryi = lax.broadcasted_iota(jnp.int32, (tk, tq), 1)
                s = jnp.where(keyi <= qryi, s, NEG_INF)
            elif moba:
                s = s + bias_sc[h, j]
            s_sc[h] = s
        for h in range(HEADS):
            s = s_sc[h]
            m_prev = m_sc[h]
            m_new = jnp.maximum(m_prev, jnp.max(s, axis=0, keepdims=True))
            alpha = jnp.exp2(m_prev - m_new)
            p = jnp.exp2(s - m_new)
            l_sc[h] = alpha * l_sc[h] + jnp.sum(p, axis=0, keepdims=True)
            p_sc[h] = p.astype(BF16)
            al_sc[h] = alpha
            m_sc[h] = m_new
        for h in range(HEADS):
            acc_sc[h] = al_sc[h] * acc_sc[h] + jnp.dot(vt_sc[j, h * dv:(h + 1) * dv, :], p_sc[h],
                                                       preferred_element_type=F32)

    def past(j, carry):
        block(j, False)
        return carry
    lax.fori_loop(0, qi, past, 0)
    block(qi, True)
    ot = jnp.concatenate([acc_sc[h] * (1.0 / l_sc[h]) for h in range(HEADS)], axis=0)
    o_ref[...] = ot.T


def _attention(q, k, v, v_col, B, S, kmean=None):
    T = B * S
    tq = ATT_TQ
    nq = S // tq
    W = HEADS * LANES
    dv = BRANCH_WIDTH // HEADS
    moba = kmean is not None
    assert nq <= SUBLANES and v_col % BRANCH_WIDTH == 0
    in_specs = [pl.BlockSpec((tq, W), lambda b, i: (b * nq + i, 0)),
                pl.BlockSpec((S, W), lambda b, i: (b, 0)),
                pl.BlockSpec((S, BRANCH_WIDTH), lambda b, i: (b, v_col // BRANCH_WIDTH))]
    scratch = [pltpu.VMEM((nq, BRANCH_WIDTH, tq), BF16)]
    if moba:
        scratch.append(pltpu.VMEM((tq, W), BF16))
    row = pltpu.VMEM((HEADS, 1, tq), F32)
    scratch += [row, row, row, pltpu.VMEM((HEADS, dv, tq), F32),
                pltpu.VMEM((HEADS, tq, tq), F32), pltpu.VMEM((HEADS, tq, tq), BF16)]
    args = [q, k, v]
    if moba:
        in_specs.append(pl.BlockSpec((1, nq, W), lambda b, i: (b, 0, 0)))
        scratch.append(pltpu.VMEM((HEADS, nq, 1, tq), F32))
        args.append(kmean)
    return pl.pallas_call(
        functools.partial(_attn_kernel, moba=moba, nk=nq),
        grid=(B, nq), in_specs=in_specs,
        out_specs=pl.BlockSpec((tq, BRANCH_WIDTH), lambda b, i: (b * nq + i, 0)),
        out_shape=jax.ShapeDtypeStruct((T, BRANCH_WIDTH), F32),
        scratch_shapes=scratch,
        compiler_params=_cp("parallel", "arbitrary"))(*args)


def _gdn_conv_kernel(x_ref, w_ref, o_ref):
    c = pl.program_id(1)
    x = x_ref[...].astype(F32)
    w = w_ref[...]
    S = x.shape[0]
    row = lax.broadcasted_iota(jnp.int32, (S, LANES), 0)
    lane = lax.broadcasted_iota(jnp.int32, (S, LANES), 1)
    y = x * w[GDN_CONV - 1:GDN_CONV, :]
    for d in range(1, GDN_CONV):
        xs = jnp.where(row >= d, pltpu.roll(x, d, 0), 0.0)
        y = y + xs * w[GDN_CONV - 1 - d:GDN_CONV - d, :]
    y = y * _sigmoid(y)
    sq = y * y
    lo = lane < GDN_HEAD_DIM
    ss0 = jnp.sum(jnp.where(lo, sq, 0.0), axis=1, keepdims=True)
    ss1 = jnp.sum(jnp.where(lo, 0.0, sq), axis=1, keepdims=True)
    inv = lax.rsqrt(jnp.where(lo, ss0, ss1) + RMS_EPS)
    nqb = BRANCH_WIDTH // LANES
    scale = jnp.where(c < nqb, GDN_HEAD_DIM ** -0.5, 1.0)
    o_ref[...] = jnp.where(c < 2 * nqb, y * inv * scale, y)


def _gdn_conv(proj, conv_w, B, S):
    T = B * S
    nb = 3 * BRANCH_WIDTH // LANES
    return pl.pallas_call(
        _gdn_conv_kernel, grid=(B, nb),
        in_specs=[pl.BlockSpec((S, LANES), lambda b, c: (b, C_GQKV // LANES + c)),
                  pl.BlockSpec((GDN_CONV, LANES), lambda b, c: (0, c))],
        out_specs=pl.BlockSpec((S, LANES), lambda b, c: (b, c)),
        out_shape=jax.ShapeDtypeStruct((T, 3 * BRANCH_WIDTH), F32),
        compiler_params=_cp("parallel", "parallel"))(proj, conv_w)


def _gdn_chunk_kernel(q_ref, k_ref, v_ref, sm_ref, alog_ref, dtb_ref,
                      u_ref, w_ref, qg_ref, qk_ref, kdt_ref, dl_ref):
    C = GDN_KCHUNK
    nh = GDN_HEADS_PER_STEP
    h0 = pl.program_id(1) * nh
    lane = lax.broadcasted_iota(jnp.int32, (C, LANES), 1)
    row = lax.broadcasted_iota(jnp.int32, (C, LANES), 0)
    lo = lane < GDN_HEAD_DIM
    tril = row >= lane
    strict = row > lane
    eye = jnp.where(row == lane, 1.0, 0.0)
    ltri = tril.astype(BF16)

    def pick(ref, k):
        x = ref[:, (k // 2) * LANES:(k // 2 + 1) * LANES]
        return jnp.where(lo, pltpu.roll(x, LANES // 2, 1) if k % 2 else x, 0.0)

    sm = sm_ref[...]
    a = sm + dtb_ref[...]
    softplus = jnp.maximum(a, 0.0) + jnp.log1p(jnp.exp(-jnp.abs(a)))
    garr = -jnp.exp(alog_ref[...]) * softplus
    sig = _sigmoid(sm)
    kh, kb, gc, decay, A = [], [], [], [], []
    for k in range(nh):
        g_col = jnp.sum(jnp.where(lane == h0 + k, garr, 0.0), axis=1, keepdims=True)
        beta = jnp.sum(jnp.where(lane == HEADS + h0 + k, sig, 0.0), axis=1, keepdims=True)
        g1 = jnp.broadcast_to(g_col, (C, LANES))
        gsum = None
        for _ in range(3):
            gb = g1.astype(BF16)
            part = jnp.dot(ltri, gb, preferred_element_type=F32)
            gsum = part if gsum is None else gsum + part
            g1 = g1 - gb.astype(F32)
        gc.append(gsum)
        decay.append(jnp.where(tril, jnp.exp(jnp.where(tril, gsum - gsum.T, 0.0)), 0.0))
        kh.append(pick(k_ref, k))
        kb.append(kh[k] * beta)
        vb = pick(v_ref, k) * beta
        A.append(jnp.where(strict, _dot_nt(kb[k], kh[k]) * decay[k], 0.0))
        eg = jnp.exp(gsum)
        qh = pick(q_ref, k)
        qk_ref[k, 0] = jnp.where(tril, _dot_nt(qh, kh[k]) * decay[k], 0.0)
        qg_ref[k, 0] = qh * eg
        glast = gsum[C - 1:C, :]
        kdt_ref[k, 0] = (kh[k] * jnp.exp(glast - gsum)).T
        dl_ref[k, 0] = jnp.broadcast_to(jnp.exp(glast), (SUBLANES, LANES))
        u_ref[k, 0] = vb
        w_ref[k, 0] = kb[k] * eg
    P = [eye - A[k] for k in range(nh)]
    X = A
    for _ in range(int(math.log2(C)) - 1):
        X = [_dot3(X[k], X[k]) for k in range(nh)]
        P = [P[k] + _dot3(P[k], X[k]) for k in range(nh)]
    for k in range(nh):
        u_ref[k, 0] = _dot3(P[k], u_ref[k, 0])
        w_ref[k, 0] = _dot3(P[k], w_ref[k, 0])


def _gdn_chunks(qkv, small, alog_row, dtb_row, B, S):
    C = GDN_KCHUNK
    n = S // C
    nh = GDN_HEADS_PER_STEP
    ng = HEADS // nh
    wq = nh // 2 * LANES
    nqb = BRANCH_WIDTH // wq
    big = jax.ShapeDtypeStruct((B * HEADS, n, C, LANES), F32)
    ospec = pl.BlockSpec((nh, 1, C, LANES), lambda b, g, c: (b * ng + g, c, 0, 0))
    par = pl.BlockSpec((1, LANES), lambda b, g, c: (0, 0))
    outs = pl.pallas_call(
        _gdn_chunk_kernel, grid=(B, ng, n),
        in_specs=[pl.BlockSpec((C, wq), lambda b, g, c: (b * n + c, g)),
                  pl.BlockSpec((C, wq), lambda b, g, c: (b * n + c, nqb + g)),
                  pl.BlockSpec((C, wq), lambda b, g, c: (b * n + c, 2 * nqb + g)),
                  pl.BlockSpec((C, LANES), lambda b, g, c: (b * n + c, 0)), par, par],
        out_specs=[ospec] * 5 + [pl.BlockSpec((nh, 1, SUBLANES, LANES), lambda b, g, c: (b * ng + g, c, 0, 0))],
        out_shape=[big] * 5 + [jax.ShapeDtypeStruct((B * HEADS, n, SUBLANES, LANES), F32)],
        compiler_params=_cp("parallel", "parallel", "parallel"))(qkv, qkv, qkv, small, alog_row, dtb_row)
    return [o.reshape(-1, LANES) for o in outs]


def _gdn_scan_kernel(u_ref, w_ref, qg_ref, qk_ref, kdt_ref, dl_ref, z_ref, g_ref, o_ref, o_sc, *, n):
    C = GDN_KCHUNK
    S = n * C

    def step(c, states):
        out = []
        for k, state in enumerate(states):
            r = pl.ds(pl.multiple_of(k * S + c * C, C), C)
            sb = state.astype(BF16)
            v_new = u_ref[r, :] - _dot(w_ref[r, :], sb)
            vb = v_new.astype(BF16)
            o_sc[r, :] = _dot(qg_ref[r, :], sb) + _dot(qk_ref[r, :], vb)
            dl = dl_ref[pl.ds(pl.multiple_of((k * n + c) * SUBLANES, SUBLANES), 1), :]
            out.append(state * dl + _dot(kdt_ref[r, :], vb))
        return tuple(out)

    lax.fori_loop(0, n, step, tuple(jnp.zeros((LANES, LANES), F32) for _ in range(2)))
    lane = lax.broadcasted_iota(jnp.int32, o_ref.shape, 1)

    def nrm(o):
        ms = jnp.sum(o * o, axis=1, keepdims=True) * (1.0 / GDN_HEAD_DIM)
        return o * lax.rsqrt(ms + RMS_EPS)

    nn = jnp.where(lane < GDN_HEAD_DIM, nrm(o_sc[0:S, :]), pltpu.roll(nrm(o_sc[S:2 * S, :]), LANES // 2, 1))
    z = z_ref[...].astype(F32)
    o_ref[...] = nn * g_ref[...] * (z * _sigmoid(z))


def _gdn_scan(parts, proj, g_row, B, S):
    n = S // GDN_KCHUNK
    npair = HEADS // 2
    seq = pl.BlockSpec((2 * S, LANES), lambda i: (i, 0))
    return pl.pallas_call(
        functools.partial(_gdn_scan_kernel, n=n), grid=(B * npair,),
        in_specs=[seq] * 5 + [pl.BlockSpec((2 * n * SUBLANES, LANES), lambda i: (i, 0)),
                              pl.BlockSpec((S, LANES), lambda i: (i // npair, C_GZ // LANES + i % npair)),
                              pl.BlockSpec((1, LANES), lambda i: (0, 0))],
        out_specs=pl.BlockSpec((S, LANES), lambda i: (i // npair, i % npair)),
        out_shape=jax.ShapeDtypeStruct((B * S, BRANCH_WIDTH), F32),
        scratch_shapes=[pltpu.VMEM((2 * S, LANES), F32)],
        compiler_params=_cp("parallel"))(*parts, proj, g_row)


def _merge_kernel(g0_ref, g1_ref, g2_ref, gb_ref, a_ref, b_ref, c_ref, wb_ref, o_ref):
    acc = None
    for n, (gl, br) in enumerate(((g0_ref, a_ref), (g1_ref, b_ref), (g2_ref, c_ref))):
        y = _sigmoid(gl[...].astype(F32) + gb_ref[n:n + 1, :]) * _dot(br[...], wb_ref[n])
        acc = y if acc is None else acc + y
    o_ref[...] = acc.astype(BF16)


def _merge(proj, gate_bias, o_a, o_b, o_c, w_branch, tm=512, tn=512):
    T = o_a.shape[0]
    D = w_branch.shape[2]
    nj = D // tn
    gspec = lambda n: pl.BlockSpec((tm, tn), lambda i, j: (i, n * nj + j))
    bspec = pl.BlockSpec((tm, BRANCH_WIDTH), lambda i, j: (i, 0))
    return pl.pallas_call(
        _merge_kernel, grid=(T // tm, nj),
        in_specs=[gspec(0), gspec(1), gspec(2), pl.BlockSpec((3, tn), lambda i, j: (0, j)),
                  bspec, bspec, bspec, pl.BlockSpec((3, BRANCH_WIDTH, tn), lambda i, j: (0, 0, j))],
        out_specs=pl.BlockSpec((tm, tn), lambda i, j: (i, j)),
        out_shape=jax.ShapeDtypeStruct((T, D), BF16),
        compiler_params=_cp("parallel", "parallel"))(proj, proj, proj, gate_bias, o_a, o_b, o_c, w_branch)


def _outproj_kernel(m_ref, w_ref, h_ref, g_ref, b_ref, o_ref, ot_ref):
    y = DEEPNORM_ALPHA * h_ref[...] + _dot(m_ref[...], w_ref[...])
    y = _ln_rows(y, g_ref[...], b_ref[...])
    o_ref[...] = y
    ot_ref[...] = y.T.astype(BF16)


def _outproj_ln(merged, w_out, h, g, b, tm=256):
    T, D = h.shape
    return pl.pallas_call(
        _outproj_kernel, grid=(T // tm,),
        in_specs=[pl.BlockSpec((tm, D), lambda i: (i, 0)), pl.BlockSpec((D, D), lambda i: (0, 0)),
                  pl.BlockSpec((tm, D), lambda i: (i, 0)),
                  pl.BlockSpec((1, D), lambda i: (0, 0)), pl.BlockSpec((1, D), lambda i: (0, 0))],
        out_specs=[pl.BlockSpec((tm, D), lambda i: (i, 0)), pl.BlockSpec((D, tm), lambda i: (0, i))],
        out_shape=[jax.ShapeDtypeStruct((T, D), F32), jax.ShapeDtypeStruct((D, T), BF16)],
        compiler_params=_cp("parallel"))(merged, w_out, h, g.reshape(1, D), b.reshape(1, D))


def _extract_top(curs, n, on_max):
    curs = list(curs)
    for k in range(n):
        for i, cur in enumerate(curs):
            m = jnp.max(cur, axis=0, keepdims=True)
            on_max(i, k, m)
            if k + 1 < n:
                curs[i] = jnp.where(cur >= m, -jnp.inf, cur)


def _peer_score_kernel(qt_ref, keys_ref, s2_ref, e2_ref, c_ref, e1_ref, top_sc, cand_sc):
    K = PEER_TOPK
    tt = qt_ref.shape[1]
    nh = PEER_SCORE_HEADS_PER_TRIP

    def heads(hp, carry):
        s = []
        for i in range(2 * nh):
            r = pl.multiple_of((hp * nh * 2 + i) * PEER_HALF, PEER_HALF)
            s.append(_dot(keys_ref[hp * nh * 2 + i], qt_ref[pl.ds(r, PEER_HALF), :]))

        def put(i, k, m):
            top_sc[i, k:k + 1, :] = m
        _extract_top(s, K + 1, put)
        r8 = lax.broadcasted_iota(jnp.int32, (SUBLANES, tt), 0)
        v1max, v2max = [], []
        for d in range(nh):
            v1a, v1x = top_sc[2 * d, 0:K, :], top_sc[2 * d, K:K + 1, :]
            v2a, v2x = top_sc[2 * d + 1, 0:K, :], top_sc[2 * d + 1, K:K + 1, :]
            v1max.append(v1a[0:1])
            v2max.append(v2a[0:1])
            cand_sc[d, 0:K, :] = v1a + v2a[0:1]
            for b in range(1, SUBLANES):
                cand_sc[d, K + SUBLANES * (b - 1):K + SUBLANES * b, :] = v1a[0:SUBLANES] + v2a[b:b + 1]
            base = K + SUBLANES * (SUBLANES - 1)
            cand_sc[d, base:base + SUBLANES, :] = v1a[0:1] + v2a[SUBLANES:K]
            cand_sc[d, base + SUBLANES:base + 2 * SUBLANES, :] = jnp.where(
                r8 == 0, v1x + v2a[0:1], jnp.where(r8 == 1, v1a[0:1] + v2x, -jnp.inf))
        st = [{"z": jnp.zeros((1, tt), F32)} for _ in range(nh)]

        def acc(d, k, m):
            if k == 0:
                st[d]["top"] = m
            if k < K:
                st[d]["z"] = st[d]["z"] + jnp.exp(m - st[d]["top"])
            if k == K - 1:
                st[d]["t16"] = m
            if k == K:
                st[d]["t17"] = m
        _extract_top([cand_sc[d] for d in range(nh)], K + 1, acc)
        for d in range(nh):
            tau = 0.5 * (st[d]["t16"] + st[d]["t17"])
            ro = pl.ds(pl.multiple_of((hp * nh + d) * N_KEYS, N_KEYS), N_KEYS)
            s2_ref[ro, :] = s[2 * d + 1]
            e2_ref[ro, :] = jnp.exp(s[2 * d + 1] - v2max[d]) / st[d]["z"]
            c_ref[ro, :] = tau - s[2 * d]
            e1_ref[ro, :] = jnp.exp(s[2 * d] - v1max[d])
        return carry

    lax.fori_loop(0, HEADS // nh, heads, 0)


def _peer_scores(qt, keys):
    T = qt.shape[1]
    tt = PEER_SCORE_TT
    R = HEADS * N_KEYS
    ncand = PEER_TOPK + SUBLANES * (SUBLANES + 1)
    ospec = pl.BlockSpec((R, tt), lambda i: (0, i))
    return pl.pallas_call(
        _peer_score_kernel, grid=(T // tt,),
        in_specs=[pl.BlockSpec((2 * R, tt), lambda i: (0, i)),
                  pl.BlockSpec((2 * HEADS, N_KEYS, PEER_HALF), lambda i: (0, 0, 0))],
        out_specs=[ospec] * 4,
        out_shape=[jax.ShapeDtypeStruct((R, T), F32)] * 4,
        scratch_shapes=[pltpu.VMEM((2 * PEER_SCORE_HEADS_PER_TRIP, 3 * SUBLANES, tt), F32),
                        pltpu.VMEM((PEER_SCORE_HEADS_PER_TRIP, ncand, tt), F32)],
        compiler_params=_cp("parallel"))(qt, keys)


def _peer_expert_kernel(ht_ref, u_ref, vtp_ref, vtl_ref, s2_ref, e2_ref, c_ref, e1_ref, h_ref, g_ref, b_ref,
                        o_ref, acc_sc, act_sc, hw_sc, bc_sc):
    j = pl.program_id(1)
    _, eb, tt = hw_sc.shape
    ng = eb // N_KEYS
    assert ng == SUBLANES
    slot = j % 2

    @pl.when(j == 0)
    def _():
        acc_sc[...] = jnp.zeros(acc_sc.shape, F32)
        hw_sc[1] = jnp.zeros(hw_sc.shape[1:], BF16)

    mxu_w = 2 * LANES
    for half in range(tt // mxu_w):
        hl = slice(half * mxu_w, (half + 1) * mxu_w)
        a = jnp.dot(u_ref[...], ht_ref[:, hl], preferred_element_type=F32)
        act_sc[:, hl] = 0.5 * a * (1.0 + lax.erf(a * (2.0 ** -0.5)))
        acc_sc[:, hl] += jnp.dot(vtp_ref[...], hw_sc[1 - slot, :, hl], preferred_element_type=F32)
        for lc in range(half * (mxu_w // LANES), (half + 1) * (mxu_w // LANES)):
            ls = slice(lc * LANES, (lc + 1) * LANES)
            for h in range(HEADS):
                r8 = pl.ds(pl.multiple_of(h * N_KEYS + j * ng, ng), ng)
                thr8, e18 = c_ref[r8, ls], e1_ref[r8, ls]
                for g in range(ng):
                    bc_sc[0, h * ng + g] = jnp.broadcast_to(thr8[g:g + 1], (SUBLANES, LANES))
                    bc_sc[1, h * ng + g] = jnp.broadcast_to(e18[g:g + 1], (SUBLANES, LANES))
            for g in range(ng):
                wsum = jnp.zeros((N_KEYS // SUBLANES, SUBLANES, LANES), F32)
                for h in range(HEADS):
                    hr = slice(h * N_KEYS, (h + 1) * N_KEYS)
                    s2 = s2_ref[hr, ls].reshape(N_KEYS // SUBLANES, SUBLANES, LANES)
                    e2 = e2_ref[hr, ls].reshape(N_KEYS // SUBLANES, SUBLANES, LANES)
                    wsum = wsum + jnp.where(s2 >= bc_sc[0, h * ng + g][None], e2 * bc_sc[1, h * ng + g][None], 0.0)
                gs = slice(g * N_KEYS, (g + 1) * N_KEYS)
                hw_sc[slot, gs, ls] = (wsum.reshape(N_KEYS, LANES) * act_sc[gs, ls]).astype(BF16)

    @pl.when(j == pl.num_programs(1) - 1)
    def _():
        acc = acc_sc[...] + jnp.dot(vtl_ref[...], hw_sc[slot], preferred_element_type=F32)
        y = DEEPNORM_ALPHA * h_ref[...] + acc.T
        o_ref[...] = _ln_rows(y, g_ref[...], b_ref[...])


def _peer_experts(ht, u, vt, s2, e2, c, e1, h, g, b):
    T, D = h.shape
    tt, eb = PEER_TT, PEER_EB
    R = HEADS * N_KEYS
    nb = N_EXPERTS // eb
    assert nb % 2 == 0
    sspec = pl.BlockSpec((R, tt), lambda i, j: (0, i))
    vec = pl.BlockSpec((1, D), lambda i, j: (0, 0))
    return pl.pallas_call(
        _peer_expert_kernel, grid=(T // tt, nb),
        in_specs=[pl.BlockSpec((D, tt), lambda i, j: (0, i)),
                  pl.BlockSpec((eb, D), lambda i, j: (j, 0)),
                  pl.BlockSpec((D, eb), lambda i, j: (0, jnp.maximum(j - 1, 0))),
                  pl.BlockSpec((D, eb), lambda i, j: (0, nb - 1)),
                  sspec, sspec, sspec, sspec,
                  pl.BlockSpec((tt, D), lambda i, j: (i, 0)), vec, vec],
        out_specs=pl.BlockSpec((tt, D), lambda i, j: (i, 0)),
        out_shape=jax.ShapeDtypeStruct((T, D), F32),
        scratch_shapes=[pltpu.VMEM((D, tt), F32), pltpu.VMEM((eb, tt), F32), pltpu.VMEM((2, eb, tt), BF16),
                        pltpu.VMEM((2, HEADS * SUBLANES, SUBLANES, LANES), F32)],
        compiler_params=_cp("parallel", "arbitrary"))(ht, u, vt, vt, s2, e2, c, e1, h, g.reshape(1, D),
                                                      b.reshape(1, D))


def _rot_half(x, r):
    return jnp.concatenate([-x[..., r:2 * r], x[..., :r]], axis=-1)


def _prep_w_in(w):
    D = w.shape[0]
    splits = (MLA_Q_RANK, MLA_KV_RANK, MLA_ROPE, 3 * BRANCH_WIDTH, 3 * BRANCH_WIDTH, BRANCH_WIDTH, HEADS, HEADS,
              3 * D)
    o = np.cumsum((0,) + splits)
    cq, ckv, kr, mqkv, gqkv, gz, ga, gb, gl = [w[:, o[i]:o[i + 1]] for i in range(9)]
    mq, mk, mv = jnp.split(mqkv, 3, axis=1)

    def slab(m):
        m = m.reshape(D, HEADS, MOBA_HEAD_DIM)
        pad = jnp.zeros((D, HEADS, LANES - MOBA_HEAD_DIM - MOBA_ROT_DIM), w.dtype)
        return jnp.concatenate([m, _rot_half(m, MOBA_ROT_DIM // 2), pad], axis=-1).reshape(D, HEADS * LANES)

    small = jnp.concatenate([ga, gb, jnp.zeros((D, LANES // 2 - 2 * HEADS), w.dtype), kr,
                             _rot_half(kr, MLA_ROPE // 2)], axis=1)
    out = jnp.concatenate([gl, cq, ckv, slab(mq), slab(mk), mv, gqkv, gz], axis=1)
    assert out.shape[1] == C_TOTAL and C_TOTAL % PROJ_TN == 0
    return out.astype(BF16), small.astype(BF16)


def _prep_w_uq(w):
    R = w.shape[0]
    w = w.reshape(R, HEADS, MLA_NOPE + MLA_ROPE)
    rope = w[..., MLA_NOPE:]
    return jnp.concatenate([w, _rot_half(rope, MLA_ROPE // 2)], axis=-1).reshape(R, HEADS * LANES).astype(BF16)


def _prep_w_ukv(w):
    R = w.shape[0]
    w = w.reshape(R, HEADS, MLA_NOPE + MLA_V)
    k = jnp.concatenate([w[..., :MLA_NOPE], jnp.zeros((R, HEADS, LANES - MLA_NOPE), w.dtype)], axis=-1)
    return jnp.concatenate([k.reshape(R, HEADS * LANES), w[..., MLA_NOPE:].reshape(R, HEADS * MLA_V)],
                           axis=1).astype(BF16)


def _rope_tables(positions):
    pos = positions.reshape(-1).astype(F32)[:, None]
    T = pos.shape[0]

    def cs(rot):
        inv = ROPE_THETA ** (-jnp.arange(0, rot, 2, dtype=F32) / rot)
        ang = pos * inv
        return jnp.cos(ang), jnp.sin(ang)

    ca, sa = cs(MLA_ROPE)
    cb, sb = cs(MOBA_ROT_DIM)
    one = lambda n: jnp.ones((T, n), F32)
    zero = lambda n: jnp.zeros((T, n), F32)
    sc_a = (MLA_NOPE + MLA_ROPE) ** -0.5 * math.log2(math.e)
    sc_b = MOBA_HEAD_DIM ** -0.5 * math.log2(math.e)
    cat = lambda *xs: jnp.concatenate(xs, axis=1)
    mla = (cat(one(MLA_NOPE), ca, ca, zero(32)) * sc_a, cat(zero(MLA_NOPE), sa, sa, zero(32)) * sc_a,
           cat(zero(MLA_NOPE), ca, ca, zero(32)), cat(zero(MLA_NOPE), sa, sa, zero(32)))
    cm = cat(cb, cb, one(MOBA_HEAD_DIM - MOBA_ROT_DIM), zero(LANES - MOBA_HEAD_DIM))
    sm = cat(sb, sb, zero(LANES - MOBA_ROT_DIM))
    moba = (cm * sc_b, sm * sc_b, cm, sm)
    return mla, moba


def _lane_row(v):
    return jnp.concatenate([v.astype(F32), jnp.zeros((LANES - v.shape[0],), F32)]).reshape(1, LANES)


def kernel(x, positions, ln_in_g, ln_in_b, w_in, mla_q_norm, mla_kv_norm, mla_w_uq, mla_w_ukv, gdn_conv_w, gdn_A_log, gdn_dt_bias, gdn_o_norm, gate_bias, w_branch, w_out, ln1_g, ln1_b, peer_w_q, peer_sub_keys, peer_u, peer_v, ln2_g, ln2_b):
    B, S, D = x.shape
    T = B * S
    assert S % ATT_TQ == 0 and S % GDN_KCHUNK == 0 and T % PEER_TT == 0 and ATT_TQ == MOBA_BLOCK
    mla_tabs, moba_tabs = _rope_tables(positions)
    h = _layer_norm(x.reshape(T, D), ln_in_g, ln_in_b)
    for l in range(DEPTH):
        w_main, w_small = _prep_w_in(w_in[l])
        proj = _mm(h, w_main, tm=512, tn=PROJ_TN, out_dtype=BF16)
        small = _mm(h, w_small, tm=512, tn=LANES)
        q_raw = _rmsnorm_mm(proj, C_CQ // MLA_Q_RANK, mla_q_norm[l], _prep_w_uq(mla_w_uq[l]), 512, 512)
        kv_raw = _rmsnorm_mm(proj, C_CKV // MLA_KV_RANK, mla_kv_norm[l], _prep_w_ukv(mla_w_ukv[l]), 512, 512)
        qa, ka = _mla_prep(q_raw, kv_raw, small, mla_tabs)
        o_a = _attention(qa, ka, kv_raw, HEADS * LANES, B, S)
        qm, km, kmean = _moba_prep(proj, moba_tabs)
        o_b = _attention(qm, km, proj, C_MV, B, S, kmean=kmean.reshape(B, S // MOBA_BLOCK, HEADS * LANES))
        qkv = _gdn_conv(proj, gdn_conv_w[l], B, S)
        parts = _gdn_chunks(qkv, small, _lane_row(gdn_A_log[l]), _lane_row(gdn_dt_bias[l]), B, S)
        o_c = _gdn_scan(parts, proj, jnp.tile(gdn_o_norm[l], 2).reshape(1, LANES), B, S)
        merged = _merge(proj, gate_bias[l], o_a, o_b, o_c, w_branch[l].astype(BF16))
        h, ht = _outproj_ln(merged, w_out[l].astype(BF16), h, ln1_g[l], ln1_b[l])
        qt = _mm(peer_w_q[l].T.astype(BF16), ht, tm=1024, tn=1024, out_dtype=BF16)
        keys = peer_sub_keys[l].reshape(2 * HEADS, N_KEYS, PEER_HALF).astype(BF16)
        s2, e2, c, e1 = _peer_scores(qt, keys)
        h = _peer_experts(ht, peer_u[l].astype(BF16), peer_v[l].astype(BF16).T, s2, e2, c, e1, h, ln2_g[l], ln2_b[l])
    return h.reshape(B, S, D)
```

```python
import functools
import math

import numpy as np
import jax
import jax.numpy as jnp
from jax import lax
from jax.experimental import pallas as pl
from jax.experimental.pallas import tpu as pltpu

F32 = jnp.float32
BF16 = jnp.bfloat16
HIGHEST = lax.Precision.HIGHEST

DEPTH = 2
ROPE_THETA = 500000.0
NEG_INF = -1e30
LN_EPS = 1e-5
RMS_EPS = 1e-6
DEEPNORM_ALPHA = (2 * DEPTH) ** 0.25
HEADS = 8
MLA_NOPE, MLA_ROPE, MLA_V = 64, 32, 64
MLA_Q_RANK, MLA_KV_RANK = 768, 256
MOBA_HEAD_DIM, MOBA_ROT_DIM, MOBA_BLOCK, MOBA_TOPK = 64, 16, 256, 3
GDN_HEAD_DIM, GDN_CONV = 64, 4
BRANCH_WIDTH = 512
N_KEYS, PEER_TOPK, PEER_HALF = 128, 16, 128
N_EXPERTS = N_KEYS * N_KEYS

LANES = 128
SUBLANES = 8
VMEM_LIMIT = 56 * 1024 * 1024

GDN_KCHUNK = 128
GDN_HEADS_PER_STEP = 4
ATT_TQ = 256
PEER_TT = 512
PEER_EB = 1024
PEER_SCORE_TT = 256
PEER_SCORE_HEADS_PER_TRIP = 2

C_GATE, C_CQ, C_CKV, C_MQ, C_MK, C_MV, C_GQKV, C_GZ, C_TOTAL = (
    0, 3072, 3840, 4096, 5120, 6144, 6656, 8192, 8704)
PROJ_TN = 4352


def _cp(*sem):
    return pltpu.CompilerParams(dimension_semantics=sem, vmem_limit_bytes=VMEM_LIMIT)


def _dot(a, b):
    return jnp.dot(a.astype(BF16), b.astype(BF16), preferred_element_type=F32)


def _dot_nt(a, b):
    return lax.dot_general(a.astype(BF16), b.astype(BF16), (((1,), (1,)), ((), ())),
                           preferred_element_type=F32)


def _split(a):
    hi = a.astype(BF16)
    return hi, (a - hi.astype(F32)).astype(BF16)


def _dot3(a, b):
    ah, al = _split(a)
    bh, bl = _split(b)
    d = functools.partial(jnp.dot, preferred_element_type=F32)
    return d(ah, bh) + (d(al, bh) + d(ah, bl))


def _ln_rows(y, g, b):
    mu = jnp.mean(y, axis=-1, keepdims=True)
    d = y - mu
    var = jnp.mean(d * d, axis=-1, keepdims=True)
    return d * lax.rsqrt(var + LN_EPS) * g + b


def _sigmoid(x):
    return 1.0 / (1.0 + jnp.exp(-x))


def _ln_kernel(x_ref, g_ref, b_ref, o_ref):
    o_ref[...] = _ln_rows(x_ref[...], g_ref[...], b_ref[...])


def _layer_norm(x, g, b, tm=512):
    T, D = x.shape
    return pl.pallas_call(
        _ln_kernel, grid=(T // tm,),
        in_specs=[pl.BlockSpec((tm, D), lambda i: (i, 0)),
                  pl.BlockSpec((1, D), lambda i: (0, 0)),
                  pl.BlockSpec((1, D), lambda i: (0, 0))],
        out_specs=pl.BlockSpec((tm, D), lambda i: (i, 0)),
        out_shape=jax.ShapeDtypeStruct((T, D), F32),
        compiler_params=_cp("parallel"))(x, g.reshape(1, D), b.reshape(1, D))


def _mm_kernel(x_ref, w_ref, o_ref):
    o_ref[...] = _dot(x_ref[...], w_ref[...]).astype(o_ref.dtype)


def _mm(x, w, tm, tn, out_dtype=F32):
    M, K = x.shape
    N = w.shape[1]
    return pl.pallas_call(
        _mm_kernel, grid=(M // tm, N // tn),
        in_specs=[pl.BlockSpec((tm, K), lambda i, j: (i, 0)),
                  pl.BlockSpec((K, tn), lambda i, j: (0, j))],
        out_specs=pl.BlockSpec((tm, tn), lambda i, j: (i, j)),
        out_shape=jax.ShapeDtypeStruct((M, N), out_dtype),
        compiler_params=_cp("parallel", "parallel"))(x, w)


def _rms_rows(x, g):
    return x * lax.rsqrt(jnp.mean(x * x, axis=-1, keepdims=True) + RMS_EPS) * g


def _head_slabs(x):
    return [x[:, h * LANES:(h + 1) * LANES] for h in range(HEADS)]


def _mla_qkv_kernel(cq_ref, ckv_ref, gq_ref, gkv_ref, wq_ref, wkv_ref, sm_ref, cq_tab, sq_tab, ck_tab, sk_tab,
                    qo_ref, ko_ref, vo_ref):
    W = HEADS * LANES
    q = _dot(_rms_rows(cq_ref[...].astype(F32), gq_ref[...]), wq_ref[...])
    kv = _dot(_rms_rows(ckv_ref[...].astype(F32), gkv_ref[...]), wkv_ref[...])
    vo_ref[...] = kv[:, W:].astype(BF16)
    cq, sq = cq_tab[...], sq_tab[...]
    sm = sm_ref[...]
    kr = sm * ck_tab[...] + pltpu.roll(sm, LANES - MLA_ROPE, 1) * sk_tab[...]
    for h in range(HEADS):
        sl = slice(h * LANES, (h + 1) * LANES)
        qs = q[:, sl]
        qo_ref[:, sl] = (qs * cq + pltpu.roll(qs, LANES - MLA_ROPE, 1) * sq).astype(BF16)
        ko_ref[:, sl] = (kv[:, sl] + kr).astype(BF16)


def _mla_qkv(proj, small, gq, gkv, wq, wkv, tabs, tm=512):
    T = proj.shape[0]
    W = HEADS * LANES
    row = lambda i: (i, 0)
    tab = pl.BlockSpec((tm, LANES), row)
    full = lambda a: pl.BlockSpec(a.shape, lambda i: (0, 0))
    gq, gkv = gq.reshape(1, -1), gkv.reshape(1, -1)
    return pl.pallas_call(
        _mla_qkv_kernel, grid=(T // tm,),
        in_specs=[pl.BlockSpec((tm, MLA_Q_RANK), lambda i: (i, C_CQ // MLA_Q_RANK)),
                  pl.BlockSpec((tm, MLA_KV_RANK), lambda i: (i, C_CKV // MLA_KV_RANK)),
                  full(gq), full(gkv), full(wq), full(wkv), tab, tab, tab, tab, tab],
        out_specs=[pl.BlockSpec((tm, W), row), pl.BlockSpec((tm, W), row), pl.BlockSpec((tm, BRANCH_WIDTH), row)],
        out_shape=[jax.ShapeDtypeStruct((T, W), BF16)] * 2 + [jax.ShapeDtypeStruct((T, BRANCH_WIDTH), BF16)],
        compiler_params=_cp("parallel"))(proj, proj, gq, gkv, wq, wkv, small, *tabs)


def _moba_prep_kernel(q_ref, k_ref, cq_ref, sq_ref, ck_ref, sk_ref, qo_ref, ko_ref, km_ref):
    cq, sq, ck, sk = cq_ref[...], sq_ref[...], ck_ref[...], sk_ref[...]
    q, k = q_ref[...].astype(F32), k_ref[...].astype(F32)
    for h, (qs, ks) in enumerate(zip(_head_slabs(q), _head_slabs(k))):
        sl = slice(h * LANES, (h + 1) * LANES)
        qo_ref[:, sl] = qs * cq + pltpu.roll(qs, LANES // 2, 1) * sq
        kk = ks * ck + pltpu.roll(ks, LANES // 2, 1) * sk
        ko_ref[:, sl] = kk.astype(BF16)
        km_ref[0, :, sl] = jnp.mean(kk, axis=0, keepdims=True)


def _moba_prep(proj, tabs):
    T = proj.shape[0]
    W = HEADS * LANES
    tm = MOBA_BLOCK
    row = lambda i: (i, 0)
    tab = pl.BlockSpec((tm, LANES), row)
    return pl.pallas_call(
        _moba_prep_kernel, grid=(T // tm,),
        in_specs=[pl.BlockSpec((tm, W), lambda i: (i, C_MQ // W)),
                  pl.BlockSpec((tm, W), lambda i: (i, C_MK // W)), tab, tab, tab, tab],
        out_specs=[pl.BlockSpec((tm, W), row), pl.BlockSpec((tm, W), row),
                   pl.BlockSpec((1, 1, W), lambda i: (i, 0, 0))],
        out_shape=[jax.ShapeDtypeStruct((T, W), F32), jax.ShapeDtypeStruct((T, W), BF16),
                   jax.ShapeDtypeStruct((T // tm, 1, W), F32)],
        compiler_params=_cp("parallel"))(proj, proj, *tabs)


def _attn_kernel(*refs, moba, nk):
    if moba:
        q_ref, k_ref, v_ref, km_ref, o_ref, vt_sc, qb_sc, m_sc, l_sc, al_sc, acc_sc, s_sc, p_sc, bias_sc = refs
    else:
        q_ref, k_ref, v_ref, o_ref, vt_sc, m_sc, l_sc, al_sc, acc_sc, s_sc, p_sc = refs
    tq = tk = ATT_TQ
    dv = BRANCH_WIDTH // HEADS
    qi = pl.program_id(1)
    hslab = [slice(h * LANES, (h + 1) * LANES) for h in range(HEADS)]

    @pl.when(qi == 0)
    def _():
        for j in range(nk):
            for c in range(tk // LANES):
                for g in range(BRANCH_WIDTH // LANES):
                    vt_sc[j, g * LANES:(g + 1) * LANES, c * LANES:(c + 1) * LANES] = (
                        v_ref[j * tk + c * LANES:j * tk + (c + 1) * LANES,
                              g * LANES:(g + 1) * LANES].astype(F32).T.astype(BF16))

    if moba:
        blk = lax.broadcasted_iota(jnp.int32, (SUBLANES, tq), 0)
        for h in range(HEADS):
            qf = q_ref[:, hslab[h]]
            qb_sc[:, hslab[h]] = qf.astype(BF16)
            km = km_ref[0, :, hslab[h]]
            if nk < SUBLANES:
                km = jnp.concatenate([km, jnp.zeros((SUBLANES - nk, LANES), F32)], axis=0)
            gate = lax.dot_general(km, qf, (((1,), (1,)), ((), ())), precision=HIGHEST,
                                   preferred_element_type=F32)
            gate = jnp.where(blk < qi, gate, -jnp.inf)
            for n in range(nk):
                gn = gate[n:n + 1, :]
                beats = jnp.where(gate > gn, 1.0, jnp.where((gate == gn) & (blk < n), 1.0, 0.0))
                cnt = jnp.sum(beats, axis=0, keepdims=True)
                bias_sc[h, n] = jnp.where((cnt < MOBA_TOPK) & (n < qi), 0.0, NEG_INF)
    qsrc = qb_sc if moba else q_ref
    m_sc[...] = jnp.full(m_sc.shape, NEG_INF, F32)
    l_sc[...] = jnp.zeros(l_sc.shape, F32)
    acc_sc[...] = jnp.zeros(acc_sc.shape, F32)

    def block(j, diag):
        rows = pl.ds(pl.multiple_of(j * tk, tk), tk)
        for h in range(HEADS):
            s = _dot_nt(k_ref[rows, hslab[h]], qsrc[:, hslab[h]])
            if diag:
                keyi = lax.broadcasted_iota(jnp.int32, (tk, tq), 0)
                qryi = lax.broadcasted_iota(jnp.int32, (tk, tq), 1)
                s = jnp.where(keyi <= qryi, s, NEG_INF)
            elif moba:
                s = s + bias_sc[h, j]
            s_sc[h] = s
        for h in range(HEADS):
            s = s_sc[h]
            m_prev = m_sc[h]
            m_new = jnp.maximum(m_prev, jnp.max(s, axis=0, keepdims=True))
            alpha = jnp.exp2(m_prev - m_new)
            p = jnp.exp2(s - m_new)
            l_sc[h] = alpha * l_sc[h] + jnp.sum(p, axis=0, keepdims=True)
            p_sc[h] = p.astype(BF16)
            al_sc[h] = alpha
            m_sc[h] = m_new
        for h in range(HEADS):
            acc_sc[h] = al_sc[h] * acc_sc[h] + jnp.dot(vt_sc[j, h * dv:(h + 1) * dv, :], p_sc[h],
                                                       preferred_element_type=F32)

    def past(j, carry):
        block(j, False)
        return carry
    lax.fori_loop(0, qi, past, 0)
    block(qi, True)
    ot = jnp.concatenate([acc_sc[h] * (1.0 / l_sc[h]) for h in range(HEADS)], axis=0)
    o_ref[...] = ot.T


def _attention(q, k, v, v_col, B, S, kmean=None):
    T = B * S
    tq = ATT_TQ
    nq = S // tq
    W = HEADS * LANES
    dv = BRANCH_WIDTH // HEADS
    moba = kmean is not None
    assert nq <= SUBLANES and v_col % BRANCH_WIDTH == 0
    in_specs = [pl.BlockSpec((tq, W), lambda b, i: (b * nq + i, 0)),
                pl.BlockSpec((S, W), lambda b, i: (b, 0)),
                pl.BlockSpec((S, BRANCH_WIDTH), lambda b, i: (b, v_col // BRANCH_WIDTH))]
    scratch = [pltpu.VMEM((nq, BRANCH_WIDTH, tq), BF16)]
    if moba:
        scratch.append(pltpu.VMEM((tq, W), BF16))
    row = pltpu.VMEM((HEADS, 1, tq), F32)
    scratch += [row, row, row, pltpu.VMEM((HEADS, dv, tq), F32),
                pltpu.VMEM((HEADS, tq, tq), F32), pltpu.VMEM((HEADS, tq, tq), BF16)]
    args = [q, k, v]
    if moba:
        in_specs.append(pl.BlockSpec((1, nq, W), lambda b, i: (b, 0, 0)))
        scratch.append(pltpu.VMEM((HEADS, nq, 1, tq), F32))
        args.append(kmean)
    return pl.pallas_call(
        functools.partial(_attn_kernel, moba=moba, nk=nq),
        grid=(B, nq), in_specs=in_specs,
        out_specs=pl.BlockSpec((tq, BRANCH_WIDTH), lambda b, i: (b * nq + i, 0)),
        out_shape=jax.ShapeDtypeStruct((T, BRANCH_WIDTH), F32),
        scratch_shapes=scratch,
        compiler_params=_cp("parallel", "arbitrary"))(*args)


def _gdn_conv_kernel(x_ref, w_ref, o_ref):
    c = pl.program_id(1)
    x = x_ref[...].astype(F32)
    w = w_ref[...]
    S = x.shape[0]
    row = lax.broadcasted_iota(jnp.int32, (S, LANES), 0)
    lane = lax.broadcasted_iota(jnp.int32, (S, LANES), 1)
    y = x * w[GDN_CONV - 1:GDN_CONV, :]
    for d in range(1, GDN_CONV):
        xs = jnp.where(row >= d, pltpu.roll(x, d, 0), 0.0)
        y = y + xs * w[GDN_CONV - 1 - d:GDN_CONV - d, :]
    y = y * _sigmoid(y)
    sq = y * y
    lo = lane < GDN_HEAD_DIM
    ss0 = jnp.sum(jnp.where(lo, sq, 0.0), axis=1, keepdims=True)
    ss1 = jnp.sum(jnp.where(lo, 0.0, sq), axis=1, keepdims=True)
    inv = lax.rsqrt(jnp.where(lo, ss0, ss1) + RMS_EPS)
    nqb = BRANCH_WIDTH // LANES
    scale = jnp.where(c < nqb, GDN_HEAD_DIM ** -0.5, 1.0)
    o_ref[...] = jnp.where(c < 2 * nqb, y * inv * scale, y)


def _gdn_conv(proj, conv_w, B, S):
    T = B * S
    nb = 3 * BRANCH_WIDTH // LANES
    return pl.pallas_call(
        _gdn_conv_kernel, grid=(B, nb),
        in_specs=[pl.BlockSpec((S, LANES), lambda b, c: (b, C_GQKV // LANES + c)),
                  pl.BlockSpec((GDN_CONV, LANES), lambda b, c: (0, c))],
        out_specs=pl.BlockSpec((S, LANES), lambda b, c: (b, c)),
        out_shape=jax.ShapeDtypeStruct((T, 3 * BRANCH_WIDTH), F32),
        compiler_params=_cp("parallel", "parallel"))(proj, conv_w)


def _gdn_chunk_kernel(q_ref, k_ref, v_ref, sm_ref, alog_ref, dtb_ref,
                      u_ref, w_ref, qg_ref, qk_ref, kdt_ref, dl_ref):
    C = GDN_KCHUNK
    nh = GDN_HEADS_PER_STEP
    h0 = pl.program_id(1) * nh
    lane = lax.broadcasted_iota(jnp.int32, (C, LANES), 1)
    row = lax.broadcasted_iota(jnp.int32, (C, LANES), 0)
    lo = lane < GDN_HEAD_DIM
    tril = row >= lane
    strict = row > lane
    eye = jnp.where(row == lane, 1.0, 0.0)
    ltri = tril.astype(BF16)

    def pick(ref, k):
        x = ref[:, (k // 2) * LANES:(k // 2 + 1) * LANES]
        return jnp.where(lo, pltpu.roll(x, LANES // 2, 1) if k % 2 else x, 0.0)

    sm = sm_ref[...]
    a = sm + dtb_ref[...]
    softplus = jnp.maximum(a, 0.0) + jnp.log1p(jnp.exp(-jnp.abs(a)))
    garr = -jnp.exp(alog_ref[...]) * softplus
    sig = _sigmoid(sm)
    kh, kb, gc, decay, A = [], [], [], [], []
    for k in range(nh):
        g_col = jnp.sum(jnp.where(lane == h0 + k, garr, 0.0), axis=1, keepdims=True)
        beta = jnp.sum(jnp.where(lane == HEADS + h0 + k, sig, 0.0), axis=1, keepdims=True)
        g1 = jnp.broadcast_to(g_col, (C, LANES))
        gsum = None
        for _ in range(3):
            gb = g1.astype(BF16)
            part = jnp.dot(ltri, gb, preferred_element_type=F32)
            gsum = part if gsum is None else gsum + part
            g1 = g1 - gb.astype(F32)
        gc.append(gsum)
        decay.append(jnp.where(tril, jnp.exp(jnp.where(tril, gsum - gsum.T, 0.0)), 0.0))
        kh.append(pick(k_ref, k))
        kb.append(kh[k] * beta)
        vb = pick(v_ref, k) * beta
        A.append(jnp.where(strict, _dot_nt(kb[k], kh[k]) * decay[k], 0.0))
        eg = jnp.exp(gsum)
        qh = pick(q_ref, k)
        qk_ref[k, 0] = jnp.where(tril, _dot_nt(qh, kh[k]) * decay[k], 0.0)
        qg_ref[k, 0] = qh * eg
        glast = gsum[C - 1:C, :]
        kdt_ref[k, 0] = (kh[k] * jnp.exp(glast - gsum)).T
        dl_ref[k, 0] = jnp.broadcast_to(jnp.exp(glast), (SUBLANES, LANES))
        u_ref[k, 0] = vb
        w_ref[k, 0] = kb[k] * eg
    def joiner(level):
        same = (row >> (level + 1)) == (lane >> (level + 1))
        return same & (((row >> level) & 1) == 1) & (((lane >> level) & 1) == 0)

    P = [eye - jnp.where(joiner(0), A[k], 0.0) for k in range(nh)]
    for level in range(1, int(math.log2(C))):
        msk = joiner(level)
        T1 = [_dot3(P[k], jnp.where(msk, A[k], 0.0)) for k in range(nh)]
        P = [P[k] - _dot3(T1[k], P[k]) for k in range(nh)]
    for k in range(nh):
        u_ref[k, 0] = _dot3(P[k], u_ref[k, 0])
        w_ref[k, 0] = _dot3(P[k], w_ref[k, 0])


def _gdn_chunks(qkv, small, alog_row, dtb_row, B, S):
    C = GDN_KCHUNK
    n = S // C
    nh = GDN_HEADS_PER_STEP
    ng = HEADS // nh
    wq = nh // 2 * LANES
    nqb = BRANCH_WIDTH // wq
    big = jax.ShapeDtypeStruct((B * HEADS, n, C, LANES), F32)
    ospec = pl.BlockSpec((nh, 1, C, LANES), lambda b, g, c: (b * ng + g, c, 0, 0))
    par = pl.BlockSpec((1, LANES), lambda b, g, c: (0, 0))
    outs = pl.pallas_call(
        _gdn_chunk_kernel, grid=(B, ng, n),
        in_specs=[pl.BlockSpec((C, wq), lambda b, g, c: (b * n + c, g)),
                  pl.BlockSpec((C, wq), lambda b, g, c: (b * n + c, nqb + g)),
                  pl.BlockSpec((C, wq), lambda b, g, c: (b * n + c, 2 * nqb + g)),
                  pl.BlockSpec((C, LANES), lambda b, g, c: (b * n + c, 0)), par, par],
        out_specs=[ospec] * 5 + [pl.BlockSpec((nh, 1, SUBLANES, LANES), lambda b, g, c: (b * ng + g, c, 0, 0))],
        out_shape=[big] * 5 + [jax.ShapeDtypeStruct((B * HEADS, n, SUBLANES, LANES), F32)],
        compiler_params=_cp("parallel", "parallel", "parallel"))(qkv, qkv, qkv, small, alog_row, dtb_row)
    return [o.reshape(-1, LANES) for o in outs]


def _gdn_scan_kernel(u_ref, w_ref, qg_ref, qk_ref, kdt_ref, dl_ref, z_ref, g_ref, o_ref, o_sc, *, n):
    C = GDN_KCHUNK
    S = n * C

    def step(c, states):
        out = []
        for k, state in enumerate(states):
            r = pl.ds(pl.multiple_of(k * S + c * C, C), C)
            sb = state.astype(BF16)
            v_new = u_ref[r, :] - _dot(w_ref[r, :], sb)
            vb = v_new.astype(BF16)
            o_sc[r, :] = _dot(qg_ref[r, :], sb) + _dot(qk_ref[r, :], vb)
            dl = dl_ref[pl.ds(pl.multiple_of((k * n + c) * SUBLANES, SUBLANES), 1), :]
            out.append(state * dl + _dot(kdt_ref[r, :], vb))
        return tuple(out)

    lax.fori_loop(0, n, step, tuple(jnp.zeros((LANES, LANES), F32) for _ in range(2)))
    lane = lax.broadcasted_iota(jnp.int32, o_ref.shape, 1)

    def nrm(o):
        ms = jnp.sum(o * o, axis=1, keepdims=True) * (1.0 / GDN_HEAD_DIM)
        return o * lax.rsqrt(ms + RMS_EPS)

    nn = jnp.where(lane < GDN_HEAD_DIM, nrm(o_sc[0:S, :]), pltpu.roll(nrm(o_sc[S:2 * S, :]), LANES // 2, 1))
    z = z_ref[...].astype(F32)
    o_ref[...] = nn * g_ref[...] * (z * _sigmoid(z))


def _gdn_scan(parts, proj, g_row, B, S):
    n = S // GDN_KCHUNK
    npair = HEADS // 2
    seq = pl.BlockSpec((2 * S, LANES), lambda i: (i, 0))
    return pl.pallas_call(
        functools.partial(_gdn_scan_kernel, n=n), grid=(B * npair,),
        in_specs=[seq] * 5 + [pl.BlockSpec((2 * n * SUBLANES, LANES), lambda i: (i, 0)),
                              pl.BlockSpec((S, LANES), lambda i: (i // npair, C_GZ // LANES + i % npair)),
                              pl.BlockSpec((1, LANES), lambda i: (0, 0))],
        out_specs=pl.BlockSpec((S, LANES), lambda i: (i // npair, i % npair)),
        out_shape=jax.ShapeDtypeStruct((B * S, BRANCH_WIDTH), F32),
        scratch_shapes=[pltpu.VMEM((2 * S, LANES), F32)],
        compiler_params=_cp("parallel"))(*parts, proj, g_row)


def _merge_out_kernel(g0_ref, g1_ref, g2_ref, gb_ref, a_ref, b_ref, c_ref, wb_ref, wo_ref, h_ref, lg_ref, lb_ref,
                      o_ref, ot_ref):
    acc = None
    for n, (gl, br) in enumerate(((g0_ref, a_ref), (g1_ref, b_ref), (g2_ref, c_ref))):
        y = _sigmoid(gl[...].astype(F32) + gb_ref[n:n + 1, :]) * _dot(br[...], wb_ref[n])
        acc = y if acc is None else acc + y
    y = DEEPNORM_ALPHA * h_ref[...] + _dot(acc, wo_ref[...])
    y = _ln_rows(y, lg_ref[...], lb_ref[...])
    o_ref[...] = y
    ot_ref[...] = y.T.astype(BF16)


def _merge_out_ln(proj, gate_bias, o_a, o_b, o_c, w_branch, w_out, h, g, b, tm=256):
    T, D = h.shape
    row = lambda i: (i, 0)
    gspec = lambda n: pl.BlockSpec((tm, D), lambda i: (i, n))
    bspec = pl.BlockSpec((tm, BRANCH_WIDTH), row)
    vec = pl.BlockSpec((1, D), lambda i: (0, 0))
    return pl.pallas_call(
        _merge_out_kernel, grid=(T // tm,),
        in_specs=[gspec(0), gspec(1), gspec(2), pl.BlockSpec((3, D), lambda i: (0, 0)), bspec, bspec, bspec,
                  pl.BlockSpec((3, BRANCH_WIDTH, D), lambda i: (0, 0, 0)), pl.BlockSpec((D, D), lambda i: (0, 0)),
                  pl.BlockSpec((tm, D), row), vec, vec],
        out_specs=[pl.BlockSpec((tm, D), row), pl.BlockSpec((D, tm), lambda i: (0, i))],
        out_shape=[jax.ShapeDtypeStruct((T, D), F32), jax.ShapeDtypeStruct((D, T), BF16)],
        compiler_params=_cp("parallel"))(proj, proj, proj, gate_bias, o_a, o_b, o_c, w_branch, w_out, h,
                                         g.reshape(1, D), b.reshape(1, D))


def _extract_top(curs, n, on_max):
    curs = list(curs)
    for k in range(n):
        for i, cur in enumerate(curs):
            m = jnp.max(cur, axis=0, keepdims=True)
            on_max(i, k, m)
            if k + 1 < n:
                curs[i] = jnp.where(cur >= m, -jnp.inf, cur)


def _peer_score_kernel(qt_ref, keys_ref, s2_ref, e2_ref, c_ref, e1_ref, top_sc, cand_sc):
    K = PEER_TOPK
    tt = qt_ref.shape[1]
    nh = PEER_SCORE_HEADS_PER_TRIP

    def heads(hp, carry):
        s = []
        for i in range(2 * nh):
            r = pl.multiple_of((hp * nh * 2 + i) * PEER_HALF, PEER_HALF)
            s.append(_dot(keys_ref[hp * nh * 2 + i], qt_ref[pl.ds(r, PEER_HALF), :]))

        def put(i, k, m):
            top_sc[i, k:k + 1, :] = m
        _extract_top(s, K + 1, put)
        r8 = lax.broadcasted_iota(jnp.int32, (SUBLANES, tt), 0)
        v1max, v2max = [], []
        for d in range(nh):
            v1a, v1x = top_sc[2 * d, 0:K, :], top_sc[2 * d, K:K + 1, :]
            v2a, v2x = top_sc[2 * d + 1, 0:K, :], top_sc[2 * d + 1, K:K + 1, :]
            v1max.append(v1a[0:1])
            v2max.append(v2a[0:1])
            cand_sc[d, 0:K, :] = v1a + v2a[0:1]
            for b in range(1, SUBLANES):
                cand_sc[d, K + SUBLANES * (b - 1):K + SUBLANES * b, :] = v1a[0:SUBLANES] + v2a[b:b + 1]
            base = K + SUBLANES * (SUBLANES - 1)
            cand_sc[d, base:base + SUBLANES, :] = v1a[0:1] + v2a[SUBLANES:K]
            cand_sc[d, base + SUBLANES:base + 2 * SUBLANES, :] = jnp.where(
                r8 == 0, v1x + v2a[0:1], jnp.where(r8 == 1, v1a[0:1] + v2x, -jnp.inf))
        st = [{"z": jnp.zeros((1, tt), F32)} for _ in range(nh)]

        def acc(d, k, m):
            if k == 0:
                st[d]["top"] = m
            if k < K:
                st[d]["z"] = st[d]["z"] + jnp.exp(m - st[d]["top"])
            if k == K - 1:
                st[d]["t16"] = m
            if k == K:
                st[d]["t17"] = m
        _extract_top([cand_sc[d] for d in range(nh)], K + 1, acc)
        for d in range(nh):
            tau = 0.5 * (st[d]["t16"] + st[d]["t17"])
            ro = pl.ds(pl.multiple_of((hp * nh + d) * N_KEYS, N_KEYS), N_KEYS)
            s2_ref[ro, :] = s[2 * d + 1]
            e2_ref[ro, :] = jnp.exp(s[2 * d + 1] - v2max[d]) / st[d]["z"]
            c_ref[ro, :] = tau - s[2 * d]
            e1_ref[ro, :] = jnp.exp(s[2 * d] - v1max[d])
        return carry

    lax.fori_loop(0, HEADS // nh, heads, 0)


def _peer_scores(qt, keys):
    T = qt.shape[1]
    tt = PEER_SCORE_TT
    R = HEADS * N_KEYS
    ncand = PEER_TOPK + SUBLANES * (SUBLANES + 1)
    ospec = pl.BlockSpec((R, tt), lambda i: (0, i))
    return pl.pallas_call(
        _peer_score_kernel, grid=(T // tt,),
        in_specs=[pl.BlockSpec((2 * R, tt), lambda i: (0, i)),
                  pl.BlockSpec((2 * HEADS, N_KEYS, PEER_HALF), lambda i: (0, 0, 0))],
        out_specs=[ospec] * 4,
        out_shape=[jax.ShapeDtypeStruct((R, T), F32)] * 4,
        scratch_shapes=[pltpu.VMEM((2 * PEER_SCORE_HEADS_PER_TRIP, 3 * SUBLANES, tt), F32),
                        pltpu.VMEM((PEER_SCORE_HEADS_PER_TRIP, ncand, tt), F32)],
        compiler_params=_cp("parallel"))(qt, keys)


def _peer_expert_kernel(ht_ref, u_ref, vtp_ref, vtl_ref, s2_ref, e2_ref, c_ref, e1_ref, h_ref, g_ref, b_ref,
                        o_ref, acc_sc, act_sc, hw_sc, bc_sc):
    j = pl.program_id(1)
    _, eb, tt = hw_sc.shape
    ng = eb // N_KEYS
    assert ng == SUBLANES
    slot = j % 2

    @pl.when(j == 0)
    def _():
        acc_sc[...] = jnp.zeros(acc_sc.shape, F32)
        hw_sc[1] = jnp.zeros(hw_sc.shape[1:], BF16)

    mxu_w = 2 * LANES
    for half in range(tt // mxu_w):
        hl = slice(half * mxu_w, (half + 1) * mxu_w)
        a = jnp.dot(u_ref[...], ht_ref[:, hl], preferred_element_type=F32)
        act_sc[:, hl] = 0.5 * a * (1.0 + lax.erf(a * (2.0 ** -0.5)))
        acc_sc[:, hl] += jnp.dot(vtp_ref[...], hw_sc[1 - slot, :, hl], preferred_element_type=F32)
        for lc in range(half * (mxu_w // LANES), (half + 1) * (mxu_w // LANES)):
            ls = slice(lc * LANES, (lc + 1) * LANES)
            for h in range(HEADS):
                r8 = pl.ds(pl.multiple_of(h * N_KEYS + j * ng, ng), ng)
                thr8, e18 = c_ref[r8, ls], e1_ref[r8, ls]
                for g in range(ng):
                    bc_sc[0, h * ng + g] = jnp.broadcast_to(thr8[g:g + 1], (SUBLANES, LANES))
                    bc_sc[1, h * ng + g] = jnp.broadcast_to(e18[g:g + 1], (SUBLANES, LANES))
            for g in range(ng):
                wsum = jnp.zeros((N_KEYS // SUBLANES, SUBLANES, LANES), F32)
                for h in range(HEADS):
                    hr = slice(h * N_KEYS, (h + 1) * N_KEYS)
                    s2 = s2_ref[hr, ls].reshape(N_KEYS // SUBLANES, SUBLANES, LANES)
                    e2 = e2_ref[hr, ls].reshape(N_KEYS // SUBLANES, SUBLANES, LANES)
                    wsum = wsum + jnp.where(s2 >= bc_sc[0, h * ng + g][None], e2 * bc_sc[1, h * ng + g][None], 0.0)
                gs = slice(g * N_KEYS, (g + 1) * N_KEYS)
                hw_sc[slot, gs, ls] = (wsum.reshape(N_KEYS, LANES) * act_sc[gs, ls]).astype(BF16)

    @pl.when(j == pl.num_programs(1) - 1)
    def _():
        acc = acc_sc[...] + jnp.dot(vtl_ref[...], hw_sc[slot], preferred_element_type=F32)
        y = DEEPNORM_ALPHA * h_ref[...] + acc.T
        o_ref[...] = _ln_rows(y, g_ref[...], b_ref[...])


def _peer_experts(ht, u, vt, s2, e2, c, e1, h, g, b):
    T, D = h.shape
    tt, eb = PEER_TT, PEER_EB
    R = HEADS * N_KEYS
    nb = N_EXPERTS // eb
    assert nb % 2 == 0
    sspec = pl.BlockSpec((R, tt), lambda i, j: (0, i))
    vec = pl.BlockSpec((1, D), lambda i, j: (0, 0))
    return pl.pallas_call(
        _peer_expert_kernel, grid=(T // tt, nb),
        in_specs=[pl.BlockSpec((D, tt), lambda i, j: (0, i)),
                  pl.BlockSpec((eb, D), lambda i, j: (j, 0)),
                  pl.BlockSpec((D, eb), lambda i, j: (0, jnp.maximum(j - 1, 0))),
                  pl.BlockSpec((D, eb), lambda i, j: (0, nb - 1)),
                  sspec, sspec, sspec, sspec,
                  pl.BlockSpec((tt, D), lambda i, j: (i, 0)), vec, vec],
        out_specs=pl.BlockSpec((tt, D), lambda i, j: (i, 0)),
        out_shape=jax.ShapeDtypeStruct((T, D), F32),
        scratch_shapes=[pltpu.VMEM((D, tt), F32), pltpu.VMEM((eb, tt), F32), pltpu.VMEM((2, eb, tt), BF16),
                        pltpu.VMEM((2, HEADS * SUBLANES, SUBLANES, LANES), F32)],
        compiler_params=_cp("parallel", "arbitrary"))(ht, u, vt, vt, s2, e2, c, e1, h, g.reshape(1, D),
                                                      b.reshape(1, D))


def _rot_half(x, r):
    return jnp.concatenate([-x[..., r:2 * r], x[..., :r]], axis=-1)


def _prep_w_in(w):
    D = w.shape[0]
    splits = (MLA_Q_RANK, MLA_KV_RANK, MLA_ROPE, 3 * BRANCH_WIDTH, 3 * BRANCH_WIDTH, BRANCH_WIDTH, HEADS, HEADS,
              3 * D)
    o = np.cumsum((0,) + splits)
    cq, ckv, kr, mqkv, gqkv, gz, ga, gb, gl = [w[:, o[i]:o[i + 1]] for i in range(9)]
    mq, mk, mv = jnp.split(mqkv, 3, axis=1)

    def slab(m):
        m = m.reshape(D, HEADS, MOBA_HEAD_DIM)
        pad = jnp.zeros((D, HEADS, LANES - MOBA_HEAD_DIM - MOBA_ROT_DIM), w.dtype)
        return jnp.concatenate([m, _rot_half(m, MOBA_ROT_DIM // 2), pad], axis=-1).reshape(D, HEADS * LANES)

    small = jnp.concatenate([ga, gb, jnp.zeros((D, LANES // 2 - 2 * HEADS), w.dtype), kr,
                             _rot_half(kr, MLA_ROPE // 2)], axis=1)
    out = jnp.concatenate([gl, cq, ckv, slab(mq), slab(mk), mv, gqkv, gz], axis=1)
    assert out.shape[1] == C_TOTAL and C_TOTAL % PROJ_TN == 0
    return out.astype(BF16), small.astype(BF16)


def _prep_w_uq(w):
    R = w.shape[0]
    w = w.reshape(R, HEADS, MLA_NOPE + MLA_ROPE)
    rope = w[..., MLA_NOPE:]
    return jnp.concatenate([w, _rot_half(rope, MLA_ROPE // 2)], axis=-1).reshape(R, HEADS * LANES).astype(BF16)


def _prep_w_ukv(w):
    R = w.shape[0]
    w = w.reshape(R, HEADS, MLA_NOPE + MLA_V)
    k = jnp.concatenate([w[..., :MLA_NOPE], jnp.zeros((R, HEADS, LANES - MLA_NOPE), w.dtype)], axis=-1)
    return jnp.concatenate([k.reshape(R, HEADS * LANES), w[..., MLA_NOPE:].reshape(R, HEADS * MLA_V)],
                           axis=1).astype(BF16)


def _rope_tables(positions):
    pos = positions.reshape(-1).astype(F32)[:, None]
    T = pos.shape[0]

    def cs(rot):
        inv = ROPE_THETA ** (-jnp.arange(0, rot, 2, dtype=F32) / rot)
        ang = pos * inv
        return jnp.cos(ang), jnp.sin(ang)

    ca, sa = cs(MLA_ROPE)
    cb, sb = cs(MOBA_ROT_DIM)
    one = lambda n: jnp.ones((T, n), F32)
    zero = lambda n: jnp.zeros((T, n), F32)
    sc_a = (MLA_NOPE + MLA_ROPE) ** -0.5 * math.log2(math.e)
    sc_b = MOBA_HEAD_DIM ** -0.5 * math.log2(math.e)
    cat = lambda *xs: jnp.concatenate(xs, axis=1)
    mla = (cat(one(MLA_NOPE), ca, ca, zero(32)) * sc_a, cat(zero(MLA_NOPE), sa, sa, zero(32)) * sc_a,
           cat(zero(MLA_NOPE), ca, ca, zero(32)), cat(zero(MLA_NOPE), sa, sa, zero(32)))
    cm = cat(cb, cb, one(MOBA_HEAD_DIM - MOBA_ROT_DIM), zero(LANES - MOBA_HEAD_DIM))
    sm = cat(sb, sb, zero(LANES - MOBA_ROT_DIM))
    moba = (cm * sc_b, sm * sc_b, cm, sm)
    return mla, moba


def _lane_row(v):
    return jnp.concatenate([v.astype(F32), jnp.zeros((LANES - v.shape[0],), F32)]).reshape(1, LANES)


def kernel(x, positions, ln_in_g, ln_in_b, w_in, mla_q_norm, mla_kv_norm, mla_w_uq, mla_w_ukv, gdn_conv_w, gdn_A_log, gdn_dt_bias, gdn_o_norm, gate_bias, w_branch, w_out, ln1_g, ln1_b, peer_w_q, peer_sub_keys, peer_u, peer_v, ln2_g, ln2_b):
    B, S, D = x.shape
    T = B * S
    assert S % ATT_TQ == 0 and S % GDN_KCHUNK == 0 and T % PEER_TT == 0 and ATT_TQ == MOBA_BLOCK
    mla_tabs, moba_tabs = _rope_tables(positions)
    h = _layer_norm(x.reshape(T, D), ln_in_g, ln_in_b)
    for l in range(DEPTH):
        w_main, w_small = _prep_w_in(w_in[l])
        proj = _mm(h, w_main, tm=512, tn=PROJ_TN, out_dtype=BF16)
        small = _mm(h, w_small, tm=512, tn=LANES)
        qa, ka, va = _mla_qkv(proj, small, mla_q_norm[l], mla_kv_norm[l], _prep_w_uq(mla_w_uq[l]),
                              _prep_w_ukv(mla_w_ukv[l]), mla_tabs)
        o_a = _attention(qa, ka, va, 0, B, S)
        qm, km, kmean = _moba_prep(proj, moba_tabs)
        o_b = _attention(qm, km, proj, C_MV, B, S, kmean=kmean.reshape(B, S // MOBA_BLOCK, HEADS * LANES))
        qkv = _gdn_conv(proj, gdn_conv_w[l], B, S)
        parts = _gdn_chunks(qkv, small, _lane_row(gdn_A_log[l]), _lane_row(gdn_dt_bias[l]), B, S)
        o_c = _gdn_scan(parts, proj, jnp.tile(gdn_o_norm[l], 2).reshape(1, LANES), B, S)
        h, ht = _merge_out_ln(proj, gate_bias[l], o_a, o_b, o_c, w_branch[l].astype(BF16), w_out[l].astype(BF16),
                              h, ln1_g[l], ln1_b[l])
        qt = _mm(peer_w_q[l].T.astype(BF16), ht, tm=1024, tn=1024, out_dtype=BF16)
        keys = peer_sub_keys[l].reshape(2 * HEADS, N_KEYS, PEER_HALF).astype(BF16)
        s2, e2, c, e1 = _peer_scores(qt, keys)
        h = _peer_experts(ht, peer_u[l].astype(BF16), peer_v[l].astype(BF16).T, s2, e2, c, e1, h, ln2_g[l], ln2_b[l])
    return h.reshape(B, S, D)
```

```python
import functools
import math

import numpy as np
import jax
import jax.numpy as jnp
from jax import lax
from jax.experimental import pallas as pl
from jax.experimental.pallas import tpu as pltpu

F32 = jnp.float32
BF16 = jnp.bfloat16
HIGHEST = lax.Precision.HIGHEST

DEPTH = 2
ROPE_THETA = 500000.0
NEG_INF = -1e30
LN_EPS = 1e-5
RMS_EPS = 1e-6
DEEPNORM_ALPHA = (2 * DEPTH) ** 0.25
HEADS = 8
MLA_NOPE, MLA_ROPE, MLA_V = 64, 32, 64
MLA_Q_RANK, MLA_KV_RANK = 768, 256
MOBA_HEAD_DIM, MOBA_ROT_DIM, MOBA_BLOCK, MOBA_TOPK = 64, 16, 256, 3
GDN_HEAD_DIM, GDN_CONV = 64, 4
BRANCH_WIDTH = 512
N_KEYS, PEER_TOPK, PEER_HALF = 128, 16, 128
N_EXPERTS = N_KEYS * N_KEYS

LANES = 128
SUBLANES = 8
VMEM_LIMIT = 56 * 1024 * 1024

GDN_KCHUNK = 128
GDN_HEADS_PER_STEP = 4
ATT_TQ = 256
PEER_TT = 512
PEER_EB = 1024
PEER_SCORE_TT = 256
PEER_SCORE_HEADS_PER_TRIP = 2

C_GATE, C_CQ, C_CKV, C_MQ, C_MK, C_MV, C_GQKV, C_GZ, C_TOTAL = (
    0, 3072, 3840, 4096, 5120, 6144, 6656, 8192, 8704)
PROJ_TN = 4352


def _cp(*sem):
    return pltpu.CompilerParams(dimension_semantics=sem, vmem_limit_bytes=VMEM_LIMIT)


def _dot(a, b):
    return jnp.dot(a.astype(BF16), b.astype(BF16), preferred_element_type=F32)


def _dot_nt(a, b):
    return lax.dot_general(a.astype(BF16), b.astype(BF16), (((1,), (1,)), ((), ())),
                           preferred_element_type=F32)


def _ln_rows(y, g, b):
    mu = jnp.mean(y, axis=-1, keepdims=True)
    d = y - mu
    var = jnp.mean(d * d, axis=-1, keepdims=True)
    return d * lax.rsqrt(var + LN_EPS) * g + b


def _sigmoid(x):
    return 1.0 / (1.0 + jnp.exp(-x))


def _ln_kernel(x_ref, g_ref, b_ref, o_ref):
    o_ref[...] = _ln_rows(x_ref[...], g_ref[...], b_ref[...])


def _layer_norm(x, g, b, tm=512):
    T, D = x.shape
    return pl.pallas_call(
        _ln_kernel, grid=(T // tm,),
        in_specs=[pl.BlockSpec((tm, D), lambda i: (i, 0)),
                  pl.BlockSpec((1, D), lambda i: (0, 0)),
                  pl.BlockSpec((1, D), lambda i: (0, 0))],
        out_specs=pl.BlockSpec((tm, D), lambda i: (i, 0)),
        out_shape=jax.ShapeDtypeStruct((T, D), F32),
        compiler_params=_cp("parallel"))(x, g.reshape(1, D), b.reshape(1, D))


def _mm_kernel(x_ref, w_ref, o_ref):
    o_ref[...] = _dot(x_ref[...], w_ref[...]).astype(o_ref.dtype)


def _mm(x, w, tm, tn, out_dtype=F32):
    M, K = x.shape
    N = w.shape[1]
    return pl.pallas_call(
        _mm_kernel, grid=(M // tm, N // tn),
        in_specs=[pl.BlockSpec((tm, K), lambda i, j: (i, 0)),
                  pl.BlockSpec((K, tn), lambda i, j: (0, j))],
        out_specs=pl.BlockSpec((tm, tn), lambda i, j: (i, j)),
        out_shape=jax.ShapeDtypeStruct((M, N), out_dtype),
        compiler_params=_cp("parallel", "parallel"))(x, w)


def _rms_rows(x, g):
    return x * lax.rsqrt(jnp.mean(x * x, axis=-1, keepdims=True) + RMS_EPS) * g


def _head_slabs(x):
    return [x[:, h * LANES:(h + 1) * LANES] for h in range(HEADS)]


def _mla_qkv_kernel(cq_ref, ckv_ref, gq_ref, gkv_ref, wq_ref, wkv_ref, sm_ref, cq_tab, sq_tab, ck_tab, sk_tab,
                    qo_ref, ko_ref, vo_ref):
    W = HEADS * LANES
    q = _dot(_rms_rows(cq_ref[...].astype(F32), gq_ref[...]), wq_ref[...])
    kv = _dot(_rms_rows(ckv_ref[...].astype(F32), gkv_ref[...]), wkv_ref[...])
    vo_ref[...] = kv[:, W:].astype(BF16)
    cq, sq = cq_tab[...], sq_tab[...]
    sm = sm_ref[...]
    kr = sm * ck_tab[...] + pltpu.roll(sm, LANES - MLA_ROPE, 1) * sk_tab[...]
    for h in range(HEADS):
        sl = slice(h * LANES, (h + 1) * LANES)
        qs = q[:, sl]
        qo_ref[:, sl] = (qs * cq + pltpu.roll(qs, LANES - MLA_ROPE, 1) * sq).astype(BF16)
        ko_ref[:, sl] = (kv[:, sl] + kr).astype(BF16)


def _mla_qkv(proj, small, gq, gkv, wq, wkv, tabs, tm=512):
    T = proj.shape[0]
    W = HEADS * LANES
    row = lambda i: (i, 0)
    tab = pl.BlockSpec((tm, LANES), row)
    full = lambda a: pl.BlockSpec(a.shape, lambda i: (0, 0))
    gq, gkv = gq.reshape(1, -1), gkv.reshape(1, -1)
    return pl.pallas_call(
        _mla_qkv_kernel, grid=(T // tm,),
        in_specs=[pl.BlockSpec((tm, MLA_Q_RANK), lambda i: (i, C_CQ // MLA_Q_RANK)),
                  pl.BlockSpec((tm, MLA_KV_RANK), lambda i: (i, C_CKV // MLA_KV_RANK)),
                  full(gq), full(gkv), full(wq), full(wkv), tab, tab, tab, tab, tab],
        out_specs=[pl.BlockSpec((tm, W), row), pl.BlockSpec((tm, W), row), pl.BlockSpec((tm, BRANCH_WIDTH), row)],
        out_shape=[jax.ShapeDtypeStruct((T, W), BF16)] * 2 + [jax.ShapeDtypeStruct((T, BRANCH_WIDTH), BF16)],
        compiler_params=_cp("parallel"))(proj, proj, gq, gkv, wq, wkv, small, *tabs)


def _moba_prep_kernel(q_ref, k_ref, cq_ref, sq_ref, ck_ref, sk_ref, qo_ref, ko_ref, km_ref):
    cq, sq, ck, sk = cq_ref[...], sq_ref[...], ck_ref[...], sk_ref[...]
    q, k = q_ref[...].astype(F32), k_ref[...].astype(F32)
    for h, (qs, ks) in enumerate(zip(_head_slabs(q), _head_slabs(k))):
        sl = slice(h * LANES, (h + 1) * LANES)
        qo_ref[:, sl] = qs * cq + pltpu.roll(qs, LANES // 2, 1) * sq
        kk = ks * ck + pltpu.roll(ks, LANES // 2, 1) * sk
        ko_ref[:, sl] = kk.astype(BF16)
        km_ref[0, :, sl] = jnp.mean(kk, axis=0, keepdims=True)


def _moba_prep(proj, tabs):
    T = proj.shape[0]
    W = HEADS * LANES
    tm = MOBA_BLOCK
    row = lambda i: (i, 0)
    tab = pl.BlockSpec((tm, LANES), row)
    return pl.pallas_call(
        _moba_prep_kernel, grid=(T // tm,),
        in_specs=[pl.BlockSpec((tm, W), lambda i: (i, C_MQ // W)),
                  pl.BlockSpec((tm, W), lambda i: (i, C_MK // W)), tab, tab, tab, tab],
        out_specs=[pl.BlockSpec((tm, W), row), pl.BlockSpec((tm, W), row),
                   pl.BlockSpec((1, 1, W), lambda i: (i, 0, 0))],
        out_shape=[jax.ShapeDtypeStruct((T, W), F32), jax.ShapeDtypeStruct((T, W), BF16),
                   jax.ShapeDtypeStruct((T // tm, 1, W), F32)],
        compiler_params=_cp("parallel"))(proj, proj, *tabs)


def _attn_kernel(*refs, moba, nk):
    if moba:
        q_ref, k_ref, v_ref, km_ref, o_ref, vt_sc, qb_sc, m_sc, l_sc, al_sc, acc_sc, s_sc, p_sc, bias_sc = refs
    else:
        q_ref, k_ref, v_ref, o_ref, vt_sc, m_sc, l_sc, al_sc, acc_sc, s_sc, p_sc = refs
    tq = tk = ATT_TQ
    dv = BRANCH_WIDTH // HEADS
    qi = pl.program_id(1)
    hslab = [slice(h * LANES, (h + 1) * LANES) for h in range(HEADS)]

    @pl.when(qi == 0)
    def _():
        for j in range(nk):
            for c in range(tk // LANES):
                for g in range(BRANCH_WIDTH // LANES):
                    vt_sc[j, g * LANES:(g + 1) * LANES, c * LANES:(c + 1) * LANES] = (
                        v_ref[j * tk + c * LANES:j * tk + (c + 1) * LANES,
                              g * LANES:(g + 1) * LANES].astype(F32).T.astype(BF16))

    if moba:
        blk = lax.broadcasted_iota(jnp.int32, (SUBLANES, tq), 0)
        for h in range(HEADS):
            qf = q_ref[:, hslab[h]]
            qb_sc[:, hslab[h]] = qf.astype(BF16)
            km = km_ref[0, :, hslab[h]]
            if nk < SUBLANES:
                km = jnp.concatenate([km, jnp.zeros((SUBLANES - nk, LANES), F32)], axis=0)
            gate = lax.dot_general(km, qf, (((1,), (1,)), ((), ())), precision=HIGHEST,
                                   preferred_element_type=F32)
            gate = jnp.where(blk < qi, gate, -jnp.inf)
            for n in range(nk):
                gn = gate[n:n + 1, :]
                beats = jnp.where(gate > gn, 1.0, jnp.where((gate == gn) & (blk < n), 1.0, 0.0))
                cnt = jnp.sum(beats, axis=0, keepdims=True)
                bias_sc[h, n] = jnp.where((cnt < MOBA_TOPK) & (n < qi), 0.0, NEG_INF)
    qsrc = qb_sc if moba else q_ref
    m_sc[...] = jnp.full(m_sc.shape, NEG_INF, F32)
    l_sc[...] = jnp.zeros(l_sc.shape, F32)
    acc_sc[...] = jnp.zeros(acc_sc.shape, F32)

    def block(j, diag):
        rows = pl.ds(pl.multiple_of(j * tk, tk), tk)
        for h in range(HEADS):
            s = _dot_nt(k_ref[rows, hslab[h]], qsrc[:, hslab[h]])
            if diag:
                keyi = lax.broadcasted_iota(jnp.int32, (tk, tq), 0)
                qryi = lax.broadcasted_iota(jnp.int32, (tk, tq), 1)
                s = jnp.where(keyi <= qryi, s, NEG_INF)
            elif moba:
                s = s + bias_sc[h, j]
            s_sc[h] = s
        for h in range(HEADS):
            s = s_sc[h]
            m_prev = m_sc[h]
            m_new = jnp.maximum(m_prev, jnp.max(s, axis=0, keepdims=True))
            alpha = jnp.exp2(m_prev - m_new)
            p = jnp.exp2(s - m_new)
            l_sc[h] = alpha * l_sc[h] + jnp.sum(p, axis=0, keepdims=True)
            p_sc[h] = p.astype(BF16)
            al_sc[h] = alpha
            m_sc[h] = m_new
        for h in range(HEADS):
            acc_sc[h] = al_sc[h] * acc_sc[h] + jnp.dot(vt_sc[j, h * dv:(h + 1) * dv, :], p_sc[h],
                                                       preferred_element_type=F32)

    def past(j, carry):
        block(j, False)
        return carry
    lax.fori_loop(0, qi, past, 0)
    block(qi, True)
    ot = jnp.concatenate([acc_sc[h] * (1.0 / l_sc[h]) for h in range(HEADS)], axis=0)
    o_ref[...] = ot.T


def _attention(q, k, v, v_col, B, S, kmean=None):
    T = B * S
    tq = ATT_TQ
    nq = S // tq
    W = HEADS * LANES
    dv = BRANCH_WIDTH // HEADS
    moba = kmean is not None
    assert nq <= SUBLANES and v_col % BRANCH_WIDTH == 0
    in_specs = [pl.BlockSpec((tq, W), lambda b, i: (b * nq + i, 0)),
                pl.BlockSpec((S, W), lambda b, i: (b, 0)),
                pl.BlockSpec((S, BRANCH_WIDTH), lambda b, i: (b, v_col // BRANCH_WIDTH))]
    scratch = [pltpu.VMEM((nq, BRANCH_WIDTH, tq), BF16)]
    if moba:
        scratch.append(pltpu.VMEM((tq, W), BF16))
    row = pltpu.VMEM((HEADS, 1, tq), F32)
    scratch += [row, row, row, pltpu.VMEM((HEADS, dv, tq), F32),
                pltpu.VMEM((HEADS, tq, tq), F32), pltpu.VMEM((HEADS, tq, tq), BF16)]
    args = [q, k, v]
    if moba:
        in_specs.append(pl.BlockSpec((1, nq, W), lambda b, i: (b, 0, 0)))
        scratch.append(pltpu.VMEM((HEADS, nq, 1, tq), F32))
        args.append(kmean)
    return pl.pallas_call(
        functools.partial(_attn_kernel, moba=moba, nk=nq),
        grid=(B, nq), in_specs=in_specs,
        out_specs=pl.BlockSpec((tq, BRANCH_WIDTH), lambda b, i: (b * nq + i, 0)),
        out_shape=jax.ShapeDtypeStruct((T, BRANCH_WIDTH), F32),
        scratch_shapes=scratch,
        compiler_params=_cp("parallel", "arbitrary"))(*args)


def _gdn_conv_kernel(x_ref, w_ref, o_ref):
    c = pl.program_id(1)
    x = x_ref[...].astype(F32)
    w = w_ref[...]
    S = x.shape[0]
    row = lax.broadcasted_iota(jnp.int32, (S, LANES), 0)
    lane = lax.broadcasted_iota(jnp.int32, (S, LANES), 1)
    y = x * w[GDN_CONV - 1:GDN_CONV, :]
    for d in range(1, GDN_CONV):
        xs = jnp.where(row >= d, pltpu.roll(x, d, 0), 0.0)
        y = y + xs * w[GDN_CONV - 1 - d:GDN_CONV - d, :]
    y = y * _sigmoid(y)
    sq = y * y
    lo = lane < GDN_HEAD_DIM
    ss0 = jnp.sum(jnp.where(lo, sq, 0.0), axis=1, keepdims=True)
    ss1 = jnp.sum(jnp.where(lo, 0.0, sq), axis=1, keepdims=True)
    inv = lax.rsqrt(jnp.where(lo, ss0, ss1) + RMS_EPS)
    nqb = BRANCH_WIDTH // LANES
    scale = jnp.where(c < nqb, GDN_HEAD_DIM ** -0.5, 1.0)
    o_ref[...] = jnp.where(c < 2 * nqb, y * inv * scale, y)


def _gdn_conv(proj, conv_w, B, S):
    T = B * S
    nb = 3 * BRANCH_WIDTH // LANES
    return pl.pallas_call(
        _gdn_conv_kernel, grid=(B, nb),
        in_specs=[pl.BlockSpec((S, LANES), lambda b, c: (b, C_GQKV // LANES + c)),
                  pl.BlockSpec((GDN_CONV, LANES), lambda b, c: (0, c))],
        out_specs=pl.BlockSpec((S, LANES), lambda b, c: (b, c)),
        out_shape=jax.ShapeDtypeStruct((T, 3 * BRANCH_WIDTH), F32),
        compiler_params=_cp("parallel", "parallel"))(proj, conv_w)


def _gdn_chunk_kernel(q_ref, k_ref, v_ref, sm_ref, alog_ref, dtb_ref,
                      u_ref, w_ref, qg_ref, qk_ref, kdt_ref, dl_ref):
    C = GDN_KCHUNK
    nh = GDN_HEADS_PER_STEP
    h0 = pl.program_id(1) * nh
    lane = lax.broadcasted_iota(jnp.int32, (C, LANES), 1)
    row = lax.broadcasted_iota(jnp.int32, (C, LANES), 0)
    lo = lane < GDN_HEAD_DIM
    tril = row >= lane
    strict = row > lane
    eye = jnp.where(row == lane, 1.0, 0.0)
    ltri = tril.astype(BF16)

    def pick(ref, k):
        x = ref[:, (k // 2) * LANES:(k // 2 + 1) * LANES]
        return jnp.where(lo, pltpu.roll(x, LANES // 2, 1) if k % 2 else x, 0.0)

    sm = sm_ref[...]
    a = sm + dtb_ref[...]
    softplus = jnp.maximum(a, 0.0) + jnp.log1p(jnp.exp(-jnp.abs(a)))
    garr = -jnp.exp(alog_ref[...]) * softplus
    sig = _sigmoid(sm)
    kh, kb, gc, decay, A = [], [], [], [], []
    for k in range(nh):
        g_col = jnp.sum(jnp.where(lane == h0 + k, garr, 0.0), axis=1, keepdims=True)
        beta = jnp.sum(jnp.where(lane == HEADS + h0 + k, sig, 0.0), axis=1, keepdims=True)
        g1 = jnp.broadcast_to(g_col, (C, LANES))
        gsum = None
        for _ in range(3):
            gb = g1.astype(BF16)
            part = jnp.dot(ltri, gb, preferred_element_type=F32)
            gsum = part if gsum is None else gsum + part
            g1 = g1 - gb.astype(F32)
        gc.append(gsum)
        decay.append(jnp.where(tril, jnp.exp(jnp.where(tril, gsum - gsum.T, 0.0)), 0.0))
        kh.append(pick(k_ref, k))
        kb.append(kh[k] * beta)
        vb = pick(v_ref, k) * beta
        A.append(jnp.where(strict, _dot_nt(kb[k], kh[k]) * decay[k], 0.0))
        eg = jnp.exp(gsum)
        qh = pick(q_ref, k)
        qk_ref[k, 0] = jnp.where(tril, _dot_nt(qh, kh[k]) * decay[k], 0.0)
        qg_ref[k, 0] = qh * eg
        glast = gsum[C - 1:C, :]
        kdt_ref[k, 0] = (kh[k] * jnp.exp(glast - gsum)).T
        dl_ref[k, 0] = jnp.broadcast_to(jnp.exp(glast), (SUBLANES, LANES))
        u_ref[k, 0] = vb
        w_ref[k, 0] = kb[k] * eg
    def joiner(level):
        same = (row >> (level + 1)) == (lane >> (level + 1))
        return same & (((row >> level) & 1) == 1) & (((lane >> level) & 1) == 0)

    P = [eye - jnp.where(joiner(0), A[k], 0.0) for k in range(nh)]
    for level in range(1, int(math.log2(C))):
        msk = joiner(level)
        T1 = [_dot(P[k], jnp.where(msk, A[k], 0.0)) for k in range(nh)]
        P = [P[k] - _dot(T1[k], P[k]) for k in range(nh)]
    for k in range(nh):
        u_ref[k, 0] = _dot(P[k], u_ref[k, 0])
        w_ref[k, 0] = _dot(P[k], w_ref[k, 0])


def _gdn_chunks(qkv, small, alog_row, dtb_row, B, S):
    C = GDN_KCHUNK
    n = S // C
    nh = GDN_HEADS_PER_STEP
    ng = HEADS // nh
    wq = nh // 2 * LANES
    nqb = BRANCH_WIDTH // wq
    big = jax.ShapeDtypeStruct((B * HEADS, n, C, LANES), F32)
    ospec = pl.BlockSpec((nh, 1, C, LANES), lambda b, g, c: (b * ng + g, c, 0, 0))
    par = pl.BlockSpec((1, LANES), lambda b, g, c: (0, 0))
    outs = pl.pallas_call(
        _gdn_chunk_kernel, grid=(B, ng, n),
        in_specs=[pl.BlockSpec((C, wq), lambda b, g, c: (b * n + c, g)),
                  pl.BlockSpec((C, wq), lambda b, g, c: (b * n + c, nqb + g)),
                  pl.BlockSpec((C, wq), lambda b, g, c: (b * n + c, 2 * nqb + g)),
                  pl.BlockSpec((C, LANES), lambda b, g, c: (b * n + c, 0)), par, par],
        out_specs=[ospec] * 5 + [pl.BlockSpec((nh, 1, SUBLANES, LANES), lambda b, g, c: (b * ng + g, c, 0, 0))],
        out_shape=[big] * 5 + [jax.ShapeDtypeStruct((B * HEADS, n, SUBLANES, LANES), F32)],
        compiler_params=_cp("parallel", "parallel", "parallel"))(qkv, qkv, qkv, small, alog_row, dtb_row)
    return [o.reshape(-1, LANES) for o in outs]


def _gdn_scan_kernel(u_ref, w_ref, qg_ref, qk_ref, kdt_ref, dl_ref, z_ref, g_ref, o_ref, o_sc, *, n):
    C = GDN_KCHUNK
    S = n * C

    def step(c, states):
        out = []
        for k, state in enumerate(states):
            r = pl.ds(pl.multiple_of(k * S + c * C, C), C)
            sb = state.astype(BF16)
            v_new = u_ref[r, :] - _dot(w_ref[r, :], sb)
            vb = v_new.astype(BF16)
            o_sc[r, :] = _dot(qg_ref[r, :], sb) + _dot(qk_ref[r, :], vb)
            dl = dl_ref[pl.ds(pl.multiple_of((k * n + c) * SUBLANES, SUBLANES), 1), :]
            out.append(state * dl + _dot(kdt_ref[r, :], vb))
        return tuple(out)

    lax.fori_loop(0, n, step, tuple(jnp.zeros((LANES, LANES), F32) for _ in range(2)))
    lane = lax.broadcasted_iota(jnp.int32, o_ref.shape, 1)

    def nrm(o):
        ms = jnp.sum(o * o, axis=1, keepdims=True) * (1.0 / GDN_HEAD_DIM)
        return o * lax.rsqrt(ms + RMS_EPS)

    nn = jnp.where(lane < GDN_HEAD_DIM, nrm(o_sc[0:S, :]), pltpu.roll(nrm(o_sc[S:2 * S, :]), LANES // 2, 1))
    z = z_ref[...].astype(F32)
    o_ref[...] = nn * g_ref[...] * (z * _sigmoid(z))


def _gdn_scan(parts, proj, g_row, B, S):
    n = S // GDN_KCHUNK
    npair = HEADS // 2
    seq = pl.BlockSpec((2 * S, LANES), lambda i: (i, 0))
    return pl.pallas_call(
        functools.partial(_gdn_scan_kernel, n=n), grid=(B * npair,),
        in_specs=[seq] * 5 + [pl.BlockSpec((2 * n * SUBLANES, LANES), lambda i: (i, 0)),
                              pl.BlockSpec((S, LANES), lambda i: (i // npair, C_GZ // LANES + i % npair)),
                              pl.BlockSpec((1, LANES), lambda i: (0, 0))],
        out_specs=pl.BlockSpec((S, LANES), lambda i: (i // npair, i % npair)),
        out_shape=jax.ShapeDtypeStruct((B * S, BRANCH_WIDTH), F32),
        scratch_shapes=[pltpu.VMEM((2 * S, LANES), F32)],
        compiler_params=_cp("parallel"))(*parts, proj, g_row)


def _merge_out_kernel(g0_ref, g1_ref, g2_ref, gb_ref, a_ref, b_ref, c_ref, wb_ref, wo_ref, h_ref, lg_ref, lb_ref,
                      o_ref, ot_ref):
    acc = None
    for n, (gl, br) in enumerate(((g0_ref, a_ref), (g1_ref, b_ref), (g2_ref, c_ref))):
        y = _sigmoid(gl[...].astype(F32) + gb_ref[n:n + 1, :]) * _dot(br[...], wb_ref[n])
        acc = y if acc is None else acc + y
    y = DEEPNORM_ALPHA * h_ref[...] + _dot(acc, wo_ref[...])
    y = _ln_rows(y, lg_ref[...], lb_ref[...])
    o_ref[...] = y
    ot_ref[...] = y.T.astype(BF16)


def _merge_out_ln(proj, gate_bias, o_a, o_b, o_c, w_branch, w_out, h, g, b, tm=256):
    T, D = h.shape
    row = lambda i: (i, 0)
    gspec = lambda n: pl.BlockSpec((tm, D), lambda i: (i, n))
    bspec = pl.BlockSpec((tm, BRANCH_WIDTH), row)
    vec = pl.BlockSpec((1, D), lambda i: (0, 0))
    return pl.pallas_call(
        _merge_out_kernel, grid=(T // tm,),
        in_specs=[gspec(0), gspec(1), gspec(2), pl.BlockSpec((3, D), lambda i: (0, 0)), bspec, bspec, bspec,
                  pl.BlockSpec((3, BRANCH_WIDTH, D), lambda i: (0, 0, 0)), pl.BlockSpec((D, D), lambda i: (0, 0)),
                  pl.BlockSpec((tm, D), row), vec, vec],
        out_specs=[pl.BlockSpec((tm, D), row), pl.BlockSpec((D, tm), lambda i: (0, i))],
        out_shape=[jax.ShapeDtypeStruct((T, D), F32), jax.ShapeDtypeStruct((D, T), BF16)],
        compiler_params=_cp("parallel"))(proj, proj, proj, gate_bias, o_a, o_b, o_c, w_branch, w_out, h,
                                         g.reshape(1, D), b.reshape(1, D))


def _extract_top(curs, n, on_max):
    curs = list(curs)
    for k in range(n):
        for i, cur in enumerate(curs):
            m = jnp.max(cur, axis=0, keepdims=True)
            on_max(i, k, m)
            if k + 1 < n:
                curs[i] = jnp.where(cur >= m, -jnp.inf, cur)


def _peer_score_kernel(qt_ref, keys_ref, s2_ref, e2_ref, c_ref, e1_ref, top_sc, cand_sc):
    K = PEER_TOPK
    tt = qt_ref.shape[1]
    nh = PEER_SCORE_HEADS_PER_TRIP

    def heads(hp, carry):
        s = []
        for i in range(2 * nh):
            r = pl.multiple_of((hp * nh * 2 + i) * PEER_HALF, PEER_HALF)
            s.append(_dot(keys_ref[hp * nh * 2 + i], qt_ref[pl.ds(r, PEER_HALF), :]))

        def put(i, k, m):
            top_sc[i, k:k + 1, :] = m
        _extract_top(s, K + 1, put)
        r8 = lax.broadcasted_iota(jnp.int32, (SUBLANES, tt), 0)
        v1max, v2max = [], []
        for d in range(nh):
            v1a, v1x = top_sc[2 * d, 0:K, :], top_sc[2 * d, K:K + 1, :]
            v2a, v2x = top_sc[2 * d + 1, 0:K, :], top_sc[2 * d + 1, K:K + 1, :]
            v1max.append(v1a[0:1])
            v2max.append(v2a[0:1])
            cand_sc[d, 0:K, :] = v1a + v2a[0:1]
            for b in range(1, SUBLANES):
                cand_sc[d, K + SUBLANES * (b - 1):K + SUBLANES * b, :] = v1a[0:SUBLANES] + v2a[b:b + 1]
            base = K + SUBLANES * (SUBLANES - 1)
            cand_sc[d, base:base + SUBLANES, :] = v1a[0:1] + v2a[SUBLANES:K]
            cand_sc[d, base + SUBLANES:base + 2 * SUBLANES, :] = jnp.where(
                r8 == 0, v1x + v2a[0:1], jnp.where(r8 == 1, v1a[0:1] + v2x, -jnp.inf))
        st = [{"z": jnp.zeros((1, tt), F32)} for _ in range(nh)]

        def acc(d, k, m):
            if k == 0:
                st[d]["top"] = m
            if k < K:
                st[d]["z"] = st[d]["z"] + jnp.exp(m - st[d]["top"])
            if k == K - 1:
                st[d]["t16"] = m
            if k == K:
                st[d]["t17"] = m
        _extract_top([cand_sc[d] for d in range(nh)], K + 1, acc)
        for d in range(nh):
            tau = 0.5 * (st[d]["t16"] + st[d]["t17"])
            ro = pl.ds(pl.multiple_of((hp * nh + d) * N_KEYS, N_KEYS), N_KEYS)
            s2_ref[ro, :] = s[2 * d + 1]
            e2_ref[ro, :] = jnp.exp(s[2 * d + 1] - v2max[d]) / st[d]["z"]
            c_ref[ro, :] = tau - s[2 * d]
            e1_ref[ro, :] = jnp.exp(s[2 * d] - v1max[d])
        return carry

    lax.fori_loop(0, HEADS // nh, heads, 0)


def _peer_scores(qt, keys):
    T = qt.shape[1]
    tt = PEER_SCORE_TT
    R = HEADS * N_KEYS
    ncand = PEER_TOPK + SUBLANES * (SUBLANES + 1)
    ospec = pl.BlockSpec((R, tt), lambda i: (0, i))
    return pl.pallas_call(
        _peer_score_kernel, grid=(T // tt,),
        in_specs=[pl.BlockSpec((2 * R, tt), lambda i: (0, i)),
                  pl.BlockSpec((2 * HEADS, N_KEYS, PEER_HALF), lambda i: (0, 0, 0))],
        out_specs=[ospec] * 4,
        out_shape=[jax.ShapeDtypeStruct((R, T), F32)] * 4,
        scratch_shapes=[pltpu.VMEM((2 * PEER_SCORE_HEADS_PER_TRIP, 3 * SUBLANES, tt), F32),
                        pltpu.VMEM((PEER_SCORE_HEADS_PER_TRIP, ncand, tt), F32)],
        compiler_params=_cp("parallel"))(qt, keys)


def _peer_expert_kernel(ht_ref, u_ref, vtp_ref, vtl_ref, s2_ref, e2_ref, c_ref, e1_ref, h_ref, g_ref, b_ref,
                        o_ref, acc_sc, act_sc, hw_sc, bc_sc):
    j = pl.program_id(1)
    _, eb, tt = hw_sc.shape
    ng = eb // N_KEYS
    assert ng == SUBLANES
    slot = j % 2

    @pl.when(j == 0)
    def _():
        acc_sc[...] = jnp.zeros(acc_sc.shape, F32)
        hw_sc[1] = jnp.zeros(hw_sc.shape[1:], BF16)

    mxu_w = 2 * LANES
    for half in range(tt // mxu_w):
        hl = slice(half * mxu_w, (half + 1) * mxu_w)
        a = jnp.dot(u_ref[...], ht_ref[:, hl], preferred_element_type=F32)
        act_sc[:, hl] = 0.5 * a * (1.0 + lax.erf(a * (2.0 ** -0.5)))
        acc_sc[:, hl] += jnp.dot(vtp_ref[...], hw_sc[1 - slot, :, hl], preferred_element_type=F32)
        for lc in range(half * (mxu_w // LANES), (half + 1) * (mxu_w // LANES)):
            ls = slice(lc * LANES, (lc + 1) * LANES)
            for h in range(HEADS):
                r8 = pl.ds(pl.multiple_of(h * N_KEYS + j * ng, ng), ng)
                thr8, e18 = c_ref[r8, ls], e1_ref[r8, ls]
                for g in range(ng):
                    bc_sc[0, h * ng + g] = jnp.broadcast_to(thr8[g:g + 1], (SUBLANES, LANES))
                    bc_sc[1, h * ng + g] = jnp.broadcast_to(e18[g:g + 1], (SUBLANES, LANES))
            for g in range(ng):
                wsum = jnp.zeros((N_KEYS // SUBLANES, SUBLANES, LANES), F32)
                for h in range(HEADS):
                    hr = slice(h * N_KEYS, (h + 1) * N_KEYS)
                    s2 = s2_ref[hr, ls].reshape(N_KEYS // SUBLANES, SUBLANES, LANES)
                    e2 = e2_ref[hr, ls].reshape(N_KEYS // SUBLANES, SUBLANES, LANES)
                    wsum = wsum + jnp.where(s2 >= bc_sc[0, h * ng + g][None], e2 * bc_sc[1, h * ng + g][None], 0.0)
                gs = slice(g * N_KEYS, (g + 1) * N_KEYS)
                hw_sc[slot, gs, ls] = (wsum.reshape(N_KEYS, LANES) * act_sc[gs, ls]).astype(BF16)

    @pl.when(j == pl.num_programs(1) - 1)
    def _():
        acc = acc_sc[...] + jnp.dot(vtl_ref[...], hw_sc[slot], preferred_element_type=F32)
        y = DEEPNORM_ALPHA * h_ref[...] + acc.T
        o_ref[...] = _ln_rows(y, g_ref[...], b_ref[...])


def _peer_experts(ht, u, vt, s2, e2, c, e1, h, g, b):
    T, D = h.shape
    tt, eb = PEER_TT, PEER_EB
    R = HEADS * N_KEYS
    nb = N_EXPERTS // eb
    assert nb % 2 == 0
    sspec = pl.BlockSpec((R, tt), lambda i, j: (0, i))
    vec = pl.BlockSpec((1, D), lambda i, j: (0, 0))
    return pl.pallas_call(
        _peer_expert_kernel, grid=(T // tt, nb),
        in_specs=[pl.BlockSpec((D, tt), lambda i, j: (0, i)),
                  pl.BlockSpec((eb, D), lambda i, j: (j, 0)),
                  pl.BlockSpec((D, eb), lambda i, j: (0, jnp.maximum(j - 1, 0))),
                  pl.BlockSpec((D, eb), lambda i, j: (0, nb - 1)),
                  sspec, sspec, sspec, sspec,
                  pl.BlockSpec((tt, D), lambda i, j: (i, 0)), vec, vec],
        out_specs=pl.BlockSpec((tt, D), lambda i, j: (i, 0)),
        out_shape=jax.ShapeDtypeStruct((T, D), F32),
        scratch_shapes=[pltpu.VMEM((D, tt), F32), pltpu.VMEM((eb, tt), F32), pltpu.VMEM((2, eb, tt), BF16),
                        pltpu.VMEM((2, HEADS * SUBLANES, SUBLANES, LANES), F32)],
        compiler_params=_cp("parallel", "arbitrary"))(ht, u, vt, vt, s2, e2, c, e1, h, g.reshape(1, D),
                                                      b.reshape(1, D))


def _rot_half(x, r):
    return jnp.concatenate([-x[..., r:2 * r], x[..., :r]], axis=-1)


def _prep_w_in(w):
    D = w.shape[0]
    splits = (MLA_Q_RANK, MLA_KV_RANK, MLA_ROPE, 3 * BRANCH_WIDTH, 3 * BRANCH_WIDTH, BRANCH_WIDTH, HEADS, HEADS,
              3 * D)
    o = np.cumsum((0,) + splits)
    cq, ckv, kr, mqkv, gqkv, gz, ga, gb, gl = [w[:, o[i]:o[i + 1]] for i in range(9)]
    mq, mk, mv = jnp.split(mqkv, 3, axis=1)

    def slab(m):
        m = m.reshape(D, HEADS, MOBA_HEAD_DIM)
        pad = jnp.zeros((D, HEADS, LANES - MOBA_HEAD_DIM - MOBA_ROT_DIM), w.dtype)
        return jnp.concatenate([m, _rot_half(m, MOBA_ROT_DIM // 2), pad], axis=-1).reshape(D, HEADS * LANES)

    small = jnp.concatenate([ga, gb, jnp.zeros((D, LANES // 2 - 2 * HEADS), w.dtype), kr,
                             _rot_half(kr, MLA_ROPE // 2)], axis=1)
    out = jnp.concatenate([gl, cq, ckv, slab(mq), slab(mk), mv, gqkv, gz], axis=1)
    assert out.shape[1] == C_TOTAL and C_TOTAL % PROJ_TN == 0
    return out.astype(BF16), small.astype(BF16)


def _prep_w_uq(w):
    R = w.shape[0]
    w = w.reshape(R, HEADS, MLA_NOPE + MLA_ROPE)
    rope = w[..., MLA_NOPE:]
    return jnp.concatenate([w, _rot_half(rope, MLA_ROPE // 2)], axis=-1).reshape(R, HEADS * LANES).astype(BF16)


def _prep_w_ukv(w):
    R = w.shape[0]
    w = w.reshape(R, HEADS, MLA_NOPE + MLA_V)
    k = jnp.concatenate([w[..., :MLA_NOPE], jnp.zeros((R, HEADS, LANES - MLA_NOPE), w.dtype)], axis=-1)
    return jnp.concatenate([k.reshape(R, HEADS * LANES), w[..., MLA_NOPE:].reshape(R, HEADS * MLA_V)],
                           axis=1).astype(BF16)


def _rope_tables(positions):
    pos = positions.reshape(-1).astype(F32)[:, None]
    T = pos.shape[0]

    def cs(rot):
        inv = ROPE_THETA ** (-jnp.arange(0, rot, 2, dtype=F32) / rot)
        ang = pos * inv
        return jnp.cos(ang), jnp.sin(ang)

    ca, sa = cs(MLA_ROPE)
    cb, sb = cs(MOBA_ROT_DIM)
    one = lambda n: jnp.ones((T, n), F32)
    zero = lambda n: jnp.zeros((T, n), F32)
    sc_a = (MLA_NOPE + MLA_ROPE) ** -0.5 * math.log2(math.e)
    sc_b = MOBA_HEAD_DIM ** -0.5 * math.log2(math.e)
    cat = lambda *xs: jnp.concatenate(xs, axis=1)
    mla = (cat(one(MLA_NOPE), ca, ca, zero(32)) * sc_a, cat(zero(MLA_NOPE), sa, sa, zero(32)) * sc_a,
           cat(zero(MLA_NOPE), ca, ca, zero(32)), cat(zero(MLA_NOPE), sa, sa, zero(32)))
    cm = cat(cb, cb, one(MOBA_HEAD_DIM - MOBA_ROT_DIM), zero(LANES - MOBA_HEAD_DIM))
    sm = cat(sb, sb, zero(LANES - MOBA_ROT_DIM))
    moba = (cm * sc_b, sm * sc_b, cm, sm)
    return mla, moba


def _lane_row(v):
    return jnp.concatenate([v.astype(F32), jnp.zeros((LANES - v.shape[0],), F32)]).reshape(1, LANES)


def kernel(x, positions, ln_in_g, ln_in_b, w_in, mla_q_norm, mla_kv_norm, mla_w_uq, mla_w_ukv, gdn_conv_w, gdn_A_log, gdn_dt_bias, gdn_o_norm, gate_bias, w_branch, w_out, ln1_g, ln1_b, peer_w_q, peer_sub_keys, peer_u, peer_v, ln2_g, ln2_b):
    B, S, D = x.shape
    T = B * S
    assert S % ATT_TQ == 0 and S % GDN_KCHUNK == 0 and T % PEER_TT == 0 and ATT_TQ == MOBA_BLOCK
    mla_tabs, moba_tabs = _rope_tables(positions)
    h = _layer_norm(x.reshape(T, D), ln_in_g, ln_in_b)
    for l in range(DEPTH):
        w_main, w_small = _prep_w_in(w_in[l])
        proj = _mm(h, w_main, tm=512, tn=PROJ_TN, out_dtype=BF16)
        small = _mm(h, w_small, tm=512, tn=LANES)
        qa, ka, va = _mla_qkv(proj, small, mla_q_norm[l], mla_kv_norm[l], _prep_w_uq(mla_w_uq[l]),
                              _prep_w_ukv(mla_w_ukv[l]), mla_tabs)
        o_a = _attention(qa, ka, va, 0, B, S)
        qm, km, kmean = _moba_prep(proj, moba_tabs)
        o_b = _attention(qm, km, proj, C_MV, B, S, kmean=kmean.reshape(B, S // MOBA_BLOCK, HEADS * LANES))
        qkv = _gdn_conv(proj, gdn_conv_w[l], B, S)
        parts = _gdn_chunks(qkv, small, _lane_row(gdn_A_log[l]), _lane_row(gdn_dt_bias[l]), B, S)
        o_c = _gdn_scan(parts, proj, jnp.tile(gdn_o_norm[l], 2).reshape(1, LANES), B, S)
        h, ht = _merge_out_ln(proj, gate_bias[l], o_a, o_b, o_c, w_branch[l].astype(BF16), w_out[l].astype(BF16),
                              h, ln1_g[l], ln1_b[l])
        qt = _mm(peer_w_q[l].T.astype(BF16), ht, tm=1024, tn=1024, out_dtype=BF16)
        keys = peer_sub_keys[l].reshape(2 * HEADS, N_KEYS, PEER_HALF).astype(BF16)
        s2, e2, c, e1 = _peer_scores(qt, keys)
        h = _peer_experts(ht, peer_u[l].astype(BF16), peer_v[l].astype(BF16).T, s2, e2, c, e1, h, ln2_g[l], ln2_b[l])
    return h.reshape(B, S, D)
```

```python
import functools
import math

import numpy as np
import jax
import jax.numpy as jnp
from jax import lax
from jax.experimental import pallas as pl
from jax.experimental.pallas import tpu as pltpu

F32 = jnp.float32
BF16 = jnp.bfloat16
HIGHEST = lax.Precision.HIGHEST

DEPTH = 2
ROPE_THETA = 500000.0
NEG_INF = -1e30
LN_EPS = 1e-5
RMS_EPS = 1e-6
DEEPNORM_ALPHA = (2 * DEPTH) ** 0.25
HEADS = 8
MLA_NOPE, MLA_ROPE, MLA_V = 64, 32, 64
MLA_Q_RANK, MLA_KV_RANK = 768, 256
MOBA_HEAD_DIM, MOBA_ROT_DIM, MOBA_BLOCK, MOBA_TOPK = 64, 16, 256, 3
GDN_HEAD_DIM, GDN_CONV = 64, 4
BRANCH_WIDTH = 512
N_KEYS, PEER_TOPK, PEER_HALF = 128, 16, 128
N_EXPERTS = N_KEYS * N_KEYS

LANES = 128
SUBLANES = 8
VMEM_LIMIT = 56 * 1024 * 1024

GDN_KCHUNK = 128
GDN_HEADS_PER_STEP = 4
ATT_TQ = 256
PEER_TT = 512
PEER_EB = 1024
PEER_SCORE_TT = 256
PEER_SCORE_HEADS_PER_TRIP = 2

C_GATE, C_CQ, C_CKV, C_MQ, C_MK, C_MV, C_GQKV, C_GZ, C_TOTAL = (
    0, 3072, 3840, 4096, 5120, 6144, 6656, 8192, 8704)
PROJ_TN = 4352


def _cp(*sem):
    return pltpu.CompilerParams(dimension_semantics=sem, vmem_limit_bytes=VMEM_LIMIT)


def _dot(a, b):
    return jnp.dot(a.astype(BF16), b.astype(BF16), preferred_element_type=F32)


def _dot_nt(a, b):
    return lax.dot_general(a.astype(BF16), b.astype(BF16), (((1,), (1,)), ((), ())),
                           preferred_element_type=F32)


def _ln_rows(y, g, b):
    mu = jnp.mean(y, axis=-1, keepdims=True)
    d = y - mu
    var = jnp.mean(d * d, axis=-1, keepdims=True)
    return d * lax.rsqrt(var + LN_EPS) * g + b


def _sigmoid(x):
    return 1.0 / (1.0 + jnp.exp(-x))


def _ln_kernel(x_ref, g_ref, b_ref, o_ref):
    o_ref[...] = _ln_rows(x_ref[...], g_ref[...], b_ref[...])


def _layer_norm(x, g, b, tm=512):
    T, D = x.shape
    return pl.pallas_call(
        _ln_kernel, grid=(T // tm,),
        in_specs=[pl.BlockSpec((tm, D), lambda i: (i, 0)),
                  pl.BlockSpec((1, D), lambda i: (0, 0)),
                  pl.BlockSpec((1, D), lambda i: (0, 0))],
        out_specs=pl.BlockSpec((tm, D), lambda i: (i, 0)),
        out_shape=jax.ShapeDtypeStruct((T, D), F32),
        compiler_params=_cp("parallel"))(x, g.reshape(1, D), b.reshape(1, D))


def _mm_kernel(x_ref, w_ref, o_ref):
    o_ref[...] = _dot(x_ref[...], w_ref[...]).astype(o_ref.dtype)


def _mm(x, w, tm, tn, out_dtype=F32):
    M, K = x.shape
    N = w.shape[1]
    return pl.pallas_call(
        _mm_kernel, grid=(M // tm, N // tn),
        in_specs=[pl.BlockSpec((tm, K), lambda i, j: (i, 0)),
                  pl.BlockSpec((K, tn), lambda i, j: (0, j))],
        out_specs=pl.BlockSpec((tm, tn), lambda i, j: (i, j)),
        out_shape=jax.ShapeDtypeStruct((M, N), out_dtype),
        compiler_params=_cp("parallel", "parallel"))(x, w)


def _rms_rows(x, g):
    return x * lax.rsqrt(jnp.mean(x * x, axis=-1, keepdims=True) + RMS_EPS) * g


def _head_slabs(x):
    return [x[:, h * LANES:(h + 1) * LANES] for h in range(HEADS)]


def _mla_qkv_kernel(cq_ref, ckv_ref, gq_ref, gkv_ref, wq_ref, wkv_ref, sm_ref, cq_tab, sq_tab, ck_tab, sk_tab,
                    qo_ref, ko_ref, vo_ref):
    W = HEADS * LANES
    q = _dot(_rms_rows(cq_ref[...].astype(F32), gq_ref[...]), wq_ref[...])
    kv = _dot(_rms_rows(ckv_ref[...].astype(F32), gkv_ref[...]), wkv_ref[...])
    vo_ref[...] = kv[:, W:].astype(BF16)
    cq, sq = cq_tab[...], sq_tab[...]
    sm = sm_ref[...]
    kr = sm * ck_tab[...] + pltpu.roll(sm, LANES - MLA_ROPE, 1) * sk_tab[...]
    for h in range(HEADS):
        sl = slice(h * LANES, (h + 1) * LANES)
        qs = q[:, sl]
        qo_ref[:, sl] = (qs * cq + pltpu.roll(qs, LANES - MLA_ROPE, 1) * sq).astype(BF16)
        ko_ref[:, sl] = (kv[:, sl] + kr).astype(BF16)


def _mla_qkv(proj, small, gq, gkv, wq, wkv, tabs, tm=512):
    T = proj.shape[0]
    W = HEADS * LANES
    row = lambda i: (i, 0)
    tab = pl.BlockSpec((tm, LANES), row)
    full = lambda a: pl.BlockSpec(a.shape, lambda i: (0, 0))
    gq, gkv = gq.reshape(1, -1), gkv.reshape(1, -1)
    return pl.pallas_call(
        _mla_qkv_kernel, grid=(T // tm,),
        in_specs=[pl.BlockSpec((tm, MLA_Q_RANK), lambda i: (i, C_CQ // MLA_Q_RANK)),
                  pl.BlockSpec((tm, MLA_KV_RANK), lambda i: (i, C_CKV // MLA_KV_RANK)),
                  full(gq), full(gkv), full(wq), full(wkv), tab, tab, tab, tab, tab],
        out_specs=[pl.BlockSpec((tm, W), row), pl.BlockSpec((tm, W), row), pl.BlockSpec((tm, BRANCH_WIDTH), row)],
        out_shape=[jax.ShapeDtypeStruct((T, W), BF16)] * 2 + [jax.ShapeDtypeStruct((T, BRANCH_WIDTH), BF16)],
        compiler_params=_cp("parallel"))(proj, proj, gq, gkv, wq, wkv, small, *tabs)


def _moba_prep_kernel(q_ref, k_ref, cq_ref, sq_ref, ck_ref, sk_ref, qo_ref, ko_ref, km_ref):
    cq, sq, ck, sk = cq_ref[...], sq_ref[...], ck_ref[...], sk_ref[...]
    q, k = q_ref[...].astype(F32), k_ref[...].astype(F32)
    for h, (qs, ks) in enumerate(zip(_head_slabs(q), _head_slabs(k))):
        sl = slice(h * LANES, (h + 1) * LANES)
        qo_ref[:, sl] = qs * cq + pltpu.roll(qs, LANES // 2, 1) * sq
        kk = ks * ck + pltpu.roll(ks, LANES // 2, 1) * sk
        ko_ref[:, sl] = kk.astype(BF16)
        km_ref[0, :, sl] = jnp.mean(kk, axis=0, keepdims=True)


def _moba_prep(proj, tabs):
    T = proj.shape[0]
    W = HEADS * LANES
    tm = MOBA_BLOCK
    row = lambda i: (i, 0)
    tab = pl.BlockSpec((tm, LANES), row)
    return pl.pallas_call(
        _moba_prep_kernel, grid=(T // tm,),
        in_specs=[pl.BlockSpec((tm, W), lambda i: (i, C_MQ // W)),
                  pl.BlockSpec((tm, W), lambda i: (i, C_MK // W)), tab, tab, tab, tab],
        out_specs=[pl.BlockSpec((tm, W), row), pl.BlockSpec((tm, W), row),
                   pl.BlockSpec((1, 1, W), lambda i: (i, 0, 0))],
        out_shape=[jax.ShapeDtypeStruct((T, W), F32), jax.ShapeDtypeStruct((T, W), BF16),
                   jax.ShapeDtypeStruct((T // tm, 1, W), F32)],
        compiler_params=_cp("parallel"))(proj, proj, *tabs)


def _attn_kernel(*refs, moba, nk):
    if moba:
        q_ref, k_ref, v_ref, km_ref, o_ref, vt_sc, qb_sc, m_sc, l_sc, al_sc, acc_sc, s_sc, p_sc, bias_sc = refs
    else:
        q_ref, k_ref, v_ref, o_ref, vt_sc, m_sc, l_sc, al_sc, acc_sc, s_sc, p_sc = refs
    tq = tk = ATT_TQ
    dv = BRANCH_WIDTH // HEADS
    qi = pl.program_id(1)
    hslab = [slice(h * LANES, (h + 1) * LANES) for h in range(HEADS)]

    @pl.when(qi == 0)
    def _():
        for j in range(nk):
            for c in range(tk // LANES):
                for g in range(BRANCH_WIDTH // LANES):
                    vt_sc[j, g * LANES:(g + 1) * LANES, c * LANES:(c + 1) * LANES] = (
                        v_ref[j * tk + c * LANES:j * tk + (c + 1) * LANES,
                              g * LANES:(g + 1) * LANES].astype(F32).T.astype(BF16))

    if moba:
        blk = lax.broadcasted_iota(jnp.int32, (SUBLANES, tq), 0)
        for h in range(HEADS):
            qf = q_ref[:, hslab[h]]
            qb_sc[:, hslab[h]] = qf.astype(BF16)
            km = km_ref[0, :, hslab[h]]
            if nk < SUBLANES:
                km = jnp.concatenate([km, jnp.zeros((SUBLANES - nk, LANES), F32)], axis=0)
            gate = lax.dot_general(km, qf, (((1,), (1,)), ((), ())), precision=HIGHEST,
                                   preferred_element_type=F32)
            gate = jnp.where(blk < qi, gate, -jnp.inf)
            for n in range(nk):
                gn = gate[n:n + 1, :]
                beats = jnp.where(gate > gn, 1.0, jnp.where((gate == gn) & (blk < n), 1.0, 0.0))
                cnt = jnp.sum(beats, axis=0, keepdims=True)
                bias_sc[h, n] = jnp.where((cnt < MOBA_TOPK) & (n < qi), 0.0, NEG_INF)
    qsrc = qb_sc if moba else q_ref
    m_sc[...] = jnp.full(m_sc.shape, NEG_INF, F32)
    l_sc[...] = jnp.zeros(l_sc.shape, F32)
    acc_sc[...] = jnp.zeros(acc_sc.shape, F32)

    def block(j, diag):
        rows = pl.ds(pl.multiple_of(j * tk, tk), tk)
        for h in range(HEADS):
            s = _dot_nt(k_ref[rows, hslab[h]], qsrc[:, hslab[h]])
            if diag:
                keyi = lax.broadcasted_iota(jnp.int32, (tk, tq), 0)
                qryi = lax.broadcasted_iota(jnp.int32, (tk, tq), 1)
                s = jnp.where(keyi <= qryi, s, NEG_INF)
            elif moba:
                s = s + bias_sc[h, j]
            s_sc[h] = s
        for h in range(HEADS):
            s = s_sc[h]
            m_prev = m_sc[h]
            m_new = jnp.maximum(m_prev, jnp.max(s, axis=0, keepdims=True))
            alpha = jnp.exp2(m_prev - m_new)
            p = jnp.exp2(s - m_new)
            l_sc[h] = alpha * l_sc[h] + jnp.sum(p, axis=0, keepdims=True)
            p_sc[h] = p.astype(BF16)
            al_sc[h] = alpha
            m_sc[h] = m_new
        for h in range(HEADS):
            acc_sc[h] = al_sc[h] * acc_sc[h] + jnp.dot(vt_sc[j, h * dv:(h + 1) * dv, :], p_sc[h],
                                                       preferred_element_type=F32)

    def past(j, carry):
        block(j, False)
        return carry
    lax.fori_loop(0, qi, past, 0)
    block(qi, True)
    ot = jnp.concatenate([acc_sc[h] * (1.0 / l_sc[h]) for h in range(HEADS)], axis=0)
    o_ref[...] = ot.T


def _attention(q, k, v, v_col, B, S, kmean=None):
    T = B * S
    tq = ATT_TQ
    nq = S // tq
    W = HEADS * LANES
    dv = BRANCH_WIDTH // HEADS
    moba = kmean is not None
    assert nq <= SUBLANES and v_col % BRANCH_WIDTH == 0
    in_specs = [pl.BlockSpec((tq, W), lambda b, i: (b * nq + i, 0)),
                pl.BlockSpec((S, W), lambda b, i: (b, 0)),
                pl.BlockSpec((S, BRANCH_WIDTH), lambda b, i: (b, v_col // BRANCH_WIDTH))]
    scratch = [pltpu.VMEM((nq, BRANCH_WIDTH, tq), BF16)]
    if moba:
        scratch.append(pltpu.VMEM((tq, W), BF16))
    row = pltpu.VMEM((HEADS, 1, tq), F32)
    scratch += [row, row, row, pltpu.VMEM((HEADS, dv, tq), F32),
                pltpu.VMEM((HEADS, tq, tq), F32), pltpu.VMEM((HEADS, tq, tq), BF16)]
    args = [q, k, v]
    if moba:
        in_specs.append(pl.BlockSpec((1, nq, W), lambda b, i: (b, 0, 0)))
        scratch.append(pltpu.VMEM((HEADS, nq, 1, tq), F32))
        args.append(kmean)
    return pl.pallas_call(
        functools.partial(_attn_kernel, moba=moba, nk=nq),
        grid=(B, nq), in_specs=in_specs,
        out_specs=pl.BlockSpec((tq, BRANCH_WIDTH), lambda b, i: (b * nq + i, 0)),
        out_shape=jax.ShapeDtypeStruct((T, BRANCH_WIDTH), F32),
        scratch_shapes=scratch,
        compiler_params=_cp("parallel", "arbitrary"))(*args)


def _gdn_conv_kernel(x_ref, w_ref, o_ref):
    c = pl.program_id(1)
    x = x_ref[...].astype(F32)
    w = w_ref[...]
    S = x.shape[0]
    row = lax.broadcasted_iota(jnp.int32, (S, LANES), 0)
    lane = lax.broadcasted_iota(jnp.int32, (S, LANES), 1)
    y = x * w[GDN_CONV - 1:GDN_CONV, :]
    for d in range(1, GDN_CONV):
        xs = jnp.where(row >= d, pltpu.roll(x, d, 0), 0.0)
        y = y + xs * w[GDN_CONV - 1 - d:GDN_CONV - d, :]
    y = y * _sigmoid(y)
    sq = y * y
    lo = lane < GDN_HEAD_DIM
    ss0 = jnp.sum(jnp.where(lo, sq, 0.0), axis=1, keepdims=True)
    ss1 = jnp.sum(jnp.where(lo, 0.0, sq), axis=1, keepdims=True)
    inv = lax.rsqrt(jnp.where(lo, ss0, ss1) + RMS_EPS)
    nqb = BRANCH_WIDTH // LANES
    scale = jnp.where(c < nqb, GDN_HEAD_DIM ** -0.5, 1.0)
    o_ref[...] = jnp.where(c < 2 * nqb, y * inv * scale, y)


def _gdn_conv(proj, conv_w, B, S):
    T = B * S
    nb = 3 * BRANCH_WIDTH // LANES
    return pl.pallas_call(
        _gdn_conv_kernel, grid=(B, nb),
        in_specs=[pl.BlockSpec((S, LANES), lambda b, c: (b, C_GQKV // LANES + c)),
                  pl.BlockSpec((GDN_CONV, LANES), lambda b, c: (0, c))],
        out_specs=pl.BlockSpec((S, LANES), lambda b, c: (b, c)),
        out_shape=jax.ShapeDtypeStruct((T, 3 * BRANCH_WIDTH), F32),
        compiler_params=_cp("parallel", "parallel"))(proj, conv_w)


def _gdn_chunk_kernel(q_ref, k_ref, v_ref, sm_ref, alog_ref, dtb_ref,
                      u_ref, w_ref, qg_ref, qk_ref, kdt_ref, dl_ref):
    C = GDN_KCHUNK
    nh = GDN_HEADS_PER_STEP
    h0 = pl.program_id(1) * nh
    lane = lax.broadcasted_iota(jnp.int32, (C, LANES), 1)
    row = lax.broadcasted_iota(jnp.int32, (C, LANES), 0)
    lo = lane < GDN_HEAD_DIM
    tril = row >= lane
    strict = row > lane
    eye = jnp.where(row == lane, 1.0, 0.0)
    ltri = tril.astype(BF16)

    def pick(ref, k):
        x = ref[:, (k // 2) * LANES:(k // 2 + 1) * LANES]
        return jnp.where(lo, pltpu.roll(x, LANES // 2, 1) if k % 2 else x, 0.0)

    sm = sm_ref[...]
    a = sm + dtb_ref[...]
    softplus = jnp.maximum(a, 0.0) + jnp.log1p(jnp.exp(-jnp.abs(a)))
    garr = -jnp.exp(alog_ref[...]) * softplus
    sig = _sigmoid(sm)
    kh, kb, gc, decay, A = [], [], [], [], []
    for k in range(nh):
        g_col = jnp.sum(jnp.where(lane == h0 + k, garr, 0.0), axis=1, keepdims=True)
        beta = jnp.sum(jnp.where(lane == HEADS + h0 + k, sig, 0.0), axis=1, keepdims=True)
        g1 = jnp.broadcast_to(g_col, (C, LANES))
        gsum = None
        for _ in range(3):
            gb = g1.astype(BF16)
            part = jnp.dot(ltri, gb, preferred_element_type=F32)
            gsum = part if gsum is None else gsum + part
            g1 = g1 - gb.astype(F32)
        gc.append(gsum)
        decay.append(jnp.where(tril, jnp.exp(jnp.where(tril, gsum - gsum.T, 0.0)), 0.0))
        kh.append(pick(k_ref, k))
        kb.append(kh[k] * beta)
        vb = pick(v_ref, k) * beta
        A.append(jnp.where(strict, _dot_nt(kb[k], kh[k]) * decay[k], 0.0))
        eg = jnp.exp(gsum)
        qh = pick(q_ref, k)
        qk_ref[k, 0] = jnp.where(tril, _dot_nt(qh, kh[k]) * decay[k], 0.0)
        qg_ref[k, 0] = qh * eg
        glast = gsum[C - 1:C, :]
        kdt_ref[k, 0] = (kh[k] * jnp.exp(glast - gsum)).T
        dl_ref[k, 0] = jnp.broadcast_to(jnp.exp(glast), (SUBLANES, LANES))
        u_ref[k, 0] = vb
        w_ref[k, 0] = kb[k] * eg
    def joiner(level):
        same = (row >> (level + 1)) == (lane >> (level + 1))
        return same & (((row >> level) & 1) == 1) & (((lane >> level) & 1) == 0)

    P = [eye - jnp.where(joiner(0), A[k], 0.0) for k in range(nh)]
    for level in range(1, int(math.log2(C))):
        msk = joiner(level)
        T1 = [_dot(P[k], jnp.where(msk, A[k], 0.0)) for k in range(nh)]
        P = [P[k] - _dot(T1[k], P[k]) for k in range(nh)]
    for k in range(nh):
        u_ref[k, 0] = _dot(P[k], u_ref[k, 0])
        w_ref[k, 0] = _dot(P[k], w_ref[k, 0])


def _gdn_chunks(qkv, small, alog_row, dtb_row, B, S):
    C = GDN_KCHUNK
    n = S // C
    nh = GDN_HEADS_PER_STEP
    ng = HEADS // nh
    wq = nh // 2 * LANES
    nqb = BRANCH_WIDTH // wq
    big = jax.ShapeDtypeStruct((B * HEADS, n, C, LANES), F32)
    ospec = pl.BlockSpec((nh, 1, C, LANES), lambda b, g, c: (b * ng + g, c, 0, 0))
    par = pl.BlockSpec((1, LANES), lambda b, g, c: (0, 0))
    outs = pl.pallas_call(
        _gdn_chunk_kernel, grid=(B, ng, n),
        in_specs=[pl.BlockSpec((C, wq), lambda b, g, c: (b * n + c, g)),
                  pl.BlockSpec((C, wq), lambda b, g, c: (b * n + c, nqb + g)),
                  pl.BlockSpec((C, wq), lambda b, g, c: (b * n + c, 2 * nqb + g)),
                  pl.BlockSpec((C, LANES), lambda b, g, c: (b * n + c, 0)), par, par],
        out_specs=[ospec] * 5 + [pl.BlockSpec((nh, 1, SUBLANES, LANES), lambda b, g, c: (b * ng + g, c, 0, 0))],
        out_shape=[big] * 5 + [jax.ShapeDtypeStruct((B * HEADS, n, SUBLANES, LANES), F32)],
        compiler_params=_cp("parallel", "parallel", "parallel"))(qkv, qkv, qkv, small, alog_row, dtb_row)
    return [o.reshape(-1, LANES) for o in outs]


def _gdn_scan_kernel(u_ref, w_ref, qg_ref, qk_ref, kdt_ref, dl_ref, z_ref, g_ref, o_ref, o_sc, *, n):
    C = GDN_KCHUNK
    S = n * C

    def step(c, states):
        out = []
        for k, state in enumerate(states):
            r = pl.ds(pl.multiple_of(k * S + c * C, C), C)
            sb = state.astype(BF16)
            v_new = u_ref[r, :] - _dot(w_ref[r, :], sb)
            vb = v_new.astype(BF16)
            o_sc[r, :] = _dot(qg_ref[r, :], sb) + _dot(qk_ref[r, :], vb)
            dl = dl_ref[pl.ds(pl.multiple_of((k * n + c) * SUBLANES, SUBLANES), 1), :]
            out.append(state * dl + _dot(kdt_ref[r, :], vb))
        return tuple(out)

    lax.fori_loop(0, n, step, tuple(jnp.zeros((LANES, LANES), F32) for _ in range(2)))
    lane = lax.broadcasted_iota(jnp.int32, o_ref.shape, 1)

    def nrm(o):
        ms = jnp.sum(o * o, axis=1, keepdims=True) * (1.0 / GDN_HEAD_DIM)
        return o * lax.rsqrt(ms + RMS_EPS)

    nn = jnp.where(lane < GDN_HEAD_DIM, nrm(o_sc[0:S, :]), pltpu.roll(nrm(o_sc[S:2 * S, :]), LANES // 2, 1))
    z = z_ref[...].astype(F32)
    o_ref[...] = nn * g_ref[...] * (z * _sigmoid(z))


def _gdn_scan(parts, proj, g_row, B, S):
    n = S // GDN_KCHUNK
    npair = HEADS // 2
    seq = pl.BlockSpec((2 * S, LANES), lambda i: (i, 0))
    return pl.pallas_call(
        functools.partial(_gdn_scan_kernel, n=n), grid=(B * npair,),
        in_specs=[seq] * 5 + [pl.BlockSpec((2 * n * SUBLANES, LANES), lambda i: (i, 0)),
                              pl.BlockSpec((S, LANES), lambda i: (i // npair, C_GZ // LANES + i % npair)),
                              pl.BlockSpec((1, LANES), lambda i: (0, 0))],
        out_specs=pl.BlockSpec((S, LANES), lambda i: (i // npair, i % npair)),
        out_shape=jax.ShapeDtypeStruct((B * S, BRANCH_WIDTH), F32),
        scratch_shapes=[pltpu.VMEM((2 * S, LANES), F32)],
        compiler_params=_cp("parallel"))(*parts, proj, g_row)


def _merge_out_kernel(g0_ref, g1_ref, g2_ref, gb_ref, a_ref, b_ref, c_ref, wb_ref, wo_ref, h_ref, lg_ref, lb_ref,
                      o_ref, ot_ref):
    acc = None
    for n, (gl, br) in enumerate(((g0_ref, a_ref), (g1_ref, b_ref), (g2_ref, c_ref))):
        y = _sigmoid(gl[...].astype(F32) + gb_ref[n:n + 1, :]) * _dot(br[...], wb_ref[n])
        acc = y if acc is None else acc + y
    y = DEEPNORM_ALPHA * h_ref[...] + _dot(acc, wo_ref[...])
    y = _ln_rows(y, lg_ref[...], lb_ref[...])
    o_ref[...] = y
    ot_ref[...] = y.T.astype(BF16)


def _merge_out_ln(proj, gate_bias, o_a, o_b, o_c, w_branch, w_out, h, g, b, tm=256):
    T, D = h.shape
    row = lambda i: (i, 0)
    gspec = lambda n: pl.BlockSpec((tm, D), lambda i: (i, n))
    bspec = pl.BlockSpec((tm, BRANCH_WIDTH), row)
    vec = pl.BlockSpec((1, D), lambda i: (0, 0))
    return pl.pallas_call(
        _merge_out_kernel, grid=(T // tm,),
        in_specs=[gspec(0), gspec(1), gspec(2), pl.BlockSpec((3, D), lambda i: (0, 0)), bspec, bspec, bspec,
                  pl.BlockSpec((3, BRANCH_WIDTH, D), lambda i: (0, 0, 0)), pl.BlockSpec((D, D), lambda i: (0, 0)),
                  pl.BlockSpec((tm, D), row), vec, vec],
        out_specs=[pl.BlockSpec((tm, D), row), pl.BlockSpec((D, tm), lambda i: (0, i))],
        out_shape=[jax.ShapeDtypeStruct((T, D), F32), jax.ShapeDtypeStruct((D, T), BF16)],
        compiler_params=_cp("parallel"))(proj, proj, proj, gate_bias, o_a, o_b, o_c, w_branch, w_out, h,
                                         g.reshape(1, D), b.reshape(1, D))


def _extract_top(curs, n, on_max):
    curs = list(curs)
    for k in range(n):
        for i, cur in enumerate(curs):
            m = jnp.max(cur, axis=0, keepdims=True)
            on_max(i, k, m)
            if k + 1 < n:
                curs[i] = jnp.where(cur >= m, -jnp.inf, cur)


def _sort_network(n):
    pairs, p = [], 1
    while p < n:
        k = p
        while k >= 1:
            for j in range(k % p, n - k, 2 * k):
                for i in range(min(k, n - j - k)):
                    if (i + j) // (2 * p) == (i + j + k) // (2 * p):
                        pairs.append((i + j, i + j + k))
            k //= 2
        p *= 2
    return pairs


def _top_of_sorted_lists(scores, n, on_max):
    nt = N_KEYS // SUBLANES
    lists = []
    for s in scores:
        rows = [s[t * SUBLANES:(t + 1) * SUBLANES, :] for t in range(nt)]
        for a, b in _sort_network(nt):
            rows[a], rows[b] = jnp.maximum(rows[a], rows[b]), jnp.minimum(rows[a], rows[b])
        lists.append(rows + [jnp.full(rows[0].shape, -jnp.inf, F32)])
    for k in range(n):
        for i, rows in enumerate(lists):
            m = jnp.max(rows[0], axis=0, keepdims=True)
            on_max(i, k, m)
            hit = rows[0] >= m
            for t in range(min(n - 1 - k, nt)):
                rows[t] = jnp.where(hit, rows[t + 1], rows[t])


def _peer_score_kernel(qt_ref, keys_ref, s2_ref, e2_ref, c_ref, e1_ref, top_sc, cand_sc):
    K = PEER_TOPK
    tt = qt_ref.shape[1]
    nh = PEER_SCORE_HEADS_PER_TRIP

    def heads(hp, carry):
        s = []
        for i in range(2 * nh):
            r = pl.multiple_of((hp * nh * 2 + i) * PEER_HALF, PEER_HALF)
            s.append(_dot(keys_ref[hp * nh * 2 + i], qt_ref[pl.ds(r, PEER_HALF), :]))

        def put(i, k, m):
            top_sc[i, k:k + 1, :] = m
        _top_of_sorted_lists(s, K + 1, put)
        r8 = lax.broadcasted_iota(jnp.int32, (SUBLANES, tt), 0)
        v1max, v2max = [], []
        for d in range(nh):
            v1a, v1x = top_sc[2 * d, 0:K, :], top_sc[2 * d, K:K + 1, :]
            v2a, v2x = top_sc[2 * d + 1, 0:K, :], top_sc[2 * d + 1, K:K + 1, :]
            v1max.append(v1a[0:1])
            v2max.append(v2a[0:1])
            cand_sc[d, 0:K, :] = v1a + v2a[0:1]
            for b in range(1, SUBLANES):
                cand_sc[d, K + SUBLANES * (b - 1):K + SUBLANES * b, :] = v1a[0:SUBLANES] + v2a[b:b + 1]
            base = K + SUBLANES * (SUBLANES - 1)
            cand_sc[d, base:base + SUBLANES, :] = v1a[0:1] + v2a[SUBLANES:K]
            cand_sc[d, base + SUBLANES:base + 2 * SUBLANES, :] = jnp.where(
                r8 == 0, v1x + v2a[0:1], jnp.where(r8 == 1, v1a[0:1] + v2x, -jnp.inf))
        st = [{"z": jnp.zeros((1, tt), F32)} for _ in range(nh)]

        def acc(d, k, m):
            if k == 0:
                st[d]["top"] = m
            if k < K:
                st[d]["z"] = st[d]["z"] + jnp.exp(m - st[d]["top"])
            if k == K - 1:
                st[d]["t16"] = m
            if k == K:
                st[d]["t17"] = m
        _extract_top([cand_sc[d] for d in range(nh)], K + 1, acc)
        for d in range(nh):
            tau = 0.5 * (st[d]["t16"] + st[d]["t17"])
            ro = pl.ds(pl.multiple_of((hp * nh + d) * N_KEYS, N_KEYS), N_KEYS)
            s2_ref[ro, :] = s[2 * d + 1]
            e2_ref[ro, :] = jnp.exp(s[2 * d + 1] - v2max[d]) / st[d]["z"]
            c_ref[ro, :] = tau - s[2 * d]
            e1_ref[ro, :] = jnp.exp(s[2 * d] - v1max[d])
        return carry

    lax.fori_loop(0, HEADS // nh, heads, 0)


def _peer_scores(qt, keys):
    T = qt.shape[1]
    tt = PEER_SCORE_TT
    R = HEADS * N_KEYS
    ncand = PEER_TOPK + SUBLANES * (SUBLANES + 1)
    ospec = pl.BlockSpec((R, tt), lambda i: (0, i))
    return pl.pallas_call(
        _peer_score_kernel, grid=(T // tt,),
        in_specs=[pl.BlockSpec((2 * R, tt), lambda i: (0, i)),
                  pl.BlockSpec((2 * HEADS, N_KEYS, PEER_HALF), lambda i: (0, 0, 0))],
        out_specs=[ospec] * 4,
        out_shape=[jax.ShapeDtypeStruct((R, T), F32)] * 4,
        scratch_shapes=[pltpu.VMEM((2 * PEER_SCORE_HEADS_PER_TRIP, 3 * SUBLANES, tt), F32),
                        pltpu.VMEM((PEER_SCORE_HEADS_PER_TRIP, ncand, tt), F32)],
        compiler_params=_cp("parallel"))(qt, keys)


def _peer_expert_kernel(ht_ref, u_ref, vtp_ref, vtl_ref, s2_ref, e2_ref, c_ref, e1_ref, h_ref, g_ref, b_ref,
                        o_ref, acc_sc, act_sc, hw_sc, bc_sc):
    j = pl.program_id(1)
    _, eb, tt = hw_sc.shape
    ng = eb // N_KEYS
    assert ng == SUBLANES
    slot = j % 2

    @pl.when(j == 0)
    def _():
        acc_sc[...] = jnp.zeros(acc_sc.shape, F32)
        hw_sc[1] = jnp.zeros(hw_sc.shape[1:], BF16)

    mxu_w = 2 * LANES
    for half in range(tt // mxu_w):
        hl = slice(half * mxu_w, (half + 1) * mxu_w)
        a = jnp.dot(u_ref[...], ht_ref[:, hl], preferred_element_type=F32)
        act_sc[:, hl] = 0.5 * a * (1.0 + lax.erf(a * (2.0 ** -0.5)))
        acc_sc[:, hl] += jnp.dot(vtp_ref[...], hw_sc[1 - slot, :, hl], preferred_element_type=F32)
        for lc in range(half * (mxu_w // LANES), (half + 1) * (mxu_w // LANES)):
            ls = slice(lc * LANES, (lc + 1) * LANES)
            for h in range(HEADS):
                r8 = pl.ds(pl.multiple_of(h * N_KEYS + j * ng, ng), ng)
                thr8, e18 = c_ref[r8, ls], e1_ref[r8, ls]
                for g in range(ng):
                    bc_sc[0, h * ng + g] = jnp.broadcast_to(thr8[g:g + 1], (SUBLANES, LANES))
                    bc_sc[1, h * ng + g] = jnp.broadcast_to(e18[g:g + 1], (SUBLANES, LANES))
            for g in range(ng):
                wsum = jnp.zeros((N_KEYS // SUBLANES, SUBLANES, LANES), F32)
                for h in range(HEADS):
                    hr = slice(h * N_KEYS, (h + 1) * N_KEYS)
                    s2 = s2_ref[hr, ls].reshape(N_KEYS // SUBLANES, SUBLANES, LANES)
                    e2 = e2_ref[hr, ls].reshape(N_KEYS // SUBLANES, SUBLANES, LANES)
                    wsum = wsum + jnp.where(s2 >= bc_sc[0, h * ng + g][None], e2 * bc_sc[1, h * ng + g][None], 0.0)
                gs = slice(g * N_KEYS, (g + 1) * N_KEYS)
                hw_sc[slot, gs, ls] = (wsum.reshape(N_KEYS, LANES) * act_sc[gs, ls]).astype(BF16)

    @pl.when(j == pl.num_programs(1) - 1)
    def _():
        acc = acc_sc[...] + jnp.dot(vtl_ref[...], hw_sc[slot], preferred_element_type=F32)
        y = DEEPNORM_ALPHA * h_ref[...] + acc.T
        o_ref[...] = _ln_rows(y, g_ref[...], b_ref[...])


def _peer_experts(ht, u, vt, s2, e2, c, e1, h, g, b):
    T, D = h.shape
    tt, eb = PEER_TT, PEER_EB
    R = HEADS * N_KEYS
    nb = N_EXPERTS // eb
    assert nb % 2 == 0
    sspec = pl.BlockSpec((R, tt), lambda i, j: (0, i))
    vec = pl.BlockSpec((1, D), lambda i, j: (0, 0))
    return pl.pallas_call(
        _peer_expert_kernel, grid=(T // tt, nb),
        in_specs=[pl.BlockSpec((D, tt), lambda i, j: (0, i)),
                  pl.BlockSpec((eb, D), lambda i, j: (j, 0)),
                  pl.BlockSpec((D, eb), lambda i, j: (0, jnp.maximum(j - 1, 0))),
                  pl.BlockSpec((D, eb), lambda i, j: (0, nb - 1)),
                  sspec, sspec, sspec, sspec,
                  pl.BlockSpec((tt, D), lambda i, j: (i, 0)), vec, vec],
        out_specs=pl.BlockSpec((tt, D), lambda i, j: (i, 0)),
        out_shape=jax.ShapeDtypeStruct((T, D), F32),
        scratch_shapes=[pltpu.VMEM((D, tt), F32), pltpu.VMEM((eb, tt), F32), pltpu.VMEM((2, eb, tt), BF16),
                        pltpu.VMEM((2, HEADS * SUBLANES, SUBLANES, LANES), F32)],
        compiler_params=_cp("parallel", "arbitrary"))(ht, u, vt, vt, s2, e2, c, e1, h, g.reshape(1, D),
                                                      b.reshape(1, D))


def _rot_half(x, r):
    return jnp.concatenate([-x[..., r:2 * r], x[..., :r]], axis=-1)


def _prep_w_in(w):
    D = w.shape[0]
    splits = (MLA_Q_RANK, MLA_KV_RANK, MLA_ROPE, 3 * BRANCH_WIDTH, 3 * BRANCH_WIDTH, BRANCH_WIDTH, HEADS, HEADS,
              3 * D)
    o = np.cumsum((0,) + splits)
    cq, ckv, kr, mqkv, gqkv, gz, ga, gb, gl = [w[:, o[i]:o[i + 1]] for i in range(9)]
    mq, mk, mv = jnp.split(mqkv, 3, axis=1)

    def slab(m):
        m = m.reshape(D, HEADS, MOBA_HEAD_DIM)
        pad = jnp.zeros((D, HEADS, LANES - MOBA_HEAD_DIM - MOBA_ROT_DIM), w.dtype)
        return jnp.concatenate([m, _rot_half(m, MOBA_ROT_DIM // 2), pad], axis=-1).reshape(D, HEADS * LANES)

    small = jnp.concatenate([ga, gb, jnp.zeros((D, LANES // 2 - 2 * HEADS), w.dtype), kr,
                             _rot_half(kr, MLA_ROPE // 2)], axis=1)
    out = jnp.concatenate([gl, cq, ckv, slab(mq), slab(mk), mv, gqkv, gz], axis=1)
    assert out.shape[1] == C_TOTAL and C_TOTAL % PROJ_TN == 0
    return out.astype(BF16), small.astype(BF16)


def _prep_w_uq(w):
    R = w.shape[0]
    w = w.reshape(R, HEADS, MLA_NOPE + MLA_ROPE)
    rope = w[..., MLA_NOPE:]
    return jnp.concatenate([w, _rot_half(rope, MLA_ROPE // 2)], axis=-1).reshape(R, HEADS * LANES).astype(BF16)


def _prep_w_ukv(w):
    R = w.shape[0]
    w = w.reshape(R, HEADS, MLA_NOPE + MLA_V)
    k = jnp.concatenate([w[..., :MLA_NOPE], jnp.zeros((R, HEADS, LANES - MLA_NOPE), w.dtype)], axis=-1)
    return jnp.concatenate([k.reshape(R, HEADS * LANES), w[..., MLA_NOPE:].reshape(R, HEADS * MLA_V)],
                           axis=1).astype(BF16)


def _rope_tables(positions):
    pos = positions.reshape(-1).astype(F32)[:, None]
    T = pos.shape[0]

    def cs(rot):
        inv = ROPE_THETA ** (-jnp.arange(0, rot, 2, dtype=F32) / rot)
        ang = pos * inv
        return jnp.cos(ang), jnp.sin(ang)

    ca, sa = cs(MLA_ROPE)
    cb, sb = cs(MOBA_ROT_DIM)
    one = lambda n: jnp.ones((T, n), F32)
    zero = lambda n: jnp.zeros((T, n), F32)
    sc_a = (MLA_NOPE + MLA_ROPE) ** -0.5 * math.log2(math.e)
    sc_b = MOBA_HEAD_DIM ** -0.5 * math.log2(math.e)
    cat = lambda *xs: jnp.concatenate(xs, axis=1)
    mla = (cat(one(MLA_NOPE), ca, ca, zero(32)) * sc_a, cat(zero(MLA_NOPE), sa, sa, zero(32)) * sc_a,
           cat(zero(MLA_NOPE), ca, ca, zero(32)), cat(zero(MLA_NOPE), sa, sa, zero(32)))
    cm = cat(cb, cb, one(MOBA_HEAD_DIM - MOBA_ROT_DIM), zero(LANES - MOBA_HEAD_DIM))
    sm = cat(sb, sb, zero(LANES - MOBA_ROT_DIM))
    moba = (cm * sc_b, sm * sc_b, cm, sm)
    return mla, moba


def _lane_row(v):
    return jnp.concatenate([v.astype(F32), jnp.zeros((LANES - v.shape[0],), F32)]).reshape(1, LANES)


def kernel(x, positions, ln_in_g, ln_in_b, w_in, mla_q_norm, mla_kv_norm, mla_w_uq, mla_w_ukv, gdn_conv_w, gdn_A_log, gdn_dt_bias, gdn_o_norm, gate_bias, w_branch, w_out, ln1_g, ln1_b, peer_w_q, peer_sub_keys, peer_u, peer_v, ln2_g, ln2_b):
    B, S, D = x.shape
    T = B * S
    assert S % ATT_TQ == 0 and S % GDN_KCHUNK == 0 and T % PEER_TT == 0 and ATT_TQ == MOBA_BLOCK
    mla_tabs, moba_tabs = _rope_tables(positions)
    h = _layer_norm(x.reshape(T, D), ln_in_g, ln_in_b)
    for l in range(DEPTH):
        w_main, w_small = _prep_w_in(w_in[l])
        proj = _mm(h, w_main, tm=512, tn=PROJ_TN, out_dtype=BF16)
        small = _mm(h, w_small, tm=512, tn=LANES)
        qa, ka, va = _mla_qkv(proj, small, mla_q_norm[l], mla_kv_norm[l], _prep_w_uq(mla_w_uq[l]),
                              _prep_w_ukv(mla_w_ukv[l]), mla_tabs)
        o_a = _attention(qa, ka, va, 0, B, S)
        qm, km, kmean = _moba_prep(proj, moba_tabs)
        o_b = _attention(qm, km, proj, C_MV, B, S, kmean=kmean.reshape(B, S // MOBA_BLOCK, HEADS * LANES))
        qkv = _gdn_conv(proj, gdn_conv_w[l], B, S)
        parts = _gdn_chunks(qkv, small, _lane_row(gdn_A_log[l]), _lane_row(gdn_dt_bias[l]), B, S)
        o_c = _gdn_scan(parts, proj, jnp.tile(gdn_o_norm[l], 2).reshape(1, LANES), B, S)
        h, ht = _merge_out_ln(proj, gate_bias[l], o_a, o_b, o_c, w_branch[l].astype(BF16), w_out[l].astype(BF16),
                              h, ln1_g[l], ln1_b[l])
        qt = _mm(peer_w_q[l].T.astype(BF16), ht, tm=1024, tn=min(1024, T), out_dtype=BF16)
        keys = peer_sub_keys[l].reshape(2 * HEADS, N_KEYS, PEER_HALF).astype(BF16)
        s2, e2, c, e1 = _peer_scores(qt, keys)
        h = _peer_experts(ht, peer_u[l].astype(BF16), peer_v[l].astype(BF16).T, s2, e2, c, e1, h, ln2_g[l], ln2_b[l])
    return h.reshape(B, S, D)
```

```python
import functools
import math

import numpy as np
import jax
import jax.numpy as jnp
from jax import lax
from jax.experimental import pallas as pl
from jax.experimental.pallas import tpu as pltpu

F32 = jnp.float32
BF16 = jnp.bfloat16
HIGHEST = lax.Precision.HIGHEST

DEPTH = 2
ROPE_THETA = 500000.0
NEG_INF = -1e30
LN_EPS = 1e-5
RMS_EPS = 1e-6
DEEPNORM_ALPHA = (2 * DEPTH) ** 0.25
HEADS = 8
MLA_NOPE, MLA_ROPE, MLA_V = 64, 32, 64
MLA_Q_RANK, MLA_KV_RANK = 768, 256
MOBA_HEAD_DIM, MOBA_ROT_DIM, MOBA_BLOCK, MOBA_TOPK = 64, 16, 256, 3
GDN_HEAD_DIM, GDN_CONV = 64, 4
BRANCH_WIDTH = 512
N_KEYS, PEER_TOPK, PEER_HALF = 128, 16, 128
N_EXPERTS = N_KEYS * N_KEYS

LANES = 128
SUBLANES = 8
VMEM_LIMIT = 56 * 1024 * 1024

GDN_KCHUNK = 128
GDN_HEADS_PER_STEP = 4
GDN_SCAN_PAIRS = 2
ATT_TQ = 256
PEER_TT = 512
PEER_EB = 1024
PEER_SCORE_TT = 256
PEER_SCORE_HEADS_PER_TRIP = 2

C_GATE, C_CQ, C_CKV, C_MQ, C_MK, C_MV, C_GQKV, C_GZ, C_TOTAL = (
    0, 3072, 3840, 4096, 5120, 6144, 6656, 8192, 8704)
PROJ_TN = 4352


def _cp(*sem):
    return pltpu.CompilerParams(dimension_semantics=sem, vmem_limit_bytes=VMEM_LIMIT)


def _dot(a, b):
    return jnp.dot(a.astype(BF16), b.astype(BF16), preferred_element_type=F32)


def _dot_nt(a, b):
    return lax.dot_general(a.astype(BF16), b.astype(BF16), (((1,), (1,)), ((), ())),
                           preferred_element_type=F32)


def _ln_rows(y, g, b):
    mu = jnp.mean(y, axis=-1, keepdims=True)
    d = y - mu
    var = jnp.mean(d * d, axis=-1, keepdims=True)
    return d * lax.rsqrt(var + LN_EPS) * g + b


def _sigmoid(x):
    return 1.0 / (1.0 + jnp.exp(-x))


def _ln_kernel(x_ref, g_ref, b_ref, o_ref):
    o_ref[...] = _ln_rows(x_ref[...], g_ref[...], b_ref[...])


def _layer_norm(x, g, b, tm=512):
    T, D = x.shape
    return pl.pallas_call(
        _ln_kernel, grid=(T // tm,),
        in_specs=[pl.BlockSpec((tm, D), lambda i: (i, 0)),
                  pl.BlockSpec((1, D), lambda i: (0, 0)),
                  pl.BlockSpec((1, D), lambda i: (0, 0))],
        out_specs=pl.BlockSpec((tm, D), lambda i: (i, 0)),
        out_shape=jax.ShapeDtypeStruct((T, D), F32),
        compiler_params=_cp("parallel"))(x, g.reshape(1, D), b.reshape(1, D))


def _mm_kernel(x_ref, w_ref, o_ref):
    o_ref[...] = _dot(x_ref[...], w_ref[...]).astype(o_ref.dtype)


def _mm(x, w, tm, tn, out_dtype=F32):
    M, K = x.shape
    N = w.shape[1]
    return pl.pallas_call(
        _mm_kernel, grid=(M // tm, N // tn),
        in_specs=[pl.BlockSpec((tm, K), lambda i, j: (i, 0)),
                  pl.BlockSpec((K, tn), lambda i, j: (0, j))],
        out_specs=pl.BlockSpec((tm, tn), lambda i, j: (i, j)),
        out_shape=jax.ShapeDtypeStruct((M, N), out_dtype),
        compiler_params=_cp("parallel", "parallel"))(x, w)


def _rms_rows(x, g):
    return x * lax.rsqrt(jnp.mean(x * x, axis=-1, keepdims=True) + RMS_EPS) * g


def _head_slabs(x):
    return [x[:, h * LANES:(h + 1) * LANES] for h in range(HEADS)]


def _mla_qkv_kernel(cq_ref, ckv_ref, gq_ref, gkv_ref, wq_ref, wkv_ref, sm_ref, cq_tab, sq_tab, ck_tab, sk_tab,
                    qo_ref, ko_ref, vo_ref):
    W = HEADS * LANES
    q = _dot(_rms_rows(cq_ref[...].astype(F32), gq_ref[...]), wq_ref[...])
    kv = _dot(_rms_rows(ckv_ref[...].astype(F32), gkv_ref[...]), wkv_ref[...])
    vo_ref[...] = kv[:, W:].astype(BF16)
    cq, sq = cq_tab[...], sq_tab[...]
    sm = sm_ref[...]
    kr = sm * ck_tab[...] + pltpu.roll(sm, LANES - MLA_ROPE, 1) * sk_tab[...]
    for h in range(HEADS):
        sl = slice(h * LANES, (h + 1) * LANES)
        qs = q[:, sl]
        qo_ref[:, sl] = (qs * cq + pltpu.roll(qs, LANES - MLA_ROPE, 1) * sq).astype(BF16)
        ko_ref[:, sl] = (kv[:, sl] + kr).astype(BF16)


def _mla_qkv(proj, small, gq, gkv, wq, wkv, tabs, tm=512):
    T = proj.shape[0]
    W = HEADS * LANES
    row = lambda i: (i, 0)
    tab = pl.BlockSpec((tm, LANES), row)
    full = lambda a: pl.BlockSpec(a.shape, lambda i: (0, 0))
    gq, gkv = gq.reshape(1, -1), gkv.reshape(1, -1)
    return pl.pallas_call(
        _mla_qkv_kernel, grid=(T // tm,),
        in_specs=[pl.BlockSpec((tm, MLA_Q_RANK), lambda i: (i, C_CQ // MLA_Q_RANK)),
                  pl.BlockSpec((tm, MLA_KV_RANK), lambda i: (i, C_CKV // MLA_KV_RANK)),
                  full(gq), full(gkv), full(wq), full(wkv), tab, tab, tab, tab, tab],
        out_specs=[pl.BlockSpec((tm, W), row), pl.BlockSpec((tm, W), row), pl.BlockSpec((tm, BRANCH_WIDTH), row)],
        out_shape=[jax.ShapeDtypeStruct((T, W), BF16)] * 2 + [jax.ShapeDtypeStruct((T, BRANCH_WIDTH), BF16)],
        compiler_params=_cp("parallel"))(proj, proj, gq, gkv, wq, wkv, small, *tabs)


def _moba_prep_kernel(q_ref, k_ref, cq_ref, sq_ref, ck_ref, sk_ref, qo_ref, ko_ref, km_ref):
    cq, sq, ck, sk = cq_ref[...], sq_ref[...], ck_ref[...], sk_ref[...]
    q, k = q_ref[...].astype(F32), k_ref[...].astype(F32)
    for h, (qs, ks) in enumerate(zip(_head_slabs(q), _head_slabs(k))):
        sl = slice(h * LANES, (h + 1) * LANES)
        qo_ref[:, sl] = qs * cq + pltpu.roll(qs, LANES // 2, 1) * sq
        kk = ks * ck + pltpu.roll(ks, LANES // 2, 1) * sk
        ko_ref[:, sl] = kk.astype(BF16)
        km_ref[0, :, sl] = jnp.mean(kk, axis=0, keepdims=True)


def _moba_prep(proj, tabs):
    T = proj.shape[0]
    W = HEADS * LANES
    tm = MOBA_BLOCK
    row = lambda i: (i, 0)
    tab = pl.BlockSpec((tm, LANES), row)
    return pl.pallas_call(
        _moba_prep_kernel, grid=(T // tm,),
        in_specs=[pl.BlockSpec((tm, W), lambda i: (i, C_MQ // W)),
                  pl.BlockSpec((tm, W), lambda i: (i, C_MK // W)), tab, tab, tab, tab],
        out_specs=[pl.BlockSpec((tm, W), row), pl.BlockSpec((tm, W), row),
                   pl.BlockSpec((1, 1, W), lambda i: (i, 0, 0))],
        out_shape=[jax.ShapeDtypeStruct((T, W), F32), jax.ShapeDtypeStruct((T, W), BF16),
                   jax.ShapeDtypeStruct((T // tm, 1, W), F32)],
        compiler_params=_cp("parallel"))(proj, proj, *tabs)


def _attn_kernel(*refs, moba, nk):
    if moba:
        q_ref, k_ref, v_ref, km_ref, o_ref, vt_sc, qb_sc, m_sc, l_sc, al_sc, acc_sc, s_sc, p_sc, bias_sc = refs
    else:
        q_ref, k_ref, v_ref, o_ref, vt_sc, m_sc, l_sc, al_sc, acc_sc, s_sc, p_sc = refs
    tq = tk = ATT_TQ
    dv = BRANCH_WIDTH // HEADS
    qi = pl.program_id(1)
    hslab = [slice(h * LANES, (h + 1) * LANES) for h in range(HEADS)]

    @pl.when(qi == 0)
    def _():
        for j in range(nk):
            for c in range(tk // LANES):
                for g in range(BRANCH_WIDTH // LANES):
                    vt_sc[j, g * LANES:(g + 1) * LANES, c * LANES:(c + 1) * LANES] = (
                        v_ref[j * tk + c * LANES:j * tk + (c + 1) * LANES,
                              g * LANES:(g + 1) * LANES].astype(F32).T.astype(BF16))

    if moba:
        blk = lax.broadcasted_iota(jnp.int32, (SUBLANES, tq), 0)
        for h in range(HEADS):
            qf = q_ref[:, hslab[h]]
            qb_sc[:, hslab[h]] = qf.astype(BF16)
            km = km_ref[0, :, hslab[h]]
            if nk < SUBLANES:
                km = jnp.concatenate([km, jnp.zeros((SUBLANES - nk, LANES), F32)], axis=0)
            gate = lax.dot_general(km, qf, (((1,), (1,)), ((), ())), precision=HIGHEST,
                                   preferred_element_type=F32)
            gate = jnp.where(blk < qi, gate, -jnp.inf)
            for n in range(nk):
                gn = gate[n:n + 1, :]
                beats = jnp.where(gate > gn, 1.0, jnp.where((gate == gn) & (blk < n), 1.0, 0.0))
                cnt = jnp.sum(beats, axis=0, keepdims=True)
                bias_sc[h, n] = jnp.where((cnt < MOBA_TOPK) & (n < qi), 0.0, NEG_INF)
    qsrc = qb_sc if moba else q_ref
    m_sc[...] = jnp.full(m_sc.shape, NEG_INF, F32)
    l_sc[...] = jnp.zeros(l_sc.shape, F32)
    acc_sc[...] = jnp.zeros(acc_sc.shape, F32)

    def block(j, diag):
        rows = pl.ds(pl.multiple_of(j * tk, tk), tk)
        for h in range(HEADS):
            s = _dot_nt(k_ref[rows, hslab[h]], qsrc[:, hslab[h]])
            if diag:
                keyi = lax.broadcasted_iota(jnp.int32, (tk, tq), 0)
                qryi = lax.broadcasted_iota(jnp.int32, (tk, tq), 1)
                s = jnp.where(keyi <= qryi, s, NEG_INF)
            elif moba:
                s = s + bias_sc[h, j]
            s_sc[h] = s
        for h in range(HEADS):
            s = s_sc[h]
            m_prev = m_sc[h]
            m_new = jnp.maximum(m_prev, jnp.max(s, axis=0, keepdims=True))
            alpha = jnp.exp2(m_prev - m_new)
            p = jnp.exp2(s - m_new)
            l_sc[h] = alpha * l_sc[h] + jnp.sum(p, axis=0, keepdims=True)
            p_sc[h] = p.astype(BF16)
            al_sc[h] = alpha
            m_sc[h] = m_new
        for h in range(HEADS):
            acc_sc[h] = al_sc[h] * acc_sc[h] + jnp.dot(vt_sc[j, h * dv:(h + 1) * dv, :], p_sc[h],
                                                       preferred_element_type=F32)

    def past(j, carry):
        block(j, False)
        return carry
    lax.fori_loop(0, qi, past, 0)
    block(qi, True)
    ot = jnp.concatenate([acc_sc[h] * (1.0 / l_sc[h]) for h in range(HEADS)], axis=0)
    o_ref[...] = ot.T


def _attention(q, k, v, v_col, B, S, kmean=None):
    T = B * S
    tq = ATT_TQ
    nq = S // tq
    W = HEADS * LANES
    dv = BRANCH_WIDTH // HEADS
    moba = kmean is not None
    assert nq <= SUBLANES and v_col % BRANCH_WIDTH == 0
    in_specs = [pl.BlockSpec((tq, W), lambda b, i: (b * nq + i, 0)),
                pl.BlockSpec((S, W), lambda b, i: (b, 0)),
                pl.BlockSpec((S, BRANCH_WIDTH), lambda b, i: (b, v_col // BRANCH_WIDTH))]
    scratch = [pltpu.VMEM((nq, BRANCH_WIDTH, tq), BF16)]
    if moba:
        scratch.append(pltpu.VMEM((tq, W), BF16))
    row = pltpu.VMEM((HEADS, 1, tq), F32)
    scratch += [row, row, row, pltpu.VMEM((HEADS, dv, tq), F32),
                pltpu.VMEM((HEADS, tq, tq), F32), pltpu.VMEM((HEADS, tq, tq), BF16)]
    args = [q, k, v]
    if moba:
        in_specs.append(pl.BlockSpec((1, nq, W), lambda b, i: (b, 0, 0)))
        scratch.append(pltpu.VMEM((HEADS, nq, 1, tq), F32))
        args.append(kmean)
    return pl.pallas_call(
        functools.partial(_attn_kernel, moba=moba, nk=nq),
        grid=(B, nq), in_specs=in_specs,
        out_specs=pl.BlockSpec((tq, BRANCH_WIDTH), lambda b, i: (b * nq + i, 0)),
        out_shape=jax.ShapeDtypeStruct((T, BRANCH_WIDTH), F32),
        scratch_shapes=scratch,
        compiler_params=_cp("parallel", "arbitrary"))(*args)


def _gdn_conv_kernel(x_ref, w_ref, o_ref):
    c = pl.program_id(1)
    x = x_ref[...].astype(F32)
    w = w_ref[...]
    S = x.shape[0]
    row = lax.broadcasted_iota(jnp.int32, (S, LANES), 0)
    lane = lax.broadcasted_iota(jnp.int32, (S, LANES), 1)
    y = x * w[GDN_CONV - 1:GDN_CONV, :]
    for d in range(1, GDN_CONV):
        xs = jnp.where(row >= d, pltpu.roll(x, d, 0), 0.0)
        y = y + xs * w[GDN_CONV - 1 - d:GDN_CONV - d, :]
    y = y * _sigmoid(y)
    sq = y * y
    lo = lane < GDN_HEAD_DIM
    ss0 = jnp.sum(jnp.where(lo, sq, 0.0), axis=1, keepdims=True)
    ss1 = jnp.sum(jnp.where(lo, 0.0, sq), axis=1, keepdims=True)
    inv = lax.rsqrt(jnp.where(lo, ss0, ss1) + RMS_EPS)
    nqb = BRANCH_WIDTH // LANES
    scale = jnp.where(c < nqb, GDN_HEAD_DIM ** -0.5, 1.0)
    o_ref[...] = jnp.where(c < 2 * nqb, y * inv * scale, y)


def _gdn_conv(proj, conv_w, B, S):
    T = B * S
    nb = 3 * BRANCH_WIDTH // LANES
    return pl.pallas_call(
        _gdn_conv_kernel, grid=(B, nb),
        in_specs=[pl.BlockSpec((S, LANES), lambda b, c: (b, C_GQKV // LANES + c)),
                  pl.BlockSpec((GDN_CONV, LANES), lambda b, c: (0, c))],
        out_specs=pl.BlockSpec((S, LANES), lambda b, c: (b, c)),
        out_shape=jax.ShapeDtypeStruct((T, 3 * BRANCH_WIDTH), F32),
        compiler_params=_cp("parallel", "parallel"))(proj, conv_w)


def _gdn_chunk_kernel(q_ref, k_ref, v_ref, sm_ref, alog_ref, dtb_ref,
                      u_ref, w_ref, qg_ref, qk_ref, kdt_ref, dl_ref):
    C = GDN_KCHUNK
    nh = GDN_HEADS_PER_STEP
    h0 = pl.program_id(1) * nh
    lane = lax.broadcasted_iota(jnp.int32, (C, LANES), 1)
    row = lax.broadcasted_iota(jnp.int32, (C, LANES), 0)
    lo = lane < GDN_HEAD_DIM
    tril = row >= lane
    strict = row > lane
    eye = jnp.where(row == lane, 1.0, 0.0)
    ltri = tril.astype(BF16)

    def pick(ref, k):
        x = ref[:, (k // 2) * LANES:(k // 2 + 1) * LANES]
        return jnp.where(lo, pltpu.roll(x, LANES // 2, 1) if k % 2 else x, 0.0)

    sm = sm_ref[...]
    a = sm + dtb_ref[...]
    softplus = jnp.maximum(a, 0.0) + jnp.log1p(jnp.exp(-jnp.abs(a)))
    garr = -jnp.exp(alog_ref[...]) * softplus
    sig = _sigmoid(sm)
    kh, kb, gc, decay, A = [], [], [], [], []
    for k in range(nh):
        g_col = jnp.sum(jnp.where(lane == h0 + k, garr, 0.0), axis=1, keepdims=True)
        beta = jnp.sum(jnp.where(lane == HEADS + h0 + k, sig, 0.0), axis=1, keepdims=True)
        g1 = jnp.broadcast_to(g_col, (C, LANES))
        gsum = None
        for _ in range(3):
            gb = g1.astype(BF16)
            part = jnp.dot(ltri, gb, preferred_element_type=F32)
            gsum = part if gsum is None else gsum + part
            g1 = g1 - gb.astype(F32)
        gc.append(gsum)
        decay.append(jnp.where(tril, jnp.exp(jnp.where(tril, gsum - gsum.T, 0.0)), 0.0))
        kh.append(pick(k_ref, k))
        kb.append(kh[k] * beta)
        vb = pick(v_ref, k) * beta
        A.append(jnp.where(strict, _dot_nt(kb[k], kh[k]) * decay[k], 0.0))
        eg = jnp.exp(gsum)
        qh = pick(q_ref, k)
        qk_ref[k, 0] = jnp.where(tril, _dot_nt(qh, kh[k]) * decay[k], 0.0).astype(BF16)
        qg_ref[k, 0] = (qh * eg).astype(BF16)
        glast = gsum[C - 1:C, :]
        kdt_ref[k, 0] = (kh[k] * jnp.exp(glast - gsum)).T.astype(BF16)
        dl_ref[k, 0] = jnp.broadcast_to(jnp.exp(glast), (SUBLANES, LANES))
        u_ref[k, 0] = vb
        w_ref[k, 0] = (kb[k] * eg).astype(BF16)
    def joiner(level):
        same = (row >> (level + 1)) == (lane >> (level + 1))
        return same & (((row >> level) & 1) == 1) & (((lane >> level) & 1) == 0)

    P = [eye - jnp.where(joiner(0), A[k], 0.0) for k in range(nh)]
    for level in range(1, int(math.log2(C))):
        msk = joiner(level)
        T1 = [_dot(P[k], jnp.where(msk, A[k], 0.0)) for k in range(nh)]
        P = [P[k] - _dot(T1[k], P[k]) for k in range(nh)]
    for k in range(nh):
        u_ref[k, 0] = _dot(P[k], u_ref[k, 0])
        w_ref[k, 0] = _dot(P[k], w_ref[k, 0]).astype(BF16)


def _gdn_chunks(qkv, small, alog_row, dtb_row, B, S):
    C = GDN_KCHUNK
    n = S // C
    nh = GDN_HEADS_PER_STEP
    ng = HEADS // nh
    wq = nh // 2 * LANES
    nqb = BRANCH_WIDTH // wq
    big = lambda dt: jax.ShapeDtypeStruct((B * HEADS, n, C, LANES), dt)
    ospec = pl.BlockSpec((nh, 1, C, LANES), lambda b, g, c: (b * ng + g, c, 0, 0))
    par = pl.BlockSpec((1, LANES), lambda b, g, c: (0, 0))
    outs = pl.pallas_call(
        _gdn_chunk_kernel, grid=(B, ng, n),
        in_specs=[pl.BlockSpec((C, wq), lambda b, g, c: (b * n + c, g)),
                  pl.BlockSpec((C, wq), lambda b, g, c: (b * n + c, nqb + g)),
                  pl.BlockSpec((C, wq), lambda b, g, c: (b * n + c, 2 * nqb + g)),
                  pl.BlockSpec((C, LANES), lambda b, g, c: (b * n + c, 0)), par, par],
        out_specs=[ospec] * 5 + [pl.BlockSpec((nh, 1, SUBLANES, LANES), lambda b, g, c: (b * ng + g, c, 0, 0))],
        out_shape=[big(F32)] + [big(BF16)] * 4 + [jax.ShapeDtypeStruct((B * HEADS, n, SUBLANES, LANES), F32)],
        compiler_params=_cp("parallel", "parallel", "parallel"))(qkv, qkv, qkv, small, alog_row, dtb_row)
    return [o.reshape(-1, LANES) for o in outs]


def _gdn_scan_kernel(u_ref, w_ref, qg_ref, qk_ref, kdt_ref, dl_ref, z_ref, g_ref, o_ref, o_sc, *, n):
    C = GDN_KCHUNK
    S = n * C
    npair = GDN_SCAN_PAIRS

    def step(c, states):
        out = []
        for k, state in enumerate(states):
            r = pl.ds(pl.multiple_of(k * S + c * C, C), C)
            sb = state.astype(BF16)
            v_new = u_ref[r, :] - _dot(w_ref[r, :], sb)
            vb = v_new.astype(BF16)
            o_sc[r, :] = _dot(qg_ref[r, :], sb) + _dot(qk_ref[r, :], vb)
            dl = dl_ref[pl.ds(pl.multiple_of((k * n + c) * SUBLANES, SUBLANES), 1), :]
            out.append(state * dl + _dot(kdt_ref[r, :], vb))
        return tuple(out)

    lax.fori_loop(0, n, step, tuple(jnp.zeros((LANES, LANES), F32) for _ in range(2 * npair)))
    lane = lax.broadcasted_iota(jnp.int32, (S, LANES), 1)

    def nrm(o):
        ms = jnp.sum(o * o, axis=1, keepdims=True) * (1.0 / GDN_HEAD_DIM)
        return o * lax.rsqrt(ms + RMS_EPS)

    for p in range(npair):
        ps = slice(p * LANES, (p + 1) * LANES)
        nn = jnp.where(lane < GDN_HEAD_DIM, nrm(o_sc[2 * p * S:(2 * p + 1) * S, :]),
                       pltpu.roll(nrm(o_sc[(2 * p + 1) * S:(2 * p + 2) * S, :]), LANES // 2, 1))
        z = z_ref[:, ps].astype(F32)
        o_ref[:, ps] = nn * g_ref[...] * (z * _sigmoid(z))


def _gdn_scan(parts, proj, g_row, B, S):
    n = S // GDN_KCHUNK
    nps = GDN_SCAN_PAIRS
    steps = HEADS // 2 // nps
    wz = nps * LANES
    seq = pl.BlockSpec((2 * nps * S, LANES), lambda i: (i, 0))
    return pl.pallas_call(
        functools.partial(_gdn_scan_kernel, n=n), grid=(B * steps,),
        in_specs=[seq] * 5 + [pl.BlockSpec((2 * nps * n * SUBLANES, LANES), lambda i: (i, 0)),
                              pl.BlockSpec((S, wz), lambda i: (i // steps, C_GZ // wz + i % steps)),
                              pl.BlockSpec((1, LANES), lambda i: (0, 0))],
        out_specs=pl.BlockSpec((S, wz), lambda i: (i // steps, i % steps)),
        out_shape=jax.ShapeDtypeStruct((B * S, BRANCH_WIDTH), F32),
        scratch_shapes=[pltpu.VMEM((2 * nps * S, LANES), F32)],
        compiler_params=_cp("parallel"))(*parts, proj, g_row)


def _merge_out_kernel(g0_ref, g1_ref, g2_ref, gb_ref, a_ref, b_ref, c_ref, wb_ref, wo_ref, h_ref, lg_ref, lb_ref,
                      o_ref, ot_ref):
    acc = None
    for n, (gl, br) in enumerate(((g0_ref, a_ref), (g1_ref, b_ref), (g2_ref, c_ref))):
        y = _sigmoid(gl[...].astype(F32) + gb_ref[n:n + 1, :]) * _dot(br[...], wb_ref[n])
        acc = y if acc is None else acc + y
    y = DEEPNORM_ALPHA * h_ref[...] + _dot(acc, wo_ref[...])
    y = _ln_rows(y, lg_ref[...], lb_ref[...])
    o_ref[...] = y
    ot_ref[...] = y.T.astype(BF16)


def _merge_out_ln(proj, gate_bias, o_a, o_b, o_c, w_branch, w_out, h, g, b, tm=256):
    T, D = h.shape
    row = lambda i: (i, 0)
    gspec = lambda n: pl.BlockSpec((tm, D), lambda i: (i, n))
    bspec = pl.BlockSpec((tm, BRANCH_WIDTH), row)
    vec = pl.BlockSpec((1, D), lambda i: (0, 0))
    return pl.pallas_call(
        _merge_out_kernel, grid=(T // tm,),
        in_specs=[gspec(0), gspec(1), gspec(2), pl.BlockSpec((3, D), lambda i: (0, 0)), bspec, bspec, bspec,
                  pl.BlockSpec((3, BRANCH_WIDTH, D), lambda i: (0, 0, 0)), pl.BlockSpec((D, D), lambda i: (0, 0)),
                  pl.BlockSpec((tm, D), row), vec, vec],
        out_specs=[pl.BlockSpec((tm, D), row), pl.BlockSpec((D, tm), lambda i: (0, i))],
        out_shape=[jax.ShapeDtypeStruct((T, D), F32), jax.ShapeDtypeStruct((D, T), BF16)],
        compiler_params=_cp("parallel"))(proj, proj, proj, gate_bias, o_a, o_b, o_c, w_branch, w_out, h,
                                         g.reshape(1, D), b.reshape(1, D))


def _extract_top(curs, n, on_max):
    curs = list(curs)
    for k in range(n):
        for i, cur in enumerate(curs):
            m = jnp.max(cur, axis=0, keepdims=True)
            on_max(i, k, m)
            if k + 1 < n:
                curs[i] = jnp.where(cur >= m, -jnp.inf, cur)


def _sort_network(n):
    pairs, p = [], 1
    while p < n:
        k = p
        while k >= 1:
            for j in range(k % p, n - k, 2 * k):
                for i in range(min(k, n - j - k)):
                    if (i + j) // (2 * p) == (i + j + k) // (2 * p):
                        pairs.append((i + j, i + j + k))
            k //= 2
        p *= 2
    return pairs


def _top_of_sorted_lists(scores, n, on_max):
    nt = N_KEYS // SUBLANES
    lists = []
    for s in scores:
        rows = [s[t * SUBLANES:(t + 1) * SUBLANES, :] for t in range(nt)]
        for a, b in _sort_network(nt):
            rows[a], rows[b] = jnp.maximum(rows[a], rows[b]), jnp.minimum(rows[a], rows[b])
        lists.append(rows + [jnp.full(rows[0].shape, -jnp.inf, F32)])
    for k in range(n):
        for i, rows in enumerate(lists):
            m = jnp.max(rows[0], axis=0, keepdims=True)
            on_max(i, k, m)
            hit = rows[0] >= m
            for t in range(min(n - 1 - k, nt)):
                rows[t] = jnp.where(hit, rows[t + 1], rows[t])


def _peer_score_kernel(qt_ref, keys_ref, s2_ref, e2_ref, c_ref, e1_ref, top_sc, cand_sc):
    K = PEER_TOPK
    tt = qt_ref.shape[1]
    nh = PEER_SCORE_HEADS_PER_TRIP

    def heads(hp, carry):
        s = []
        for i in range(2 * nh):
            r = pl.multiple_of((hp * nh * 2 + i) * PEER_HALF, PEER_HALF)
            s.append(_dot(keys_ref[hp * nh * 2 + i], qt_ref[pl.ds(r, PEER_HALF), :]))

        def put(i, k, m):
            top_sc[i, k:k + 1, :] = m
        _top_of_sorted_lists(s, K + 1, put)
        r8 = lax.broadcasted_iota(jnp.int32, (SUBLANES, tt), 0)
        v1max, v2max = [], []
        for d in range(nh):
            v1a, v1x = top_sc[2 * d, 0:K, :], top_sc[2 * d, K:K + 1, :]
            v2a, v2x = top_sc[2 * d + 1, 0:K, :], top_sc[2 * d + 1, K:K + 1, :]
            v1max.append(v1a[0:1])
            v2max.append(v2a[0:1])
            cand_sc[d, 0:K, :] = v1a + v2a[0:1]
            for b in range(1, SUBLANES):
                cand_sc[d, K + SUBLANES * (b - 1):K + SUBLANES * b, :] = v1a[0:SUBLANES] + v2a[b:b + 1]
            base = K + SUBLANES * (SUBLANES - 1)
            cand_sc[d, base:base + SUBLANES, :] = v1a[0:1] + v2a[SUBLANES:K]
            cand_sc[d, base + SUBLANES:base + 2 * SUBLANES, :] = jnp.where(
                r8 == 0, v1x + v2a[0:1], jnp.where(r8 == 1, v1a[0:1] + v2x, -jnp.inf))
        st = [{"z": jnp.zeros((1, tt), F32)} for _ in range(nh)]

        def acc(d, k, m):
            if k == 0:
                st[d]["top"] = m
            if k < K:
                st[d]["z"] = st[d]["z"] + jnp.exp(m - st[d]["top"])
            if k == K - 1:
                st[d]["t16"] = m
            if k == K:
                st[d]["t17"] = m
        _extract_top([cand_sc[d] for d in range(nh)], K + 1, acc)
        for d in range(nh):
            tau = 0.5 * (st[d]["t16"] + st[d]["t17"])
            ro = pl.ds(pl.multiple_of((hp * nh + d) * N_KEYS, N_KEYS), N_KEYS)
            s2_ref[ro, :] = s[2 * d + 1]
            e2_ref[ro, :] = jnp.exp(s[2 * d + 1] - v2max[d]) / st[d]["z"]
            c_ref[ro, :] = tau - s[2 * d]
            e1_ref[ro, :] = jnp.exp(s[2 * d] - v1max[d])
        return carry

    lax.fori_loop(0, HEADS // nh, heads, 0)


def _peer_scores(qt, keys):
    T = qt.shape[1]
    tt = PEER_SCORE_TT
    R = HEADS * N_KEYS
    ncand = PEER_TOPK + SUBLANES * (SUBLANES + 1)
    ospec = pl.BlockSpec((R, tt), lambda i: (0, i))
    return pl.pallas_call(
        _peer_score_kernel, grid=(T // tt,),
        in_specs=[pl.BlockSpec((2 * R, tt), lambda i: (0, i)),
                  pl.BlockSpec((2 * HEADS, N_KEYS, PEER_HALF), lambda i: (0, 0, 0))],
        out_specs=[ospec] * 4,
        out_shape=[jax.ShapeDtypeStruct((R, T), F32)] * 4,
        scratch_shapes=[pltpu.VMEM((2 * PEER_SCORE_HEADS_PER_TRIP, 3 * SUBLANES, tt), F32),
                        pltpu.VMEM((PEER_SCORE_HEADS_PER_TRIP, ncand, tt), F32)],
        compiler_params=_cp("parallel"))(qt, keys)


def _peer_expert_kernel(ht_ref, u_ref, vtp_ref, vtl_ref, s2_ref, e2_ref, c_ref, e1_ref, h_ref, g_ref, b_ref,
                        o_ref, acc_sc, act_sc, hw_sc, bc_sc):
    j = pl.program_id(1)
    _, eb, tt = hw_sc.shape
    ng = eb // N_KEYS
    assert ng == SUBLANES
    slot = j % 2

    @pl.when(j == 0)
    def _():
        acc_sc[...] = jnp.zeros(acc_sc.shape, F32)
        hw_sc[1] = jnp.zeros(hw_sc.shape[1:], BF16)

    mxu_w = 2 * LANES
    for half in range(tt // mxu_w):
        hl = slice(half * mxu_w, (half + 1) * mxu_w)
        a = jnp.dot(u_ref[...], ht_ref[:, hl], preferred_element_type=F32)
        act_sc[:, hl] = 0.5 * a * (1.0 + lax.erf(a * (2.0 ** -0.5)))
        acc_sc[:, hl] += jnp.dot(vtp_ref[...], hw_sc[1 - slot, :, hl], preferred_element_type=F32)
        for lc in range(half * (mxu_w // LANES), (half + 1) * (mxu_w // LANES)):
            ls = slice(lc * LANES, (lc + 1) * LANES)
            for h in range(HEADS):
                r8 = pl.ds(pl.multiple_of(h * N_KEYS + j * ng, ng), ng)
                thr8, e18 = c_ref[r8, ls], e1_ref[r8, ls]
                for g in range(ng):
                    bc_sc[0, h * ng + g] = jnp.broadcast_to(thr8[g:g + 1], (SUBLANES, LANES))
                    bc_sc[1, h * ng + g] = jnp.broadcast_to(e18[g:g + 1], (SUBLANES, LANES))
            for g in range(ng):
                wsum = jnp.zeros((N_KEYS // SUBLANES, SUBLANES, LANES), F32)
                for h in range(HEADS):
                    hr = slice(h * N_KEYS, (h + 1) * N_KEYS)
                    s2 = s2_ref[hr, ls].reshape(N_KEYS // SUBLANES, SUBLANES, LANES)
                    e2 = e2_ref[hr, ls].reshape(N_KEYS // SUBLANES, SUBLANES, LANES)
                    wsum = wsum + jnp.where(s2 >= bc_sc[0, h * ng + g][None], e2 * bc_sc[1, h * ng + g][None], 0.0)
                gs = slice(g * N_KEYS, (g + 1) * N_KEYS)
                hw_sc[slot, gs, ls] = (wsum.reshape(N_KEYS, LANES) * act_sc[gs, ls]).astype(BF16)

    @pl.when(j == pl.num_programs(1) - 1)
    def _():
        acc = acc_sc[...] + jnp.dot(vtl_ref[...], hw_sc[slot], preferred_element_type=F32)
        y = DEEPNORM_ALPHA * h_ref[...] + acc.T
        o_ref[...] = _ln_rows(y, g_ref[...], b_ref[...])


def _peer_experts(ht, u, vt, s2, e2, c, e1, h, g, b):
    T, D = h.shape
    tt, eb = PEER_TT, PEER_EB
    R = HEADS * N_KEYS
    nb = N_EXPERTS // eb
    assert nb % 2 == 0
    sspec = pl.BlockSpec((R, tt), lambda i, j: (0, i))
    vec = pl.BlockSpec((1, D), lambda i, j: (0, 0))
    return pl.pallas_call(
        _peer_expert_kernel, grid=(T // tt, nb),
        in_specs=[pl.BlockSpec((D, tt), lambda i, j: (0, i)),
                  pl.BlockSpec((eb, D), lambda i, j: (j, 0)),
                  pl.BlockSpec((D, eb), lambda i, j: (0, jnp.maximum(j - 1, 0))),
                  pl.BlockSpec((D, eb), lambda i, j: (0, nb - 1)),
                  sspec, sspec, sspec, sspec,
                  pl.BlockSpec((tt, D), lambda i, j: (i, 0)), vec, vec],
        out_specs=pl.BlockSpec((tt, D), lambda i, j: (i, 0)),
        out_shape=jax.ShapeDtypeStruct((T, D), F32),
        scratch_shapes=[pltpu.VMEM((D, tt), F32), pltpu.VMEM((eb, tt), F32), pltpu.VMEM((2, eb, tt), BF16),
                        pltpu.VMEM((2, HEADS * SUBLANES, SUBLANES, LANES), F32)],
        compiler_params=_cp("parallel", "arbitrary"))(ht, u, vt, vt, s2, e2, c, e1, h, g.reshape(1, D),
                                                      b.reshape(1, D))


def _rot_half(x, r):
    return jnp.concatenate([-x[..., r:2 * r], x[..., :r]], axis=-1)


def _prep_w_in(w):
    D = w.shape[0]
    splits = (MLA_Q_RANK, MLA_KV_RANK, MLA_ROPE, 3 * BRANCH_WIDTH, 3 * BRANCH_WIDTH, BRANCH_WIDTH, HEADS, HEADS,
              3 * D)
    o = np.cumsum((0,) + splits)
    cq, ckv, kr, mqkv, gqkv, gz, ga, gb, gl = [w[:, o[i]:o[i + 1]] for i in range(9)]
    mq, mk, mv = jnp.split(mqkv, 3, axis=1)

    def slab(m):
        m = m.reshape(D, HEADS, MOBA_HEAD_DIM)
        pad = jnp.zeros((D, HEADS, LANES - MOBA_HEAD_DIM - MOBA_ROT_DIM), w.dtype)
        return jnp.concatenate([m, _rot_half(m, MOBA_ROT_DIM // 2), pad], axis=-1).reshape(D, HEADS * LANES)

    small = jnp.concatenate([ga, gb, jnp.zeros((D, LANES // 2 - 2 * HEADS), w.dtype), kr,
                             _rot_half(kr, MLA_ROPE // 2)], axis=1)
    out = jnp.concatenate([gl, cq, ckv, slab(mq), slab(mk), mv, gqkv, gz], axis=1)
    assert out.shape[1] == C_TOTAL and C_TOTAL % PROJ_TN == 0
    return out.astype(BF16), small.astype(BF16)


def _prep_w_uq(w):
    R = w.shape[0]
    w = w.reshape(R, HEADS, MLA_NOPE + MLA_ROPE)
    rope = w[..., MLA_NOPE:]
    return jnp.concatenate([w, _rot_half(rope, MLA_ROPE // 2)], axis=-1).reshape(R, HEADS * LANES).astype(BF16)


def _prep_w_ukv(w):
    R = w.shape[0]
    w = w.reshape(R, HEADS, MLA_NOPE + MLA_V)
    k = jnp.concatenate([w[..., :MLA_NOPE], jnp.zeros((R, HEADS, LANES - MLA_NOPE), w.dtype)], axis=-1)
    return jnp.concatenate([k.reshape(R, HEADS * LANES), w[..., MLA_NOPE:].reshape(R, HEADS * MLA_V)],
                           axis=1).astype(BF16)


def _rope_tables(positions):
    pos = positions.reshape(-1).astype(F32)[:, None]
    T = pos.shape[0]

    def cs(rot):
        inv = ROPE_THETA ** (-jnp.arange(0, rot, 2, dtype=F32) / rot)
        ang = pos * inv
        return jnp.cos(ang), jnp.sin(ang)

    ca, sa = cs(MLA_ROPE)
    cb, sb = cs(MOBA_ROT_DIM)
    one = lambda n: jnp.ones((T, n), F32)
    zero = lambda n: jnp.zeros((T, n), F32)
    sc_a = (MLA_NOPE + MLA_ROPE) ** -0.5 * math.log2(math.e)
    sc_b = MOBA_HEAD_DIM ** -0.5 * math.log2(math.e)
    cat = lambda *xs: jnp.concatenate(xs, axis=1)
    mla = (cat(one(MLA_NOPE), ca, ca, zero(32)) * sc_a, cat(zero(MLA_NOPE), sa, sa, zero(32)) * sc_a,
           cat(zero(MLA_NOPE), ca, ca, zero(32)), cat(zero(MLA_NOPE), sa, sa, zero(32)))
    cm = cat(cb, cb, one(MOBA_HEAD_DIM - MOBA_ROT_DIM), zero(LANES - MOBA_HEAD_DIM))
    sm = cat(sb, sb, zero(LANES - MOBA_ROT_DIM))
    moba = (cm * sc_b, sm * sc_b, cm, sm)
    return mla, moba


def _lane_row(v):
    return jnp.concatenate([v.astype(F32), jnp.zeros((LANES - v.shape[0],), F32)]).reshape(1, LANES)


def kernel(x, positions, ln_in_g, ln_in_b, w_in, mla_q_norm, mla_kv_norm, mla_w_uq, mla_w_ukv, gdn_conv_w, gdn_A_log, gdn_dt_bias, gdn_o_norm, gate_bias, w_branch, w_out, ln1_g, ln1_b, peer_w_q, peer_sub_keys, peer_u, peer_v, ln2_g, ln2_b):
    B, S, D = x.shape
    T = B * S
    assert S % ATT_TQ == 0 and S % GDN_KCHUNK == 0 and T % PEER_TT == 0 and ATT_TQ == MOBA_BLOCK
    mla_tabs, moba_tabs = _rope_tables(positions)
    h = _layer_norm(x.reshape(T, D), ln_in_g, ln_in_b)
    for l in range(DEPTH):
        w_main, w_small = _prep_w_in(w_in[l])
        proj = _mm(h, w_main, tm=512, tn=PROJ_TN, out_dtype=BF16)
        small = _mm(h, w_small, tm=512, tn=LANES)
        qa, ka, va = _mla_qkv(proj, small, mla_q_norm[l], mla_kv_norm[l], _prep_w_uq(mla_w_uq[l]),
                              _prep_w_ukv(mla_w_ukv[l]), mla_tabs)
        o_a = _attention(qa, ka, va, 0, B, S)
        qm, km, kmean = _moba_prep(proj, moba_tabs)
        o_b = _attention(qm, km, proj, C_MV, B, S, kmean=kmean.reshape(B, S // MOBA_BLOCK, HEADS * LANES))
        qkv = _gdn_conv(proj, gdn_conv_w[l], B, S)
        parts = _gdn_chunks(qkv, small, _lane_row(gdn_A_log[l]), _lane_row(gdn_dt_bias[l]), B, S)
        o_c = _gdn_scan(parts, proj, jnp.tile(gdn_o_norm[l], 2).reshape(1, LANES), B, S)
        h, ht = _merge_out_ln(proj, gate_bias[l], o_a, o_b, o_c, w_branch[l].astype(BF16), w_out[l].astype(BF16),
                              h, ln1_g[l], ln1_b[l])
        qt = _mm(peer_w_q[l].T.astype(BF16), ht, tm=1024, tn=min(1024, T), out_dtype=BF16)
        keys = peer_sub_keys[l].reshape(2 * HEADS, N_KEYS, PEER_HALF).astype(BF16)
        s2, e2, c, e1 = _peer_scores(qt, keys)
        h = _peer_experts(ht, peer_u[l].astype(BF16), peer_v[l].astype(BF16).T, s2, e2, c, e1, h, ln2_g[l], ln2_b[l])
    return h.reshape(B, S, D)
```

```python
import functools
import math

import numpy as np
import jax
import jax.numpy as jnp
from jax import lax
from jax.experimental import pallas as pl
from jax.experimental.pallas import tpu as pltpu

F32 = jnp.float32
BF16 = jnp.bfloat16
HIGHEST = lax.Precision.HIGHEST

DEPTH = 2
ROPE_THETA = 500000.0
NEG_INF = -1e30
LN_EPS = 1e-5
RMS_EPS = 1e-6
DEEPNORM_ALPHA = (2 * DEPTH) ** 0.25
HEADS = 8
MLA_NOPE, MLA_ROPE, MLA_V = 64, 32, 64
MLA_Q_RANK, MLA_KV_RANK = 768, 256
MOBA_HEAD_DIM, MOBA_ROT_DIM, MOBA_BLOCK, MOBA_TOPK = 64, 16, 256, 3
GDN_HEAD_DIM, GDN_CONV = 64, 4
BRANCH_WIDTH = 512
N_KEYS, PEER_TOPK, PEER_HALF = 128, 16, 128
N_EXPERTS = N_KEYS * N_KEYS

LANES = 128
SUBLANES = 8
VMEM_LIMIT = 56 * 1024 * 1024

GDN_KCHUNK = 128
GDN_HEADS_PER_STEP = 8
GDN_SCAN_PAIRS = 2
ATT_TQ = 256
PEER_TT = 512
PEER_EB = 1024
PEER_SCORE_TT = 256
PEER_SCORE_HEADS_PER_TRIP = 2

C_GATE, C_CQ, C_CKV, C_MQ, C_MK, C_MV, C_GQKV, C_GZ, C_TOTAL = (
    0, 3072, 3840, 4096, 5120, 6144, 6656, 8192, 8704)
PROJ_TN = 4352


def _cp(*sem):
    return pltpu.CompilerParams(dimension_semantics=sem, vmem_limit_bytes=VMEM_LIMIT)


def _dot(a, b):
    return jnp.dot(a.astype(BF16), b.astype(BF16), preferred_element_type=F32)


def _dot_nt(a, b):
    return lax.dot_general(a.astype(BF16), b.astype(BF16), (((1,), (1,)), ((), ())),
                           preferred_element_type=F32)


def _ln_rows(y, g, b):
    mu = jnp.mean(y, axis=-1, keepdims=True)
    d = y - mu
    var = jnp.mean(d * d, axis=-1, keepdims=True)
    return d * lax.rsqrt(var + LN_EPS) * g + b


def _sigmoid(x):
    return 1.0 / (1.0 + jnp.exp(-x))


def _ln_kernel(x_ref, g_ref, b_ref, o_ref):
    o_ref[...] = _ln_rows(x_ref[...], g_ref[...], b_ref[...])


def _layer_norm(x, g, b, tm=512):
    T, D = x.shape
    return pl.pallas_call(
        _ln_kernel, grid=(T // tm,),
        in_specs=[pl.BlockSpec((tm, D), lambda i: (i, 0)),
                  pl.BlockSpec((1, D), lambda i: (0, 0)),
                  pl.BlockSpec((1, D), lambda i: (0, 0))],
        out_specs=pl.BlockSpec((tm, D), lambda i: (i, 0)),
        out_shape=jax.ShapeDtypeStruct((T, D), F32),
        compiler_params=_cp("parallel"))(x, g.reshape(1, D), b.reshape(1, D))


def _mm_kernel(x_ref, w_ref, o_ref):
    o_ref[...] = _dot(x_ref[...], w_ref[...]).astype(o_ref.dtype)


def _mm(x, w, tm, tn, out_dtype=F32):
    M, K = x.shape
    N = w.shape[1]
    return pl.pallas_call(
        _mm_kernel, grid=(M // tm, N // tn),
        in_specs=[pl.BlockSpec((tm, K), lambda i, j: (i, 0)),
                  pl.BlockSpec((K, tn), lambda i, j: (0, j))],
        out_specs=pl.BlockSpec((tm, tn), lambda i, j: (i, j)),
        out_shape=jax.ShapeDtypeStruct((M, N), out_dtype),
        compiler_params=_cp("parallel", "parallel"))(x, w)


def _rms_rows(x, g):
    return x * lax.rsqrt(jnp.mean(x * x, axis=-1, keepdims=True) + RMS_EPS) * g


def _head_slabs(x):
    return [x[:, h * LANES:(h + 1) * LANES] for h in range(HEADS)]


def _mla_qkv_kernel(cq_ref, ckv_ref, gq_ref, gkv_ref, wq_ref, wkv_ref, sm_ref, cq_tab, sq_tab, ck_tab, sk_tab,
                    qo_ref, ko_ref, vo_ref):
    W = HEADS * LANES
    q = _dot(_rms_rows(cq_ref[...].astype(F32), gq_ref[...]), wq_ref[...])
    kv = _dot(_rms_rows(ckv_ref[...].astype(F32), gkv_ref[...]), wkv_ref[...])
    vo_ref[...] = kv[:, W:].astype(BF16)
    cq, sq = cq_tab[...], sq_tab[...]
    sm = sm_ref[...]
    kr = sm * ck_tab[...] + pltpu.roll(sm, LANES - MLA_ROPE, 1) * sk_tab[...]
    for h in range(HEADS):
        sl = slice(h * LANES, (h + 1) * LANES)
        qs = q[:, sl]
        qo_ref[:, sl] = (qs * cq + pltpu.roll(qs, LANES - MLA_ROPE, 1) * sq).astype(BF16)
        ko_ref[:, sl] = (kv[:, sl] + kr).astype(BF16)


def _mla_qkv(proj, small, gq, gkv, wq, wkv, tabs, tm=512):
    T = proj.shape[0]
    W = HEADS * LANES
    row = lambda i: (i, 0)
    tab = pl.BlockSpec((tm, LANES), row)
    full = lambda a: pl.BlockSpec(a.shape, lambda i: (0, 0))
    gq, gkv = gq.reshape(1, -1), gkv.reshape(1, -1)
    return pl.pallas_call(
        _mla_qkv_kernel, grid=(T // tm,),
        in_specs=[pl.BlockSpec((tm, MLA_Q_RANK), lambda i: (i, C_CQ // MLA_Q_RANK)),
                  pl.BlockSpec((tm, MLA_KV_RANK), lambda i: (i, C_CKV // MLA_KV_RANK)),
                  full(gq), full(gkv), full(wq), full(wkv), tab, tab, tab, tab, tab],
        out_specs=[pl.BlockSpec((tm, W), row), pl.BlockSpec((tm, W), row), pl.BlockSpec((tm, BRANCH_WIDTH), row)],
        out_shape=[jax.ShapeDtypeStruct((T, W), BF16)] * 2 + [jax.ShapeDtypeStruct((T, BRANCH_WIDTH), BF16)],
        compiler_params=_cp("parallel"))(proj, proj, gq, gkv, wq, wkv, small, *tabs)


def _moba_prep_kernel(q_ref, k_ref, cq_ref, sq_ref, ck_ref, sk_ref, qo_ref, ko_ref, km_ref):
    cq, sq, ck, sk = cq_ref[...], sq_ref[...], ck_ref[...], sk_ref[...]
    q, k = q_ref[...].astype(F32), k_ref[...].astype(F32)
    for h, (qs, ks) in enumerate(zip(_head_slabs(q), _head_slabs(k))):
        sl = slice(h * LANES, (h + 1) * LANES)
        qo_ref[:, sl] = qs * cq + pltpu.roll(qs, LANES // 2, 1) * sq
        kk = ks * ck + pltpu.roll(ks, LANES // 2, 1) * sk
        ko_ref[:, sl] = kk.astype(BF16)
        km_ref[0, :, sl] = jnp.mean(kk, axis=0, keepdims=True)


def _moba_prep(proj, tabs):
    T = proj.shape[0]
    W = HEADS * LANES
    tm = MOBA_BLOCK
    row = lambda i: (i, 0)
    tab = pl.BlockSpec((tm, LANES), row)
    return pl.pallas_call(
        _moba_prep_kernel, grid=(T // tm,),
        in_specs=[pl.BlockSpec((tm, W), lambda i: (i, C_MQ // W)),
                  pl.BlockSpec((tm, W), lambda i: (i, C_MK // W)), tab, tab, tab, tab],
        out_specs=[pl.BlockSpec((tm, W), row), pl.BlockSpec((tm, W), row),
                   pl.BlockSpec((1, 1, W), lambda i: (i, 0, 0))],
        out_shape=[jax.ShapeDtypeStruct((T, W), F32), jax.ShapeDtypeStruct((T, W), BF16),
                   jax.ShapeDtypeStruct((T // tm, 1, W), F32)],
        compiler_params=_cp("parallel"))(proj, proj, *tabs)


def _attn_kernel(*refs, moba, nk):
    if moba:
        q_ref, k_ref, v_ref, km_ref, o_ref, vt_sc, qb_sc, m_sc, l_sc, al_sc, acc_sc, s_sc, p_sc, bias_sc = refs
    else:
        q_ref, k_ref, v_ref, o_ref, vt_sc, m_sc, l_sc, al_sc, acc_sc, s_sc, p_sc = refs
    tq = tk = ATT_TQ
    dv = BRANCH_WIDTH // HEADS
    qi = pl.program_id(1)
    hslab = [slice(h * LANES, (h + 1) * LANES) for h in range(HEADS)]

    @pl.when(qi == 0)
    def _():
        for j in range(nk):
            for c in range(tk // LANES):
                for g in range(BRANCH_WIDTH // LANES):
                    vt_sc[j, g * LANES:(g + 1) * LANES, c * LANES:(c + 1) * LANES] = (
                        v_ref[j * tk + c * LANES:j * tk + (c + 1) * LANES,
                              g * LANES:(g + 1) * LANES].astype(F32).T.astype(BF16))

    if moba:
        blk = lax.broadcasted_iota(jnp.int32, (SUBLANES, tq), 0)
        for h in range(HEADS):
            qf = q_ref[:, hslab[h]]
            qb_sc[:, hslab[h]] = qf.astype(BF16)
            km = km_ref[0, :, hslab[h]]
            if nk < SUBLANES:
                km = jnp.concatenate([km, jnp.zeros((SUBLANES - nk, LANES), F32)], axis=0)
            gate = lax.dot_general(km, qf, (((1,), (1,)), ((), ())), precision=HIGHEST,
                                   preferred_element_type=F32)
            gate = jnp.where(blk < qi, gate, -jnp.inf)
            for n in range(nk):
                gn = gate[n:n + 1, :]
                beats = jnp.where(gate > gn, 1.0, jnp.where((gate == gn) & (blk < n), 1.0, 0.0))
                cnt = jnp.sum(beats, axis=0, keepdims=True)
                bias_sc[h, n] = jnp.where((cnt < MOBA_TOPK) & (n < qi), 0.0, NEG_INF)
    qsrc = qb_sc if moba else q_ref
    m_sc[...] = jnp.full(m_sc.shape, NEG_INF, F32)
    l_sc[...] = jnp.zeros(l_sc.shape, F32)
    acc_sc[...] = jnp.zeros(acc_sc.shape, F32)

    def block(j, diag):
        rows = pl.ds(pl.multiple_of(j * tk, tk), tk)
        for h in range(HEADS):
            s = _dot_nt(k_ref[rows, hslab[h]], qsrc[:, hslab[h]])
            if diag:
                keyi = lax.broadcasted_iota(jnp.int32, (tk, tq), 0)
                qryi = lax.broadcasted_iota(jnp.int32, (tk, tq), 1)
                s = jnp.where(keyi <= qryi, s, NEG_INF)
            elif moba:
                s = s + bias_sc[h, j]
            s_sc[h] = s
        for h in range(HEADS):
            s = s_sc[h]
            m_prev = m_sc[h]
            m_new = jnp.maximum(m_prev, jnp.max(s, axis=0, keepdims=True))
            alpha = jnp.exp2(m_prev - m_new)
            p = jnp.exp2(s - m_new)
            l_sc[h] = alpha * l_sc[h] + jnp.sum(p, axis=0, keepdims=True)
            p_sc[h] = p.astype(BF16)
            al_sc[h] = alpha
            m_sc[h] = m_new
        for h in range(HEADS):
            acc_sc[h] = al_sc[h] * acc_sc[h] + jnp.dot(vt_sc[j, h * dv:(h + 1) * dv, :], p_sc[h],
                                                       preferred_element_type=F32)

    def past(j, carry):
        block(j, False)
        return carry
    lax.fori_loop(0, qi, past, 0)
    block(qi, True)
    ot = jnp.concatenate([acc_sc[h] * (1.0 / l_sc[h]) for h in range(HEADS)], axis=0)
    o_ref[...] = ot.T


def _attention(q, k, v, v_col, B, S, kmean=None):
    T = B * S
    tq = ATT_TQ
    nq = S // tq
    W = HEADS * LANES
    dv = BRANCH_WIDTH // HEADS
    moba = kmean is not None
    assert nq <= SUBLANES and v_col % BRANCH_WIDTH == 0
    in_specs = [pl.BlockSpec((tq, W), lambda b, i: (b * nq + i, 0)),
                pl.BlockSpec((S, W), lambda b, i: (b, 0)),
                pl.BlockSpec((S, BRANCH_WIDTH), lambda b, i: (b, v_col // BRANCH_WIDTH))]
    scratch = [pltpu.VMEM((nq, BRANCH_WIDTH, tq), BF16)]
    if moba:
        scratch.append(pltpu.VMEM((tq, W), BF16))
    row = pltpu.VMEM((HEADS, 1, tq), F32)
    scratch += [row, row, row, pltpu.VMEM((HEADS, dv, tq), F32),
                pltpu.VMEM((HEADS, tq, tq), F32), pltpu.VMEM((HEADS, tq, tq), BF16)]
    args = [q, k, v]
    if moba:
        in_specs.append(pl.BlockSpec((1, nq, W), lambda b, i: (b, 0, 0)))
        scratch.append(pltpu.VMEM((HEADS, nq, 1, tq), F32))
        args.append(kmean)
    return pl.pallas_call(
        functools.partial(_attn_kernel, moba=moba, nk=nq),
        grid=(B, nq), in_specs=in_specs,
        out_specs=pl.BlockSpec((tq, BRANCH_WIDTH), lambda b, i: (b * nq + i, 0)),
        out_shape=jax.ShapeDtypeStruct((T, BRANCH_WIDTH), F32),
        scratch_shapes=scratch,
        compiler_params=_cp("parallel", "arbitrary"))(*args)


def _gdn_conv_kernel(x_ref, w_ref, o_ref):
    c = pl.program_id(1)
    x = x_ref[...].astype(F32)
    w = w_ref[...]
    S = x.shape[0]
    row = lax.broadcasted_iota(jnp.int32, (S, LANES), 0)
    lane = lax.broadcasted_iota(jnp.int32, (S, LANES), 1)
    y = x * w[GDN_CONV - 1:GDN_CONV, :]
    for d in range(1, GDN_CONV):
        xs = jnp.where(row >= d, pltpu.roll(x, d, 0), 0.0)
        y = y + xs * w[GDN_CONV - 1 - d:GDN_CONV - d, :]
    y = y * _sigmoid(y)
    sq = y * y
    lo = lane < GDN_HEAD_DIM
    ss0 = jnp.sum(jnp.where(lo, sq, 0.0), axis=1, keepdims=True)
    ss1 = jnp.sum(jnp.where(lo, 0.0, sq), axis=1, keepdims=True)
    inv = lax.rsqrt(jnp.where(lo, ss0, ss1) + RMS_EPS)
    nqb = BRANCH_WIDTH // LANES
    scale = jnp.where(c < nqb, GDN_HEAD_DIM ** -0.5, 1.0)
    o_ref[...] = jnp.where(c < 2 * nqb, y * inv * scale, y)


def _gdn_conv(proj, conv_w, B, S):
    T = B * S
    nb = 3 * BRANCH_WIDTH // LANES
    return pl.pallas_call(
        _gdn_conv_kernel, grid=(B, nb),
        in_specs=[pl.BlockSpec((S, LANES), lambda b, c: (b, C_GQKV // LANES + c)),
                  pl.BlockSpec((GDN_CONV, LANES), lambda b, c: (0, c))],
        out_specs=pl.BlockSpec((S, LANES), lambda b, c: (b, c)),
        out_shape=jax.ShapeDtypeStruct((T, 3 * BRANCH_WIDTH), F32),
        compiler_params=_cp("parallel", "parallel"))(proj, conv_w)


def _gdn_chunk_kernel(q_ref, k_ref, v_ref, sm_ref, alog_ref, dtb_ref,
                      u_ref, w_ref, qg_ref, qk_ref, kdt_ref, dl_ref):
    C = GDN_KCHUNK
    nh = GDN_HEADS_PER_STEP
    h0 = pl.program_id(1) * nh
    lane = lax.broadcasted_iota(jnp.int32, (C, LANES), 1)
    row = lax.broadcasted_iota(jnp.int32, (C, LANES), 0)
    lo = lane < GDN_HEAD_DIM
    tril = row >= lane
    strict = row > lane
    eye = jnp.where(row == lane, 1.0, 0.0)
    ltri = tril.astype(BF16)

    def pick(ref, k):
        x = ref[:, (k // 2) * LANES:(k // 2 + 1) * LANES]
        return jnp.where(lo, pltpu.roll(x, LANES // 2, 1) if k % 2 else x, 0.0)

    sm = sm_ref[...]
    a = sm + dtb_ref[...]
    softplus = jnp.maximum(a, 0.0) + jnp.log1p(jnp.exp(-jnp.abs(a)))
    garr = -jnp.exp(alog_ref[...]) * softplus
    sig = _sigmoid(sm)
    kh, kb, gc, decay, A = [], [], [], [], []
    for k in range(nh):
        g_col = jnp.sum(jnp.where(lane == h0 + k, garr, 0.0), axis=1, keepdims=True)
        beta = jnp.sum(jnp.where(lane == HEADS + h0 + k, sig, 0.0), axis=1, keepdims=True)
        g1 = jnp.broadcast_to(g_col, (C, LANES))
        gsum = None
        for _ in range(3):
            gb = g1.astype(BF16)
            part = jnp.dot(ltri, gb, preferred_element_type=F32)
            gsum = part if gsum is None else gsum + part
            g1 = g1 - gb.astype(F32)
        gc.append(gsum)
        decay.append(jnp.where(tril, jnp.exp(jnp.where(tril, gsum - gsum.T, 0.0)), 0.0))
        kh.append(pick(k_ref, k))
        kb.append(kh[k] * beta)
        vb = pick(v_ref, k) * beta
        A.append(jnp.where(strict, _dot_nt(kb[k], kh[k]) * decay[k], 0.0))
        eg = jnp.exp(gsum)
        qh = pick(q_ref, k)
        qk_ref[k, 0] = jnp.where(tril, _dot_nt(qh, kh[k]) * decay[k], 0.0).astype(BF16)
        qg_ref[k, 0] = (qh * eg).astype(BF16)
        glast = gsum[C - 1:C, :]
        kdt_ref[k, 0] = (kh[k] * jnp.exp(glast - gsum)).T.astype(BF16)
        dl_ref[k, 0] = jnp.broadcast_to(jnp.exp(glast), (SUBLANES, LANES))
        u_ref[k, 0] = vb
        w_ref[k, 0] = (kb[k] * eg).astype(BF16)
    def joiner(level):
        same = (row >> (level + 1)) == (lane >> (level + 1))
        return same & (((row >> level) & 1) == 1) & (((lane >> level) & 1) == 0)

    P = [eye - jnp.where(joiner(0), A[k], 0.0) for k in range(nh)]
    for level in range(1, int(math.log2(C))):
        msk = joiner(level)
        T1 = [_dot(P[k], jnp.where(msk, A[k], 0.0)) for k in range(nh)]
        P = [P[k] - _dot(T1[k], P[k]) for k in range(nh)]
    for k in range(nh):
        u_ref[k, 0] = _dot(P[k], u_ref[k, 0])
        w_ref[k, 0] = _dot(P[k], w_ref[k, 0]).astype(BF16)


def _gdn_chunks(qkv, small, alog_row, dtb_row, B, S):
    C = GDN_KCHUNK
    n = S // C
    nh = GDN_HEADS_PER_STEP
    ng = HEADS // nh
    wq = nh // 2 * LANES
    nqb = BRANCH_WIDTH // wq
    big = lambda dt: jax.ShapeDtypeStruct((B * HEADS, n, C, LANES), dt)
    ospec = pl.BlockSpec((nh, 1, C, LANES), lambda b, g, c: (b * ng + g, c, 0, 0))
    par = pl.BlockSpec((1, LANES), lambda b, g, c: (0, 0))
    outs = pl.pallas_call(
        _gdn_chunk_kernel, grid=(B, ng, n),
        in_specs=[pl.BlockSpec((C, wq), lambda b, g, c: (b * n + c, g)),
                  pl.BlockSpec((C, wq), lambda b, g, c: (b * n + c, nqb + g)),
                  pl.BlockSpec((C, wq), lambda b, g, c: (b * n + c, 2 * nqb + g)),
                  pl.BlockSpec((C, LANES), lambda b, g, c: (b * n + c, 0)), par, par],
        out_specs=[ospec] * 5 + [pl.BlockSpec((nh, 1, SUBLANES, LANES), lambda b, g, c: (b * ng + g, c, 0, 0))],
        out_shape=[big(F32)] + [big(BF16)] * 4 + [jax.ShapeDtypeStruct((B * HEADS, n, SUBLANES, LANES), F32)],
        compiler_params=_cp("parallel", "parallel", "parallel"))(qkv, qkv, qkv, small, alog_row, dtb_row)
    return [o.reshape(-1, LANES) for o in outs]


def _gdn_scan_kernel(u_ref, w_ref, qg_ref, qk_ref, kdt_ref, dl_ref, z_ref, g_ref, o_ref, o_sc, *, n):
    C = GDN_KCHUNK
    S = n * C
    npair = GDN_SCAN_PAIRS

    def step(c, states):
        out = []
        for k, state in enumerate(states):
            r = pl.ds(pl.multiple_of(k * S + c * C, C), C)
            sb = state.astype(BF16)
            v_new = u_ref[r, :] - _dot(w_ref[r, :], sb)
            vb = v_new.astype(BF16)
            o_sc[r, :] = _dot(qg_ref[r, :], sb) + _dot(qk_ref[r, :], vb)
            dl = dl_ref[pl.ds(pl.multiple_of((k * n + c) * SUBLANES, SUBLANES), 1), :]
            out.append(state * dl + _dot(kdt_ref[r, :], vb))
        return tuple(out)

    lax.fori_loop(0, n, step, tuple(jnp.zeros((LANES, LANES), F32) for _ in range(2 * npair)))
    lane = lax.broadcasted_iota(jnp.int32, (S, LANES), 1)

    def nrm(o):
        ms = jnp.sum(o * o, axis=1, keepdims=True) * (1.0 / GDN_HEAD_DIM)
        return o * lax.rsqrt(ms + RMS_EPS)

    for p in range(npair):
        ps = slice(p * LANES, (p + 1) * LANES)
        nn = jnp.where(lane < GDN_HEAD_DIM, nrm(o_sc[2 * p * S:(2 * p + 1) * S, :]),
                       pltpu.roll(nrm(o_sc[(2 * p + 1) * S:(2 * p + 2) * S, :]), LANES // 2, 1))
        z = z_ref[:, ps].astype(F32)
        o_ref[:, ps] = nn * g_ref[...] * (z * _sigmoid(z))


def _gdn_scan(parts, proj, g_row, B, S):
    n = S // GDN_KCHUNK
    nps = GDN_SCAN_PAIRS
    steps = HEADS // 2 // nps
    wz = nps * LANES
    seq = pl.BlockSpec((2 * nps * S, LANES), lambda i: (i, 0))
    return pl.pallas_call(
        functools.partial(_gdn_scan_kernel, n=n), grid=(B * steps,),
        in_specs=[seq] * 5 + [pl.BlockSpec((2 * nps * n * SUBLANES, LANES), lambda i: (i, 0)),
                              pl.BlockSpec((S, wz), lambda i: (i // steps, C_GZ // wz + i % steps)),
                              pl.BlockSpec((1, LANES), lambda i: (0, 0))],
        out_specs=pl.BlockSpec((S, wz), lambda i: (i // steps, i % steps)),
        out_shape=jax.ShapeDtypeStruct((B * S, BRANCH_WIDTH), F32),
        scratch_shapes=[pltpu.VMEM((2 * nps * S, LANES), F32)],
        compiler_params=_cp("parallel"))(*parts, proj, g_row)


def _merge_out_kernel(g0_ref, g1_ref, g2_ref, gb_ref, a_ref, b_ref, c_ref, wb_ref, wo_ref, h_ref, lg_ref, lb_ref,
                      o_ref, ot_ref):
    acc = None
    for n, (gl, br) in enumerate(((g0_ref, a_ref), (g1_ref, b_ref), (g2_ref, c_ref))):
        y = _sigmoid(gl[...].astype(F32) + gb_ref[n:n + 1, :]) * _dot(br[...], wb_ref[n])
        acc = y if acc is None else acc + y
    y = DEEPNORM_ALPHA * h_ref[...] + _dot(acc, wo_ref[...])
    y = _ln_rows(y, lg_ref[...], lb_ref[...])
    o_ref[...] = y
    ot_ref[...] = y.T.astype(BF16)


def _merge_out_ln(proj, gate_bias, o_a, o_b, o_c, w_branch, w_out, h, g, b, tm=256):
    T, D = h.shape
    row = lambda i: (i, 0)
    gspec = lambda n: pl.BlockSpec((tm, D), lambda i: (i, n))
    bspec = pl.BlockSpec((tm, BRANCH_WIDTH), row)
    vec = pl.BlockSpec((1, D), lambda i: (0, 0))
    return pl.pallas_call(
        _merge_out_kernel, grid=(T // tm,),
        in_specs=[gspec(0), gspec(1), gspec(2), pl.BlockSpec((3, D), lambda i: (0, 0)), bspec, bspec, bspec,
                  pl.BlockSpec((3, BRANCH_WIDTH, D), lambda i: (0, 0, 0)), pl.BlockSpec((D, D), lambda i: (0, 0)),
                  pl.BlockSpec((tm, D), row), vec, vec],
        out_specs=[pl.BlockSpec((tm, D), row), pl.BlockSpec((D, tm), lambda i: (0, i))],
        out_shape=[jax.ShapeDtypeStruct((T, D), F32), jax.ShapeDtypeStruct((D, T), BF16)],
        compiler_params=_cp("parallel"))(proj, proj, proj, gate_bias, o_a, o_b, o_c, w_branch, w_out, h,
                                         g.reshape(1, D), b.reshape(1, D))


def _extract_top(curs, n, on_max):
    curs = list(curs)
    for k in range(n):
        for i, cur in enumerate(curs):
            m = jnp.max(cur, axis=0, keepdims=True)
            on_max(i, k, m)
            if k + 1 < n:
                curs[i] = jnp.where(cur >= m, -jnp.inf, cur)


def _sort_network(n):
    pairs, p = [], 1
    while p < n:
        k = p
        while k >= 1:
            for j in range(k % p, n - k, 2 * k):
                for i in range(min(k, n - j - k)):
                    if (i + j) // (2 * p) == (i + j + k) // (2 * p):
                        pairs.append((i + j, i + j + k))
            k //= 2
        p *= 2
    return pairs


def _top_of_sorted_lists(scores, n, on_max):
    nt = N_KEYS // SUBLANES
    lists = []
    for s in scores:
        rows = [s[t * SUBLANES:(t + 1) * SUBLANES, :] for t in range(nt)]
        for a, b in _sort_network(nt):
            rows[a], rows[b] = jnp.maximum(rows[a], rows[b]), jnp.minimum(rows[a], rows[b])
        lists.append(rows + [jnp.full(rows[0].shape, -jnp.inf, F32)])
    for k in range(n):
        for i, rows in enumerate(lists):
            m = jnp.max(rows[0], axis=0, keepdims=True)
            on_max(i, k, m)
            hit = rows[0] >= m
            for t in range(min(n - 1 - k, nt)):
                rows[t] = jnp.where(hit, rows[t + 1], rows[t])


def _peer_score_kernel(qt_ref, keys_ref, s2_ref, e2_ref, c_ref, e1_ref, top_sc, cand_sc):
    K = PEER_TOPK
    tt = qt_ref.shape[1]
    nh = PEER_SCORE_HEADS_PER_TRIP

    def heads(hp, carry):
        s = []
        for i in range(2 * nh):
            r = pl.multiple_of((hp * nh * 2 + i) * PEER_HALF, PEER_HALF)
            s.append(_dot(keys_ref[hp * nh * 2 + i], qt_ref[pl.ds(r, PEER_HALF), :]))

        def put(i, k, m):
            top_sc[i, k:k + 1, :] = m
        _top_of_sorted_lists(s, K + 1, put)
        r8 = lax.broadcasted_iota(jnp.int32, (SUBLANES, tt), 0)
        v1max, v2max = [], []
        for d in range(nh):
            v1a, v1x = top_sc[2 * d, 0:K, :], top_sc[2 * d, K:K + 1, :]
            v2a, v2x = top_sc[2 * d + 1, 0:K, :], top_sc[2 * d + 1, K:K + 1, :]
            v1max.append(v1a[0:1])
            v2max.append(v2a[0:1])
            cand_sc[d, 0:K, :] = v1a + v2a[0:1]
            for b in range(1, SUBLANES):
                cand_sc[d, K + SUBLANES * (b - 1):K + SUBLANES * b, :] = v1a[0:SUBLANES] + v2a[b:b + 1]
            base = K + SUBLANES * (SUBLANES - 1)
            cand_sc[d, base:base + SUBLANES, :] = v1a[0:1] + v2a[SUBLANES:K]
            cand_sc[d, base + SUBLANES:base + 2 * SUBLANES, :] = jnp.where(
                r8 == 0, v1x + v2a[0:1], jnp.where(r8 == 1, v1a[0:1] + v2x, -jnp.inf))
        st = [{"z": jnp.zeros((1, tt), F32)} for _ in range(nh)]

        def acc(d, k, m):
            if k == 0:
                st[d]["top"] = m
            if k < K:
                st[d]["z"] = st[d]["z"] + jnp.exp(m - st[d]["top"])
            if k == K - 1:
                st[d]["t16"] = m
            if k == K:
                st[d]["t17"] = m
        _extract_top([cand_sc[d] for d in range(nh)], K + 1, acc)
        for d in range(nh):
            tau = 0.5 * (st[d]["t16"] + st[d]["t17"])
            ro = pl.ds(pl.multiple_of((hp * nh + d) * N_KEYS, N_KEYS), N_KEYS)
            s2_ref[ro, :] = s[2 * d + 1]
            e2_ref[ro, :] = jnp.exp(s[2 * d + 1] - v2max[d]) / st[d]["z"]
            c_ref[ro, :] = tau - s[2 * d]
            e1_ref[ro, :] = jnp.exp(s[2 * d] - v1max[d])
        return carry

    lax.fori_loop(0, HEADS // nh, heads, 0)


def _peer_scores(qt, keys):
    T = qt.shape[1]
    tt = PEER_SCORE_TT
    R = HEADS * N_KEYS
    ncand = PEER_TOPK + SUBLANES * (SUBLANES + 1)
    ospec = pl.BlockSpec((R, tt), lambda i: (0, i))
    return pl.pallas_call(
        _peer_score_kernel, grid=(T // tt,),
        in_specs=[pl.BlockSpec((2 * R, tt), lambda i: (0, i)),
                  pl.BlockSpec((2 * HEADS, N_KEYS, PEER_HALF), lambda i: (0, 0, 0))],
        out_specs=[ospec] * 4,
        out_shape=[jax.ShapeDtypeStruct((R, T), F32)] * 4,
        scratch_shapes=[pltpu.VMEM((2 * PEER_SCORE_HEADS_PER_TRIP, 3 * SUBLANES, tt), F32),
                        pltpu.VMEM((PEER_SCORE_HEADS_PER_TRIP, ncand, tt), F32)],
        compiler_params=_cp("parallel"))(qt, keys)


def _peer_expert_kernel(ht_ref, u_ref, vtp_ref, vtl_ref, s2_ref, e2_ref, c_ref, e1_ref, h_ref, g_ref, b_ref,
                        o_ref, acc_sc, act_sc, hw_sc, bc_sc):
    j = pl.program_id(1)
    _, eb, tt = hw_sc.shape
    ng = eb // N_KEYS
    assert ng == SUBLANES
    slot = j % 2

    @pl.when(j == 0)
    def _():
        acc_sc[...] = jnp.zeros(acc_sc.shape, F32)
        hw_sc[1] = jnp.zeros(hw_sc.shape[1:], BF16)

    mxu_w = 2 * LANES
    for half in range(tt // mxu_w):
        hl = slice(half * mxu_w, (half + 1) * mxu_w)
        a = jnp.dot(u_ref[...], ht_ref[:, hl], preferred_element_type=F32)
        act_sc[:, hl] = 0.5 * a * (1.0 + lax.erf(a * (2.0 ** -0.5)))
        acc_sc[:, hl] += jnp.dot(vtp_ref[...], hw_sc[1 - slot, :, hl], preferred_element_type=F32)
        for lc in range(half * (mxu_w // LANES), (half + 1) * (mxu_w // LANES)):
            ls = slice(lc * LANES, (lc + 1) * LANES)
            for h in range(HEADS):
                r8 = pl.ds(pl.multiple_of(h * N_KEYS + j * ng, ng), ng)
                thr8, e18 = c_ref[r8, ls], e1_ref[r8, ls]
                for g in range(ng):
                    bc_sc[0, h * ng + g] = jnp.broadcast_to(thr8[g:g + 1], (SUBLANES, LANES))
                    bc_sc[1, h * ng + g] = jnp.broadcast_to(e18[g:g + 1], (SUBLANES, LANES))
            for g in range(ng):
                wsum = jnp.zeros((N_KEYS // SUBLANES, SUBLANES, LANES), F32)
                for h in range(HEADS):
                    hr = slice(h * N_KEYS, (h + 1) * N_KEYS)
                    s2 = s2_ref[hr, ls].reshape(N_KEYS // SUBLANES, SUBLANES, LANES)
                    e2 = e2_ref[hr, ls].reshape(N_KEYS // SUBLANES, SUBLANES, LANES)
                    wsum = wsum + jnp.where(s2 >= bc_sc[0, h * ng + g][None], e2 * bc_sc[1, h * ng + g][None], 0.0)
                gs = slice(g * N_KEYS, (g + 1) * N_KEYS)
                hw_sc[slot, gs, ls] = (wsum.reshape(N_KEYS, LANES) * act_sc[gs, ls]).astype(BF16)

    @pl.when(j == pl.num_programs(1) - 1)
    def _():
        acc = acc_sc[...] + jnp.dot(vtl_ref[...], hw_sc[slot], preferred_element_type=F32)
        y = DEEPNORM_ALPHA * h_ref[...] + acc.T
        o_ref[...] = _ln_rows(y, g_ref[...], b_ref[...])


def _peer_experts(ht, u, vt, s2, e2, c, e1, h, g, b):
    T, D = h.shape
    tt, eb = PEER_TT, PEER_EB
    R = HEADS * N_KEYS
    nb = N_EXPERTS // eb
    assert nb % 2 == 0
    sspec = pl.BlockSpec((R, tt), lambda i, j: (0, i))
    vec = pl.BlockSpec((1, D), lambda i, j: (0, 0))
    return pl.pallas_call(
        _peer_expert_kernel, grid=(T // tt, nb),
        in_specs=[pl.BlockSpec((D, tt), lambda i, j: (0, i)),
                  pl.BlockSpec((eb, D), lambda i, j: (j, 0)),
                  pl.BlockSpec((D, eb), lambda i, j: (0, jnp.maximum(j - 1, 0))),
                  pl.BlockSpec((D, eb), lambda i, j: (0, nb - 1)),
                  sspec, sspec, sspec, sspec,
                  pl.BlockSpec((tt, D), lambda i, j: (i, 0)), vec, vec],
        out_specs=pl.BlockSpec((tt, D), lambda i, j: (i, 0)),
        out_shape=jax.ShapeDtypeStruct((T, D), F32),
        scratch_shapes=[pltpu.VMEM((D, tt), F32), pltpu.VMEM((eb, tt), F32), pltpu.VMEM((2, eb, tt), BF16),
                        pltpu.VMEM((2, HEADS * SUBLANES, SUBLANES, LANES), F32)],
        compiler_params=_cp("parallel", "arbitrary"))(ht, u, vt, vt, s2, e2, c, e1, h, g.reshape(1, D),
                                                      b.reshape(1, D))


def _rot_half(x, r):
    return jnp.concatenate([-x[..., r:2 * r], x[..., :r]], axis=-1)


def _prep_w_in(w):
    D = w.shape[0]
    splits = (MLA_Q_RANK, MLA_KV_RANK, MLA_ROPE, 3 * BRANCH_WIDTH, 3 * BRANCH_WIDTH, BRANCH_WIDTH, HEADS, HEADS,
              3 * D)
    o = np.cumsum((0,) + splits)
    cq, ckv, kr, mqkv, gqkv, gz, ga, gb, gl = [w[:, o[i]:o[i + 1]] for i in range(9)]
    mq, mk, mv = jnp.split(mqkv, 3, axis=1)

    def slab(m):
        m = m.reshape(D, HEADS, MOBA_HEAD_DIM)
        pad = jnp.zeros((D, HEADS, LANES - MOBA_HEAD_DIM - MOBA_ROT_DIM), w.dtype)
        return jnp.concatenate([m, _rot_half(m, MOBA_ROT_DIM // 2), pad], axis=-1).reshape(D, HEADS * LANES)

    small = jnp.concatenate([ga, gb, jnp.zeros((D, LANES // 2 - 2 * HEADS), w.dtype), kr,
                             _rot_half(kr, MLA_ROPE // 2)], axis=1)
    out = jnp.concatenate([gl, cq, ckv, slab(mq), slab(mk), mv, gqkv, gz], axis=1)
    assert out.shape[1] == C_TOTAL and C_TOTAL % PROJ_TN == 0
    return out.astype(BF16), small.astype(BF16)


def _prep_w_uq(w):
    R = w.shape[0]
    w = w.reshape(R, HEADS, MLA_NOPE + MLA_ROPE)
    rope = w[..., MLA_NOPE:]
    return jnp.concatenate([w, _rot_half(rope, MLA_ROPE // 2)], axis=-1).reshape(R, HEADS * LANES).astype(BF16)


def _prep_w_ukv(w):
    R = w.shape[0]
    w = w.reshape(R, HEADS, MLA_NOPE + MLA_V)
    k = jnp.concatenate([w[..., :MLA_NOPE], jnp.zeros((R, HEADS, LANES - MLA_NOPE), w.dtype)], axis=-1)
    return jnp.concatenate([k.reshape(R, HEADS * LANES), w[..., MLA_NOPE:].reshape(R, HEADS * MLA_V)],
                           axis=1).astype(BF16)


def _rope_tables(positions):
    pos = positions.reshape(-1).astype(F32)[:, None]
    T = pos.shape[0]

    def cs(rot):
        inv = ROPE_THETA ** (-jnp.arange(0, rot, 2, dtype=F32) / rot)
        ang = pos * inv
        return jnp.cos(ang), jnp.sin(ang)

    ca, sa = cs(MLA_ROPE)
    cb, sb = cs(MOBA_ROT_DIM)
    one = lambda n: jnp.ones((T, n), F32)
    zero = lambda n: jnp.zeros((T, n), F32)
    sc_a = (MLA_NOPE + MLA_ROPE) ** -0.5 * math.log2(math.e)
    sc_b = MOBA_HEAD_DIM ** -0.5 * math.log2(math.e)
    cat = lambda *xs: jnp.concatenate(xs, axis=1)
    mla = (cat(one(MLA_NOPE), ca, ca, zero(32)) * sc_a, cat(zero(MLA_NOPE), sa, sa, zero(32)) * sc_a,
           cat(zero(MLA_NOPE), ca, ca, zero(32)), cat(zero(MLA_NOPE), sa, sa, zero(32)))
    cm = cat(cb, cb, one(MOBA_HEAD_DIM - MOBA_ROT_DIM), zero(LANES - MOBA_HEAD_DIM))
    sm = cat(sb, sb, zero(LANES - MOBA_ROT_DIM))
    moba = (cm * sc_b, sm * sc_b, cm, sm)
    return mla, moba


def _lane_row(v):
    return jnp.concatenate([v.astype(F32), jnp.zeros((LANES - v.shape[0],), F32)]).reshape(1, LANES)


def kernel(x, positions, ln_in_g, ln_in_b, w_in, mla_q_norm, mla_kv_norm, mla_w_uq, mla_w_ukv, gdn_conv_w, gdn_A_log, gdn_dt_bias, gdn_o_norm, gate_bias, w_branch, w_out, ln1_g, ln1_b, peer_w_q, peer_sub_keys, peer_u, peer_v, ln2_g, ln2_b):
    B, S, D = x.shape
    T = B * S
    assert S % ATT_TQ == 0 and S % GDN_KCHUNK == 0 and T % PEER_TT == 0 and ATT_TQ == MOBA_BLOCK
    mla_tabs, moba_tabs = _rope_tables(positions)
    h = _layer_norm(x.reshape(T, D), ln_in_g, ln_in_b)
    for l in range(DEPTH):
        w_main, w_small = _prep_w_in(w_in[l])
        proj = _mm(h, w_main, tm=512, tn=PROJ_TN, out_dtype=BF16)
        small = _mm(h, w_small, tm=512, tn=LANES)
        qa, ka, va = _mla_qkv(proj, small, mla_q_norm[l], mla_kv_norm[l], _prep_w_uq(mla_w_uq[l]),
                              _prep_w_ukv(mla_w_ukv[l]), mla_tabs)
        o_a = _attention(qa, ka, va, 0, B, S)
        qm, km, kmean = _moba_prep(proj, moba_tabs)
        o_b = _attention(qm, km, proj, C_MV, B, S, kmean=kmean.reshape(B, S // MOBA_BLOCK, HEADS * LANES))
        qkv = _gdn_conv(proj, gdn_conv_w[l], B, S)
        parts = _gdn_chunks(qkv, small, _lane_row(gdn_A_log[l]), _lane_row(gdn_dt_bias[l]), B, S)
        o_c = _gdn_scan(parts, proj, jnp.tile(gdn_o_norm[l], 2).reshape(1, LANES), B, S)
        h, ht = _merge_out_ln(proj, gate_bias[l], o_a, o_b, o_c, w_branch[l].astype(BF16), w_out[l].astype(BF16),
                              h, ln1_g[l], ln1_b[l])
        qt = _mm(peer_w_q[l].T.astype(BF16), ht, tm=1024, tn=min(1024, T), out_dtype=BF16)
        keys = peer_sub_keys[l].reshape(2 * HEADS, N_KEYS, PEER_HALF).astype(BF16)
        s2, e2, c, e1 = _peer_scores(qt, keys)
        h = _peer_experts(ht, peer_u[l].astype(BF16), peer_v[l].astype(BF16).T, s2, e2, c, e1, h, ln2_g[l], ln2_b[l])
    return h.reshape(B, S, D)
```

```python
import functools
import math

import numpy as np
import jax
import jax.numpy as jnp
from jax import lax
from jax.experimental import pallas as pl
from jax.experimental.pallas import tpu as pltpu

F32 = jnp.float32
BF16 = jnp.bfloat16
HIGHEST = lax.Precision.HIGHEST

DEPTH = 2
ROPE_THETA = 500000.0
NEG_INF = -1e30
LN_EPS = 1e-5
RMS_EPS = 1e-6
DEEPNORM_ALPHA = (2 * DEPTH) ** 0.25
HEADS = 8
MLA_NOPE, MLA_ROPE, MLA_V = 64, 32, 64
MLA_Q_RANK, MLA_KV_RANK = 768, 256
MOBA_HEAD_DIM, MOBA_ROT_DIM, MOBA_BLOCK, MOBA_TOPK = 64, 16, 256, 3
GDN_HEAD_DIM, GDN_CONV = 64, 4
BRANCH_WIDTH = 512
N_KEYS, PEER_TOPK, PEER_HALF = 128, 16, 128
N_EXPERTS = N_KEYS * N_KEYS

LANES = 128
SUBLANES = 8
VMEM_LIMIT = 56 * 1024 * 1024

GDN_KCHUNK = 128
GDN_HEADS_PER_STEP = 8
GDN_SCAN_PAIRS = 2
ATT_TQ = 256
PEER_TT = 512
PEER_EB = 2048
PEER_SCORE_TT = 256
PEER_SCORE_HEADS_PER_TRIP = 4

C_GATE, C_CQ, C_CKV, C_MQ, C_MK, C_MV, C_GQKV, C_GZ, C_TOTAL = (
    0, 3072, 3840, 4096, 5120, 6144, 6656, 8192, 8704)
PROJ_TN = 4352


def _cp(*sem):
    return pltpu.CompilerParams(dimension_semantics=sem, vmem_limit_bytes=VMEM_LIMIT)


def _dot(a, b):
    return jnp.dot(a.astype(BF16), b.astype(BF16), preferred_element_type=F32)


def _dot_nt(a, b):
    return lax.dot_general(a.astype(BF16), b.astype(BF16), (((1,), (1,)), ((), ())),
                           preferred_element_type=F32)


def _ln_rows(y, g, b):
    mu = jnp.mean(y, axis=-1, keepdims=True)
    d = y - mu
    var = jnp.mean(d * d, axis=-1, keepdims=True)
    return d * lax.rsqrt(var + LN_EPS) * g + b


def _sigmoid(x):
    return 1.0 / (1.0 + jnp.exp(-x))


def _ln_kernel(x_ref, g_ref, b_ref, o_ref):
    o_ref[...] = _ln_rows(x_ref[...], g_ref[...], b_ref[...])


def _layer_norm(x, g, b, tm=512):
    T, D = x.shape
    return pl.pallas_call(
        _ln_kernel, grid=(T // tm,),
        in_specs=[pl.BlockSpec((tm, D), lambda i: (i, 0)),
                  pl.BlockSpec((1, D), lambda i: (0, 0)),
                  pl.BlockSpec((1, D), lambda i: (0, 0))],
        out_specs=pl.BlockSpec((tm, D), lambda i: (i, 0)),
        out_shape=jax.ShapeDtypeStruct((T, D), F32),
        compiler_params=_cp("parallel"))(x, g.reshape(1, D), b.reshape(1, D))


def _mm_kernel(x_ref, w_ref, o_ref):
    o_ref[...] = _dot(x_ref[...], w_ref[...]).astype(o_ref.dtype)


def _mm(x, w, tm, tn, out_dtype=F32):
    M, K = x.shape
    N = w.shape[1]
    return pl.pallas_call(
        _mm_kernel, grid=(M // tm, N // tn),
        in_specs=[pl.BlockSpec((tm, K), lambda i, j: (i, 0)),
                  pl.BlockSpec((K, tn), lambda i, j: (0, j))],
        out_specs=pl.BlockSpec((tm, tn), lambda i, j: (i, j)),
        out_shape=jax.ShapeDtypeStruct((M, N), out_dtype),
        compiler_params=_cp("parallel", "parallel"))(x, w)


def _rms_rows(x, g):
    return x * lax.rsqrt(jnp.mean(x * x, axis=-1, keepdims=True) + RMS_EPS) * g


def _head_slabs(x):
    return [x[:, h * LANES:(h + 1) * LANES] for h in range(HEADS)]


def _mla_qkv_kernel(cq_ref, ckv_ref, gq_ref, gkv_ref, wq_ref, wkv_ref, sm_ref, cq_tab, sq_tab, ck_tab, sk_tab,
                    qo_ref, ko_ref, vo_ref):
    W = HEADS * LANES
    q = _dot(_rms_rows(cq_ref[...].astype(F32), gq_ref[...]), wq_ref[...])
    kv = _dot(_rms_rows(ckv_ref[...].astype(F32), gkv_ref[...]), wkv_ref[...])
    vo_ref[...] = kv[:, W:].astype(BF16)
    cq, sq = cq_tab[...], sq_tab[...]
    sm = sm_ref[...]
    kr = sm * ck_tab[...] + pltpu.roll(sm, LANES - MLA_ROPE, 1) * sk_tab[...]
    for h in range(HEADS):
        sl = slice(h * LANES, (h + 1) * LANES)
        qs = q[:, sl]
        qo_ref[:, sl] = (qs * cq + pltpu.roll(qs, LANES - MLA_ROPE, 1) * sq).astype(BF16)
        ko_ref[:, sl] = (kv[:, sl] + kr).astype(BF16)


def _mla_qkv(proj, small, gq, gkv, wq, wkv, tabs, tm=512):
    T = proj.shape[0]
    W = HEADS * LANES
    row = lambda i: (i, 0)
    tab = pl.BlockSpec((tm, LANES), row)
    full = lambda a: pl.BlockSpec(a.shape, lambda i: (0, 0))
    gq, gkv = gq.reshape(1, -1), gkv.reshape(1, -1)
    return pl.pallas_call(
        _mla_qkv_kernel, grid=(T // tm,),
        in_specs=[pl.BlockSpec((tm, MLA_Q_RANK), lambda i: (i, C_CQ // MLA_Q_RANK)),
                  pl.BlockSpec((tm, MLA_KV_RANK), lambda i: (i, C_CKV // MLA_KV_RANK)),
                  full(gq), full(gkv), full(wq), full(wkv), tab, tab, tab, tab, tab],
        out_specs=[pl.BlockSpec((tm, W), row), pl.BlockSpec((tm, W), row), pl.BlockSpec((tm, BRANCH_WIDTH), row)],
        out_shape=[jax.ShapeDtypeStruct((T, W), BF16)] * 2 + [jax.ShapeDtypeStruct((T, BRANCH_WIDTH), BF16)],
        compiler_params=_cp("parallel"))(proj, proj, gq, gkv, wq, wkv, small, *tabs)


def _moba_prep_kernel(q_ref, k_ref, cq_ref, sq_ref, ck_ref, sk_ref, qo_ref, ko_ref, km_ref):
    cq, sq, ck, sk = cq_ref[...], sq_ref[...], ck_ref[...], sk_ref[...]
    q, k = q_ref[...].astype(F32), k_ref[...].astype(F32)
    for h, (qs, ks) in enumerate(zip(_head_slabs(q), _head_slabs(k))):
        sl = slice(h * LANES, (h + 1) * LANES)
        qo_ref[:, sl] = qs * cq + pltpu.roll(qs, LANES // 2, 1) * sq
        kk = ks * ck + pltpu.roll(ks, LANES // 2, 1) * sk
        ko_ref[:, sl] = kk.astype(BF16)
        km_ref[0, :, sl] = jnp.mean(kk, axis=0, keepdims=True)


def _moba_prep(proj, tabs):
    T = proj.shape[0]
    W = HEADS * LANES
    tm = MOBA_BLOCK
    row = lambda i: (i, 0)
    tab = pl.BlockSpec((tm, LANES), row)
    return pl.pallas_call(
        _moba_prep_kernel, grid=(T // tm,),
        in_specs=[pl.BlockSpec((tm, W), lambda i: (i, C_MQ // W)),
                  pl.BlockSpec((tm, W), lambda i: (i, C_MK // W)), tab, tab, tab, tab],
        out_specs=[pl.BlockSpec((tm, W), row), pl.BlockSpec((tm, W), row),
                   pl.BlockSpec((1, 1, W), lambda i: (i, 0, 0))],
        out_shape=[jax.ShapeDtypeStruct((T, W), F32), jax.ShapeDtypeStruct((T, W), BF16),
                   jax.ShapeDtypeStruct((T // tm, 1, W), F32)],
        compiler_params=_cp("parallel"))(proj, proj, *tabs)


def _attn_kernel(*refs, moba, nk):
    if moba:
        q_ref, k_ref, v_ref, km_ref, o_ref, vt_sc, qb_sc, m_sc, l_sc, al_sc, acc_sc, s_sc, p_sc, bias_sc = refs
    else:
        q_ref, k_ref, v_ref, o_ref, vt_sc, m_sc, l_sc, al_sc, acc_sc, s_sc, p_sc = refs
    tq = tk = ATT_TQ
    dv = BRANCH_WIDTH // HEADS
    qi = pl.program_id(1)
    hslab = [slice(h * LANES, (h + 1) * LANES) for h in range(HEADS)]

    @pl.when(qi == 0)
    def _():
        for j in range(nk):
            for c in range(tk // LANES):
                for g in range(BRANCH_WIDTH // LANES):
                    vt_sc[j, g * LANES:(g + 1) * LANES, c * LANES:(c + 1) * LANES] = (
                        v_ref[j * tk + c * LANES:j * tk + (c + 1) * LANES,
                              g * LANES:(g + 1) * LANES].astype(F32).T.astype(BF16))

    if moba:
        blk = lax.broadcasted_iota(jnp.int32, (SUBLANES, tq), 0)
        for h in range(HEADS):
            qf = q_ref[:, hslab[h]]
            qb_sc[:, hslab[h]] = qf.astype(BF16)
            km = km_ref[0, :, hslab[h]]
            if nk < SUBLANES:
                km = jnp.concatenate([km, jnp.zeros((SUBLANES - nk, LANES), F32)], axis=0)
            gate = lax.dot_general(km, qf, (((1,), (1,)), ((), ())), precision=HIGHEST,
                                   preferred_element_type=F32)
            gate = jnp.where(blk < qi, gate, -jnp.inf)
            for n in range(nk):
                gn = gate[n:n + 1, :]
                beats = jnp.where(gate > gn, 1.0, jnp.where((gate == gn) & (blk < n), 1.0, 0.0))
                cnt = jnp.sum(beats, axis=0, keepdims=True)
                bias_sc[h, n] = jnp.where((cnt < MOBA_TOPK) & (n < qi), 0.0, NEG_INF)
    qsrc = qb_sc if moba else q_ref
    m_sc[...] = jnp.full(m_sc.shape, NEG_INF, F32)
    l_sc[...] = jnp.zeros(l_sc.shape, F32)
    acc_sc[...] = jnp.zeros(acc_sc.shape, F32)

    def block(j, diag):
        rows = pl.ds(pl.multiple_of(j * tk, tk), tk)
        for h in range(HEADS):
            s = _dot_nt(k_ref[rows, hslab[h]], qsrc[:, hslab[h]])
            if diag:
                keyi = lax.broadcasted_iota(jnp.int32, (tk, tq), 0)
                qryi = lax.broadcasted_iota(jnp.int32, (tk, tq), 1)
                s = jnp.where(keyi <= qryi, s, NEG_INF)
            elif moba:
                s = s + bias_sc[h, j]
            s_sc[h] = s
        for h in range(HEADS):
            s = s_sc[h]
            m_prev = m_sc[h]
            m_new = jnp.maximum(m_prev, jnp.max(s, axis=0, keepdims=True))
            alpha = jnp.exp2(m_prev - m_new)
            p = jnp.exp2(s - m_new)
            l_sc[h] = alpha * l_sc[h] + jnp.sum(p, axis=0, keepdims=True)
            p_sc[h] = p.astype(BF16)
            al_sc[h] = alpha
            m_sc[h] = m_new
        for h in range(HEADS):
            acc_sc[h] = al_sc[h] * acc_sc[h] + jnp.dot(vt_sc[j, h * dv:(h + 1) * dv, :], p_sc[h],
                                                       preferred_element_type=F32)

    def past(j, carry):
        block(j, False)
        return carry
    lax.fori_loop(0, qi, past, 0)
    block(qi, True)
    ot = jnp.concatenate([acc_sc[h] * (1.0 / l_sc[h]) for h in range(HEADS)], axis=0)
    o_ref[...] = ot.T


def _attention(q, k, v, v_col, B, S, kmean=None):
    T = B * S
    tq = ATT_TQ
    nq = S // tq
    W = HEADS * LANES
    dv = BRANCH_WIDTH // HEADS
    moba = kmean is not None
    assert nq <= SUBLANES and v_col % BRANCH_WIDTH == 0
    in_specs = [pl.BlockSpec((tq, W), lambda b, i: (b * nq + i, 0)),
                pl.BlockSpec((S, W), lambda b, i: (b, 0)),
                pl.BlockSpec((S, BRANCH_WIDTH), lambda b, i: (b, v_col // BRANCH_WIDTH))]
    scratch = [pltpu.VMEM((nq, BRANCH_WIDTH, tq), BF16)]
    if moba:
        scratch.append(pltpu.VMEM((tq, W), BF16))
    row = pltpu.VMEM((HEADS, 1, tq), F32)
    scratch += [row, row, row, pltpu.VMEM((HEADS, dv, tq), F32),
                pltpu.VMEM((HEADS, tq, tq), F32), pltpu.VMEM((HEADS, tq, tq), BF16)]
    args = [q, k, v]
    if moba:
        in_specs.append(pl.BlockSpec((1, nq, W), lambda b, i: (b, 0, 0)))
        scratch.append(pltpu.VMEM((HEADS, nq, 1, tq), F32))
        args.append(kmean)
    return pl.pallas_call(
        functools.partial(_attn_kernel, moba=moba, nk=nq),
        grid=(B, nq), in_specs=in_specs,
        out_specs=pl.BlockSpec((tq, BRANCH_WIDTH), lambda b, i: (b * nq + i, 0)),
        out_shape=jax.ShapeDtypeStruct((T, BRANCH_WIDTH), F32),
        scratch_shapes=scratch,
        compiler_params=_cp("parallel", "arbitrary"))(*args)


def _gdn_conv_kernel(x_ref, w_ref, o_ref):
    c = pl.program_id(1)
    x = x_ref[...].astype(F32)
    w = w_ref[...]
    S = x.shape[0]
    row = lax.broadcasted_iota(jnp.int32, (S, LANES), 0)
    lane = lax.broadcasted_iota(jnp.int32, (S, LANES), 1)
    y = x * w[GDN_CONV - 1:GDN_CONV, :]
    for d in range(1, GDN_CONV):
        xs = jnp.where(row >= d, pltpu.roll(x, d, 0), 0.0)
        y = y + xs * w[GDN_CONV - 1 - d:GDN_CONV - d, :]
    y = y * _sigmoid(y)
    sq = y * y
    lo = lane < GDN_HEAD_DIM
    ss0 = jnp.sum(jnp.where(lo, sq, 0.0), axis=1, keepdims=True)
    ss1 = jnp.sum(jnp.where(lo, 0.0, sq), axis=1, keepdims=True)
    inv = lax.rsqrt(jnp.where(lo, ss0, ss1) + RMS_EPS)
    nqb = BRANCH_WIDTH // LANES
    scale = jnp.where(c < nqb, GDN_HEAD_DIM ** -0.5, 1.0)
    o_ref[...] = jnp.where(c < 2 * nqb, y * inv * scale, y)


def _gdn_conv(proj, conv_w, B, S):
    T = B * S
    nb = 3 * BRANCH_WIDTH // LANES
    return pl.pallas_call(
        _gdn_conv_kernel, grid=(B, nb),
        in_specs=[pl.BlockSpec((S, LANES), lambda b, c: (b, C_GQKV // LANES + c)),
                  pl.BlockSpec((GDN_CONV, LANES), lambda b, c: (0, c))],
        out_specs=pl.BlockSpec((S, LANES), lambda b, c: (b, c)),
        out_shape=jax.ShapeDtypeStruct((T, 3 * BRANCH_WIDTH), F32),
        compiler_params=_cp("parallel", "parallel"))(proj, conv_w)


def _gdn_chunk_kernel(q_ref, k_ref, v_ref, sm_ref, alog_ref, dtb_ref,
                      u_ref, w_ref, qg_ref, qk_ref, kdt_ref, dl_ref):
    C = GDN_KCHUNK
    nh = GDN_HEADS_PER_STEP
    h0 = pl.program_id(1) * nh
    lane = lax.broadcasted_iota(jnp.int32, (C, LANES), 1)
    row = lax.broadcasted_iota(jnp.int32, (C, LANES), 0)
    lo = lane < GDN_HEAD_DIM
    tril = row >= lane
    strict = row > lane
    eye = jnp.where(row == lane, 1.0, 0.0)
    ltri = tril.astype(BF16)

    def pick(ref, k):
        x = ref[:, (k // 2) * LANES:(k // 2 + 1) * LANES]
        return jnp.where(lo, pltpu.roll(x, LANES // 2, 1) if k % 2 else x, 0.0)

    sm = sm_ref[...]
    a = sm + dtb_ref[...]
    softplus = jnp.maximum(a, 0.0) + jnp.log1p(jnp.exp(-jnp.abs(a)))
    garr = -jnp.exp(alog_ref[...]) * softplus
    sig = _sigmoid(sm)
    kh, kb, gc, decay, A = [], [], [], [], []
    for k in range(nh):
        g_col = jnp.sum(jnp.where(lane == h0 + k, garr, 0.0), axis=1, keepdims=True)
        beta = jnp.sum(jnp.where(lane == HEADS + h0 + k, sig, 0.0), axis=1, keepdims=True)
        g1 = jnp.broadcast_to(g_col, (C, LANES))
        gsum = None
        for _ in range(3):
            gb = g1.astype(BF16)
            part = jnp.dot(ltri, gb, preferred_element_type=F32)
            gsum = part if gsum is None else gsum + part
            g1 = g1 - gb.astype(F32)
        gc.append(gsum)
        decay.append(jnp.where(tril, jnp.exp(jnp.where(tril, gsum - gsum.T, 0.0)), 0.0))
        kh.append(pick(k_ref, k))
        kb.append(kh[k] * beta)
        vb = pick(v_ref, k) * beta
        A.append(jnp.where(strict, _dot_nt(kb[k], kh[k]) * decay[k], 0.0))
        eg = jnp.exp(gsum)
        qh = pick(q_ref, k)
        qk_ref[k, 0] = jnp.where(tril, _dot_nt(qh, kh[k]) * decay[k], 0.0).astype(BF16)
        qg_ref[k, 0] = (qh * eg).astype(BF16)
        glast = gsum[C - 1:C, :]
        kdt_ref[k, 0] = (kh[k] * jnp.exp(glast - gsum)).T.astype(BF16)
        dl_ref[k, 0] = jnp.broadcast_to(jnp.exp(glast), (SUBLANES, LANES))
        u_ref[k, 0] = vb
        w_ref[k, 0] = (kb[k] * eg).astype(BF16)
    def joiner(level):
        same = (row >> (level + 1)) == (lane >> (level + 1))
        return same & (((row >> level) & 1) == 1) & (((lane >> level) & 1) == 0)

    P = [eye - jnp.where(joiner(0), A[k], 0.0) for k in range(nh)]
    for level in range(1, int(math.log2(C))):
        msk = joiner(level)
        T1 = [_dot(P[k], jnp.where(msk, A[k], 0.0)) for k in range(nh)]
        P = [P[k] - _dot(T1[k], P[k]) for k in range(nh)]
    for k in range(nh):
        u_ref[k, 0] = _dot(P[k], u_ref[k, 0])
        w_ref[k, 0] = _dot(P[k], w_ref[k, 0]).astype(BF16)


def _gdn_chunks(qkv, small, alog_row, dtb_row, B, S):
    C = GDN_KCHUNK
    n = S // C
    nh = GDN_HEADS_PER_STEP
    ng = HEADS // nh
    wq = nh // 2 * LANES
    nqb = BRANCH_WIDTH // wq
    big = lambda dt: jax.ShapeDtypeStruct((B * HEADS, n, C, LANES), dt)
    ospec = pl.BlockSpec((nh, 1, C, LANES), lambda b, g, c: (b * ng + g, c, 0, 0))
    par = pl.BlockSpec((1, LANES), lambda b, g, c: (0, 0))
    outs = pl.pallas_call(
        _gdn_chunk_kernel, grid=(B, ng, n),
        in_specs=[pl.BlockSpec((C, wq), lambda b, g, c: (b * n + c, g)),
                  pl.BlockSpec((C, wq), lambda b, g, c: (b * n + c, nqb + g)),
                  pl.BlockSpec((C, wq), lambda b, g, c: (b * n + c, 2 * nqb + g)),
                  pl.BlockSpec((C, LANES), lambda b, g, c: (b * n + c, 0)), par, par],
        out_specs=[ospec] * 5 + [pl.BlockSpec((nh, 1, SUBLANES, LANES), lambda b, g, c: (b * ng + g, c, 0, 0))],
        out_shape=[big(F32)] + [big(BF16)] * 4 + [jax.ShapeDtypeStruct((B * HEADS, n, SUBLANES, LANES), F32)],
        compiler_params=_cp("parallel", "parallel", "parallel"))(qkv, qkv, qkv, small, alog_row, dtb_row)
    return [o.reshape(-1, LANES) for o in outs]


def _gdn_scan_kernel(u_ref, w_ref, qg_ref, qk_ref, kdt_ref, dl_ref, z_ref, g_ref, o_ref, o_sc, *, n):
    C = GDN_KCHUNK
    S = n * C
    npair = GDN_SCAN_PAIRS

    def step(c, states):
        out = []
        for k, state in enumerate(states):
            r = pl.ds(pl.multiple_of(k * S + c * C, C), C)
            sb = state.astype(BF16)
            v_new = u_ref[r, :] - _dot(w_ref[r, :], sb)
            vb = v_new.astype(BF16)
            o_sc[r, :] = _dot(qg_ref[r, :], sb) + _dot(qk_ref[r, :], vb)
            dl = dl_ref[pl.ds(pl.multiple_of((k * n + c) * SUBLANES, SUBLANES), 1), :]
            out.append(state * dl + _dot(kdt_ref[r, :], vb))
        return tuple(out)

    lax.fori_loop(0, n, step, tuple(jnp.zeros((LANES, LANES), F32) for _ in range(2 * npair)))
    lane = lax.broadcasted_iota(jnp.int32, (S, LANES), 1)

    def nrm(o):
        ms = jnp.sum(o * o, axis=1, keepdims=True) * (1.0 / GDN_HEAD_DIM)
        return o * lax.rsqrt(ms + RMS_EPS)

    for p in range(npair):
        ps = slice(p * LANES, (p + 1) * LANES)
        nn = jnp.where(lane < GDN_HEAD_DIM, nrm(o_sc[2 * p * S:(2 * p + 1) * S, :]),
                       pltpu.roll(nrm(o_sc[(2 * p + 1) * S:(2 * p + 2) * S, :]), LANES // 2, 1))
        z = z_ref[:, ps].astype(F32)
        o_ref[:, ps] = nn * g_ref[...] * (z * _sigmoid(z))


def _gdn_scan(parts, proj, g_row, B, S):
    n = S // GDN_KCHUNK
    nps = GDN_SCAN_PAIRS
    steps = HEADS // 2 // nps
    wz = nps * LANES
    seq = pl.BlockSpec((2 * nps * S, LANES), lambda i: (i, 0))
    return pl.pallas_call(
        functools.partial(_gdn_scan_kernel, n=n), grid=(B * steps,),
        in_specs=[seq] * 5 + [pl.BlockSpec((2 * nps * n * SUBLANES, LANES), lambda i: (i, 0)),
                              pl.BlockSpec((S, wz), lambda i: (i // steps, C_GZ // wz + i % steps)),
                              pl.BlockSpec((1, LANES), lambda i: (0, 0))],
        out_specs=pl.BlockSpec((S, wz), lambda i: (i // steps, i % steps)),
        out_shape=jax.ShapeDtypeStruct((B * S, BRANCH_WIDTH), F32),
        scratch_shapes=[pltpu.VMEM((2 * nps * S, LANES), F32)],
        compiler_params=_cp("parallel"))(*parts, proj, g_row)


def _merge_out_kernel(g0_ref, g1_ref, g2_ref, gb_ref, a_ref, b_ref, c_ref, wb_ref, wo_ref, h_ref, lg_ref, lb_ref,
                      o_ref, ot_ref):
    acc = None
    for n, (gl, br) in enumerate(((g0_ref, a_ref), (g1_ref, b_ref), (g2_ref, c_ref))):
        y = _sigmoid(gl[...].astype(F32) + gb_ref[n:n + 1, :]) * _dot(br[...], wb_ref[n])
        acc = y if acc is None else acc + y
    y = DEEPNORM_ALPHA * h_ref[...] + _dot(acc, wo_ref[...])
    y = _ln_rows(y, lg_ref[...], lb_ref[...])
    o_ref[...] = y
    ot_ref[...] = y.T.astype(BF16)


def _merge_out_ln(proj, gate_bias, o_a, o_b, o_c, w_branch, w_out, h, g, b, tm=256):
    T, D = h.shape
    row = lambda i: (i, 0)
    gspec = lambda n: pl.BlockSpec((tm, D), lambda i: (i, n))
    bspec = pl.BlockSpec((tm, BRANCH_WIDTH), row)
    vec = pl.BlockSpec((1, D), lambda i: (0, 0))
    return pl.pallas_call(
        _merge_out_kernel, grid=(T // tm,),
        in_specs=[gspec(0), gspec(1), gspec(2), pl.BlockSpec((3, D), lambda i: (0, 0)), bspec, bspec, bspec,
                  pl.BlockSpec((3, BRANCH_WIDTH, D), lambda i: (0, 0, 0)), pl.BlockSpec((D, D), lambda i: (0, 0)),
                  pl.BlockSpec((tm, D), row), vec, vec],
        out_specs=[pl.BlockSpec((tm, D), row), pl.BlockSpec((D, tm), lambda i: (0, i))],
        out_shape=[jax.ShapeDtypeStruct((T, D), F32), jax.ShapeDtypeStruct((D, T), BF16)],
        compiler_params=_cp("parallel"))(proj, proj, proj, gate_bias, o_a, o_b, o_c, w_branch, w_out, h,
                                         g.reshape(1, D), b.reshape(1, D))


def _extract_top(curs, n, on_max):
    curs = list(curs)
    for k in range(n):
        for i, cur in enumerate(curs):
            m = jnp.max(cur, axis=0, keepdims=True)
            on_max(i, k, m)
            if k + 1 < n:
                curs[i] = jnp.where(cur >= m, -jnp.inf, cur)


def _sort_network(n):
    pairs, p = [], 1
    while p < n:
        k = p
        while k >= 1:
            for j in range(k % p, n - k, 2 * k):
                for i in range(min(k, n - j - k)):
                    if (i + j) // (2 * p) == (i + j + k) // (2 * p):
                        pairs.append((i + j, i + j + k))
            k //= 2
        p *= 2
    return pairs


def _top_of_sorted_lists(scores, n, on_max):
    nt = N_KEYS // SUBLANES
    lists = []
    for s in scores:
        rows = [s[t * SUBLANES:(t + 1) * SUBLANES, :] for t in range(nt)]
        for a, b in _sort_network(nt):
            rows[a], rows[b] = jnp.maximum(rows[a], rows[b]), jnp.minimum(rows[a], rows[b])
        lists.append(rows + [jnp.full(rows[0].shape, -jnp.inf, F32)])
    for k in range(n):
        for i, rows in enumerate(lists):
            m = jnp.max(rows[0], axis=0, keepdims=True)
            on_max(i, k, m)
            hit = rows[0] >= m
            for t in range(min(n - 1 - k, nt)):
                rows[t] = jnp.where(hit, rows[t + 1], rows[t])


def _peer_score_kernel(qt_ref, keys_ref, s2_ref, e2_ref, c_ref, e1_ref, top_sc, cand_sc):
    K = PEER_TOPK
    tt = qt_ref.shape[1]
    nh = PEER_SCORE_HEADS_PER_TRIP

    def heads(hp, carry):
        s = []
        for i in range(2 * nh):
            r = pl.multiple_of((hp * nh * 2 + i) * PEER_HALF, PEER_HALF)
            s.append(_dot(keys_ref[hp * nh * 2 + i], qt_ref[pl.ds(r, PEER_HALF), :]))

        def put(i, k, m):
            top_sc[i, k:k + 1, :] = m
        _top_of_sorted_lists(s, K + 1, put)
        r8 = lax.broadcasted_iota(jnp.int32, (SUBLANES, tt), 0)
        v1max, v2max = [], []
        for d in range(nh):
            v1a, v1x = top_sc[2 * d, 0:K, :], top_sc[2 * d, K:K + 1, :]
            v2a, v2x = top_sc[2 * d + 1, 0:K, :], top_sc[2 * d + 1, K:K + 1, :]
            v1max.append(v1a[0:1])
            v2max.append(v2a[0:1])
            cand_sc[d, 0:K, :] = v1a + v2a[0:1]
            for b in range(1, SUBLANES):
                cand_sc[d, K + SUBLANES * (b - 1):K + SUBLANES * b, :] = v1a[0:SUBLANES] + v2a[b:b + 1]
            base = K + SUBLANES * (SUBLANES - 1)
            cand_sc[d, base:base + SUBLANES, :] = v1a[0:1] + v2a[SUBLANES:K]
            cand_sc[d, base + SUBLANES:base + 2 * SUBLANES, :] = jnp.where(
                r8 == 0, v1x + v2a[0:1], jnp.where(r8 == 1, v1a[0:1] + v2x, -jnp.inf))
        st = [{"z": jnp.zeros((1, tt), F32)} for _ in range(nh)]

        def acc(d, k, m):
            if k == 0:
                st[d]["top"] = m
            if k < K:
                st[d]["z"] = st[d]["z"] + jnp.exp(m - st[d]["top"])
            if k == K - 1:
                st[d]["t16"] = m
            if k == K:
                st[d]["t17"] = m
        _extract_top([cand_sc[d] for d in range(nh)], K + 1, acc)
        for d in range(nh):
            tau = 0.5 * (st[d]["t16"] + st[d]["t17"])
            ro = pl.ds(pl.multiple_of((hp * nh + d) * N_KEYS, N_KEYS), N_KEYS)
            s2_ref[ro, :] = s[2 * d + 1]
            e2_ref[ro, :] = jnp.exp(s[2 * d + 1] - v2max[d]) / st[d]["z"]
            c_ref[ro, :] = tau - s[2 * d]
            e1_ref[ro, :] = jnp.exp(s[2 * d] - v1max[d])
        return carry

    lax.fori_loop(0, HEADS // nh, heads, 0)


def _peer_scores(qt, keys):
    T = qt.shape[1]
    tt = PEER_SCORE_TT
    R = HEADS * N_KEYS
    ncand = PEER_TOPK + SUBLANES * (SUBLANES + 1)
    ospec = pl.BlockSpec((R, tt), lambda i: (0, i))
    return pl.pallas_call(
        _peer_score_kernel, grid=(T // tt,),
        in_specs=[pl.BlockSpec((2 * R, tt), lambda i: (0, i)),
                  pl.BlockSpec((2 * HEADS, N_KEYS, PEER_HALF), lambda i: (0, 0, 0))],
        out_specs=[ospec] * 4,
        out_shape=[jax.ShapeDtypeStruct((R, T), F32)] * 4,
        scratch_shapes=[pltpu.VMEM((2 * PEER_SCORE_HEADS_PER_TRIP, 3 * SUBLANES, tt), F32),
                        pltpu.VMEM((PEER_SCORE_HEADS_PER_TRIP, ncand, tt), F32)],
        compiler_params=_cp("parallel"))(qt, keys)


def _peer_expert_kernel(ht_ref, u_ref, vtp_ref, vtl_ref, s2_ref, e2_ref, c_ref, e1_ref, h_ref, g_ref, b_ref,
                        o_ref, acc_sc, act_sc, hw_sc, bc_sc):
    j = pl.program_id(1)
    _, eb, tt = hw_sc.shape
    ng = eb // N_KEYS
    assert c_ref.shape == (HEADS, ng, tt)
    slot = j % 2

    @pl.when(j == 0)
    def _():
        acc_sc[...] = jnp.zeros(acc_sc.shape, F32)
        hw_sc[1] = jnp.zeros(hw_sc.shape[1:], BF16)

    mxu_w = 2 * LANES
    for half in range(tt // mxu_w):
        hl = slice(half * mxu_w, (half + 1) * mxu_w)
        a = jnp.dot(u_ref[...], ht_ref[:, hl], preferred_element_type=F32)
        act_sc[:, hl] = 0.5 * a * (1.0 + lax.erf(a * (2.0 ** -0.5)))
        acc_sc[:, hl] += jnp.dot(vtp_ref[...], hw_sc[1 - slot, :, hl], preferred_element_type=F32)
        for lc in range(half * (mxu_w // LANES), (half + 1) * (mxu_w // LANES)):
            ls = slice(lc * LANES, (lc + 1) * LANES)
            for h in range(HEADS):
                thr, e1 = c_ref[h, :, ls], e1_ref[h, :, ls]
                for g in range(ng):
                    bc_sc[0, h * ng + g] = jnp.broadcast_to(thr[g:g + 1], (SUBLANES, LANES))
                    bc_sc[1, h * ng + g] = jnp.broadcast_to(e1[g:g + 1], (SUBLANES, LANES))
            for g in range(ng):
                wsum = jnp.zeros((N_KEYS // SUBLANES, SUBLANES, LANES), F32)
                for h in range(HEADS):
                    hr = slice(h * N_KEYS, (h + 1) * N_KEYS)
                    s2 = s2_ref[hr, ls].reshape(N_KEYS // SUBLANES, SUBLANES, LANES)
                    e2 = e2_ref[hr, ls].reshape(N_KEYS // SUBLANES, SUBLANES, LANES)
                    wsum = wsum + jnp.where(s2 >= bc_sc[0, h * ng + g][None], e2 * bc_sc[1, h * ng + g][None], 0.0)
                gs = slice(g * N_KEYS, (g + 1) * N_KEYS)
                hw_sc[slot, gs, ls] = (wsum.reshape(N_KEYS, LANES) * act_sc[gs, ls]).astype(BF16)

    @pl.when(j == pl.num_programs(1) - 1)
    def _():
        acc = acc_sc[...] + jnp.dot(vtl_ref[...], hw_sc[slot], preferred_element_type=F32)
        y = DEEPNORM_ALPHA * h_ref[...] + acc.T
        o_ref[...] = _ln_rows(y, g_ref[...], b_ref[...])


def _peer_experts(ht, u, vt, s2, e2, c, e1, h, g, b):
    T, D = h.shape
    tt, eb = PEER_TT, PEER_EB
    R = HEADS * N_KEYS
    nb = N_EXPERTS // eb
    assert nb % 2 == 0
    ng = eb // N_KEYS
    sspec = pl.BlockSpec((R, tt), lambda i, j: (0, i))
    gspec = pl.BlockSpec((HEADS, ng, tt), lambda i, j: (0, j, i))
    vec = pl.BlockSpec((1, D), lambda i, j: (0, 0))
    return pl.pallas_call(
        _peer_expert_kernel, grid=(T // tt, nb),
        in_specs=[pl.BlockSpec((D, tt), lambda i, j: (0, i)),
                  pl.BlockSpec((eb, D), lambda i, j: (j, 0)),
                  pl.BlockSpec((D, eb), lambda i, j: (0, jnp.maximum(j - 1, 0))),
                  pl.BlockSpec((D, eb), lambda i, j: (0, nb - 1)),
                  sspec, sspec, gspec, gspec,
                  pl.BlockSpec((tt, D), lambda i, j: (i, 0)), vec, vec],
        out_specs=pl.BlockSpec((tt, D), lambda i, j: (i, 0)),
        out_shape=jax.ShapeDtypeStruct((T, D), F32),
        scratch_shapes=[pltpu.VMEM((D, tt), F32), pltpu.VMEM((eb, tt), F32), pltpu.VMEM((2, eb, tt), BF16),
                        pltpu.VMEM((2, HEADS * ng, SUBLANES, LANES), F32)],
        compiler_params=_cp("parallel", "arbitrary"))(ht, u, vt, vt, s2, e2, c.reshape(HEADS, N_KEYS, T),
                                                      e1.reshape(HEADS, N_KEYS, T), h, g.reshape(1, D),
                                                      b.reshape(1, D))


def _rot_half(x, r):
    return jnp.concatenate([-x[..., r:2 * r], x[..., :r]], axis=-1)


def _prep_w_in(w):
    D = w.shape[0]
    splits = (MLA_Q_RANK, MLA_KV_RANK, MLA_ROPE, 3 * BRANCH_WIDTH, 3 * BRANCH_WIDTH, BRANCH_WIDTH, HEADS, HEADS,
              3 * D)
    o = np.cumsum((0,) + splits)
    cq, ckv, kr, mqkv, gqkv, gz, ga, gb, gl = [w[:, o[i]:o[i + 1]] for i in range(9)]
    mq, mk, mv = jnp.split(mqkv, 3, axis=1)

    def slab(m):
        m = m.reshape(D, HEADS, MOBA_HEAD_DIM)
        pad = jnp.zeros((D, HEADS, LANES - MOBA_HEAD_DIM - MOBA_ROT_DIM), w.dtype)
        return jnp.concatenate([m, _rot_half(m, MOBA_ROT_DIM // 2), pad], axis=-1).reshape(D, HEADS * LANES)

    small = jnp.concatenate([ga, gb, jnp.zeros((D, LANES // 2 - 2 * HEADS), w.dtype), kr,
                             _rot_half(kr, MLA_ROPE // 2)], axis=1)
    out = jnp.concatenate([gl, cq, ckv, slab(mq), slab(mk), mv, gqkv, gz], axis=1)
    assert out.shape[1] == C_TOTAL and C_TOTAL % PROJ_TN == 0
    return out.astype(BF16), small.astype(BF16)


def _prep_w_uq(w):
    R = w.shape[0]
    w = w.reshape(R, HEADS, MLA_NOPE + MLA_ROPE)
    rope = w[..., MLA_NOPE:]
    return jnp.concatenate([w, _rot_half(rope, MLA_ROPE // 2)], axis=-1).reshape(R, HEADS * LANES).astype(BF16)


def _prep_w_ukv(w):
    R = w.shape[0]
    w = w.reshape(R, HEADS, MLA_NOPE + MLA_V)
    k = jnp.concatenate([w[..., :MLA_NOPE], jnp.zeros((R, HEADS, LANES - MLA_NOPE), w.dtype)], axis=-1)
    return jnp.concatenate([k.reshape(R, HEADS * LANES), w[..., MLA_NOPE:].reshape(R, HEADS * MLA_V)],
                           axis=1).astype(BF16)


def _rope_tables(positions):
    pos = positions.reshape(-1).astype(F32)[:, None]
    T = pos.shape[0]

    def cs(rot):
        inv = ROPE_THETA ** (-jnp.arange(0, rot, 2, dtype=F32) / rot)
        ang = pos * inv
        return jnp.cos(ang), jnp.sin(ang)

    ca, sa = cs(MLA_ROPE)
    cb, sb = cs(MOBA_ROT_DIM)
    one = lambda n: jnp.ones((T, n), F32)
    zero = lambda n: jnp.zeros((T, n), F32)
    sc_a = (MLA_NOPE + MLA_ROPE) ** -0.5 * math.log2(math.e)
    sc_b = MOBA_HEAD_DIM ** -0.5 * math.log2(math.e)
    cat = lambda *xs: jnp.concatenate(xs, axis=1)
    mla = (cat(one(MLA_NOPE), ca, ca, zero(32)) * sc_a, cat(zero(MLA_NOPE), sa, sa, zero(32)) * sc_a,
           cat(zero(MLA_NOPE), ca, ca, zero(32)), cat(zero(MLA_NOPE), sa, sa, zero(32)))
    cm = cat(cb, cb, one(MOBA_HEAD_DIM - MOBA_ROT_DIM), zero(LANES - MOBA_HEAD_DIM))
    sm = cat(sb, sb, zero(LANES - MOBA_ROT_DIM))
    moba = (cm * sc_b, sm * sc_b, cm, sm)
    return mla, moba


def _lane_row(v):
    return jnp.concatenate([v.astype(F32), jnp.zeros((LANES - v.shape[0],), F32)]).reshape(1, LANES)


def kernel(x, positions, ln_in_g, ln_in_b, w_in, mla_q_norm, mla_kv_norm, mla_w_uq, mla_w_ukv, gdn_conv_w, gdn_A_log, gdn_dt_bias, gdn_o_norm, gate_bias, w_branch, w_out, ln1_g, ln1_b, peer_w_q, peer_sub_keys, peer_u, peer_v, ln2_g, ln2_b):
    B, S, D = x.shape
    T = B * S
    assert S % ATT_TQ == 0 and S % GDN_KCHUNK == 0 and T % PEER_TT == 0 and ATT_TQ == MOBA_BLOCK
    mla_tabs, moba_tabs = _rope_tables(positions)
    h = _layer_norm(x.reshape(T, D), ln_in_g, ln_in_b)
    for l in range(DEPTH):
        w_main, w_small = _prep_w_in(w_in[l])
        proj = _mm(h, w_main, tm=512, tn=PROJ_TN, out_dtype=BF16)
        small = _mm(h, w_small, tm=512, tn=LANES)
        qa, ka, va = _mla_qkv(proj, small, mla_q_norm[l], mla_kv_norm[l], _prep_w_uq(mla_w_uq[l]),
                              _prep_w_ukv(mla_w_ukv[l]), mla_tabs)
        o_a = _attention(qa, ka, va, 0, B, S)
        qm, km, kmean = _moba_prep(proj, moba_tabs)
        o_b = _attention(qm, km, proj, C_MV, B, S, kmean=kmean.reshape(B, S // MOBA_BLOCK, HEADS * LANES))
        qkv = _gdn_conv(proj, gdn_conv_w[l], B, S)
        parts = _gdn_chunks(qkv, small, _lane_row(gdn_A_log[l]), _lane_row(gdn_dt_bias[l]), B, S)
        o_c = _gdn_scan(parts, proj, jnp.tile(gdn_o_norm[l], 2).reshape(1, LANES), B, S)
        h, ht = _merge_out_ln(proj, gate_bias[l], o_a, o_b, o_c, w_branch[l].astype(BF16), w_out[l].astype(BF16),
                              h, ln1_g[l], ln1_b[l])
        qt = _mm(peer_w_q[l].T.astype(BF16), ht, tm=1024, tn=min(1024, T), out_dtype=BF16)
        keys = peer_sub_keys[l].reshape(2 * HEADS, N_KEYS, PEER_HALF).astype(BF16)
        s2, e2, c, e1 = _peer_scores(qt, keys)
        h = _peer_experts(ht, peer_u[l].astype(BF16), peer_v[l].astype(BF16).T, s2, e2, c, e1, h, ln2_g[l], ln2_b[l])
    return h.reshape(B, S, D)
```

```python
import functools
import math

import numpy as np
import jax
import jax.numpy as jnp
from jax import lax
from jax.experimental import pallas as pl
from jax.experimental.pallas import tpu as pltpu

F32 = jnp.float32
BF16 = jnp.bfloat16
HIGHEST = lax.Precision.HIGHEST

DEPTH = 2
ROPE_THETA = 500000.0
NEG_INF = -1e30
LN_EPS = 1e-5
RMS_EPS = 1e-6
DEEPNORM_ALPHA = (2 * DEPTH) ** 0.25
HEADS = 8
MLA_NOPE, MLA_ROPE, MLA_V = 64, 32, 64
MLA_Q_RANK, MLA_KV_RANK = 768, 256
MOBA_HEAD_DIM, MOBA_ROT_DIM, MOBA_BLOCK, MOBA_TOPK = 64, 16, 256, 3
GDN_HEAD_DIM, GDN_CONV = 64, 4
BRANCH_WIDTH = 512
N_KEYS, PEER_TOPK, PEER_HALF = 128, 16, 128
N_EXPERTS = N_KEYS * N_KEYS

LANES = 128
SUBLANES = 8
VMEM_LIMIT = 56 * 1024 * 1024

GDN_KCHUNK = 128
GDN_HEADS_PER_STEP = 8
GDN_SCAN_PAIRS = 2
ATT_TQ = 256
PEER_TT = 512
PEER_EB = 1024
PEER_SCORE_TT = 256
PEER_SCORE_HEADS_PER_TRIP = 4

C_GATE, C_CQ, C_CKV, C_MQ, C_MK, C_MV, C_GQKV, C_GZ, C_TOTAL = (
    0, 3072, 3840, 4096, 5120, 6144, 6656, 8192, 8704)
PROJ_TN = 4352


def _cp(*sem):
    return pltpu.CompilerParams(dimension_semantics=sem, vmem_limit_bytes=VMEM_LIMIT)


def _dot(a, b):
    return jnp.dot(a.astype(BF16), b.astype(BF16), preferred_element_type=F32)


def _dot_nt(a, b):
    return lax.dot_general(a.astype(BF16), b.astype(BF16), (((1,), (1,)), ((), ())),
                           preferred_element_type=F32)


def _ln_rows(y, g, b):
    mu = jnp.mean(y, axis=-1, keepdims=True)
    d = y - mu
    var = jnp.mean(d * d, axis=-1, keepdims=True)
    return d * lax.rsqrt(var + LN_EPS) * g + b


def _sigmoid(x):
    return 1.0 / (1.0 + jnp.exp(-x))


def _ln_kernel(x_ref, g_ref, b_ref, o_ref):
    o_ref[...] = _ln_rows(x_ref[...], g_ref[...], b_ref[...])


def _layer_norm(x, g, b, tm=512):
    T, D = x.shape
    return pl.pallas_call(
        _ln_kernel, grid=(T // tm,),
        in_specs=[pl.BlockSpec((tm, D), lambda i: (i, 0)),
                  pl.BlockSpec((1, D), lambda i: (0, 0)),
                  pl.BlockSpec((1, D), lambda i: (0, 0))],
        out_specs=pl.BlockSpec((tm, D), lambda i: (i, 0)),
        out_shape=jax.ShapeDtypeStruct((T, D), F32),
        compiler_params=_cp("parallel"))(x, g.reshape(1, D), b.reshape(1, D))


def _mm_kernel(x_ref, w_ref, o_ref):
    o_ref[...] = _dot(x_ref[...], w_ref[...]).astype(o_ref.dtype)


def _mm(x, w, tm, tn, out_dtype=F32):
    M, K = x.shape
    N = w.shape[1]
    return pl.pallas_call(
        _mm_kernel, grid=(M // tm, N // tn),
        in_specs=[pl.BlockSpec((tm, K), lambda i, j: (i, 0)),
                  pl.BlockSpec((K, tn), lambda i, j: (0, j))],
        out_specs=pl.BlockSpec((tm, tn), lambda i, j: (i, j)),
        out_shape=jax.ShapeDtypeStruct((M, N), out_dtype),
        compiler_params=_cp("parallel", "parallel"))(x, w)


def _rms_rows(x, g):
    return x * lax.rsqrt(jnp.mean(x * x, axis=-1, keepdims=True) + RMS_EPS) * g


def _head_slabs(x):
    return [x[:, h * LANES:(h + 1) * LANES] for h in range(HEADS)]


def _mla_qkv_kernel(cq_ref, ckv_ref, gq_ref, gkv_ref, wq_ref, wkv_ref, sm_ref, cq_tab, sq_tab, ck_tab, sk_tab,
                    qo_ref, ko_ref, vo_ref):
    W = HEADS * LANES
    q = _dot(_rms_rows(cq_ref[...].astype(F32), gq_ref[...]), wq_ref[...])
    kv = _dot(_rms_rows(ckv_ref[...].astype(F32), gkv_ref[...]), wkv_ref[...])
    vo_ref[...] = kv[:, W:].astype(BF16)
    cq, sq = cq_tab[...], sq_tab[...]
    sm = sm_ref[...]
    kr = sm * ck_tab[...] + pltpu.roll(sm, LANES - MLA_ROPE, 1) * sk_tab[...]
    for h in range(HEADS):
        sl = slice(h * LANES, (h + 1) * LANES)
        qs = q[:, sl]
        qo_ref[:, sl] = (qs * cq + pltpu.roll(qs, LANES - MLA_ROPE, 1) * sq).astype(BF16)
        ko_ref[:, sl] = (kv[:, sl] + kr).astype(BF16)


def _mla_qkv(proj, small, gq, gkv, wq, wkv, tabs, tm=512):
    T = proj.shape[0]
    W = HEADS * LANES
    row = lambda i: (i, 0)
    tab = pl.BlockSpec((tm, LANES), row)
    full = lambda a: pl.BlockSpec(a.shape, lambda i: (0, 0))
    gq, gkv = gq.reshape(1, -1), gkv.reshape(1, -1)
    return pl.pallas_call(
        _mla_qkv_kernel, grid=(T // tm,),
        in_specs=[pl.BlockSpec((tm, MLA_Q_RANK), lambda i: (i, C_CQ // MLA_Q_RANK)),
                  pl.BlockSpec((tm, MLA_KV_RANK), lambda i: (i, C_CKV // MLA_KV_RANK)),
                  full(gq), full(gkv), full(wq), full(wkv), tab, tab, tab, tab, tab],
        out_specs=[pl.BlockSpec((tm, W), row), pl.BlockSpec((tm, W), row), pl.BlockSpec((tm, BRANCH_WIDTH), row)],
        out_shape=[jax.ShapeDtypeStruct((T, W), BF16)] * 2 + [jax.ShapeDtypeStruct((T, BRANCH_WIDTH), BF16)],
        compiler_params=_cp("parallel"))(proj, proj, gq, gkv, wq, wkv, small, *tabs)


def _moba_prep_kernel(q_ref, k_ref, cq_ref, sq_ref, ck_ref, sk_ref, qo_ref, ko_ref, km_ref):
    cq, sq, ck, sk = cq_ref[...], sq_ref[...], ck_ref[...], sk_ref[...]
    q, k = q_ref[...].astype(F32), k_ref[...].astype(F32)
    for h, (qs, ks) in enumerate(zip(_head_slabs(q), _head_slabs(k))):
        sl = slice(h * LANES, (h + 1) * LANES)
        qo_ref[:, sl] = qs * cq + pltpu.roll(qs, LANES // 2, 1) * sq
        kk = ks * ck + pltpu.roll(ks, LANES // 2, 1) * sk
        ko_ref[:, sl] = kk.astype(BF16)
        km_ref[0, :, sl] = jnp.mean(kk, axis=0, keepdims=True)


def _moba_prep(proj, tabs):
    T = proj.shape[0]
    W = HEADS * LANES
    tm = MOBA_BLOCK
    row = lambda i: (i, 0)
    tab = pl.BlockSpec((tm, LANES), row)
    return pl.pallas_call(
        _moba_prep_kernel, grid=(T // tm,),
        in_specs=[pl.BlockSpec((tm, W), lambda i: (i, C_MQ // W)),
                  pl.BlockSpec((tm, W), lambda i: (i, C_MK // W)), tab, tab, tab, tab],
        out_specs=[pl.BlockSpec((tm, W), row), pl.BlockSpec((tm, W), row),
                   pl.BlockSpec((1, 1, W), lambda i: (i, 0, 0))],
        out_shape=[jax.ShapeDtypeStruct((T, W), F32), jax.ShapeDtypeStruct((T, W), BF16),
                   jax.ShapeDtypeStruct((T // tm, 1, W), F32)],
        compiler_params=_cp("parallel"))(proj, proj, *tabs)


def _attn_kernel(*refs, moba, nk):
    if moba:
        q_ref, k_ref, v_ref, km_ref, o_ref, vt_sc, qb_sc, m_sc, l_sc, al_sc, acc_sc, s_sc, p_sc, bias_sc = refs
    else:
        q_ref, k_ref, v_ref, o_ref, vt_sc, m_sc, l_sc, al_sc, acc_sc, s_sc, p_sc = refs
    tq = tk = ATT_TQ
    dv = BRANCH_WIDTH // HEADS
    qi = pl.program_id(1)
    hslab = [slice(h * LANES, (h + 1) * LANES) for h in range(HEADS)]

    @pl.when(qi == 0)
    def _():
        for j in range(nk):
            for c in range(tk // LANES):
                for g in range(BRANCH_WIDTH // LANES):
                    vt_sc[j, g * LANES:(g + 1) * LANES, c * LANES:(c + 1) * LANES] = (
                        v_ref[j * tk + c * LANES:j * tk + (c + 1) * LANES,
                              g * LANES:(g + 1) * LANES].astype(F32).T.astype(BF16))

    if moba:
        blk = lax.broadcasted_iota(jnp.int32, (SUBLANES, tq), 0)
        for h in range(HEADS):
            qf = q_ref[:, hslab[h]]
            qb_sc[:, hslab[h]] = qf.astype(BF16)
            km = km_ref[0, :, hslab[h]]
            if nk < SUBLANES:
                km = jnp.concatenate([km, jnp.zeros((SUBLANES - nk, LANES), F32)], axis=0)
            gate = lax.dot_general(km, qf, (((1,), (1,)), ((), ())), precision=HIGHEST,
                                   preferred_element_type=F32)
            gate = jnp.where(blk < qi, gate, -jnp.inf)
            for n in range(nk):
                gn = gate[n:n + 1, :]
                beats = jnp.where(gate > gn, 1.0, jnp.where((gate == gn) & (blk < n), 1.0, 0.0))
                cnt = jnp.sum(beats, axis=0, keepdims=True)
                bias_sc[h, n] = jnp.where((cnt < MOBA_TOPK) & (n < qi), 0.0, NEG_INF)
    qsrc = qb_sc if moba else q_ref
    m_sc[...] = jnp.full(m_sc.shape, NEG_INF, F32)
    l_sc[...] = jnp.zeros(l_sc.shape, F32)
    acc_sc[...] = jnp.zeros(acc_sc.shape, F32)

    def block(j, diag):
        rows = pl.ds(pl.multiple_of(j * tk, tk), tk)
        for h in range(HEADS):
            s = _dot_nt(k_ref[rows, hslab[h]], qsrc[:, hslab[h]])
            if diag:
                keyi = lax.broadcasted_iota(jnp.int32, (tk, tq), 0)
                qryi = lax.broadcasted_iota(jnp.int32, (tk, tq), 1)
                s = jnp.where(keyi <= qryi, s, NEG_INF)
            elif moba:
                s = s + bias_sc[h, j]
            s_sc[h] = s
        for h in range(HEADS):
            s = s_sc[h]
            m_prev = m_sc[h]
            m_new = jnp.maximum(m_prev, jnp.max(s, axis=0, keepdims=True))
            alpha = jnp.exp2(m_prev - m_new)
            p = jnp.exp2(s - m_new)
            l_sc[h] = alpha * l_sc[h] + jnp.sum(p, axis=0, keepdims=True)
            p_sc[h] = p.astype(BF16)
            al_sc[h] = alpha
            m_sc[h] = m_new
        for h in range(HEADS):
            acc_sc[h] = al_sc[h] * acc_sc[h] + jnp.dot(vt_sc[j, h * dv:(h + 1) * dv, :], p_sc[h],
                                                       preferred_element_type=F32)

    def past(j, carry):
        block(j, False)
        return carry
    lax.fori_loop(0, qi, past, 0)
    block(qi, True)
    ot = jnp.concatenate([acc_sc[h] * (1.0 / l_sc[h]) for h in range(HEADS)], axis=0)
    o_ref[...] = ot.T


def _attention(q, k, v, v_col, B, S, kmean=None):
    T = B * S
    tq = ATT_TQ
    nq = S // tq
    W = HEADS * LANES
    dv = BRANCH_WIDTH // HEADS
    moba = kmean is not None
    assert nq <= SUBLANES and v_col % BRANCH_WIDTH == 0
    in_specs = [pl.BlockSpec((tq, W), lambda b, i: (b * nq + i, 0)),
                pl.BlockSpec((S, W), lambda b, i: (b, 0)),
                pl.BlockSpec((S, BRANCH_WIDTH), lambda b, i: (b, v_col // BRANCH_WIDTH))]
    scratch = [pltpu.VMEM((nq, BRANCH_WIDTH, tq), BF16)]
    if moba:
        scratch.append(pltpu.VMEM((tq, W), BF16))
    row = pltpu.VMEM((HEADS, 1, tq), F32)
    scratch += [row, row, row, pltpu.VMEM((HEADS, dv, tq), F32),
                pltpu.VMEM((HEADS, tq, tq), F32), pltpu.VMEM((HEADS, tq, tq), BF16)]
    args = [q, k, v]
    if moba:
        in_specs.append(pl.BlockSpec((1, nq, W), lambda b, i: (b, 0, 0)))
        scratch.append(pltpu.VMEM((HEADS, nq, 1, tq), F32))
        args.append(kmean)
    return pl.pallas_call(
        functools.partial(_attn_kernel, moba=moba, nk=nq),
        grid=(B, nq), in_specs=in_specs,
        out_specs=pl.BlockSpec((tq, BRANCH_WIDTH), lambda b, i: (b * nq + i, 0)),
        out_shape=jax.ShapeDtypeStruct((T, BRANCH_WIDTH), F32),
        scratch_shapes=scratch,
        compiler_params=_cp("parallel", "arbitrary"))(*args)


def _gdn_conv_kernel(x_ref, w_ref, o_ref):
    c = pl.program_id(1)
    x = x_ref[...].astype(F32)
    w = w_ref[...]
    S = x.shape[0]
    row = lax.broadcasted_iota(jnp.int32, (S, LANES), 0)
    lane = lax.broadcasted_iota(jnp.int32, (S, LANES), 1)
    y = x * w[GDN_CONV - 1:GDN_CONV, :]
    for d in range(1, GDN_CONV):
        xs = jnp.where(row >= d, pltpu.roll(x, d, 0), 0.0)
        y = y + xs * w[GDN_CONV - 1 - d:GDN_CONV - d, :]
    y = y * _sigmoid(y)
    sq = y * y
    lo = lane < GDN_HEAD_DIM
    ss0 = jnp.sum(jnp.where(lo, sq, 0.0), axis=1, keepdims=True)
    ss1 = jnp.sum(jnp.where(lo, 0.0, sq), axis=1, keepdims=True)
    inv = lax.rsqrt(jnp.where(lo, ss0, ss1) + RMS_EPS)
    nqb = BRANCH_WIDTH // LANES
    scale = jnp.where(c < nqb, GDN_HEAD_DIM ** -0.5, 1.0)
    o_ref[...] = jnp.where(c < 2 * nqb, y * inv * scale, y)


def _gdn_conv(proj, conv_w, B, S):
    T = B * S
    nb = 3 * BRANCH_WIDTH // LANES
    return pl.pallas_call(
        _gdn_conv_kernel, grid=(B, nb),
        in_specs=[pl.BlockSpec((S, LANES), lambda b, c: (b, C_GQKV // LANES + c)),
                  pl.BlockSpec((GDN_CONV, LANES), lambda b, c: (0, c))],
        out_specs=pl.BlockSpec((S, LANES), lambda b, c: (b, c)),
        out_shape=jax.ShapeDtypeStruct((T, 3 * BRANCH_WIDTH), F32),
        compiler_params=_cp("parallel", "parallel"))(proj, conv_w)


def _gdn_chunk_kernel(q_ref, k_ref, v_ref, sm_ref, alog_ref, dtb_ref,
                      u_ref, w_ref, qg_ref, qk_ref, kdt_ref, dl_ref):
    C = GDN_KCHUNK
    nh = GDN_HEADS_PER_STEP
    h0 = pl.program_id(1) * nh
    lane = lax.broadcasted_iota(jnp.int32, (C, LANES), 1)
    row = lax.broadcasted_iota(jnp.int32, (C, LANES), 0)
    lo = lane < GDN_HEAD_DIM
    tril = row >= lane
    strict = row > lane
    eye = jnp.where(row == lane, 1.0, 0.0)
    ltri = tril.astype(BF16)

    def pick(ref, k):
        x = ref[:, (k // 2) * LANES:(k // 2 + 1) * LANES]
        return jnp.where(lo, pltpu.roll(x, LANES // 2, 1) if k % 2 else x, 0.0)

    sm = sm_ref[...]
    a = sm + dtb_ref[...]
    softplus = jnp.maximum(a, 0.0) + jnp.log1p(jnp.exp(-jnp.abs(a)))
    garr = -jnp.exp(alog_ref[...]) * softplus
    sig = _sigmoid(sm)
    kh, kb, gc, decay, A = [], [], [], [], []
    for k in range(nh):
        g_col = jnp.sum(jnp.where(lane == h0 + k, garr, 0.0), axis=1, keepdims=True)
        beta = jnp.sum(jnp.where(lane == HEADS + h0 + k, sig, 0.0), axis=1, keepdims=True)
        g1 = jnp.broadcast_to(g_col, (C, LANES))
        gsum = None
        for _ in range(3):
            gb = g1.astype(BF16)
            part = jnp.dot(ltri, gb, preferred_element_type=F32)
            gsum = part if gsum is None else gsum + part
            g1 = g1 - gb.astype(F32)
        gc.append(gsum)
        decay.append(jnp.where(tril, jnp.exp(jnp.where(tril, gsum - gsum.T, 0.0)), 0.0))
        kh.append(pick(k_ref, k))
        kb.append(kh[k] * beta)
        vb = pick(v_ref, k) * beta
        A.append(jnp.where(strict, _dot_nt(kb[k], kh[k]) * decay[k], 0.0))
        eg = jnp.exp(gsum)
        qh = pick(q_ref, k)
        qk_ref[k, 0] = jnp.where(tril, _dot_nt(qh, kh[k]) * decay[k], 0.0).astype(BF16)
        qg_ref[k, 0] = (qh * eg).astype(BF16)
        glast = gsum[C - 1:C, :]
        kdt_ref[k, 0] = (kh[k] * jnp.exp(glast - gsum)).T.astype(BF16)
        dl_ref[k, 0] = jnp.broadcast_to(jnp.exp(glast), (SUBLANES, LANES))
        u_ref[k, 0] = vb
        w_ref[k, 0] = (kb[k] * eg).astype(BF16)
    def joiner(level):
        same = (row >> (level + 1)) == (lane >> (level + 1))
        return same & (((row >> level) & 1) == 1) & (((lane >> level) & 1) == 0)

    P = [eye - jnp.where(joiner(0), A[k], 0.0) for k in range(nh)]
    for level in range(1, int(math.log2(C))):
        msk = joiner(level)
        T1 = [_dot(P[k], jnp.where(msk, A[k], 0.0)) for k in range(nh)]
        P = [P[k] - _dot(T1[k], P[k]) for k in range(nh)]
    for k in range(nh):
        u_ref[k, 0] = _dot(P[k], u_ref[k, 0])
        w_ref[k, 0] = _dot(P[k], w_ref[k, 0]).astype(BF16)


def _gdn_chunks(qkv, small, alog_row, dtb_row, B, S):
    C = GDN_KCHUNK
    n = S // C
    nh = GDN_HEADS_PER_STEP
    ng = HEADS // nh
    wq = nh // 2 * LANES
    nqb = BRANCH_WIDTH // wq
    big = lambda dt: jax.ShapeDtypeStruct((B * HEADS, n, C, LANES), dt)
    ospec = pl.BlockSpec((nh, 1, C, LANES), lambda b, g, c: (b * ng + g, c, 0, 0))
    par = pl.BlockSpec((1, LANES), lambda b, g, c: (0, 0))
    outs = pl.pallas_call(
        _gdn_chunk_kernel, grid=(B, ng, n),
        in_specs=[pl.BlockSpec((C, wq), lambda b, g, c: (b * n + c, g)),
                  pl.BlockSpec((C, wq), lambda b, g, c: (b * n + c, nqb + g)),
                  pl.BlockSpec((C, wq), lambda b, g, c: (b * n + c, 2 * nqb + g)),
                  pl.BlockSpec((C, LANES), lambda b, g, c: (b * n + c, 0)), par, par],
        out_specs=[ospec] * 5 + [pl.BlockSpec((nh, 1, SUBLANES, LANES), lambda b, g, c: (b * ng + g, c, 0, 0))],
        out_shape=[big(F32)] + [big(BF16)] * 4 + [jax.ShapeDtypeStruct((B * HEADS, n, SUBLANES, LANES), F32)],
        compiler_params=_cp("parallel", "parallel", "parallel"))(qkv, qkv, qkv, small, alog_row, dtb_row)
    return [o.reshape(-1, LANES) for o in outs]


def _gdn_scan_kernel(u_ref, w_ref, qg_ref, qk_ref, kdt_ref, dl_ref, z_ref, g_ref, o_ref, o_sc, *, n):
    C = GDN_KCHUNK
    S = n * C
    npair = GDN_SCAN_PAIRS

    def step(c, states):
        out = []
        for k, state in enumerate(states):
            r = pl.ds(pl.multiple_of(k * S + c * C, C), C)
            sb = state.astype(BF16)
            v_new = u_ref[r, :] - _dot(w_ref[r, :], sb)
            vb = v_new.astype(BF16)
            o_sc[r, :] = _dot(qg_ref[r, :], sb) + _dot(qk_ref[r, :], vb)
            dl = dl_ref[pl.ds(pl.multiple_of((k * n + c) * SUBLANES, SUBLANES), 1), :]
            out.append(state * dl + _dot(kdt_ref[r, :], vb))
        return tuple(out)

    lax.fori_loop(0, n, step, tuple(jnp.zeros((LANES, LANES), F32) for _ in range(2 * npair)))
    lane = lax.broadcasted_iota(jnp.int32, (S, LANES), 1)

    def nrm(o):
        ms = jnp.sum(o * o, axis=1, keepdims=True) * (1.0 / GDN_HEAD_DIM)
        return o * lax.rsqrt(ms + RMS_EPS)

    for p in range(npair):
        ps = slice(p * LANES, (p + 1) * LANES)
        nn = jnp.where(lane < GDN_HEAD_DIM, nrm(o_sc[2 * p * S:(2 * p + 1) * S, :]),
                       pltpu.roll(nrm(o_sc[(2 * p + 1) * S:(2 * p + 2) * S, :]), LANES // 2, 1))
        z = z_ref[:, ps].astype(F32)
        o_ref[:, ps] = nn * g_ref[...] * (z * _sigmoid(z))


def _gdn_scan(parts, proj, g_row, B, S):
    n = S // GDN_KCHUNK
    nps = GDN_SCAN_PAIRS
    steps = HEADS // 2 // nps
    wz = nps * LANES
    seq = pl.BlockSpec((2 * nps * S, LANES), lambda i: (i, 0))
    return pl.pallas_call(
        functools.partial(_gdn_scan_kernel, n=n), grid=(B * steps,),
        in_specs=[seq] * 5 + [pl.BlockSpec((2 * nps * n * SUBLANES, LANES), lambda i: (i, 0)),
                              pl.BlockSpec((S, wz), lambda i: (i // steps, C_GZ // wz + i % steps)),
                              pl.BlockSpec((1, LANES), lambda i: (0, 0))],
        out_specs=pl.BlockSpec((S, wz), lambda i: (i // steps, i % steps)),
        out_shape=jax.ShapeDtypeStruct((B * S, BRANCH_WIDTH), F32),
        scratch_shapes=[pltpu.VMEM((2 * nps * S, LANES), F32)],
        compiler_params=_cp("parallel"))(*parts, proj, g_row)


def _merge_out_kernel(g0_ref, g1_ref, g2_ref, gb_ref, a_ref, b_ref, c_ref, wb_ref, wo_ref, h_ref, lg_ref, lb_ref,
                      o_ref, ot_ref):
    acc = None
    for n, (gl, br) in enumerate(((g0_ref, a_ref), (g1_ref, b_ref), (g2_ref, c_ref))):
        y = _sigmoid(gl[...].astype(F32) + gb_ref[n:n + 1, :]) * _dot(br[...], wb_ref[n])
        acc = y if acc is None else acc + y
    y = DEEPNORM_ALPHA * h_ref[...] + _dot(acc, wo_ref[...])
    y = _ln_rows(y, lg_ref[...], lb_ref[...])
    o_ref[...] = y
    ot_ref[...] = y.T.astype(BF16)


def _merge_out_ln(proj, gate_bias, o_a, o_b, o_c, w_branch, w_out, h, g, b, tm=256):
    T, D = h.shape
    row = lambda i: (i, 0)
    gspec = lambda n: pl.BlockSpec((tm, D), lambda i: (i, n))
    bspec = pl.BlockSpec((tm, BRANCH_WIDTH), row)
    vec = pl.BlockSpec((1, D), lambda i: (0, 0))
    return pl.pallas_call(
        _merge_out_kernel, grid=(T // tm,),
        in_specs=[gspec(0), gspec(1), gspec(2), pl.BlockSpec((3, D), lambda i: (0, 0)), bspec, bspec, bspec,
                  pl.BlockSpec((3, BRANCH_WIDTH, D), lambda i: (0, 0, 0)), pl.BlockSpec((D, D), lambda i: (0, 0)),
                  pl.BlockSpec((tm, D), row), vec, vec],
        out_specs=[pl.BlockSpec((tm, D), row), pl.BlockSpec((D, tm), lambda i: (0, i))],
        out_shape=[jax.ShapeDtypeStruct((T, D), F32), jax.ShapeDtypeStruct((D, T), BF16)],
        compiler_params=_cp("parallel"))(proj, proj, proj, gate_bias, o_a, o_b, o_c, w_branch, w_out, h,
                                         g.reshape(1, D), b.reshape(1, D))


def _extract_top(curs, n, on_max):
    curs = list(curs)
    for k in range(n):
        for i, cur in enumerate(curs):
            m = jnp.max(cur, axis=0, keepdims=True)
            on_max(i, k, m)
            if k + 1 < n:
                curs[i] = jnp.where(cur >= m, -jnp.inf, cur)


def _sort_network(n):
    pairs, p = [], 1
    while p < n:
        k = p
        while k >= 1:
            for j in range(k % p, n - k, 2 * k):
                for i in range(min(k, n - j - k)):
                    if (i + j) // (2 * p) == (i + j + k) // (2 * p):
                        pairs.append((i + j, i + j + k))
            k //= 2
        p *= 2
    return pairs


def _top_of_sorted_lists(scores, n, on_max):
    nt = N_KEYS // SUBLANES
    lists = []
    for s in scores:
        rows = [s[t * SUBLANES:(t + 1) * SUBLANES, :] for t in range(nt)]
        for a, b in _sort_network(nt):
            rows[a], rows[b] = jnp.maximum(rows[a], rows[b]), jnp.minimum(rows[a], rows[b])
        lists.append(rows + [jnp.full(rows[0].shape, -jnp.inf, F32)])
    for k in range(n):
        for i, rows in enumerate(lists):
            m = jnp.max(rows[0], axis=0, keepdims=True)
            on_max(i, k, m)
            hit = rows[0] >= m
            for t in range(min(n - 1 - k, nt)):
                rows[t] = jnp.where(hit, rows[t + 1], rows[t])


def _peer_score_kernel(qt_ref, keys_ref, s2_ref, e2_ref, c_ref, e1_ref, top_sc, cand_sc):
    K = PEER_TOPK
    tt = qt_ref.shape[1]
    nh = PEER_SCORE_HEADS_PER_TRIP

    def heads(hp, carry):
        s = []
        for i in range(2 * nh):
            r = pl.multiple_of((hp * nh * 2 + i) * PEER_HALF, PEER_HALF)
            s.append(_dot(keys_ref[hp * nh * 2 + i], qt_ref[pl.ds(r, PEER_HALF), :]))

        def put(i, k, m):
            top_sc[i, k:k + 1, :] = m
        _top_of_sorted_lists(s, K + 1, put)
        r8 = lax.broadcasted_iota(jnp.int32, (SUBLANES, tt), 0)
        v1max, v2max = [], []
        for d in range(nh):
            v1a, v1x = top_sc[2 * d, 0:K, :], top_sc[2 * d, K:K + 1, :]
            v2a, v2x = top_sc[2 * d + 1, 0:K, :], top_sc[2 * d + 1, K:K + 1, :]
            v1max.append(v1a[0:1])
            v2max.append(v2a[0:1])
            cand_sc[d, 0:K, :] = v1a + v2a[0:1]
            for b in range(1, SUBLANES):
                cand_sc[d, K + SUBLANES * (b - 1):K + SUBLANES * b, :] = v1a[0:SUBLANES] + v2a[b:b + 1]
            base = K + SUBLANES * (SUBLANES - 1)
            cand_sc[d, base:base + SUBLANES, :] = v1a[0:1] + v2a[SUBLANES:K]
            cand_sc[d, base + SUBLANES:base + 2 * SUBLANES, :] = jnp.where(
                r8 == 0, v1x + v2a[0:1], jnp.where(r8 == 1, v1a[0:1] + v2x, -jnp.inf))
        st = [{"z": jnp.zeros((1, tt), F32)} for _ in range(nh)]

        def acc(d, k, m):
            if k == 0:
                st[d]["top"] = m
            if k < K:
                st[d]["z"] = st[d]["z"] + jnp.exp(m - st[d]["top"])
            if k == K - 1:
                st[d]["t16"] = m
            if k == K:
                st[d]["t17"] = m
        _extract_top([cand_sc[d] for d in range(nh)], K + 1, acc)
        for d in range(nh):
            tau = 0.5 * (st[d]["t16"] + st[d]["t17"])
            ro = pl.ds(pl.multiple_of((hp * nh + d) * N_KEYS, N_KEYS), N_KEYS)
            s2_ref[ro, :] = s[2 * d + 1]
            e2_ref[ro, :] = jnp.exp(s[2 * d + 1] - v2max[d]) / st[d]["z"]
            c_ref[ro, :] = tau - s[2 * d]
            e1_ref[ro, :] = jnp.exp(s[2 * d] - v1max[d])
        return carry

    lax.fori_loop(0, HEADS // nh, heads, 0)


def _peer_scores(qt, keys):
    T = qt.shape[1]
    tt = PEER_SCORE_TT
    R = HEADS * N_KEYS
    ncand = PEER_TOPK + SUBLANES * (SUBLANES + 1)
    ospec = pl.BlockSpec((R, tt), lambda i: (0, i))
    return pl.pallas_call(
        _peer_score_kernel, grid=(T // tt,),
        in_specs=[pl.BlockSpec((2 * R, tt), lambda i: (0, i)),
                  pl.BlockSpec((2 * HEADS, N_KEYS, PEER_HALF), lambda i: (0, 0, 0))],
        out_specs=[ospec] * 4,
        out_shape=[jax.ShapeDtypeStruct((R, T), F32)] * 4,
        scratch_shapes=[pltpu.VMEM((2 * PEER_SCORE_HEADS_PER_TRIP, 3 * SUBLANES, tt), F32),
                        pltpu.VMEM((PEER_SCORE_HEADS_PER_TRIP, ncand, tt), F32)],
        compiler_params=_cp("parallel"))(qt, keys)


def _peer_expert_kernel(ht_ref, u_ref, vtp_ref, vtl_ref, s2_ref, e2_ref, c_ref, e1_ref, h_ref, g_ref, b_ref,
                        o_ref, acc_sc, act_sc, hw_sc, bc_sc):
    j = pl.program_id(1)
    _, eb, tt = hw_sc.shape
    ng = eb // N_KEYS
    assert c_ref.shape == (HEADS, ng, tt)
    slot = j % 2

    @pl.when(j == 0)
    def _():
        acc_sc[...] = jnp.zeros(acc_sc.shape, F32)
        hw_sc[1] = jnp.zeros(hw_sc.shape[1:], BF16)

    mxu_w = 2 * LANES
    for half in range(tt // mxu_w):
        hl = slice(half * mxu_w, (half + 1) * mxu_w)
        a = jnp.dot(u_ref[...], ht_ref[:, hl], preferred_element_type=F32)
        act_sc[:, hl] = 0.5 * a * (1.0 + lax.erf(a * (2.0 ** -0.5)))
        acc_sc[:, hl] += jnp.dot(vtp_ref[...], hw_sc[1 - slot, :, hl], preferred_element_type=F32)
        for lc in range(half * (mxu_w // LANES), (half + 1) * (mxu_w // LANES)):
            ls = slice(lc * LANES, (lc + 1) * LANES)
            for h in range(HEADS):
                thr, e1 = c_ref[h, :, ls], e1_ref[h, :, ls]
                for g in range(ng):
                    bc_sc[0, h * ng + g] = jnp.broadcast_to(thr[g:g + 1], (SUBLANES, LANES))
                    bc_sc[1, h * ng + g] = jnp.broadcast_to(e1[g:g + 1], (SUBLANES, LANES))
            for g in range(ng):
                wsum = jnp.zeros((N_KEYS // SUBLANES, SUBLANES, LANES), F32)
                for h in range(HEADS):
                    hr = slice(h * N_KEYS, (h + 1) * N_KEYS)
                    s2 = s2_ref[hr, ls].reshape(N_KEYS // SUBLANES, SUBLANES, LANES)
                    e2 = e2_ref[hr, ls].reshape(N_KEYS // SUBLANES, SUBLANES, LANES)
                    wsum = wsum + jnp.where(s2 >= bc_sc[0, h * ng + g][None], e2 * bc_sc[1, h * ng + g][None], 0.0)
                gs = slice(g * N_KEYS, (g + 1) * N_KEYS)
                hw_sc[slot, gs, ls] = (wsum.reshape(N_KEYS, LANES) * act_sc[gs, ls]).astype(BF16)

    @pl.when(j == pl.num_programs(1) - 1)
    def _():
        acc = acc_sc[...] + jnp.dot(vtl_ref[...], hw_sc[slot], preferred_element_type=F32)
        y = DEEPNORM_ALPHA * h_ref[...] + acc.T
        o_ref[...] = _ln_rows(y, g_ref[...], b_ref[...])


def _peer_experts(ht, u, vt, s2, e2, c, e1, h, g, b):
    T, D = h.shape
    tt, eb = PEER_TT, PEER_EB
    R = HEADS * N_KEYS
    nb = N_EXPERTS // eb
    assert nb % 2 == 0
    ng = eb // N_KEYS
    sspec = pl.BlockSpec((R, tt), lambda i, j: (0, i))
    gspec = pl.BlockSpec((HEADS, ng, tt), lambda i, j: (0, j, i))
    vec = pl.BlockSpec((1, D), lambda i, j: (0, 0))
    return pl.pallas_call(
        _peer_expert_kernel, grid=(T // tt, nb),
        in_specs=[pl.BlockSpec((D, tt), lambda i, j: (0, i)),
                  pl.BlockSpec((eb, D), lambda i, j: (j, 0)),
                  pl.BlockSpec((D, eb), lambda i, j: (0, jnp.maximum(j - 1, 0))),
                  pl.BlockSpec((D, eb), lambda i, j: (0, nb - 1)),
                  sspec, sspec, gspec, gspec,
                  pl.BlockSpec((tt, D), lambda i, j: (i, 0)), vec, vec],
        out_specs=pl.BlockSpec((tt, D), lambda i, j: (i, 0)),
        out_shape=jax.ShapeDtypeStruct((T, D), F32),
        scratch_shapes=[pltpu.VMEM((D, tt), F32), pltpu.VMEM((eb, tt), F32), pltpu.VMEM((2, eb, tt), BF16),
                        pltpu.VMEM((2, HEADS * ng, SUBLANES, LANES), F32)],
        compiler_params=_cp("parallel", "arbitrary"))(ht, u, vt, vt, s2, e2, c.reshape(HEADS, N_KEYS, T),
                                                      e1.reshape(HEADS, N_KEYS, T), h, g.reshape(1, D),
                                                      b.reshape(1, D))


def _rot_half(x, r):
    return jnp.concatenate([-x[..., r:2 * r], x[..., :r]], axis=-1)


def _prep_w_in(w):
    D = w.shape[0]
    splits = (MLA_Q_RANK, MLA_KV_RANK, MLA_ROPE, 3 * BRANCH_WIDTH, 3 * BRANCH_WIDTH, BRANCH_WIDTH, HEADS, HEADS,
              3 * D)
    o = np.cumsum((0,) + splits)
    cq, ckv, kr, mqkv, gqkv, gz, ga, gb, gl = [w[:, o[i]:o[i + 1]] for i in range(9)]
    mq, mk, mv = jnp.split(mqkv, 3, axis=1)

    def slab(m):
        m = m.reshape(D, HEADS, MOBA_HEAD_DIM)
        pad = jnp.zeros((D, HEADS, LANES - MOBA_HEAD_DIM - MOBA_ROT_DIM), w.dtype)
        return jnp.concatenate([m, _rot_half(m, MOBA_ROT_DIM // 2), pad], axis=-1).reshape(D, HEADS * LANES)

    small = jnp.concatenate([ga, gb, jnp.zeros((D, LANES // 2 - 2 * HEADS), w.dtype), kr,
                             _rot_half(kr, MLA_ROPE // 2)], axis=1)
    out = jnp.concatenate([gl, cq, ckv, slab(mq), slab(mk), mv, gqkv, gz], axis=1)
    assert out.shape[1] == C_TOTAL and C_TOTAL % PROJ_TN == 0
    return out.astype(BF16), small.astype(BF16)


def _prep_w_uq(w):
    R = w.shape[0]
    w = w.reshape(R, HEADS, MLA_NOPE + MLA_ROPE)
    rope = w[..., MLA_NOPE:]
    return jnp.concatenate([w, _rot_half(rope, MLA_ROPE // 2)], axis=-1).reshape(R, HEADS * LANES).astype(BF16)


def _prep_w_ukv(w):
    R = w.shape[0]
    w = w.reshape(R, HEADS, MLA_NOPE + MLA_V)
    k = jnp.concatenate([w[..., :MLA_NOPE], jnp.zeros((R, HEADS, LANES - MLA_NOPE), w.dtype)], axis=-1)
    return jnp.concatenate([k.reshape(R, HEADS * LANES), w[..., MLA_NOPE:].reshape(R, HEADS * MLA_V)],
                           axis=1).astype(BF16)


def _rope_tables(positions):
    pos = positions.reshape(-1).astype(F32)[:, None]
    T = pos.shape[0]

    def cs(rot):
        inv = ROPE_THETA ** (-jnp.arange(0, rot, 2, dtype=F32) / rot)
        ang = pos * inv
        return jnp.cos(ang), jnp.sin(ang)

    ca, sa = cs(MLA_ROPE)
    cb, sb = cs(MOBA_ROT_DIM)
    one = lambda n: jnp.ones((T, n), F32)
    zero = lambda n: jnp.zeros((T, n), F32)
    sc_a = (MLA_NOPE + MLA_ROPE) ** -0.5 * math.log2(math.e)
    sc_b = MOBA_HEAD_DIM ** -0.5 * math.log2(math.e)
    cat = lambda *xs: jnp.concatenate(xs, axis=1)
    mla = (cat(one(MLA_NOPE), ca, ca, zero(32)) * sc_a, cat(zero(MLA_NOPE), sa, sa, zero(32)) * sc_a,
           cat(zero(MLA_NOPE), ca, ca, zero(32)), cat(zero(MLA_NOPE), sa, sa, zero(32)))
    cm = cat(cb, cb, one(MOBA_HEAD_DIM - MOBA_ROT_DIM), zero(LANES - MOBA_HEAD_DIM))
    sm = cat(sb, sb, zero(LANES - MOBA_ROT_DIM))
    moba = (cm * sc_b, sm * sc_b, cm, sm)
    return mla, moba


def _lane_row(v):
    return jnp.concatenate([v.astype(F32), jnp.zeros((LANES - v.shape[0],), F32)]).reshape(1, LANES)


def kernel(x, positions, ln_in_g, ln_in_b, w_in, mla_q_norm, mla_kv_norm, mla_w_uq, mla_w_ukv, gdn_conv_w, gdn_A_log, gdn_dt_bias, gdn_o_norm, gate_bias, w_branch, w_out, ln1_g, ln1_b, peer_w_q, peer_sub_keys, peer_u, peer_v, ln2_g, ln2_b):
    B, S, D = x.shape
    T = B * S
    assert S % ATT_TQ == 0 and S % GDN_KCHUNK == 0 and T % PEER_TT == 0 and ATT_TQ == MOBA_BLOCK
    mla_tabs, moba_tabs = _rope_tables(positions)
    h = _layer_norm(x.reshape(T, D), ln_in_g, ln_in_b)
    for l in range(DEPTH):
        w_main, w_small = _prep_w_in(w_in[l])
        proj = _mm(h, w_main, tm=512, tn=PROJ_TN, out_dtype=BF16)
        small = _mm(h, w_small, tm=512, tn=LANES)
        qa, ka, va = _mla_qkv(proj, small, mla_q_norm[l], mla_kv_norm[l], _prep_w_uq(mla_w_uq[l]),
                              _prep_w_ukv(mla_w_ukv[l]), mla_tabs)
        o_a = _attention(qa, ka, va, 0, B, S)
        qm, km, kmean = _moba_prep(proj, moba_tabs)
        o_b = _attention(qm, km, proj, C_MV, B, S, kmean=kmean.reshape(B, S // MOBA_BLOCK, HEADS * LANES))
        qkv = _gdn_conv(proj, gdn_conv_w[l], B, S)
        parts = _gdn_chunks(qkv, small, _lane_row(gdn_A_log[l]), _lane_row(gdn_dt_bias[l]), B, S)
        o_c = _gdn_scan(parts, proj, jnp.tile(gdn_o_norm[l], 2).reshape(1, LANES), B, S)
        h, ht = _merge_out_ln(proj, gate_bias[l], o_a, o_b, o_c, w_branch[l].astype(BF16), w_out[l].astype(BF16),
                              h, ln1_g[l], ln1_b[l])
        qt = _mm(peer_w_q[l].T.astype(BF16), ht, tm=1024, tn=min(1024, T), out_dtype=BF16)
        keys = peer_sub_keys[l].reshape(2 * HEADS, N_KEYS, PEER_HALF).astype(BF16)
        s2, e2, c, e1 = _peer_scores(qt, keys)
        h = _peer_experts(ht, peer_u[l].astype(BF16), peer_v[l].astype(BF16).T, s2, e2, c, e1, h, ln2_g[l], ln2_b[l])
    return h.reshape(B, S, D)
```

```python
import functools
import math

import numpy as np
import jax
import jax.numpy as jnp
from jax import lax
from jax.experimental import pallas as pl
from jax.experimental.pallas import tpu as pltpu

F32 = jnp.float32
BF16 = jnp.bfloat16
HIGHEST = lax.Precision.HIGHEST

DEPTH = 2
ROPE_THETA = 500000.0
NEG_INF = -1e30
LN_EPS = 1e-5
RMS_EPS = 1e-6
DEEPNORM_ALPHA = (2 * DEPTH) ** 0.25
HEADS = 8
MLA_NOPE, MLA_ROPE, MLA_V = 64, 32, 64
MLA_Q_RANK, MLA_KV_RANK = 768, 256
MOBA_HEAD_DIM, MOBA_ROT_DIM, MOBA_BLOCK, MOBA_TOPK = 64, 16, 256, 3
GDN_HEAD_DIM, GDN_CONV = 64, 4
BRANCH_WIDTH = 512
N_KEYS, PEER_TOPK, PEER_HALF = 128, 16, 128
N_EXPERTS = N_KEYS * N_KEYS

LANES = 128
SUBLANES = 8
VMEM_LIMIT = 56 * 1024 * 1024

GDN_KCHUNK = 128
GDN_HEADS_PER_STEP = 8
GDN_SCAN_PAIRS = 2
ATT_TQ = 256
PEER_TT = 512
PEER_EB = 1024
PEER_SCORE_TT = 256
PEER_SCORE_HEADS_PER_TRIP = 4

C_GATE, C_CQ, C_CKV, C_MQ, C_MK, C_MV, C_GQKV, C_GZ, C_TOTAL = (
    0, 3072, 3840, 4096, 5120, 6144, 6656, 8192, 8704)
PROJ_TN = 4352


def _cp(*sem):
    return pltpu.CompilerParams(dimension_semantics=sem, vmem_limit_bytes=VMEM_LIMIT)


def _dot(a, b):
    return jnp.dot(a.astype(BF16), b.astype(BF16), preferred_element_type=F32)


def _dot_nt(a, b):
    return lax.dot_general(a.astype(BF16), b.astype(BF16), (((1,), (1,)), ((), ())),
                           preferred_element_type=F32)


def _ln_rows(y, g, b):
    mu = jnp.mean(y, axis=-1, keepdims=True)
    d = y - mu
    var = jnp.mean(d * d, axis=-1, keepdims=True)
    return d * lax.rsqrt(var + LN_EPS) * g + b


def _sigmoid(x):
    return 1.0 / (1.0 + jnp.exp(-x))


def _ln_kernel(x_ref, g_ref, b_ref, o_ref):
    o_ref[...] = _ln_rows(x_ref[...], g_ref[...], b_ref[...])


def _layer_norm(x, g, b, tm=512):
    T, D = x.shape
    return pl.pallas_call(
        _ln_kernel, grid=(T // tm,),
        in_specs=[pl.BlockSpec((tm, D), lambda i: (i, 0)),
                  pl.BlockSpec((1, D), lambda i: (0, 0)),
                  pl.BlockSpec((1, D), lambda i: (0, 0))],
        out_specs=pl.BlockSpec((tm, D), lambda i: (i, 0)),
        out_shape=jax.ShapeDtypeStruct((T, D), F32),
        compiler_params=_cp("parallel"))(x, g.reshape(1, D), b.reshape(1, D))


def _mm_kernel(x_ref, w_ref, o_ref):
    o_ref[...] = _dot(x_ref[...], w_ref[...]).astype(o_ref.dtype)


def _mm(x, w, tm, tn, out_dtype=F32):
    M, K = x.shape
    N = w.shape[1]
    return pl.pallas_call(
        _mm_kernel, grid=(N // tn, M // tm),
        in_specs=[pl.BlockSpec((tm, K), lambda j, i: (i, 0)),
                  pl.BlockSpec((K, tn), lambda j, i: (0, j))],
        out_specs=pl.BlockSpec((tm, tn), lambda j, i: (i, j)),
        out_shape=jax.ShapeDtypeStruct((M, N), out_dtype),
        compiler_params=_cp("parallel", "parallel"))(x, w)


def _rms_rows(x, g):
    return x * lax.rsqrt(jnp.mean(x * x, axis=-1, keepdims=True) + RMS_EPS) * g


def _head_slabs(x):
    return [x[:, h * LANES:(h + 1) * LANES] for h in range(HEADS)]


def _mla_qkv_kernel(cq_ref, ckv_ref, gq_ref, gkv_ref, wq_ref, wkv_ref, sm_ref, cq_tab, sq_tab, ck_tab, sk_tab,
                    qo_ref, ko_ref, vo_ref):
    W = HEADS * LANES
    q = _dot(_rms_rows(cq_ref[...].astype(F32), gq_ref[...]), wq_ref[...])
    kv = _dot(_rms_rows(ckv_ref[...].astype(F32), gkv_ref[...]), wkv_ref[...])
    vo_ref[...] = kv[:, W:].astype(BF16)
    cq, sq = cq_tab[...], sq_tab[...]
    sm = sm_ref[...]
    kr = sm * ck_tab[...] + pltpu.roll(sm, LANES - MLA_ROPE, 1) * sk_tab[...]
    for h in range(HEADS):
        sl = slice(h * LANES, (h + 1) * LANES)
        qs = q[:, sl]
        qo_ref[:, sl] = (qs * cq + pltpu.roll(qs, LANES - MLA_ROPE, 1) * sq).astype(BF16)
        ko_ref[:, sl] = (kv[:, sl] + kr).astype(BF16)


def _mla_qkv(proj, small, gq, gkv, wq, wkv, tabs, tm=512):
    T = proj.shape[0]
    W = HEADS * LANES
    row = lambda i: (i, 0)
    tab = pl.BlockSpec((tm, LANES), row)
    full = lambda a: pl.BlockSpec(a.shape, lambda i: (0, 0))
    gq, gkv = gq.reshape(1, -1), gkv.reshape(1, -1)
    return pl.pallas_call(
        _mla_qkv_kernel, grid=(T // tm,),
        in_specs=[pl.BlockSpec((tm, MLA_Q_RANK), lambda i: (i, C_CQ // MLA_Q_RANK)),
                  pl.BlockSpec((tm, MLA_KV_RANK), lambda i: (i, C_CKV // MLA_KV_RANK)),
                  full(gq), full(gkv), full(wq), full(wkv), tab, tab, tab, tab, tab],
        out_specs=[pl.BlockSpec((tm, W), row), pl.BlockSpec((tm, W), row), pl.BlockSpec((tm, BRANCH_WIDTH), row)],
        out_shape=[jax.ShapeDtypeStruct((T, W), BF16)] * 2 + [jax.ShapeDtypeStruct((T, BRANCH_WIDTH), BF16)],
        compiler_params=_cp("parallel"))(proj, proj, gq, gkv, wq, wkv, small, *tabs)


def _moba_prep_kernel(q_ref, k_ref, cq_ref, sq_ref, ck_ref, sk_ref, qo_ref, ko_ref, km_ref):
    cq, sq, ck, sk = cq_ref[...], sq_ref[...], ck_ref[...], sk_ref[...]
    q, k = q_ref[...].astype(F32), k_ref[...].astype(F32)
    for h, (qs, ks) in enumerate(zip(_head_slabs(q), _head_slabs(k))):
        sl = slice(h * LANES, (h + 1) * LANES)
        qo_ref[:, sl] = qs * cq + pltpu.roll(qs, LANES // 2, 1) * sq
        kk = ks * ck + pltpu.roll(ks, LANES // 2, 1) * sk
        ko_ref[:, sl] = kk.astype(BF16)
        km_ref[0, :, sl] = jnp.mean(kk, axis=0, keepdims=True)


def _moba_prep(proj, tabs):
    T = proj.shape[0]
    W = HEADS * LANES
    tm = MOBA_BLOCK
    row = lambda i: (i, 0)
    tab = pl.BlockSpec((tm, LANES), row)
    return pl.pallas_call(
        _moba_prep_kernel, grid=(T // tm,),
        in_specs=[pl.BlockSpec((tm, W), lambda i: (i, C_MQ // W)),
                  pl.BlockSpec((tm, W), lambda i: (i, C_MK // W)), tab, tab, tab, tab],
        out_specs=[pl.BlockSpec((tm, W), row), pl.BlockSpec((tm, W), row),
                   pl.BlockSpec((1, 1, W), lambda i: (i, 0, 0))],
        out_shape=[jax.ShapeDtypeStruct((T, W), F32), jax.ShapeDtypeStruct((T, W), BF16),
                   jax.ShapeDtypeStruct((T // tm, 1, W), F32)],
        compiler_params=_cp("parallel"))(proj, proj, *tabs)


def _attn_kernel(*refs, moba, nk):
    if moba:
        q_ref, k_ref, v_ref, km_ref, o_ref, vt_sc, qb_sc, m_sc, l_sc, al_sc, acc_sc, s_sc, p_sc, bias_sc = refs
    else:
        q_ref, k_ref, v_ref, o_ref, vt_sc, m_sc, l_sc, al_sc, acc_sc, s_sc, p_sc = refs
    tq = tk = ATT_TQ
    dv = BRANCH_WIDTH // HEADS
    qi = pl.program_id(1)
    hslab = [slice(h * LANES, (h + 1) * LANES) for h in range(HEADS)]

    @pl.when(qi == 0)
    def _():
        for j in range(nk):
            for c in range(tk // LANES):
                for g in range(BRANCH_WIDTH // LANES):
                    vt_sc[j, g * LANES:(g + 1) * LANES, c * LANES:(c + 1) * LANES] = (
                        v_ref[j * tk + c * LANES:j * tk + (c + 1) * LANES,
                              g * LANES:(g + 1) * LANES].astype(F32).T.astype(BF16))

    if moba:
        blk = lax.broadcasted_iota(jnp.int32, (SUBLANES, tq), 0)
        for h in range(HEADS):
            qf = q_ref[:, hslab[h]]
            qb_sc[:, hslab[h]] = qf.astype(BF16)
            km = km_ref[0, :, hslab[h]]
            if nk < SUBLANES:
                km = jnp.concatenate([km, jnp.zeros((SUBLANES - nk, LANES), F32)], axis=0)
            gate = lax.dot_general(km, qf, (((1,), (1,)), ((), ())), precision=HIGHEST,
                                   preferred_element_type=F32)
            gate = jnp.where(blk < qi, gate, -jnp.inf)
            for n in range(nk):
                gn = gate[n:n + 1, :]
                beats = jnp.where(gate > gn, 1.0, jnp.where((gate == gn) & (blk < n), 1.0, 0.0))
                cnt = jnp.sum(beats, axis=0, keepdims=True)
                bias_sc[h, n] = jnp.where((cnt < MOBA_TOPK) & (n < qi), 0.0, NEG_INF)
    qsrc = qb_sc if moba else q_ref
    m_sc[...] = jnp.full(m_sc.shape, NEG_INF, F32)
    l_sc[...] = jnp.zeros(l_sc.shape, F32)
    acc_sc[...] = jnp.zeros(acc_sc.shape, F32)

    def block(j, diag):
        rows = pl.ds(pl.multiple_of(j * tk, tk), tk)
        for h in range(HEADS):
            s = _dot_nt(k_ref[rows, hslab[h]], qsrc[:, hslab[h]])
            if diag:
                keyi = lax.broadcasted_iota(jnp.int32, (tk, tq), 0)
                qryi = lax.broadcasted_iota(jnp.int32, (tk, tq), 1)
                s = jnp.where(keyi <= qryi, s, NEG_INF)
            elif moba:
                s = s + bias_sc[h, j]
            s_sc[h] = s
        for h in range(HEADS):
            s = s_sc[h]
            m_prev = m_sc[h]
            m_new = jnp.maximum(m_prev, jnp.max(s, axis=0, keepdims=True))
            alpha = jnp.exp2(m_prev - m_new)
            p = jnp.exp2(s - m_new)
            l_sc[h] = alpha * l_sc[h] + jnp.sum(p, axis=0, keepdims=True)
            p_sc[h] = p.astype(BF16)
            al_sc[h] = alpha
            m_sc[h] = m_new
        for h in range(HEADS):
            acc_sc[h] = al_sc[h] * acc_sc[h] + jnp.dot(vt_sc[j, h * dv:(h + 1) * dv, :], p_sc[h],
                                                       preferred_element_type=F32)

    def past(j, carry):
        block(j, False)
        return carry
    lax.fori_loop(0, qi, past, 0)
    block(qi, True)
    ot = jnp.concatenate([acc_sc[h] * (1.0 / l_sc[h]) for h in range(HEADS)], axis=0)
    o_ref[...] = ot.T


def _attention(q, k, v, v_col, B, S, kmean=None):
    T = B * S
    tq = ATT_TQ
    nq = S // tq
    W = HEADS * LANES
    dv = BRANCH_WIDTH // HEADS
    moba = kmean is not None
    assert nq <= SUBLANES and v_col % BRANCH_WIDTH == 0
    in_specs = [pl.BlockSpec((tq, W), lambda b, i: (b * nq + i, 0)),
                pl.BlockSpec((S, W), lambda b, i: (b, 0)),
                pl.BlockSpec((S, BRANCH_WIDTH), lambda b, i: (b, v_col // BRANCH_WIDTH))]
    scratch = [pltpu.VMEM((nq, BRANCH_WIDTH, tq), BF16)]
    if moba:
        scratch.append(pltpu.VMEM((tq, W), BF16))
    row = pltpu.VMEM((HEADS, 1, tq), F32)
    scratch += [row, row, row, pltpu.VMEM((HEADS, dv, tq), F32),
                pltpu.VMEM((HEADS, tq, tq), F32), pltpu.VMEM((HEADS, tq, tq), BF16)]
    args = [q, k, v]
    if moba:
        in_specs.append(pl.BlockSpec((1, nq, W), lambda b, i: (b, 0, 0)))
        scratch.append(pltpu.VMEM((HEADS, nq, 1, tq), F32))
        args.append(kmean)
    return pl.pallas_call(
        functools.partial(_attn_kernel, moba=moba, nk=nq),
        grid=(B, nq), in_specs=in_specs,
        out_specs=pl.BlockSpec((tq, BRANCH_WIDTH), lambda b, i: (b * nq + i, 0)),
        out_shape=jax.ShapeDtypeStruct((T, BRANCH_WIDTH), F32),
        scratch_shapes=scratch,
        compiler_params=_cp("parallel", "arbitrary"))(*args)


def _gdn_conv_kernel(x_ref, w_ref, o_ref):
    c = pl.program_id(1)
    x = x_ref[...].astype(F32)
    w = w_ref[...]
    S = x.shape[0]
    row = lax.broadcasted_iota(jnp.int32, (S, LANES), 0)
    lane = lax.broadcasted_iota(jnp.int32, (S, LANES), 1)
    y = x * w[GDN_CONV - 1:GDN_CONV, :]
    for d in range(1, GDN_CONV):
        xs = jnp.where(row >= d, pltpu.roll(x, d, 0), 0.0)
        y = y + xs * w[GDN_CONV - 1 - d:GDN_CONV - d, :]
    y = y * _sigmoid(y)
    sq = y * y
    lo = lane < GDN_HEAD_DIM
    ss0 = jnp.sum(jnp.where(lo, sq, 0.0), axis=1, keepdims=True)
    ss1 = jnp.sum(jnp.where(lo, 0.0, sq), axis=1, keepdims=True)
    inv = lax.rsqrt(jnp.where(lo, ss0, ss1) + RMS_EPS)
    nqb = BRANCH_WIDTH // LANES
    scale = jnp.where(c < nqb, GDN_HEAD_DIM ** -0.5, 1.0)
    o_ref[...] = jnp.where(c < 2 * nqb, y * inv * scale, y)


def _gdn_conv(proj, conv_w, B, S):
    T = B * S
    nb = 3 * BRANCH_WIDTH // LANES
    return pl.pallas_call(
        _gdn_conv_kernel, grid=(B, nb),
        in_specs=[pl.BlockSpec((S, LANES), lambda b, c: (b, C_GQKV // LANES + c)),
                  pl.BlockSpec((GDN_CONV, LANES), lambda b, c: (0, c))],
        out_specs=pl.BlockSpec((S, LANES), lambda b, c: (b, c)),
        out_shape=jax.ShapeDtypeStruct((T, 3 * BRANCH_WIDTH), F32),
        compiler_params=_cp("parallel", "parallel"))(proj, conv_w)


def _gdn_chunk_kernel(q_ref, k_ref, v_ref, sm_ref, alog_ref, dtb_ref,
                      u_ref, w_ref, qg_ref, qk_ref, kdt_ref, dl_ref):
    C = GDN_KCHUNK
    nh = GDN_HEADS_PER_STEP
    h0 = pl.program_id(1) * nh
    lane = lax.broadcasted_iota(jnp.int32, (C, LANES), 1)
    row = lax.broadcasted_iota(jnp.int32, (C, LANES), 0)
    lo = lane < GDN_HEAD_DIM
    tril = row >= lane
    strict = row > lane
    eye = jnp.where(row == lane, 1.0, 0.0)
    ltri = tril.astype(BF16)

    def pick(ref, k):
        x = ref[:, (k // 2) * LANES:(k // 2 + 1) * LANES]
        return jnp.where(lo, pltpu.roll(x, LANES // 2, 1) if k % 2 else x, 0.0)

    sm = sm_ref[...]
    a = sm + dtb_ref[...]
    softplus = jnp.maximum(a, 0.0) + jnp.log1p(jnp.exp(-jnp.abs(a)))
    garr = -jnp.exp(alog_ref[...]) * softplus
    sig = _sigmoid(sm)
    kh, kb, gc, decay, A = [], [], [], [], []
    for k in range(nh):
        g_col = jnp.sum(jnp.where(lane == h0 + k, garr, 0.0), axis=1, keepdims=True)
        beta = jnp.sum(jnp.where(lane == HEADS + h0 + k, sig, 0.0), axis=1, keepdims=True)
        g1 = jnp.broadcast_to(g_col, (C, LANES))
        gsum = None
        for _ in range(3):
            gb = g1.astype(BF16)
            part = jnp.dot(ltri, gb, preferred_element_type=F32)
            gsum = part if gsum is None else gsum + part
            g1 = g1 - gb.astype(F32)
        gc.append(gsum)
        decay.append(jnp.where(tril, jnp.exp(jnp.where(tril, gsum - gsum.T, 0.0)), 0.0))
        kh.append(pick(k_ref, k))
        kb.append(kh[k] * beta)
        vb = pick(v_ref, k) * beta
        A.append(jnp.where(strict, _dot_nt(kb[k], kh[k]) * decay[k], 0.0))
        eg = jnp.exp(gsum)
        qh = pick(q_ref, k)
        qk_ref[k, 0] = jnp.where(tril, _dot_nt(qh, kh[k]) * decay[k], 0.0).astype(BF16)
        qg_ref[k, 0] = (qh * eg).astype(BF16)
        glast = gsum[C - 1:C, :]
        kdt_ref[k, 0] = (kh[k] * jnp.exp(glast - gsum)).T.astype(BF16)
        dl_ref[k, 0] = jnp.broadcast_to(jnp.exp(glast), (SUBLANES, LANES))
        u_ref[k, 0] = vb
        w_ref[k, 0] = (kb[k] * eg).astype(BF16)
    def joiner(level):
        same = (row >> (level + 1)) == (lane >> (level + 1))
        return same & (((row >> level) & 1) == 1) & (((lane >> level) & 1) == 0)

    P = [eye - jnp.where(joiner(0), A[k], 0.0) for k in range(nh)]
    for level in range(1, int(math.log2(C))):
        msk = joiner(level)
        T1 = [_dot(P[k], jnp.where(msk, A[k], 0.0)) for k in range(nh)]
        P = [P[k] - _dot(T1[k], P[k]) for k in range(nh)]
    for k in range(nh):
        u_ref[k, 0] = _dot(P[k], u_ref[k, 0])
        w_ref[k, 0] = _dot(P[k], w_ref[k, 0]).astype(BF16)


def _gdn_chunks(qkv, small, alog_row, dtb_row, B, S):
    C = GDN_KCHUNK
    n = S // C
    nh = GDN_HEADS_PER_STEP
    ng = HEADS // nh
    wq = nh // 2 * LANES
    nqb = BRANCH_WIDTH // wq
    big = lambda dt: jax.ShapeDtypeStruct((B * HEADS, n, C, LANES), dt)
    ospec = pl.BlockSpec((nh, 1, C, LANES), lambda b, g, c: (b * ng + g, c, 0, 0))
    par = pl.BlockSpec((1, LANES), lambda b, g, c: (0, 0))
    outs = pl.pallas_call(
        _gdn_chunk_kernel, grid=(B, ng, n),
        in_specs=[pl.BlockSpec((C, wq), lambda b, g, c: (b * n + c, g)),
                  pl.BlockSpec((C, wq), lambda b, g, c: (b * n + c, nqb + g)),
                  pl.BlockSpec((C, wq), lambda b, g, c: (b * n + c, 2 * nqb + g)),
                  pl.BlockSpec((C, LANES), lambda b, g, c: (b * n + c, 0)), par, par],
        out_specs=[ospec] * 5 + [pl.BlockSpec((nh, 1, SUBLANES, LANES), lambda b, g, c: (b * ng + g, c, 0, 0))],
        out_shape=[big(F32)] + [big(BF16)] * 4 + [jax.ShapeDtypeStruct((B * HEADS, n, SUBLANES, LANES), F32)],
        compiler_params=_cp("parallel", "parallel", "parallel"))(qkv, qkv, qkv, small, alog_row, dtb_row)
    return [o.reshape(-1, LANES) for o in outs]


def _gdn_scan_kernel(u_ref, w_ref, qg_ref, qk_ref, kdt_ref, dl_ref, z_ref, g_ref, o_ref, o_sc, *, n):
    C = GDN_KCHUNK
    S = n * C
    npair = GDN_SCAN_PAIRS

    def step(c, states):
        out = []
        for k, state in enumerate(states):
            r = pl.ds(pl.multiple_of(k * S + c * C, C), C)
            sb = state.astype(BF16)
            v_new = u_ref[r, :] - _dot(w_ref[r, :], sb)
            vb = v_new.astype(BF16)
            o_sc[r, :] = _dot(qg_ref[r, :], sb) + _dot(qk_ref[r, :], vb)
            dl = dl_ref[pl.ds(pl.multiple_of((k * n + c) * SUBLANES, SUBLANES), 1), :]
            out.append(state * dl + _dot(kdt_ref[r, :], vb))
        return tuple(out)

    lax.fori_loop(0, n, step, tuple(jnp.zeros((LANES, LANES), F32) for _ in range(2 * npair)))
    lane = lax.broadcasted_iota(jnp.int32, (S, LANES), 1)

    def nrm(o):
        ms = jnp.sum(o * o, axis=1, keepdims=True) * (1.0 / GDN_HEAD_DIM)
        return o * lax.rsqrt(ms + RMS_EPS)

    for p in range(npair):
        ps = slice(p * LANES, (p + 1) * LANES)
        nn = jnp.where(lane < GDN_HEAD_DIM, nrm(o_sc[2 * p * S:(2 * p + 1) * S, :]),
                       pltpu.roll(nrm(o_sc[(2 * p + 1) * S:(2 * p + 2) * S, :]), LANES // 2, 1))
        z = z_ref[:, ps].astype(F32)
        o_ref[:, ps] = nn * g_ref[...] * (z * _sigmoid(z))


def _gdn_scan(parts, proj, g_row, B, S):
    n = S // GDN_KCHUNK
    nps = GDN_SCAN_PAIRS
    steps = HEADS // 2 // nps
    wz = nps * LANES
    seq = pl.BlockSpec((2 * nps * S, LANES), lambda i: (i, 0))
    return pl.pallas_call(
        functools.partial(_gdn_scan_kernel, n=n), grid=(B * steps,),
        in_specs=[seq] * 5 + [pl.BlockSpec((2 * nps * n * SUBLANES, LANES), lambda i: (i, 0)),
                              pl.BlockSpec((S, wz), lambda i: (i // steps, C_GZ // wz + i % steps)),
                              pl.BlockSpec((1, LANES), lambda i: (0, 0))],
        out_specs=pl.BlockSpec((S, wz), lambda i: (i // steps, i % steps)),
        out_shape=jax.ShapeDtypeStruct((B * S, BRANCH_WIDTH), F32),
        scratch_shapes=[pltpu.VMEM((2 * nps * S, LANES), F32)],
        compiler_params=_cp("parallel"))(*parts, proj, g_row)


def _merge_out_kernel(g0_ref, g1_ref, g2_ref, gb_ref, a_ref, b_ref, c_ref, wb_ref, wo_ref, h_ref, lg_ref, lb_ref,
                      o_ref, ot_ref):
    acc = None
    for n, (gl, br) in enumerate(((g0_ref, a_ref), (g1_ref, b_ref), (g2_ref, c_ref))):
        y = _sigmoid(gl[...].astype(F32) + gb_ref[n:n + 1, :]) * _dot(br[...], wb_ref[n])
        acc = y if acc is None else acc + y
    y = DEEPNORM_ALPHA * h_ref[...] + _dot(acc, wo_ref[...])
    y = _ln_rows(y, lg_ref[...], lb_ref[...])
    o_ref[...] = y
    ot_ref[...] = y.T.astype(BF16)


def _merge_out_ln(proj, gate_bias, o_a, o_b, o_c, w_branch, w_out, h, g, b, tm=256):
    T, D = h.shape
    row = lambda i: (i, 0)
    gspec = lambda n: pl.BlockSpec((tm, D), lambda i: (i, n))
    bspec = pl.BlockSpec((tm, BRANCH_WIDTH), row)
    vec = pl.BlockSpec((1, D), lambda i: (0, 0))
    return pl.pallas_call(
        _merge_out_kernel, grid=(T // tm,),
        in_specs=[gspec(0), gspec(1), gspec(2), pl.BlockSpec((3, D), lambda i: (0, 0)), bspec, bspec, bspec,
                  pl.BlockSpec((3, BRANCH_WIDTH, D), lambda i: (0, 0, 0)), pl.BlockSpec((D, D), lambda i: (0, 0)),
                  pl.BlockSpec((tm, D), row), vec, vec],
        out_specs=[pl.BlockSpec((tm, D), row), pl.BlockSpec((D, tm), lambda i: (0, i))],
        out_shape=[jax.ShapeDtypeStruct((T, D), F32), jax.ShapeDtypeStruct((D, T), BF16)],
        compiler_params=_cp("parallel"))(proj, proj, proj, gate_bias, o_a, o_b, o_c, w_branch, w_out, h,
                                         g.reshape(1, D), b.reshape(1, D))


def _extract_top(curs, n, on_max):
    curs = list(curs)
    for k in range(n):
        for i, cur in enumerate(curs):
            m = jnp.max(cur, axis=0, keepdims=True)
            on_max(i, k, m)
            if k + 1 < n:
                curs[i] = jnp.where(cur >= m, -jnp.inf, cur)


def _sort_network(n):
    pairs, p = [], 1
    while p < n:
        k = p
        while k >= 1:
            for j in range(k % p, n - k, 2 * k):
                for i in range(min(k, n - j - k)):
                    if (i + j) // (2 * p) == (i + j + k) // (2 * p):
                        pairs.append((i + j, i + j + k))
            k //= 2
        p *= 2
    return pairs


def _top_of_sorted_lists(scores, n, on_max):
    nt = N_KEYS // SUBLANES
    lists = []
    for s in scores:
        rows = [s[t * SUBLANES:(t + 1) * SUBLANES, :] for t in range(nt)]
        for a, b in _sort_network(nt):
            rows[a], rows[b] = jnp.maximum(rows[a], rows[b]), jnp.minimum(rows[a], rows[b])
        lists.append(rows + [jnp.full(rows[0].shape, -jnp.inf, F32)])
    for k in range(n):
        for i, rows in enumerate(lists):
            m = jnp.max(rows[0], axis=0, keepdims=True)
            on_max(i, k, m)
            hit = rows[0] >= m
            for t in range(min(n - 1 - k, nt)):
                rows[t] = jnp.where(hit, rows[t + 1], rows[t])


def _peer_score_kernel(qt_ref, keys_ref, s2_ref, e2_ref, c_ref, e1_ref, top_sc, cand_sc):
    K = PEER_TOPK
    tt = qt_ref.shape[1]
    nh = PEER_SCORE_HEADS_PER_TRIP

    def heads(hp, carry):
        s = []
        for i in range(2 * nh):
            r = pl.multiple_of((hp * nh * 2 + i) * PEER_HALF, PEER_HALF)
            s.append(_dot(keys_ref[hp * nh * 2 + i], qt_ref[pl.ds(r, PEER_HALF), :]))

        def put(i, k, m):
            top_sc[i, k:k + 1, :] = m
        _top_of_sorted_lists(s, K + 1, put)
        r8 = lax.broadcasted_iota(jnp.int32, (SUBLANES, tt), 0)
        v1max, v2max = [], []
        for d in range(nh):
            v1a, v1x = top_sc[2 * d, 0:K, :], top_sc[2 * d, K:K + 1, :]
            v2a, v2x = top_sc[2 * d + 1, 0:K, :], top_sc[2 * d + 1, K:K + 1, :]
            v1max.append(v1a[0:1])
            v2max.append(v2a[0:1])
            cand_sc[d, 0:K, :] = v1a + v2a[0:1]
            for b in range(1, SUBLANES):
                cand_sc[d, K + SUBLANES * (b - 1):K + SUBLANES * b, :] = v1a[0:SUBLANES] + v2a[b:b + 1]
            base = K + SUBLANES * (SUBLANES - 1)
            cand_sc[d, base:base + SUBLANES, :] = v1a[0:1] + v2a[SUBLANES:K]
            cand_sc[d, base + SUBLANES:base + 2 * SUBLANES, :] = jnp.where(
                r8 == 0, v1x + v2a[0:1], jnp.where(r8 == 1, v1a[0:1] + v2x, -jnp.inf))
        st = [{"z": jnp.zeros((1, tt), F32)} for _ in range(nh)]

        def acc(d, k, m):
            if k == 0:
                st[d]["top"] = m
            if k < K:
                st[d]["z"] = st[d]["z"] + jnp.exp(m - st[d]["top"])
            if k == K - 1:
                st[d]["t16"] = m
            if k == K:
                st[d]["t17"] = m
        _extract_top([cand_sc[d] for d in range(nh)], K + 1, acc)
        for d in range(nh):
            tau = 0.5 * (st[d]["t16"] + st[d]["t17"])
            ro = pl.ds(pl.multiple_of((hp * nh + d) * N_KEYS, N_KEYS), N_KEYS)
            s2_ref[ro, :] = s[2 * d + 1]
            e2_ref[ro, :] = jnp.exp(s[2 * d + 1] - v2max[d]) / st[d]["z"]
            c_ref[ro, :] = tau - s[2 * d]
            e1_ref[ro, :] = jnp.exp(s[2 * d] - v1max[d])
        return carry

    lax.fori_loop(0, HEADS // nh, heads, 0)


def _peer_scores(qt, keys):
    T = qt.shape[1]
    tt = PEER_SCORE_TT
    R = HEADS * N_KEYS
    ncand = PEER_TOPK + SUBLANES * (SUBLANES + 1)
    ospec = pl.BlockSpec((R, tt), lambda i: (0, i))
    return pl.pallas_call(
        _peer_score_kernel, grid=(T // tt,),
        in_specs=[pl.BlockSpec((2 * R, tt), lambda i: (0, i)),
                  pl.BlockSpec((2 * HEADS, N_KEYS, PEER_HALF), lambda i: (0, 0, 0))],
        out_specs=[ospec] * 4,
        out_shape=[jax.ShapeDtypeStruct((R, T), F32)] * 4,
        scratch_shapes=[pltpu.VMEM((2 * PEER_SCORE_HEADS_PER_TRIP, 3 * SUBLANES, tt), F32),
                        pltpu.VMEM((PEER_SCORE_HEADS_PER_TRIP, ncand, tt), F32)],
        compiler_params=_cp("parallel"))(qt, keys)


def _peer_expert_kernel(ht_ref, u_ref, vtp_ref, vtl_ref, s2_ref, e2_ref, c_ref, e1_ref, h_ref, g_ref, b_ref,
                        o_ref, acc_sc, act_sc, hw_sc, bc_sc):
    j = pl.program_id(1)
    _, eb, tt = hw_sc.shape
    ng = eb // N_KEYS
    assert c_ref.shape == (HEADS, ng, tt)
    slot = j % 2

    @pl.when(j == 0)
    def _():
        acc_sc[...] = jnp.zeros(acc_sc.shape, F32)
        hw_sc[1] = jnp.zeros(hw_sc.shape[1:], BF16)

    mxu_w = 2 * LANES
    for half in range(tt // mxu_w):
        hl = slice(half * mxu_w, (half + 1) * mxu_w)
        a = jnp.dot(u_ref[...], ht_ref[:, hl], preferred_element_type=F32)
        act_sc[:, hl] = 0.5 * a * (1.0 + lax.erf(a * (2.0 ** -0.5)))
        acc_sc[:, hl] += jnp.dot(vtp_ref[...], hw_sc[1 - slot, :, hl], preferred_element_type=F32)
        for lc in range(half * (mxu_w // LANES), (half + 1) * (mxu_w // LANES)):
            ls = slice(lc * LANES, (lc + 1) * LANES)
            for h in range(HEADS):
                thr, e1 = c_ref[h, :, ls], e1_ref[h, :, ls]
                for g in range(ng):
                    bc_sc[0, h * ng + g] = jnp.broadcast_to(thr[g:g + 1], (SUBLANES, LANES))
                    bc_sc[1, h * ng + g] = jnp.broadcast_to(e1[g:g + 1], (SUBLANES, LANES))
            for g in range(ng):
                wsum = jnp.zeros((N_KEYS // SUBLANES, SUBLANES, LANES), F32)
                for h in range(HEADS):
                    hr = slice(h * N_KEYS, (h + 1) * N_KEYS)
                    s2 = s2_ref[hr, ls].reshape(N_KEYS // SUBLANES, SUBLANES, LANES)
                    e2 = e2_ref[hr, ls].reshape(N_KEYS // SUBLANES, SUBLANES, LANES)
                    wsum = wsum + jnp.where(s2 >= bc_sc[0, h * ng + g][None], e2 * bc_sc[1, h * ng + g][None], 0.0)
                gs = slice(g * N_KEYS, (g + 1) * N_KEYS)
                hw_sc[slot, gs, ls] = (wsum.reshape(N_KEYS, LANES) * act_sc[gs, ls]).astype(BF16)

    @pl.when(j == pl.num_programs(1) - 1)
    def _():
        acc = acc_sc[...] + jnp.dot(vtl_ref[...], hw_sc[slot], preferred_element_type=F32)
        y = DEEPNORM_ALPHA * h_ref[...] + acc.T
        o_ref[...] = _ln_rows(y, g_ref[...], b_ref[...])


def _peer_experts(ht, u, vt, s2, e2, c, e1, h, g, b):
    T, D = h.shape
    tt, eb = PEER_TT, PEER_EB
    R = HEADS * N_KEYS
    nb = N_EXPERTS // eb
    assert nb % 2 == 0
    ng = eb // N_KEYS
    sspec = pl.BlockSpec((R, tt), lambda i, j: (0, i))
    gspec = pl.BlockSpec((HEADS, ng, tt), lambda i, j: (0, j, i))
    vec = pl.BlockSpec((1, D), lambda i, j: (0, 0))
    return pl.pallas_call(
        _peer_expert_kernel, grid=(T // tt, nb),
        in_specs=[pl.BlockSpec((D, tt), lambda i, j: (0, i)),
                  pl.BlockSpec((eb, D), lambda i, j: (j, 0)),
                  pl.BlockSpec((D, eb), lambda i, j: (0, jnp.maximum(j - 1, 0))),
                  pl.BlockSpec((D, eb), lambda i, j: (0, nb - 1)),
                  sspec, sspec, gspec, gspec,
                  pl.BlockSpec((tt, D), lambda i, j: (i, 0)), vec, vec],
        out_specs=pl.BlockSpec((tt, D), lambda i, j: (i, 0)),
        out_shape=jax.ShapeDtypeStruct((T, D), F32),
        scratch_shapes=[pltpu.VMEM((D, tt), F32), pltpu.VMEM((eb, tt), F32), pltpu.VMEM((2, eb, tt), BF16),
                        pltpu.VMEM((2, HEADS * ng, SUBLANES, LANES), F32)],
        compiler_params=_cp("parallel", "arbitrary"))(ht, u, vt, vt, s2, e2, c.reshape(HEADS, N_KEYS, T),
                                                      e1.reshape(HEADS, N_KEYS, T), h, g.reshape(1, D),
                                                      b.reshape(1, D))


def _rot_half(x, r):
    return jnp.concatenate([-x[..., r:2 * r], x[..., :r]], axis=-1)


def _prep_w_in(w):
    D = w.shape[0]
    splits = (MLA_Q_RANK, MLA_KV_RANK, MLA_ROPE, 3 * BRANCH_WIDTH, 3 * BRANCH_WIDTH, BRANCH_WIDTH, HEADS, HEADS,
              3 * D)
    o = np.cumsum((0,) + splits)
    cq, ckv, kr, mqkv, gqkv, gz, ga, gb, gl = [w[:, o[i]:o[i + 1]] for i in range(9)]
    mq, mk, mv = jnp.split(mqkv, 3, axis=1)

    def slab(m):
        m = m.reshape(D, HEADS, MOBA_HEAD_DIM)
        pad = jnp.zeros((D, HEADS, LANES - MOBA_HEAD_DIM - MOBA_ROT_DIM), w.dtype)
        return jnp.concatenate([m, _rot_half(m, MOBA_ROT_DIM // 2), pad], axis=-1).reshape(D, HEADS * LANES)

    small = jnp.concatenate([ga, gb, jnp.zeros((D, LANES // 2 - 2 * HEADS), w.dtype), kr,
                             _rot_half(kr, MLA_ROPE // 2)], axis=1)
    out = jnp.concatenate([gl, cq, ckv, slab(mq), slab(mk), mv, gqkv, gz], axis=1)
    assert out.shape[1] == C_TOTAL and C_TOTAL % PROJ_TN == 0
    return out.astype(BF16), small.astype(BF16)


def _prep_w_uq(w):
    R = w.shape[0]
    w = w.reshape(R, HEADS, MLA_NOPE + MLA_ROPE)
    rope = w[..., MLA_NOPE:]
    return jnp.concatenate([w, _rot_half(rope, MLA_ROPE // 2)], axis=-1).reshape(R, HEADS * LANES).astype(BF16)


def _prep_w_ukv(w):
    R = w.shape[0]
    w = w.reshape(R, HEADS, MLA_NOPE + MLA_V)
    k = jnp.concatenate([w[..., :MLA_NOPE], jnp.zeros((R, HEADS, LANES - MLA_NOPE), w.dtype)], axis=-1)
    return jnp.concatenate([k.reshape(R, HEADS * LANES), w[..., MLA_NOPE:].reshape(R, HEADS * MLA_V)],
                           axis=1).astype(BF16)


def _rope_tables(positions):
    pos = positions.reshape(-1).astype(F32)[:, None]
    T = pos.shape[0]

    def cs(rot):
        inv = ROPE_THETA ** (-jnp.arange(0, rot, 2, dtype=F32) / rot)
        ang = pos * inv
        return jnp.cos(ang), jnp.sin(ang)

    ca, sa = cs(MLA_ROPE)
    cb, sb = cs(MOBA_ROT_DIM)
    one = lambda n: jnp.ones((T, n), F32)
    zero = lambda n: jnp.zeros((T, n), F32)
    sc_a = (MLA_NOPE + MLA_ROPE) ** -0.5 * math.log2(math.e)
    sc_b = MOBA_HEAD_DIM ** -0.5 * math.log2(math.e)
    cat = lambda *xs: jnp.concatenate(xs, axis=1)
    mla = (cat(one(MLA_NOPE), ca, ca, zero(32)) * sc_a, cat(zero(MLA_NOPE), sa, sa, zero(32)) * sc_a,
           cat(zero(MLA_NOPE), ca, ca, zero(32)), cat(zero(MLA_NOPE), sa, sa, zero(32)))
    cm = cat(cb, cb, one(MOBA_HEAD_DIM - MOBA_ROT_DIM), zero(LANES - MOBA_HEAD_DIM))
    sm = cat(sb, sb, zero(LANES - MOBA_ROT_DIM))
    moba = (cm * sc_b, sm * sc_b, cm, sm)
    return mla, moba


def _lane_row(v):
    return jnp.concatenate([v.astype(F32), jnp.zeros((LANES - v.shape[0],), F32)]).reshape(1, LANES)


def kernel(x, positions, ln_in_g, ln_in_b, w_in, mla_q_norm, mla_kv_norm, mla_w_uq, mla_w_ukv, gdn_conv_w, gdn_A_log, gdn_dt_bias, gdn_o_norm, gate_bias, w_branch, w_out, ln1_g, ln1_b, peer_w_q, peer_sub_keys, peer_u, peer_v, ln2_g, ln2_b):
    B, S, D = x.shape
    T = B * S
    assert S % ATT_TQ == 0 and S % GDN_KCHUNK == 0 and T % PEER_TT == 0 and ATT_TQ == MOBA_BLOCK
    mla_tabs, moba_tabs = _rope_tables(positions)
    h = _layer_norm(x.reshape(T, D), ln_in_g, ln_in_b)
    for l in range(DEPTH):
        w_main, w_small = _prep_w_in(w_in[l])
        proj = _mm(h, w_main, tm=512, tn=PROJ_TN, out_dtype=BF16)
        small = _mm(h, w_small, tm=512, tn=LANES)
        qa, ka, va = _mla_qkv(proj, small, mla_q_norm[l], mla_kv_norm[l], _prep_w_uq(mla_w_uq[l]),
                              _prep_w_ukv(mla_w_ukv[l]), mla_tabs)
        o_a = _attention(qa, ka, va, 0, B, S)
        qm, km, kmean = _moba_prep(proj, moba_tabs)
        o_b = _attention(qm, km, proj, C_MV, B, S, kmean=kmean.reshape(B, S // MOBA_BLOCK, HEADS * LANES))
        qkv = _gdn_conv(proj, gdn_conv_w[l], B, S)
        parts = _gdn_chunks(qkv, small, _lane_row(gdn_A_log[l]), _lane_row(gdn_dt_bias[l]), B, S)
        o_c = _gdn_scan(parts, proj, jnp.tile(gdn_o_norm[l], 2).reshape(1, LANES), B, S)
        h, ht = _merge_out_ln(proj, gate_bias[l], o_a, o_b, o_c, w_branch[l].astype(BF16), w_out[l].astype(BF16),
                              h, ln1_g[l], ln1_b[l])
        qt = _mm(peer_w_q[l].T.astype(BF16), ht, tm=1024, tn=min(1024, T), out_dtype=BF16)
        keys = peer_sub_keys[l].reshape(2 * HEADS, N_KEYS, PEER_HALF).astype(BF16)
        s2, e2, c, e1 = _peer_scores(qt, keys)
        h = _peer_experts(ht, peer_u[l].astype(BF16), peer_v[l].astype(BF16).T, s2, e2, c, e1, h, ln2_g[l], ln2_b[l])
    return h.reshape(B, S, D)
```

```python
import functools
import math

import numpy as np
import jax
import jax.numpy as jnp
from jax import lax
from jax.experimental import pallas as pl
from jax.experimental.pallas import tpu as pltpu

F32 = jnp.float32
BF16 = jnp.bfloat16
HIGHEST = lax.Precision.HIGHEST

DEPTH = 2
ROPE_THETA = 500000.0
NEG_INF = -1e30
LN_EPS = 1e-5
RMS_EPS = 1e-6
DEEPNORM_ALPHA = (2 * DEPTH) ** 0.25
HEADS = 8
MLA_NOPE, MLA_ROPE, MLA_V = 64, 32, 64
MLA_Q_RANK, MLA_KV_RANK = 768, 256
MOBA_HEAD_DIM, MOBA_ROT_DIM, MOBA_BLOCK, MOBA_TOPK = 64, 16, 256, 3
GDN_HEAD_DIM, GDN_CONV = 64, 4
BRANCH_WIDTH = 512
N_KEYS, PEER_TOPK, PEER_HALF = 128, 16, 128
N_EXPERTS = N_KEYS * N_KEYS

LANES = 128
SUBLANES = 8
VMEM_LIMIT = 56 * 1024 * 1024

GDN_KCHUNK = 128
GDN_HEADS_PER_STEP = 8
GDN_SCAN_PAIRS = 2
ATT_TQ = 256
PEER_TT = 512
PEER_EB = 1024
PEER_SCORE_TT = 256
PEER_SCORE_HEADS_PER_TRIP = 4

C_GATE, C_CQ, C_CKV, C_MQ, C_MK, C_MV, C_GQKV, C_GZ, C_TOTAL = (
    0, 3072, 3840, 4096, 5120, 6144, 6656, 8192, 8704)
PROJ_TN = 4352


def _cp(*sem):
    return pltpu.CompilerParams(dimension_semantics=sem, vmem_limit_bytes=VMEM_LIMIT)


def _dot(a, b):
    return jnp.dot(a.astype(BF16), b.astype(BF16), preferred_element_type=F32)


def _dot_nt(a, b):
    return lax.dot_general(a.astype(BF16), b.astype(BF16), (((1,), (1,)), ((), ())),
                           preferred_element_type=F32)


def _ln_rows(y, g, b):
    mu = jnp.mean(y, axis=-1, keepdims=True)
    d = y - mu
    var = jnp.mean(d * d, axis=-1, keepdims=True)
    return d * lax.rsqrt(var + LN_EPS) * g + b


def _sigmoid(x):
    return 1.0 / (1.0 + jnp.exp(-x))


def _ln_kernel(x_ref, g_ref, b_ref, o_ref):
    o_ref[...] = _ln_rows(x_ref[...], g_ref[...], b_ref[...])


def _layer_norm(x, g, b, tm=512):
    T, D = x.shape
    return pl.pallas_call(
        _ln_kernel, grid=(T // tm,),
        in_specs=[pl.BlockSpec((tm, D), lambda i: (i, 0)),
                  pl.BlockSpec((1, D), lambda i: (0, 0)),
                  pl.BlockSpec((1, D), lambda i: (0, 0))],
        out_specs=pl.BlockSpec((tm, D), lambda i: (i, 0)),
        out_shape=jax.ShapeDtypeStruct((T, D), F32),
        compiler_params=_cp("parallel"))(x, g.reshape(1, D), b.reshape(1, D))


def _mm_kernel(x_ref, w_ref, o_ref):
    o_ref[...] = _dot(x_ref[...], w_ref[...]).astype(o_ref.dtype)


def _mm(x, w, tm, tn, out_dtype=F32):
    M, K = x.shape
    N = w.shape[1]
    return pl.pallas_call(
        _mm_kernel, grid=(N // tn, M // tm),
        in_specs=[pl.BlockSpec((tm, K), lambda j, i: (i, 0)),
                  pl.BlockSpec((K, tn), lambda j, i: (0, j))],
        out_specs=pl.BlockSpec((tm, tn), lambda j, i: (i, j)),
        out_shape=jax.ShapeDtypeStruct((M, N), out_dtype),
        compiler_params=_cp("parallel", "parallel"))(x, w)


def _rms_rows(x, g):
    return x * lax.rsqrt(jnp.mean(x * x, axis=-1, keepdims=True) + RMS_EPS) * g


def _head_slabs(x):
    return [x[:, h * LANES:(h + 1) * LANES] for h in range(HEADS)]


def _mla_qkv_kernel(cq_ref, ckv_ref, gq_ref, gkv_ref, wq_ref, wkv_ref, sm_ref, cq_tab, sq_tab, ck_tab, sk_tab,
                    qo_ref, ko_ref, vo_ref):
    W = HEADS * LANES
    q = _dot(_rms_rows(cq_ref[...].astype(F32), gq_ref[...]), wq_ref[...])
    kv = _dot(_rms_rows(ckv_ref[...].astype(F32), gkv_ref[...]), wkv_ref[...])
    vo_ref[...] = kv[:, W:].astype(BF16)
    cq, sq = cq_tab[...], sq_tab[...]
    sm = sm_ref[...]
    kr = sm * ck_tab[...] + pltpu.roll(sm, LANES - MLA_ROPE, 1) * sk_tab[...]
    for h in range(HEADS):
        sl = slice(h * LANES, (h + 1) * LANES)
        qs = q[:, sl]
        qo_ref[:, sl] = (qs * cq + pltpu.roll(qs, LANES - MLA_ROPE, 1) * sq).astype(BF16)
        ko_ref[:, sl] = (kv[:, sl] + kr).astype(BF16)


def _mla_qkv(proj, small, gq, gkv, wq, wkv, tabs, tm=512):
    T = proj.shape[0]
    W = HEADS * LANES
    row = lambda i: (i, 0)
    tab = pl.BlockSpec((tm, LANES), row)
    full = lambda a: pl.BlockSpec(a.shape, lambda i: (0, 0))
    gq, gkv = gq.reshape(1, -1), gkv.reshape(1, -1)
    return pl.pallas_call(
        _mla_qkv_kernel, grid=(T // tm,),
        in_specs=[pl.BlockSpec((tm, MLA_Q_RANK), lambda i: (i, C_CQ // MLA_Q_RANK)),
                  pl.BlockSpec((tm, MLA_KV_RANK), lambda i: (i, C_CKV // MLA_KV_RANK)),
                  full(gq), full(gkv), full(wq), full(wkv), tab, tab, tab, tab, tab],
        out_specs=[pl.BlockSpec((tm, W), row), pl.BlockSpec((tm, W), row), pl.BlockSpec((tm, BRANCH_WIDTH), row)],
        out_shape=[jax.ShapeDtypeStruct((T, W), BF16)] * 2 + [jax.ShapeDtypeStruct((T, BRANCH_WIDTH), BF16)],
        compiler_params=_cp("parallel"))(proj, proj, gq, gkv, wq, wkv, small, *tabs)


def _moba_prep_kernel(q_ref, k_ref, cq_ref, sq_ref, ck_ref, sk_ref, qo_ref, ko_ref, km_ref):
    cq, sq, ck, sk = cq_ref[...], sq_ref[...], ck_ref[...], sk_ref[...]
    q, k = q_ref[...].astype(F32), k_ref[...].astype(F32)
    for h, (qs, ks) in enumerate(zip(_head_slabs(q), _head_slabs(k))):
        sl = slice(h * LANES, (h + 1) * LANES)
        qo_ref[:, sl] = (qs * cq + pltpu.roll(qs, LANES // 2, 1) * sq).astype(BF16)
        kk = ks * ck + pltpu.roll(ks, LANES // 2, 1) * sk
        ko_ref[:, sl] = kk.astype(BF16)
        km_ref[0, :, sl] = jnp.mean(kk, axis=0, keepdims=True)


def _moba_prep(proj, tabs):
    T = proj.shape[0]
    W = HEADS * LANES
    tm = MOBA_BLOCK
    row = lambda i: (i, 0)
    tab = pl.BlockSpec((tm, LANES), row)
    return pl.pallas_call(
        _moba_prep_kernel, grid=(T // tm,),
        in_specs=[pl.BlockSpec((tm, W), lambda i: (i, C_MQ // W)),
                  pl.BlockSpec((tm, W), lambda i: (i, C_MK // W)), tab, tab, tab, tab],
        out_specs=[pl.BlockSpec((tm, W), row), pl.BlockSpec((tm, W), row),
                   pl.BlockSpec((1, 1, W), lambda i: (i, 0, 0))],
        out_shape=[jax.ShapeDtypeStruct((T, W), BF16), jax.ShapeDtypeStruct((T, W), BF16),
                   jax.ShapeDtypeStruct((T // tm, 1, W), F32)],
        compiler_params=_cp("parallel"))(proj, proj, *tabs)


def _attn_kernel(*refs, moba, nk):
    if moba:
        q_ref, k_ref, v_ref, km_ref, o_ref, vt_sc, m_sc, l_sc, al_sc, acc_sc, s_sc, p_sc, bias_sc = refs
    else:
        q_ref, k_ref, v_ref, o_ref, vt_sc, m_sc, l_sc, al_sc, acc_sc, s_sc, p_sc = refs
    tq = tk = ATT_TQ
    dv = BRANCH_WIDTH // HEADS
    qi = pl.program_id(1)
    hslab = [slice(h * LANES, (h + 1) * LANES) for h in range(HEADS)]

    @pl.when(qi == 0)
    def _():
        for j in range(nk):
            for c in range(tk // LANES):
                for g in range(BRANCH_WIDTH // LANES):
                    vt_sc[j, g * LANES:(g + 1) * LANES, c * LANES:(c + 1) * LANES] = (
                        v_ref[j * tk + c * LANES:j * tk + (c + 1) * LANES,
                              g * LANES:(g + 1) * LANES].astype(F32).T.astype(BF16))

    if moba:
        blk = lax.broadcasted_iota(jnp.int32, (SUBLANES, tq), 0)
        for h in range(HEADS):
            qf = q_ref[:, hslab[h]].astype(F32)
            km = km_ref[0, :, hslab[h]]
            if nk < SUBLANES:
                km = jnp.concatenate([km, jnp.zeros((SUBLANES - nk, LANES), F32)], axis=0)
            gate = lax.dot_general(km, qf, (((1,), (1,)), ((), ())), precision=HIGHEST,
                                   preferred_element_type=F32)
            gate = jnp.where(blk < qi, gate, -jnp.inf)
            for n in range(nk):
                gn = gate[n:n + 1, :]
                beats = jnp.where(gate > gn, 1.0, jnp.where((gate == gn) & (blk < n), 1.0, 0.0))
                cnt = jnp.sum(beats, axis=0, keepdims=True)
                bias_sc[h, n] = jnp.where((cnt < MOBA_TOPK) & (n < qi), 0.0, NEG_INF)
    m_sc[...] = jnp.full(m_sc.shape, NEG_INF, F32)
    l_sc[...] = jnp.zeros(l_sc.shape, F32)
    acc_sc[...] = jnp.zeros(acc_sc.shape, F32)

    def block(j, diag):
        rows = pl.ds(pl.multiple_of(j * tk, tk), tk)
        for h in range(HEADS):
            s = _dot_nt(k_ref[rows, hslab[h]], q_ref[:, hslab[h]])
            if diag:
                keyi = lax.broadcasted_iota(jnp.int32, (tk, tq), 0)
                qryi = lax.broadcasted_iota(jnp.int32, (tk, tq), 1)
                s = jnp.where(keyi <= qryi, s, NEG_INF)
            elif moba:
                s = s + bias_sc[h, j]
            s_sc[h] = s
        for h in range(HEADS):
            s = s_sc[h]
            m_prev = m_sc[h]
            m_new = jnp.maximum(m_prev, jnp.max(s, axis=0, keepdims=True))
            alpha = jnp.exp2(m_prev - m_new)
            p = jnp.exp2(s - m_new)
            l_sc[h] = alpha * l_sc[h] + jnp.sum(p, axis=0, keepdims=True)
            p_sc[h] = p.astype(BF16)
            al_sc[h] = alpha
            m_sc[h] = m_new
        for h in range(HEADS):
            acc_sc[h] = al_sc[h] * acc_sc[h] + jnp.dot(vt_sc[j, h * dv:(h + 1) * dv, :], p_sc[h],
                                                       preferred_element_type=F32)

    def past(j, carry):
        block(j, False)
        return carry
    lax.fori_loop(0, qi, past, 0)
    block(qi, True)
    ot = jnp.concatenate([acc_sc[h] * (1.0 / l_sc[h]) for h in range(HEADS)], axis=0)
    o_ref[...] = ot.T


def _attention(q, k, v, v_col, B, S, kmean=None):
    T = B * S
    tq = ATT_TQ
    nq = S // tq
    W = HEADS * LANES
    dv = BRANCH_WIDTH // HEADS
    moba = kmean is not None
    assert nq <= SUBLANES and v_col % BRANCH_WIDTH == 0
    in_specs = [pl.BlockSpec((tq, W), lambda b, i: (b * nq + i, 0)),
                pl.BlockSpec((S, W), lambda b, i: (b, 0)),
                pl.BlockSpec((S, BRANCH_WIDTH), lambda b, i: (b, v_col // BRANCH_WIDTH))]
    scratch = [pltpu.VMEM((nq, BRANCH_WIDTH, tq), BF16)]
    row = pltpu.VMEM((HEADS, 1, tq), F32)
    scratch += [row, row, row, pltpu.VMEM((HEADS, dv, tq), F32),
                pltpu.VMEM((HEADS, tq, tq), F32), pltpu.VMEM((HEADS, tq, tq), BF16)]
    args = [q, k, v]
    if moba:
        in_specs.append(pl.BlockSpec((1, nq, W), lambda b, i: (b, 0, 0)))
        scratch.append(pltpu.VMEM((HEADS, nq, 1, tq), F32))
        args.append(kmean)
    return pl.pallas_call(
        functools.partial(_attn_kernel, moba=moba, nk=nq),
        grid=(B, nq), in_specs=in_specs,
        out_specs=pl.BlockSpec((tq, BRANCH_WIDTH), lambda b, i: (b * nq + i, 0)),
        out_shape=jax.ShapeDtypeStruct((T, BRANCH_WIDTH), F32),
        scratch_shapes=scratch,
        compiler_params=_cp("parallel", "arbitrary"))(*args)


def _gdn_conv_kernel(x_ref, w_ref, o_ref):
    c = pl.program_id(1)
    x = x_ref[...].astype(F32)
    w = w_ref[...]
    S = x.shape[0]
    row = lax.broadcasted_iota(jnp.int32, (S, LANES), 0)
    lane = lax.broadcasted_iota(jnp.int32, (S, LANES), 1)
    y = x * w[GDN_CONV - 1:GDN_CONV, :]
    for d in range(1, GDN_CONV):
        xs = jnp.where(row >= d, pltpu.roll(x, d, 0), 0.0)
        y = y + xs * w[GDN_CONV - 1 - d:GDN_CONV - d, :]
    y = y * _sigmoid(y)
    sq = y * y
    lo = lane < GDN_HEAD_DIM
    ss0 = jnp.sum(jnp.where(lo, sq, 0.0), axis=1, keepdims=True)
    ss1 = jnp.sum(jnp.where(lo, 0.0, sq), axis=1, keepdims=True)
    inv = lax.rsqrt(jnp.where(lo, ss0, ss1) + RMS_EPS)
    nqb = BRANCH_WIDTH // LANES
    scale = jnp.where(c < nqb, GDN_HEAD_DIM ** -0.5, 1.0)
    o_ref[...] = jnp.where(c < 2 * nqb, y * inv * scale, y).astype(BF16)


def _gdn_conv(proj, conv_w, B, S):
    T = B * S
    nb = 3 * BRANCH_WIDTH // LANES
    return pl.pallas_call(
        _gdn_conv_kernel, grid=(B, nb),
        in_specs=[pl.BlockSpec((S, LANES), lambda b, c: (b, C_GQKV // LANES + c)),
                  pl.BlockSpec((GDN_CONV, LANES), lambda b, c: (0, c))],
        out_specs=pl.BlockSpec((S, LANES), lambda b, c: (b, c)),
        out_shape=jax.ShapeDtypeStruct((T, 3 * BRANCH_WIDTH), BF16),
        compiler_params=_cp("parallel", "parallel"))(proj, conv_w)


def _gdn_chunk_kernel(q_ref, k_ref, v_ref, sm_ref, alog_ref, dtb_ref,
                      u_ref, w_ref, qg_ref, qk_ref, kdt_ref, dl_ref):
    C = GDN_KCHUNK
    nh = GDN_HEADS_PER_STEP
    h0 = pl.program_id(1) * nh
    lane = lax.broadcasted_iota(jnp.int32, (C, LANES), 1)
    row = lax.broadcasted_iota(jnp.int32, (C, LANES), 0)
    lo = lane < GDN_HEAD_DIM
    tril = row >= lane
    strict = row > lane
    eye = jnp.where(row == lane, 1.0, 0.0)
    ltri = tril.astype(BF16)

    def pick(ref, k):
        x = ref[:, (k // 2) * LANES:(k // 2 + 1) * LANES].astype(F32)
        return jnp.where(lo, pltpu.roll(x, LANES // 2, 1) if k % 2 else x, 0.0)

    sm = sm_ref[...]
    a = sm + dtb_ref[...]
    softplus = jnp.maximum(a, 0.0) + jnp.log1p(jnp.exp(-jnp.abs(a)))
    garr = -jnp.exp(alog_ref[...]) * softplus
    sig = _sigmoid(sm)
    kh, kb, gc, decay, A = [], [], [], [], []
    for k in range(nh):
        g_col = jnp.sum(jnp.where(lane == h0 + k, garr, 0.0), axis=1, keepdims=True)
        beta = jnp.sum(jnp.where(lane == HEADS + h0 + k, sig, 0.0), axis=1, keepdims=True)
        g1 = jnp.broadcast_to(g_col, (C, LANES))
        gsum = None
        for _ in range(3):
            gb = g1.astype(BF16)
            part = jnp.dot(ltri, gb, preferred_element_type=F32)
            gsum = part if gsum is None else gsum + part
            g1 = g1 - gb.astype(F32)
        gc.append(gsum)
        decay.append(jnp.where(tril, jnp.exp(jnp.where(tril, gsum - gsum.T, 0.0)), 0.0))
        kh.append(pick(k_ref, k))
        kb.append(kh[k] * beta)
        vb = pick(v_ref, k) * beta
        A.append(jnp.where(strict, _dot_nt(kb[k], kh[k]) * decay[k], 0.0))
        eg = jnp.exp(gsum)
        qh = pick(q_ref, k)
        qk_ref[k, 0] = jnp.where(tril, _dot_nt(qh, kh[k]) * decay[k], 0.0).astype(BF16)
        qg_ref[k, 0] = (qh * eg).astype(BF16)
        glast = gsum[C - 1:C, :]
        kdt_ref[k, 0] = (kh[k] * jnp.exp(glast - gsum)).T.astype(BF16)
        dl_ref[k, 0] = jnp.broadcast_to(jnp.exp(glast), (SUBLANES, LANES))
        u_ref[k, 0] = vb
        w_ref[k, 0] = (kb[k] * eg).astype(BF16)
    def joiner(level):
        same = (row >> (level + 1)) == (lane >> (level + 1))
        return same & (((row >> level) & 1) == 1) & (((lane >> level) & 1) == 0)

    P = [eye - jnp.where(joiner(0), A[k], 0.0) for k in range(nh)]
    for level in range(1, int(math.log2(C))):
        msk = joiner(level)
        T1 = [_dot(P[k], jnp.where(msk, A[k], 0.0)) for k in range(nh)]
        P = [P[k] - _dot(T1[k], P[k]) for k in range(nh)]
    for k in range(nh):
        u_ref[k, 0] = _dot(P[k], u_ref[k, 0])
        w_ref[k, 0] = _dot(P[k], w_ref[k, 0]).astype(BF16)


def _gdn_chunks(qkv, small, alog_row, dtb_row, B, S):
    C = GDN_KCHUNK
    n = S // C
    nh = GDN_HEADS_PER_STEP
    ng = HEADS // nh
    wq = nh // 2 * LANES
    nqb = BRANCH_WIDTH // wq
    big = lambda dt: jax.ShapeDtypeStruct((B * HEADS, n, C, LANES), dt)
    ospec = pl.BlockSpec((nh, 1, C, LANES), lambda b, g, c: (b * ng + g, c, 0, 0))
    par = pl.BlockSpec((1, LANES), lambda b, g, c: (0, 0))
    outs = pl.pallas_call(
        _gdn_chunk_kernel, grid=(B, ng, n),
        in_specs=[pl.BlockSpec((C, wq), lambda b, g, c: (b * n + c, g)),
                  pl.BlockSpec((C, wq), lambda b, g, c: (b * n + c, nqb + g)),
                  pl.BlockSpec((C, wq), lambda b, g, c: (b * n + c, 2 * nqb + g)),
                  pl.BlockSpec((C, LANES), lambda b, g, c: (b * n + c, 0)), par, par],
        out_specs=[ospec] * 5 + [pl.BlockSpec((nh, 1, SUBLANES, LANES), lambda b, g, c: (b * ng + g, c, 0, 0))],
        out_shape=[big(F32)] + [big(BF16)] * 4 + [jax.ShapeDtypeStruct((B * HEADS, n, SUBLANES, LANES), F32)],
        compiler_params=_cp("parallel", "parallel", "parallel"))(qkv, qkv, qkv, small, alog_row, dtb_row)
    return [o.reshape(-1, LANES) for o in outs]


def _gdn_scan_kernel(u_ref, w_ref, qg_ref, qk_ref, kdt_ref, dl_ref, z_ref, g_ref, o_ref, o_sc, *, n):
    C = GDN_KCHUNK
    S = n * C
    npair = GDN_SCAN_PAIRS

    def step(c, states):
        out = []
        for k, state in enumerate(states):
            r = pl.ds(pl.multiple_of(k * S + c * C, C), C)
            sb = state.astype(BF16)
            v_new = u_ref[r, :] - _dot(w_ref[r, :], sb)
            vb = v_new.astype(BF16)
            o_sc[r, :] = _dot(qg_ref[r, :], sb) + _dot(qk_ref[r, :], vb)
            dl = dl_ref[pl.ds(pl.multiple_of((k * n + c) * SUBLANES, SUBLANES), 1), :]
            out.append(state * dl + _dot(kdt_ref[r, :], vb))
        return tuple(out)

    lax.fori_loop(0, n, step, tuple(jnp.zeros((LANES, LANES), F32) for _ in range(2 * npair)))
    lane = lax.broadcasted_iota(jnp.int32, (S, LANES), 1)

    def nrm(o):
        ms = jnp.sum(o * o, axis=1, keepdims=True) * (1.0 / GDN_HEAD_DIM)
        return o * lax.rsqrt(ms + RMS_EPS)

    for p in range(npair):
        ps = slice(p * LANES, (p + 1) * LANES)
        nn = jnp.where(lane < GDN_HEAD_DIM, nrm(o_sc[2 * p * S:(2 * p + 1) * S, :]),
                       pltpu.roll(nrm(o_sc[(2 * p + 1) * S:(2 * p + 2) * S, :]), LANES // 2, 1))
        z = z_ref[:, ps].astype(F32)
        o_ref[:, ps] = nn * g_ref[...] * (z * _sigmoid(z))


def _gdn_scan(parts, proj, g_row, B, S):
    n = S // GDN_KCHUNK
    nps = GDN_SCAN_PAIRS
    steps = HEADS // 2 // nps
    wz = nps * LANES
    seq = pl.BlockSpec((2 * nps * S, LANES), lambda i: (i, 0))
    return pl.pallas_call(
        functools.partial(_gdn_scan_kernel, n=n), grid=(B * steps,),
        in_specs=[seq] * 5 + [pl.BlockSpec((2 * nps * n * SUBLANES, LANES), lambda i: (i, 0)),
                              pl.BlockSpec((S, wz), lambda i: (i // steps, C_GZ // wz + i % steps)),
                              pl.BlockSpec((1, LANES), lambda i: (0, 0))],
        out_specs=pl.BlockSpec((S, wz), lambda i: (i // steps, i % steps)),
        out_shape=jax.ShapeDtypeStruct((B * S, BRANCH_WIDTH), F32),
        scratch_shapes=[pltpu.VMEM((2 * nps * S, LANES), F32)],
        compiler_params=_cp("parallel"))(*parts, proj, g_row)


def _merge_out_kernel(g0_ref, g1_ref, g2_ref, gb_ref, a_ref, b_ref, c_ref, wb_ref, wo_ref, h_ref, lg_ref, lb_ref,
                      o_ref, ot_ref):
    acc = None
    for n, (gl, br) in enumerate(((g0_ref, a_ref), (g1_ref, b_ref), (g2_ref, c_ref))):
        y = _sigmoid(gl[...].astype(F32) + gb_ref[n:n + 1, :]) * _dot(br[...], wb_ref[n])
        acc = y if acc is None else acc + y
    y = DEEPNORM_ALPHA * h_ref[...] + _dot(acc, wo_ref[...])
    y = _ln_rows(y, lg_ref[...], lb_ref[...])
    o_ref[...] = y
    ot_ref[...] = y.T.astype(BF16)


def _merge_out_ln(proj, gate_bias, o_a, o_b, o_c, w_branch, w_out, h, g, b, tm=512):
    T, D = h.shape
    row = lambda i: (i, 0)
    gspec = lambda n: pl.BlockSpec((tm, D), lambda i: (i, n))
    bspec = pl.BlockSpec((tm, BRANCH_WIDTH), row)
    vec = pl.BlockSpec((1, D), lambda i: (0, 0))
    return pl.pallas_call(
        _merge_out_kernel, grid=(T // tm,),
        in_specs=[gspec(0), gspec(1), gspec(2), pl.BlockSpec((3, D), lambda i: (0, 0)), bspec, bspec, bspec,
                  pl.BlockSpec((3, BRANCH_WIDTH, D), lambda i: (0, 0, 0)), pl.BlockSpec((D, D), lambda i: (0, 0)),
                  pl.BlockSpec((tm, D), row), vec, vec],
        out_specs=[pl.BlockSpec((tm, D), row), pl.BlockSpec((D, tm), lambda i: (0, i))],
        out_shape=[jax.ShapeDtypeStruct((T, D), F32), jax.ShapeDtypeStruct((D, T), BF16)],
        compiler_params=_cp("parallel"))(proj, proj, proj, gate_bias, o_a, o_b, o_c, w_branch, w_out, h,
                                         g.reshape(1, D), b.reshape(1, D))


def _extract_top(curs, n, on_max):
    curs = list(curs)
    for k in range(n):
        for i, cur in enumerate(curs):
            m = jnp.max(cur, axis=0, keepdims=True)
            on_max(i, k, m)
            if k + 1 < n:
                curs[i] = jnp.where(cur >= m, -jnp.inf, cur)


def _sort_network(n):
    pairs, p = [], 1
    while p < n:
        k = p
        while k >= 1:
            for j in range(k % p, n - k, 2 * k):
                for i in range(min(k, n - j - k)):
                    if (i + j) // (2 * p) == (i + j + k) // (2 * p):
                        pairs.append((i + j, i + j + k))
            k //= 2
        p *= 2
    return pairs


def _top_of_sorted_lists(scores, n, on_max):
    nt = N_KEYS // SUBLANES
    lists = []
    for s in scores:
        rows = [s[t * SUBLANES:(t + 1) * SUBLANES, :] for t in range(nt)]
        for a, b in _sort_network(nt):
            rows[a], rows[b] = jnp.maximum(rows[a], rows[b]), jnp.minimum(rows[a], rows[b])
        lists.append(rows + [jnp.full(rows[0].shape, -jnp.inf, F32)])
    for k in range(n):
        for i, rows in enumerate(lists):
            m = jnp.max(rows[0], axis=0, keepdims=True)
            on_max(i, k, m)
            hit = rows[0] >= m
            for t in range(min(n - 1 - k, nt)):
                rows[t] = jnp.where(hit, rows[t + 1], rows[t])


def _peer_score_kernel(qt_ref, keys_ref, s2_ref, e2_ref, c_ref, e1_ref, top_sc, cand_sc):
    K = PEER_TOPK
    tt = qt_ref.shape[1]
    nh = PEER_SCORE_HEADS_PER_TRIP

    def heads(hp, carry):
        s = []
        for i in range(2 * nh):
            r = pl.multiple_of((hp * nh * 2 + i) * PEER_HALF, PEER_HALF)
            s.append(_dot(keys_ref[hp * nh * 2 + i], qt_ref[pl.ds(r, PEER_HALF), :]))

        def put(i, k, m):
            top_sc[i, k:k + 1, :] = m
        _top_of_sorted_lists(s, K + 1, put)
        r8 = lax.broadcasted_iota(jnp.int32, (SUBLANES, tt), 0)
        v1max, v2max = [], []
        for d in range(nh):
            v1a, v1x = top_sc[2 * d, 0:K, :], top_sc[2 * d, K:K + 1, :]
            v2a, v2x = top_sc[2 * d + 1, 0:K, :], top_sc[2 * d + 1, K:K + 1, :]
            v1max.append(v1a[0:1])
            v2max.append(v2a[0:1])
            cand_sc[d, 0:K, :] = v1a + v2a[0:1]
            for b in range(1, SUBLANES):
                cand_sc[d, K + SUBLANES * (b - 1):K + SUBLANES * b, :] = v1a[0:SUBLANES] + v2a[b:b + 1]
            base = K + SUBLANES * (SUBLANES - 1)
            cand_sc[d, base:base + SUBLANES, :] = v1a[0:1] + v2a[SUBLANES:K]
            cand_sc[d, base + SUBLANES:base + 2 * SUBLANES, :] = jnp.where(
                r8 == 0, v1x + v2a[0:1], jnp.where(r8 == 1, v1a[0:1] + v2x, -jnp.inf))
        st = [{"z": jnp.zeros((1, tt), F32)} for _ in range(nh)]

        def acc(d, k, m):
            if k == 0:
                st[d]["top"] = m
            if k < K:
                st[d]["z"] = st[d]["z"] + jnp.exp(m - st[d]["top"])
            if k == K - 1:
                st[d]["t16"] = m
            if k == K:
                st[d]["t17"] = m
        _extract_top([cand_sc[d] for d in range(nh)], K + 1, acc)
        for d in range(nh):
            tau = 0.5 * (st[d]["t16"] + st[d]["t17"])
            ro = pl.ds(pl.multiple_of((hp * nh + d) * N_KEYS, N_KEYS), N_KEYS)
            s2_ref[ro, :] = s[2 * d + 1]
            e2_ref[ro, :] = jnp.exp(s[2 * d + 1] - v2max[d]) / st[d]["z"]
            c_ref[ro, :] = tau - s[2 * d]
            e1_ref[ro, :] = jnp.exp(s[2 * d] - v1max[d])
        return carry

    lax.fori_loop(0, HEADS // nh, heads, 0)


def _peer_scores(qt, keys):
    T = qt.shape[1]
    tt = PEER_SCORE_TT
    R = HEADS * N_KEYS
    ncand = PEER_TOPK + SUBLANES * (SUBLANES + 1)
    ospec = pl.BlockSpec((R, tt), lambda i: (0, i))
    return pl.pallas_call(
        _peer_score_kernel, grid=(T // tt,),
        in_specs=[pl.BlockSpec((2 * R, tt), lambda i: (0, i)),
                  pl.BlockSpec((2 * HEADS, N_KEYS, PEER_HALF), lambda i: (0, 0, 0))],
        out_specs=[ospec] * 4,
        out_shape=[jax.ShapeDtypeStruct((R, T), F32)] * 4,
        scratch_shapes=[pltpu.VMEM((2 * PEER_SCORE_HEADS_PER_TRIP, 3 * SUBLANES, tt), F32),
                        pltpu.VMEM((PEER_SCORE_HEADS_PER_TRIP, ncand, tt), F32)],
        compiler_params=_cp("parallel"))(qt, keys)


def _peer_expert_kernel(ht_ref, u_ref, vtp_ref, vtl_ref, s2_ref, e2_ref, c_ref, e1_ref, h_ref, g_ref, b_ref,
                        o_ref, acc_sc, act_sc, hw_sc, bc_sc):
    j = pl.program_id(1)
    _, eb, tt = hw_sc.shape
    ng = eb // N_KEYS
    assert c_ref.shape == (HEADS, ng, tt)
    slot = j % 2

    @pl.when(j == 0)
    def _():
        acc_sc[...] = jnp.zeros(acc_sc.shape, F32)
        hw_sc[1] = jnp.zeros(hw_sc.shape[1:], BF16)

    mxu_w = 2 * LANES
    for half in range(tt // mxu_w):
        hl = slice(half * mxu_w, (half + 1) * mxu_w)
        a = jnp.dot(u_ref[...], ht_ref[:, hl], preferred_element_type=F32)
        act_sc[:, hl] = 0.5 * a * (1.0 + lax.erf(a * (2.0 ** -0.5)))
        acc_sc[:, hl] += jnp.dot(vtp_ref[...], hw_sc[1 - slot, :, hl], preferred_element_type=F32)
        for lc in range(half * (mxu_w // LANES), (half + 1) * (mxu_w // LANES)):
            ls = slice(lc * LANES, (lc + 1) * LANES)
            for h in range(HEADS):
                thr, e1 = c_ref[h, :, ls], e1_ref[h, :, ls]
                for g in range(ng):
                    bc_sc[0, h * ng + g] = jnp.broadcast_to(thr[g:g + 1], (SUBLANES, LANES))
                    bc_sc[1, h * ng + g] = jnp.broadcast_to(e1[g:g + 1], (SUBLANES, LANES))
            for g in range(ng):
                wsum = jnp.zeros((N_KEYS // SUBLANES, SUBLANES, LANES), F32)
                for h in range(HEADS):
                    hr = slice(h * N_KEYS, (h + 1) * N_KEYS)
                    s2 = s2_ref[hr, ls].reshape(N_KEYS // SUBLANES, SUBLANES, LANES)
                    e2 = e2_ref[hr, ls].reshape(N_KEYS // SUBLANES, SUBLANES, LANES)
                    wsum = wsum + jnp.where(s2 >= bc_sc[0, h * ng + g][None], e2 * bc_sc[1, h * ng + g][None], 0.0)
                gs = slice(g * N_KEYS, (g + 1) * N_KEYS)
                hw_sc[slot, gs, ls] = (wsum.reshape(N_KEYS, LANES) * act_sc[gs, ls]).astype(BF16)

    @pl.when(j == pl.num_programs(1) - 1)
    def _():
        acc = acc_sc[...] + jnp.dot(vtl_ref[...], hw_sc[slot], preferred_element_type=F32)
        y = DEEPNORM_ALPHA * h_ref[...] + acc.T
        o_ref[...] = _ln_rows(y, g_ref[...], b_ref[...])


def _peer_experts(ht, u, vt, s2, e2, c, e1, h, g, b):
    T, D = h.shape
    tt, eb = PEER_TT, PEER_EB
    R = HEADS * N_KEYS
    nb = N_EXPERTS // eb
    assert nb % 2 == 0
    ng = eb // N_KEYS
    sspec = pl.BlockSpec((R, tt), lambda i, j: (0, i))
    gspec = pl.BlockSpec((HEADS, ng, tt), lambda i, j: (0, j, i))
    vec = pl.BlockSpec((1, D), lambda i, j: (0, 0))
    return pl.pallas_call(
        _peer_expert_kernel, grid=(T // tt, nb),
        in_specs=[pl.BlockSpec((D, tt), lambda i, j: (0, i)),
                  pl.BlockSpec((eb, D), lambda i, j: (j, 0)),
                  pl.BlockSpec((D, eb), lambda i, j: (0, jnp.maximum(j - 1, 0))),
                  pl.BlockSpec((D, eb), lambda i, j: (0, nb - 1)),
                  sspec, sspec, gspec, gspec,
                  pl.BlockSpec((tt, D), lambda i, j: (i, 0)), vec, vec],
        out_specs=pl.BlockSpec((tt, D), lambda i, j: (i, 0)),
        out_shape=jax.ShapeDtypeStruct((T, D), F32),
        scratch_shapes=[pltpu.VMEM((D, tt), F32), pltpu.VMEM((eb, tt), F32), pltpu.VMEM((2, eb, tt), BF16),
                        pltpu.VMEM((2, HEADS * ng, SUBLANES, LANES), F32)],
        compiler_params=_cp("parallel", "arbitrary"))(ht, u, vt, vt, s2, e2, c.reshape(HEADS, N_KEYS, T),
                                                      e1.reshape(HEADS, N_KEYS, T), h, g.reshape(1, D),
                                                      b.reshape(1, D))


def _rot_half(x, r):
    return jnp.concatenate([-x[..., r:2 * r], x[..., :r]], axis=-1)


def _prep_w_in(w):
    D = w.shape[0]
    splits = (MLA_Q_RANK, MLA_KV_RANK, MLA_ROPE, 3 * BRANCH_WIDTH, 3 * BRANCH_WIDTH, BRANCH_WIDTH, HEADS, HEADS,
              3 * D)
    o = np.cumsum((0,) + splits)
    cq, ckv, kr, mqkv, gqkv, gz, ga, gb, gl = [w[:, o[i]:o[i + 1]] for i in range(9)]
    mq, mk, mv = jnp.split(mqkv, 3, axis=1)

    def slab(m):
        m = m.reshape(D, HEADS, MOBA_HEAD_DIM)
        pad = jnp.zeros((D, HEADS, LANES - MOBA_HEAD_DIM - MOBA_ROT_DIM), w.dtype)
        return jnp.concatenate([m, _rot_half(m, MOBA_ROT_DIM // 2), pad], axis=-1).reshape(D, HEADS * LANES)

    small = jnp.concatenate([ga, gb, jnp.zeros((D, LANES // 2 - 2 * HEADS), w.dtype), kr,
                             _rot_half(kr, MLA_ROPE // 2)], axis=1)
    out = jnp.concatenate([gl, cq, ckv, slab(mq), slab(mk), mv, gqkv, gz], axis=1)
    assert out.shape[1] == C_TOTAL and C_TOTAL % PROJ_TN == 0
    return out.astype(BF16), small.astype(BF16)


def _prep_w_uq(w):
    R = w.shape[0]
    w = w.reshape(R, HEADS, MLA_NOPE + MLA_ROPE)
    rope = w[..., MLA_NOPE:]
    return jnp.concatenate([w, _rot_half(rope, MLA_ROPE // 2)], axis=-1).reshape(R, HEADS * LANES).astype(BF16)


def _prep_w_ukv(w):
    R = w.shape[0]
    w = w.reshape(R, HEADS, MLA_NOPE + MLA_V)
    k = jnp.concatenate([w[..., :MLA_NOPE], jnp.zeros((R, HEADS, LANES - MLA_NOPE), w.dtype)], axis=-1)
    return jnp.concatenate([k.reshape(R, HEADS * LANES), w[..., MLA_NOPE:].reshape(R, HEADS * MLA_V)],
                           axis=1).astype(BF16)


def _rope_tables(positions):
    pos = positions.reshape(-1).astype(F32)[:, None]
    T = pos.shape[0]

    def cs(rot):
        inv = ROPE_THETA ** (-jnp.arange(0, rot, 2, dtype=F32) / rot)
        ang = pos * inv
        return jnp.cos(ang), jnp.sin(ang)

    ca, sa = cs(MLA_ROPE)
    cb, sb = cs(MOBA_ROT_DIM)
    one = lambda n: jnp.ones((T, n), F32)
    zero = lambda n: jnp.zeros((T, n), F32)
    sc_a = (MLA_NOPE + MLA_ROPE) ** -0.5 * math.log2(math.e)
    sc_b = MOBA_HEAD_DIM ** -0.5 * math.log2(math.e)
    cat = lambda *xs: jnp.concatenate(xs, axis=1)
    mla = (cat(one(MLA_NOPE), ca, ca, zero(32)) * sc_a, cat(zero(MLA_NOPE), sa, sa, zero(32)) * sc_a,
           cat(zero(MLA_NOPE), ca, ca, zero(32)), cat(zero(MLA_NOPE), sa, sa, zero(32)))
    cm = cat(cb, cb, one(MOBA_HEAD_DIM - MOBA_ROT_DIM), zero(LANES - MOBA_HEAD_DIM))
    sm = cat(sb, sb, zero(LANES - MOBA_ROT_DIM))
    moba = (cm * sc_b, sm * sc_b, cm, sm)
    return mla, moba


def _lane_row(v):
    return jnp.concatenate([v.astype(F32), jnp.zeros((LANES - v.shape[0],), F32)]).reshape(1, LANES)


def kernel(x, positions, ln_in_g, ln_in_b, w_in, mla_q_norm, mla_kv_norm, mla_w_uq, mla_w_ukv, gdn_conv_w, gdn_A_log, gdn_dt_bias, gdn_o_norm, gate_bias, w_branch, w_out, ln1_g, ln1_b, peer_w_q, peer_sub_keys, peer_u, peer_v, ln2_g, ln2_b):
    B, S, D = x.shape
    T = B * S
    assert S % ATT_TQ == 0 and S % GDN_KCHUNK == 0 and T % PEER_TT == 0 and ATT_TQ == MOBA_BLOCK
    mla_tabs, moba_tabs = _rope_tables(positions)
    h = _layer_norm(x.reshape(T, D), ln_in_g, ln_in_b)
    for l in range(DEPTH):
        w_main, w_small = _prep_w_in(w_in[l])
        proj = _mm(h, w_main, tm=512, tn=PROJ_TN, out_dtype=BF16)
        small = _mm(h, w_small, tm=512, tn=LANES)
        qa, ka, va = _mla_qkv(proj, small, mla_q_norm[l], mla_kv_norm[l], _prep_w_uq(mla_w_uq[l]),
                              _prep_w_ukv(mla_w_ukv[l]), mla_tabs)
        o_a = _attention(qa, ka, va, 0, B, S)
        qm, km, kmean = _moba_prep(proj, moba_tabs)
        o_b = _attention(qm, km, proj, C_MV, B, S, kmean=kmean.reshape(B, S // MOBA_BLOCK, HEADS * LANES))
        qkv = _gdn_conv(proj, gdn_conv_w[l], B, S)
        parts = _gdn_chunks(qkv, small, _lane_row(gdn_A_log[l]), _lane_row(gdn_dt_bias[l]), B, S)
        o_c = _gdn_scan(parts, proj, jnp.tile(gdn_o_norm[l], 2).reshape(1, LANES), B, S)
        h, ht = _merge_out_ln(proj, gate_bias[l], o_a, o_b, o_c, w_branch[l].astype(BF16), w_out[l].astype(BF16),
                              h, ln1_g[l], ln1_b[l])
        qt = _mm(peer_w_q[l].T.astype(BF16), ht, tm=1024, tn=min(1024, T), out_dtype=BF16)
        keys = peer_sub_keys[l].reshape(2 * HEADS, N_KEYS, PEER_HALF).astype(BF16)
        s2, e2, c, e1 = _peer_scores(qt, keys)
        h = _peer_experts(ht, peer_u[l].astype(BF16), peer_v[l].astype(BF16).T, s2, e2, c, e1, h, ln2_g[l], ln2_b[l])
    return h.reshape(B, S, D)
```

```python
import functools
import math

import numpy as np
import jax
import jax.numpy as jnp
from jax import lax
from jax.experimental import pallas as pl
from jax.experimental.pallas import tpu as pltpu

F32 = jnp.float32
BF16 = jnp.bfloat16
HIGHEST = lax.Precision.HIGHEST

DEPTH = 2
ROPE_THETA = 500000.0
NEG_INF = -1e30
LN_EPS = 1e-5
RMS_EPS = 1e-6
DEEPNORM_ALPHA = (2 * DEPTH) ** 0.25
HEADS = 8
MLA_NOPE, MLA_ROPE, MLA_V = 64, 32, 64
MLA_Q_RANK, MLA_KV_RANK = 768, 256
MOBA_HEAD_DIM, MOBA_ROT_DIM, MOBA_BLOCK, MOBA_TOPK = 64, 16, 256, 3
GDN_HEAD_DIM, GDN_CONV = 64, 4
BRANCH_WIDTH = 512
N_KEYS, PEER_TOPK, PEER_HALF = 128, 16, 128
N_EXPERTS = N_KEYS * N_KEYS

LANES = 128
SUBLANES = 8
VMEM_LIMIT = 56 * 1024 * 1024

GDN_KCHUNK = 128
GDN_HEADS_PER_STEP = 8
GDN_SCAN_PAIRS = 2
ATT_TQ = 256
PEER_TT = 512
PEER_EB = 1024
PEER_SCORE_TT = 256
PEER_SCORE_HEADS_PER_TRIP = 4

C_GATE, C_CQ, C_CKV, C_MQ, C_MK, C_MV, C_GQKV, C_GZ, C_TOTAL = (
    0, 3072, 3840, 4096, 5120, 6144, 6656, 8192, 8704)
PROJ_TN = 4352


def _cp(*sem):
    return pltpu.CompilerParams(dimension_semantics=sem, vmem_limit_bytes=VMEM_LIMIT)


def _dot(a, b):
    return jnp.dot(a.astype(BF16), b.astype(BF16), preferred_element_type=F32)


def _dot_nt(a, b):
    return lax.dot_general(a.astype(BF16), b.astype(BF16), (((1,), (1,)), ((), ())),
                           preferred_element_type=F32)


def _ln_rows(y, g, b):
    mu = jnp.mean(y, axis=-1, keepdims=True)
    d = y - mu
    var = jnp.mean(d * d, axis=-1, keepdims=True)
    return d * lax.rsqrt(var + LN_EPS) * g + b


def _sigmoid(x):
    return 1.0 / (1.0 + jnp.exp(-x))


def _ln_kernel(x_ref, g_ref, b_ref, o_ref):
    o_ref[...] = _ln_rows(x_ref[...], g_ref[...], b_ref[...])


def _layer_norm(x, g, b, tm=512):
    T, D = x.shape
    return pl.pallas_call(
        _ln_kernel, grid=(T // tm,),
        in_specs=[pl.BlockSpec((tm, D), lambda i: (i, 0)),
                  pl.BlockSpec((1, D), lambda i: (0, 0)),
                  pl.BlockSpec((1, D), lambda i: (0, 0))],
        out_specs=pl.BlockSpec((tm, D), lambda i: (i, 0)),
        out_shape=jax.ShapeDtypeStruct((T, D), F32),
        compiler_params=_cp("parallel"))(x, g.reshape(1, D), b.reshape(1, D))


def _mm_kernel(x_ref, w_ref, o_ref):
    o_ref[...] = _dot(x_ref[...], w_ref[...]).astype(o_ref.dtype)


def _mm(x, w, tm, tn, out_dtype=F32):
    M, K = x.shape
    N = w.shape[1]
    return pl.pallas_call(
        _mm_kernel, grid=(N // tn, M // tm),
        in_specs=[pl.BlockSpec((tm, K), lambda j, i: (i, 0)),
                  pl.BlockSpec((K, tn), lambda j, i: (0, j))],
        out_specs=pl.BlockSpec((tm, tn), lambda j, i: (i, j)),
        out_shape=jax.ShapeDtypeStruct((M, N), out_dtype),
        compiler_params=_cp("parallel", "parallel"))(x, w)


def _rms_rows(x, g):
    return x * lax.rsqrt(jnp.mean(x * x, axis=-1, keepdims=True) + RMS_EPS) * g


def _head_slabs(x):
    return [x[:, h * LANES:(h + 1) * LANES] for h in range(HEADS)]


def _mla_qkv_kernel(cq_ref, ckv_ref, gq_ref, gkv_ref, wq_ref, wkv_ref, sm_ref, cq_tab, sq_tab, ck_tab, sk_tab,
                    qo_ref, ko_ref, vo_ref):
    W = HEADS * LANES
    q = _dot(_rms_rows(cq_ref[...].astype(F32), gq_ref[...]), wq_ref[...])
    kv = _dot(_rms_rows(ckv_ref[...].astype(F32), gkv_ref[...]), wkv_ref[...])
    vo_ref[...] = kv[:, W:].astype(BF16)
    cq, sq = cq_tab[...], sq_tab[...]
    sm = sm_ref[...]
    kr = sm * ck_tab[...] + pltpu.roll(sm, LANES - MLA_ROPE, 1) * sk_tab[...]
    for h in range(HEADS):
        sl = slice(h * LANES, (h + 1) * LANES)
        qs = q[:, sl]
        qo_ref[:, sl] = (qs * cq + pltpu.roll(qs, LANES - MLA_ROPE, 1) * sq).astype(BF16)
        ko_ref[:, sl] = (kv[:, sl] + kr).astype(BF16)


def _mla_qkv(proj, small, gq, gkv, wq, wkv, tabs, tm=512):
    T = proj.shape[0]
    W = HEADS * LANES
    row = lambda i: (i, 0)
    tab = pl.BlockSpec((tm, LANES), row)
    full = lambda a: pl.BlockSpec(a.shape, lambda i: (0, 0))
    gq, gkv = gq.reshape(1, -1), gkv.reshape(1, -1)
    return pl.pallas_call(
        _mla_qkv_kernel, grid=(T // tm,),
        in_specs=[pl.BlockSpec((tm, MLA_Q_RANK), lambda i: (i, C_CQ // MLA_Q_RANK)),
                  pl.BlockSpec((tm, MLA_KV_RANK), lambda i: (i, C_CKV // MLA_KV_RANK)),
                  full(gq), full(gkv), full(wq), full(wkv), tab, tab, tab, tab, tab],
        out_specs=[pl.BlockSpec((tm, W), row), pl.BlockSpec((tm, W), row), pl.BlockSpec((tm, BRANCH_WIDTH), row)],
        out_shape=[jax.ShapeDtypeStruct((T, W), BF16)] * 2 + [jax.ShapeDtypeStruct((T, BRANCH_WIDTH), BF16)],
        compiler_params=_cp("parallel"))(proj, proj, gq, gkv, wq, wkv, small, *tabs)


def _moba_prep_kernel(q_ref, k_ref, cq_ref, sq_ref, ck_ref, sk_ref, qo_ref, ko_ref, km_ref):
    cq, sq, ck, sk = cq_ref[...], sq_ref[...], ck_ref[...], sk_ref[...]
    q, k = q_ref[...].astype(F32), k_ref[...].astype(F32)
    for h, (qs, ks) in enumerate(zip(_head_slabs(q), _head_slabs(k))):
        sl = slice(h * LANES, (h + 1) * LANES)
        qo_ref[:, sl] = (qs * cq + pltpu.roll(qs, LANES // 2, 1) * sq).astype(BF16)
        kk = ks * ck + pltpu.roll(ks, LANES // 2, 1) * sk
        ko_ref[:, sl] = kk.astype(BF16)
        km_ref[0, :, sl] = jnp.mean(kk, axis=0, keepdims=True)


def _moba_prep(proj, tabs):
    T = proj.shape[0]
    W = HEADS * LANES
    tm = MOBA_BLOCK
    row = lambda i: (i, 0)
    tab = pl.BlockSpec((tm, LANES), row)
    return pl.pallas_call(
        _moba_prep_kernel, grid=(T // tm,),
        in_specs=[pl.BlockSpec((tm, W), lambda i: (i, C_MQ // W)),
                  pl.BlockSpec((tm, W), lambda i: (i, C_MK // W)), tab, tab, tab, tab],
        out_specs=[pl.BlockSpec((tm, W), row), pl.BlockSpec((tm, W), row),
                   pl.BlockSpec((1, 1, W), lambda i: (i, 0, 0))],
        out_shape=[jax.ShapeDtypeStruct((T, W), BF16), jax.ShapeDtypeStruct((T, W), BF16),
                   jax.ShapeDtypeStruct((T // tm, 1, W), F32)],
        compiler_params=_cp("parallel"))(proj, proj, *tabs)


def _attn_kernel(*refs, moba, nk):
    if moba:
        q_ref, k_ref, v_ref, km_ref, o_ref, vt_sc, m_sc, l_sc, al_sc, acc_sc, s_sc, p_sc, bias_sc = refs
    else:
        q_ref, k_ref, v_ref, o_ref, vt_sc, m_sc, l_sc, al_sc, acc_sc, s_sc, p_sc = refs
    tq = tk = ATT_TQ
    dv = BRANCH_WIDTH // HEADS
    qi = pl.program_id(1)
    hslab = [slice(h * LANES, (h + 1) * LANES) for h in range(HEADS)]

    @pl.when(qi == 0)
    def _():
        for j in range(nk):
            for c in range(tk // LANES):
                for g in range(BRANCH_WIDTH // LANES):
                    vt_sc[j, g * LANES:(g + 1) * LANES, c * LANES:(c + 1) * LANES] = (
                        v_ref[j * tk + c * LANES:j * tk + (c + 1) * LANES,
                              g * LANES:(g + 1) * LANES].astype(F32).T.astype(BF16))

    if moba:
        blk = lax.broadcasted_iota(jnp.int32, (SUBLANES, tq), 0)
        for h in range(HEADS):
            qf = q_ref[:, hslab[h]].astype(F32)
            km = km_ref[0, :, hslab[h]]
            if nk < SUBLANES:
                km = jnp.concatenate([km, jnp.zeros((SUBLANES - nk, LANES), F32)], axis=0)
            gate = lax.dot_general(km, qf, (((1,), (1,)), ((), ())), precision=HIGHEST,
                                   preferred_element_type=F32)
            gate = jnp.where(blk < qi, gate, -jnp.inf)
            for n in range(nk):
                gn = gate[n:n + 1, :]
                beats = jnp.where(gate > gn, 1.0, jnp.where((gate == gn) & (blk < n), 1.0, 0.0))
                cnt = jnp.sum(beats, axis=0, keepdims=True)
                bias_sc[h, n] = jnp.where((cnt < MOBA_TOPK) & (n < qi), 0.0, NEG_INF)
    m_sc[...] = jnp.full(m_sc.shape, NEG_INF, F32)
    l_sc[...] = jnp.zeros(l_sc.shape, F32)
    acc_sc[...] = jnp.zeros(acc_sc.shape, F32)

    def block(j, diag):
        rows = pl.ds(pl.multiple_of(j * tk, tk), tk)
        for h in range(HEADS):
            s = _dot_nt(k_ref[rows, hslab[h]], q_ref[:, hslab[h]])
            if diag:
                keyi = lax.broadcasted_iota(jnp.int32, (tk, tq), 0)
                qryi = lax.broadcasted_iota(jnp.int32, (tk, tq), 1)
                s = jnp.where(keyi <= qryi, s, NEG_INF)
            elif moba:
                s = s + bias_sc[h, j]
            s_sc[h] = s
        for h in range(HEADS):
            s = s_sc[h]
            m_prev = m_sc[h]
            m_new = jnp.maximum(m_prev, jnp.max(s, axis=0, keepdims=True))
            alpha = jnp.exp2(m_prev - m_new)
            p = jnp.exp2(s - m_new)
            l_sc[h] = alpha * l_sc[h] + jnp.sum(p, axis=0, keepdims=True)
            p_sc[h] = p.astype(BF16)
            al_sc[h] = alpha
            m_sc[h] = m_new
        for h in range(HEADS):
            acc_sc[h] = al_sc[h] * acc_sc[h] + jnp.dot(vt_sc[j, h * dv:(h + 1) * dv, :], p_sc[h],
                                                       preferred_element_type=F32)

    def past(j, carry):
        block(j, False)
        return carry
    lax.fori_loop(0, qi, past, 0)
    block(qi, True)
    ot = jnp.concatenate([acc_sc[h] * (1.0 / l_sc[h]) for h in range(HEADS)], axis=0)
    o_ref[...] = ot.T


def _attention(q, k, v, v_col, B, S, kmean=None):
    T = B * S
    tq = ATT_TQ
    nq = S // tq
    W = HEADS * LANES
    dv = BRANCH_WIDTH // HEADS
    moba = kmean is not None
    assert nq <= SUBLANES and v_col % BRANCH_WIDTH == 0
    in_specs = [pl.BlockSpec((tq, W), lambda b, i: (b * nq + i, 0)),
                pl.BlockSpec((S, W), lambda b, i: (b, 0)),
                pl.BlockSpec((S, BRANCH_WIDTH), lambda b, i: (b, v_col // BRANCH_WIDTH))]
    scratch = [pltpu.VMEM((nq, BRANCH_WIDTH, tq), BF16)]
    row = pltpu.VMEM((HEADS, 1, tq), F32)
    scratch += [row, row, row, pltpu.VMEM((HEADS, dv, tq), F32),
                pltpu.VMEM((HEADS, tq, tq), F32), pltpu.VMEM((HEADS, tq, tq), BF16)]
    args = [q, k, v]
    if moba:
        in_specs.append(pl.BlockSpec((1, nq, W), lambda b, i: (b, 0, 0)))
        scratch.append(pltpu.VMEM((HEADS, nq, 1, tq), F32))
        args.append(kmean)
    return pl.pallas_call(
        functools.partial(_attn_kernel, moba=moba, nk=nq),
        grid=(B, nq), in_specs=in_specs,
        out_specs=pl.BlockSpec((tq, BRANCH_WIDTH), lambda b, i: (b * nq + i, 0)),
        out_shape=jax.ShapeDtypeStruct((T, BRANCH_WIDTH), F32),
        scratch_shapes=scratch,
        compiler_params=_cp("parallel", "arbitrary"))(*args)


def _gdn_conv_kernel(x_ref, w_ref, o_ref):
    c = pl.program_id(1)
    x = x_ref[...].astype(F32)
    w = w_ref[...]
    S = x.shape[0]
    row = lax.broadcasted_iota(jnp.int32, (S, LANES), 0)
    lane = lax.broadcasted_iota(jnp.int32, (S, LANES), 1)
    y = x * w[GDN_CONV - 1:GDN_CONV, :]
    for d in range(1, GDN_CONV):
        xs = jnp.where(row >= d, pltpu.roll(x, d, 0), 0.0)
        y = y + xs * w[GDN_CONV - 1 - d:GDN_CONV - d, :]
    y = y * _sigmoid(y)
    sq = y * y
    lo = lane < GDN_HEAD_DIM
    ss0 = jnp.sum(jnp.where(lo, sq, 0.0), axis=1, keepdims=True)
    ss1 = jnp.sum(jnp.where(lo, 0.0, sq), axis=1, keepdims=True)
    inv = lax.rsqrt(jnp.where(lo, ss0, ss1) + RMS_EPS)
    nqb = BRANCH_WIDTH // LANES
    scale = jnp.where(c < nqb, GDN_HEAD_DIM ** -0.5, 1.0)
    o_ref[...] = jnp.where(c < 2 * nqb, y * inv * scale, y).astype(BF16)


def _gdn_conv(proj, conv_w, B, S):
    T = B * S
    nb = 3 * BRANCH_WIDTH // LANES
    return pl.pallas_call(
        _gdn_conv_kernel, grid=(B, nb),
        in_specs=[pl.BlockSpec((S, LANES), lambda b, c: (b, C_GQKV // LANES + c)),
                  pl.BlockSpec((GDN_CONV, LANES), lambda b, c: (0, c))],
        out_specs=pl.BlockSpec((S, LANES), lambda b, c: (b, c)),
        out_shape=jax.ShapeDtypeStruct((T, 3 * BRANCH_WIDTH), BF16),
        compiler_params=_cp("parallel", "parallel"))(proj, conv_w)


def _gdn_chunk_kernel(q_ref, k_ref, v_ref, sm_ref, alog_ref, dtb_ref,
                      u_ref, w_ref, qg_ref, qk_ref, kdt_ref, dl_ref):
    C = GDN_KCHUNK
    nh = GDN_HEADS_PER_STEP
    h0 = pl.program_id(1) * nh
    lane = lax.broadcasted_iota(jnp.int32, (C, LANES), 1)
    row = lax.broadcasted_iota(jnp.int32, (C, LANES), 0)
    lo = lane < GDN_HEAD_DIM
    tril = row >= lane
    strict = row > lane
    eye = jnp.where(row == lane, 1.0, 0.0)
    ltri = tril.astype(BF16)

    def pick(ref, k):
        x = ref[:, (k // 2) * LANES:(k // 2 + 1) * LANES].astype(F32)
        return jnp.where(lo, pltpu.roll(x, LANES // 2, 1) if k % 2 else x, 0.0)

    sm = sm_ref[...]
    a = sm + dtb_ref[...]
    softplus = jnp.maximum(a, 0.0) + jnp.log1p(jnp.exp(-jnp.abs(a)))
    garr = -jnp.exp(alog_ref[...]) * softplus
    sig = _sigmoid(sm)
    kh, kb, gc, decay, A = [], [], [], [], []
    for k in range(nh):
        g_col = jnp.sum(jnp.where(lane == h0 + k, garr, 0.0), axis=1, keepdims=True)
        beta = jnp.sum(jnp.where(lane == HEADS + h0 + k, sig, 0.0), axis=1, keepdims=True)
        g1 = jnp.broadcast_to(g_col, (C, LANES))
        gsum = None
        for _ in range(3):
            gb = g1.astype(BF16)
            part = jnp.dot(ltri, gb, preferred_element_type=F32)
            gsum = part if gsum is None else gsum + part
            g1 = g1 - gb.astype(F32)
        gc.append(gsum)
        decay.append(jnp.where(tril, jnp.exp(jnp.where(tril, gsum - gsum.T, 0.0)), 0.0))
        kh.append(pick(k_ref, k))
        kb.append(kh[k] * beta)
        vb = pick(v_ref, k) * beta
        A.append(jnp.where(strict, _dot_nt(kb[k], kh[k]) * decay[k], 0.0))
        eg = jnp.exp(gsum)
        qh = pick(q_ref, k)
        qk_ref[k, 0] = jnp.where(tril, _dot_nt(qh, kh[k]) * decay[k], 0.0).astype(BF16)
        qg_ref[k, 0] = (qh * eg).astype(BF16)
        glast = gsum[C - 1:C, :]
        kdt_ref[k, 0] = (kh[k] * jnp.exp(glast - gsum)).T.astype(BF16)
        dl_ref[k, 0] = jnp.broadcast_to(jnp.exp(glast), (SUBLANES, LANES))
        u_ref[k, 0] = vb
        w_ref[k, 0] = (kb[k] * eg).astype(BF16)
    def joiner(level):
        same = (row >> (level + 1)) == (lane >> (level + 1))
        return same & (((row >> level) & 1) == 1) & (((lane >> level) & 1) == 0)

    P = [eye - jnp.where(joiner(0), A[k], 0.0) for k in range(nh)]
    for level in range(1, int(math.log2(C))):
        msk = joiner(level)
        T1 = [_dot(P[k], jnp.where(msk, A[k], 0.0)) for k in range(nh)]
        P = [P[k] - _dot(T1[k], P[k]) for k in range(nh)]
    for k in range(nh):
        u_ref[k, 0] = _dot(P[k], u_ref[k, 0])
        w_ref[k, 0] = _dot(P[k], w_ref[k, 0]).astype(BF16)


def _gdn_chunks(qkv, small, alog_row, dtb_row, B, S):
    C = GDN_KCHUNK
    n = S // C
    nh = GDN_HEADS_PER_STEP
    ng = HEADS // nh
    wq = nh // 2 * LANES
    nqb = BRANCH_WIDTH // wq
    big = lambda dt: jax.ShapeDtypeStruct((B * HEADS, n, C, LANES), dt)
    ospec = pl.BlockSpec((nh, 1, C, LANES), lambda b, g, c: (b * ng + g, c, 0, 0))
    par = pl.BlockSpec((1, LANES), lambda b, g, c: (0, 0))
    outs = pl.pallas_call(
        _gdn_chunk_kernel, grid=(B, ng, n),
        in_specs=[pl.BlockSpec((C, wq), lambda b, g, c: (b * n + c, g)),
                  pl.BlockSpec((C, wq), lambda b, g, c: (b * n + c, nqb + g)),
                  pl.BlockSpec((C, wq), lambda b, g, c: (b * n + c, 2 * nqb + g)),
                  pl.BlockSpec((C, LANES), lambda b, g, c: (b * n + c, 0)), par, par],
        out_specs=[ospec] * 5 + [pl.BlockSpec((nh, 1, SUBLANES, LANES), lambda b, g, c: (b * ng + g, c, 0, 0))],
        out_shape=[big(F32)] + [big(BF16)] * 4 + [jax.ShapeDtypeStruct((B * HEADS, n, SUBLANES, LANES), F32)],
        compiler_params=_cp("parallel", "parallel", "parallel"))(qkv, qkv, qkv, small, alog_row, dtb_row)
    return [o.reshape(-1, LANES) for o in outs]


def _gdn_scan_kernel(u_ref, w_ref, qg_ref, qk_ref, kdt_ref, dl_ref, z_ref, g_ref, o_ref, o_sc, *, n):
    C = GDN_KCHUNK
    S = n * C
    npair = GDN_SCAN_PAIRS

    def step(c, states):
        out = []
        for k, state in enumerate(states):
            r = pl.ds(pl.multiple_of(k * S + c * C, C), C)
            sb = state.astype(BF16)
            v_new = u_ref[r, :] - _dot(w_ref[r, :], sb)
            vb = v_new.astype(BF16)
            o_sc[r, :] = _dot(qg_ref[r, :], sb) + _dot(qk_ref[r, :], vb)
            dl = dl_ref[pl.ds(pl.multiple_of((k * n + c) * SUBLANES, SUBLANES), 1), :]
            out.append(state * dl + _dot(kdt_ref[r, :], vb))
        return tuple(out)

    lax.fori_loop(0, n, step, tuple(jnp.zeros((LANES, LANES), F32) for _ in range(2 * npair)))
    lane = lax.broadcasted_iota(jnp.int32, (S, LANES), 1)

    def nrm(o):
        ms = jnp.sum(o * o, axis=1, keepdims=True) * (1.0 / GDN_HEAD_DIM)
        return o * lax.rsqrt(ms + RMS_EPS)

    for p in range(npair):
        ps = slice(p * LANES, (p + 1) * LANES)
        nn = jnp.where(lane < GDN_HEAD_DIM, nrm(o_sc[2 * p * S:(2 * p + 1) * S, :]),
                       pltpu.roll(nrm(o_sc[(2 * p + 1) * S:(2 * p + 2) * S, :]), LANES // 2, 1))
        z = z_ref[:, ps].astype(F32)
        o_ref[:, ps] = nn * g_ref[...] * (z * _sigmoid(z))


def _gdn_scan(parts, proj, g_row, B, S):
    n = S // GDN_KCHUNK
    nps = GDN_SCAN_PAIRS
    steps = HEADS // 2 // nps
    wz = nps * LANES
    seq = pl.BlockSpec((2 * nps * S, LANES), lambda i: (i, 0))
    return pl.pallas_call(
        functools.partial(_gdn_scan_kernel, n=n), grid=(B * steps,),
        in_specs=[seq] * 5 + [pl.BlockSpec((2 * nps * n * SUBLANES, LANES), lambda i: (i, 0)),
                              pl.BlockSpec((S, wz), lambda i: (i // steps, C_GZ // wz + i % steps)),
                              pl.BlockSpec((1, LANES), lambda i: (0, 0))],
        out_specs=pl.BlockSpec((S, wz), lambda i: (i // steps, i % steps)),
        out_shape=jax.ShapeDtypeStruct((B * S, BRANCH_WIDTH), F32),
        scratch_shapes=[pltpu.VMEM((2 * nps * S, LANES), F32)],
        compiler_params=_cp("parallel"))(*parts, proj, g_row)


def _merge_out_kernel(g0_ref, g1_ref, g2_ref, gb_ref, a_ref, b_ref, c_ref, wb_ref, wo_ref, h_ref, lg_ref, lb_ref,
                      o_ref, ot_ref):
    acc = None
    for n, (gl, br) in enumerate(((g0_ref, a_ref), (g1_ref, b_ref), (g2_ref, c_ref))):
        y = _sigmoid(gl[...].astype(F32) + gb_ref[n:n + 1, :]) * _dot(br[...], wb_ref[n])
        acc = y if acc is None else acc + y
    y = DEEPNORM_ALPHA * h_ref[...] + _dot(acc, wo_ref[...])
    y = _ln_rows(y, lg_ref[...], lb_ref[...])
    o_ref[...] = y
    ot_ref[...] = y.T.astype(BF16)


def _merge_out_ln(proj, gate_bias, o_a, o_b, o_c, w_branch, w_out, h, g, b, tm=512):
    T, D = h.shape
    row = lambda i: (i, 0)
    gspec = lambda n: pl.BlockSpec((tm, D), lambda i: (i, n))
    bspec = pl.BlockSpec((tm, BRANCH_WIDTH), row)
    vec = pl.BlockSpec((1, D), lambda i: (0, 0))
    return pl.pallas_call(
        _merge_out_kernel, grid=(T // tm,),
        in_specs=[gspec(0), gspec(1), gspec(2), pl.BlockSpec((3, D), lambda i: (0, 0)), bspec, bspec, bspec,
                  pl.BlockSpec((3, BRANCH_WIDTH, D), lambda i: (0, 0, 0)), pl.BlockSpec((D, D), lambda i: (0, 0)),
                  pl.BlockSpec((tm, D), row), vec, vec],
        out_specs=[pl.BlockSpec((tm, D), row), pl.BlockSpec((D, tm), lambda i: (0, i))],
        out_shape=[jax.ShapeDtypeStruct((T, D), F32), jax.ShapeDtypeStruct((D, T), BF16)],
        compiler_params=_cp("parallel"))(proj, proj, proj, gate_bias, o_a, o_b, o_c, w_branch, w_out, h,
                                         g.reshape(1, D), b.reshape(1, D))


def _sort_network(n):
    pairs, p = [], 1
    while p < n:
        k = p
        while k >= 1:
            for j in range(k % p, n - k, 2 * k):
                for i in range(min(k, n - j - k)):
                    if (i + j) // (2 * p) == (i + j + k) // (2 * p):
                        pairs.append((i + j, i + j + k))
            k //= 2
        p *= 2
    return pairs


def _top_of_sorted_lists(scores, n, on_max):
    lists = []
    for s in scores:
        nt = s.shape[0] // SUBLANES
        rows = [s[t * SUBLANES:(t + 1) * SUBLANES, :] for t in range(nt)]
        rows += [None] * ((1 << (nt - 1).bit_length()) - nt)
        for a, b in _sort_network(len(rows)):
            if rows[b] is None:
                continue
            if rows[a] is None:
                rows[a], rows[b] = rows[b], None
            else:
                rows[a], rows[b] = jnp.maximum(rows[a], rows[b]), jnp.minimum(rows[a], rows[b])
        lists.append(rows[:nt] + [None])
    for k in range(n):
        for i, rows in enumerate(lists):
            m = jnp.max(rows[0], axis=0, keepdims=True)
            on_max(i, k, m)
            hit = rows[0] >= m
            for t in range(min(n - 1 - k, len(rows) - 1)):
                nxt = -jnp.inf if rows[t + 1] is None else rows[t + 1]
                rows[t] = jnp.where(hit, nxt, rows[t])


def _peer_score_kernel(qt_ref, keys_ref, s2_ref, e2_ref, c_ref, e1_ref, top_sc, cand_sc):
    K = PEER_TOPK
    tt = qt_ref.shape[1]
    nh = PEER_SCORE_HEADS_PER_TRIP

    def heads(hp, carry):
        s = []
        for i in range(2 * nh):
            r = pl.multiple_of((hp * nh * 2 + i) * PEER_HALF, PEER_HALF)
            s.append(_dot(keys_ref[hp * nh * 2 + i], qt_ref[pl.ds(r, PEER_HALF), :]))

        def put(i, k, m):
            top_sc[i, k:k + 1, :] = m
        _top_of_sorted_lists(s, K + 1, put)
        r8 = lax.broadcasted_iota(jnp.int32, (SUBLANES, tt), 0)
        v1max, v2max = [], []
        for d in range(nh):
            v1a, v1x = top_sc[2 * d, 0:K, :], top_sc[2 * d, K:K + 1, :]
            v2a, v2x = top_sc[2 * d + 1, 0:K, :], top_sc[2 * d + 1, K:K + 1, :]
            v1max.append(v1a[0:1])
            v2max.append(v2a[0:1])
            cand_sc[d, 0:K, :] = v1a + v2a[0:1]
            for b in range(1, SUBLANES):
                cand_sc[d, K + SUBLANES * (b - 1):K + SUBLANES * b, :] = v1a[0:SUBLANES] + v2a[b:b + 1]
            base = K + SUBLANES * (SUBLANES - 1)
            cand_sc[d, base:base + SUBLANES, :] = v1a[0:1] + v2a[SUBLANES:K]
            cand_sc[d, base + SUBLANES:base + 2 * SUBLANES, :] = jnp.where(
                r8 == 0, v1x + v2a[0:1], jnp.where(r8 == 1, v1a[0:1] + v2x, -jnp.inf))
        st = [{"z": jnp.zeros((1, tt), F32)} for _ in range(nh)]

        def acc(d, k, m):
            if k == 0:
                st[d]["top"] = m
            if k < K:
                st[d]["z"] = st[d]["z"] + jnp.exp(m - st[d]["top"])
            if k == K - 1:
                st[d]["t16"] = m
            if k == K:
                st[d]["t17"] = m
        _top_of_sorted_lists([cand_sc[d] for d in range(nh)], K + 1, acc)
        for d in range(nh):
            tau = 0.5 * (st[d]["t16"] + st[d]["t17"])
            ro = pl.ds(pl.multiple_of((hp * nh + d) * N_KEYS, N_KEYS), N_KEYS)
            s2_ref[ro, :] = s[2 * d + 1]
            e2_ref[ro, :] = jnp.exp(s[2 * d + 1] - v2max[d]) / st[d]["z"]
            c_ref[ro, :] = tau - s[2 * d]
            e1_ref[ro, :] = jnp.exp(s[2 * d] - v1max[d])
        return carry

    lax.fori_loop(0, HEADS // nh, heads, 0)


def _peer_scores(qt, keys):
    T = qt.shape[1]
    tt = PEER_SCORE_TT
    R = HEADS * N_KEYS
    ncand = PEER_TOPK + SUBLANES * (SUBLANES + 1)
    ospec = pl.BlockSpec((R, tt), lambda i: (0, i))
    return pl.pallas_call(
        _peer_score_kernel, grid=(T // tt,),
        in_specs=[pl.BlockSpec((2 * R, tt), lambda i: (0, i)),
                  pl.BlockSpec((2 * HEADS, N_KEYS, PEER_HALF), lambda i: (0, 0, 0))],
        out_specs=[ospec] * 4,
        out_shape=[jax.ShapeDtypeStruct((R, T), F32)] * 4,
        scratch_shapes=[pltpu.VMEM((2 * PEER_SCORE_HEADS_PER_TRIP, 3 * SUBLANES, tt), F32),
                        pltpu.VMEM((PEER_SCORE_HEADS_PER_TRIP, ncand, tt), F32)],
        compiler_params=_cp("parallel"))(qt, keys)


def _peer_expert_kernel(ht_ref, u_ref, vtp_ref, vtl_ref, s2_ref, e2_ref, c_ref, e1_ref, h_ref, g_ref, b_ref,
                        o_ref, acc_sc, act_sc, hw_sc, bc_sc):
    j = pl.program_id(1)
    _, eb, tt = hw_sc.shape
    ng = eb // N_KEYS
    assert c_ref.shape == (HEADS, ng, tt)
    slot = j % 2

    @pl.when(j == 0)
    def _():
        acc_sc[...] = jnp.zeros(acc_sc.shape, F32)
        hw_sc[1] = jnp.zeros(hw_sc.shape[1:], BF16)

    mxu_w = 2 * LANES
    for half in range(tt // mxu_w):
        hl = slice(half * mxu_w, (half + 1) * mxu_w)
        a = jnp.dot(u_ref[...], ht_ref[:, hl], preferred_element_type=F32)
        act_sc[:, hl] = 0.5 * a * (1.0 + lax.erf(a * (2.0 ** -0.5)))
        acc_sc[:, hl] += jnp.dot(vtp_ref[...], hw_sc[1 - slot, :, hl], preferred_element_type=F32)
        for lc in range(half * (mxu_w // LANES), (half + 1) * (mxu_w // LANES)):
            ls = slice(lc * LANES, (lc + 1) * LANES)
            for h in range(HEADS):
                thr, e1 = c_ref[h, :, ls], e1_ref[h, :, ls]
                for g in range(ng):
                    bc_sc[0, h * ng + g] = jnp.broadcast_to(thr[g:g + 1], (SUBLANES, LANES))
                    bc_sc[1, h * ng + g] = jnp.broadcast_to(e1[g:g + 1], (SUBLANES, LANES))
            for g in range(ng):
                wsum = jnp.zeros((N_KEYS // SUBLANES, SUBLANES, LANES), F32)
                for h in range(HEADS):
                    hr = slice(h * N_KEYS, (h + 1) * N_KEYS)
                    s2 = s2_ref[hr, ls].reshape(N_KEYS // SUBLANES, SUBLANES, LANES)
                    e2 = e2_ref[hr, ls].reshape(N_KEYS // SUBLANES, SUBLANES, LANES)
                    wsum = wsum + jnp.where(s2 >= bc_sc[0, h * ng + g][None], e2 * bc_sc[1, h * ng + g][None], 0.0)
                gs = slice(g * N_KEYS, (g + 1) * N_KEYS)
                hw_sc[slot, gs, ls] = (wsum.reshape(N_KEYS, LANES) * act_sc[gs, ls]).astype(BF16)

    @pl.when(j == pl.num_programs(1) - 1)
    def _():
        acc = acc_sc[...] + jnp.dot(vtl_ref[...], hw_sc[slot], preferred_element_type=F32)
        y = DEEPNORM_ALPHA * h_ref[...] + acc.T
        o_ref[...] = _ln_rows(y, g_ref[...], b_ref[...])


def _peer_experts(ht, u, vt, s2, e2, c, e1, h, g, b):
    T, D = h.shape
    tt, eb = PEER_TT, PEER_EB
    R = HEADS * N_KEYS
    nb = N_EXPERTS // eb
    assert nb % 2 == 0
    ng = eb // N_KEYS
    sspec = pl.BlockSpec((R, tt), lambda i, j: (0, i))
    gspec = pl.BlockSpec((HEADS, ng, tt), lambda i, j: (0, j, i))
    vec = pl.BlockSpec((1, D), lambda i, j: (0, 0))
    return pl.pallas_call(
        _peer_expert_kernel, grid=(T // tt, nb),
        in_specs=[pl.BlockSpec((D, tt), lambda i, j: (0, i)),
                  pl.BlockSpec((eb, D), lambda i, j: (j, 0)),
                  pl.BlockSpec((D, eb), lambda i, j: (0, jnp.maximum(j - 1, 0))),
                  pl.BlockSpec((D, eb), lambda i, j: (0, nb - 1)),
                  sspec, sspec, gspec, gspec,
                  pl.BlockSpec((tt, D), lambda i, j: (i, 0)), vec, vec],
        out_specs=pl.BlockSpec((tt, D), lambda i, j: (i, 0)),
        out_shape=jax.ShapeDtypeStruct((T, D), F32),
        scratch_shapes=[pltpu.VMEM((D, tt), F32), pltpu.VMEM((eb, tt), F32), pltpu.VMEM((2, eb, tt), BF16),
                        pltpu.VMEM((2, HEADS * ng, SUBLANES, LANES), F32)],
        compiler_params=_cp("parallel", "arbitrary"))(ht, u, vt, vt, s2, e2, c.reshape(HEADS, N_KEYS, T),
                                                      e1.reshape(HEADS, N_KEYS, T), h, g.reshape(1, D),
                                                      b.reshape(1, D))


def _rot_half(x, r):
    return jnp.concatenate([-x[..., r:2 * r], x[..., :r]], axis=-1)


def _prep_w_in(w):
    D = w.shape[0]
    splits = (MLA_Q_RANK, MLA_KV_RANK, MLA_ROPE, 3 * BRANCH_WIDTH, 3 * BRANCH_WIDTH, BRANCH_WIDTH, HEADS, HEADS,
              3 * D)
    o = np.cumsum((0,) + splits)
    cq, ckv, kr, mqkv, gqkv, gz, ga, gb, gl = [w[:, o[i]:o[i + 1]] for i in range(9)]
    mq, mk, mv = jnp.split(mqkv, 3, axis=1)

    def slab(m):
        m = m.reshape(D, HEADS, MOBA_HEAD_DIM)
        pad = jnp.zeros((D, HEADS, LANES - MOBA_HEAD_DIM - MOBA_ROT_DIM), w.dtype)
        return jnp.concatenate([m, _rot_half(m, MOBA_ROT_DIM // 2), pad], axis=-1).reshape(D, HEADS * LANES)

    small = jnp.concatenate([ga, gb, jnp.zeros((D, LANES // 2 - 2 * HEADS), w.dtype), kr,
                             _rot_half(kr, MLA_ROPE // 2)], axis=1)
    out = jnp.concatenate([gl, cq, ckv, slab(mq), slab(mk), mv, gqkv, gz], axis=1)
    assert out.shape[1] == C_TOTAL and C_TOTAL % PROJ_TN == 0
    return out.astype(BF16), small.astype(BF16)


def _prep_w_uq(w):
    R = w.shape[0]
    w = w.reshape(R, HEADS, MLA_NOPE + MLA_ROPE)
    rope = w[..., MLA_NOPE:]
    return jnp.concatenate([w, _rot_half(rope, MLA_ROPE // 2)], axis=-1).reshape(R, HEADS * LANES).astype(BF16)


def _prep_w_ukv(w):
    R = w.shape[0]
    w = w.reshape(R, HEADS, MLA_NOPE + MLA_V)
    k = jnp.concatenate([w[..., :MLA_NOPE], jnp.zeros((R, HEADS, LANES - MLA_NOPE), w.dtype)], axis=-1)
    return jnp.concatenate([k.reshape(R, HEADS * LANES), w[..., MLA_NOPE:].reshape(R, HEADS * MLA_V)],
                           axis=1).astype(BF16)


def _rope_tables(positions):
    pos = positions.reshape(-1).astype(F32)[:, None]
    T = pos.shape[0]

    def cs(rot):
        inv = ROPE_THETA ** (-jnp.arange(0, rot, 2, dtype=F32) / rot)
        ang = pos * inv
        return jnp.cos(ang), jnp.sin(ang)

    ca, sa = cs(MLA_ROPE)
    cb, sb = cs(MOBA_ROT_DIM)
    one = lambda n: jnp.ones((T, n), F32)
    zero = lambda n: jnp.zeros((T, n), F32)
    sc_a = (MLA_NOPE + MLA_ROPE) ** -0.5 * math.log2(math.e)
    sc_b = MOBA_HEAD_DIM ** -0.5 * math.log2(math.e)
    cat = lambda *xs: jnp.concatenate(xs, axis=1)
    mla = (cat(one(MLA_NOPE), ca, ca, zero(32)) * sc_a, cat(zero(MLA_NOPE), sa, sa, zero(32)) * sc_a,
           cat(zero(MLA_NOPE), ca, ca, zero(32)), cat(zero(MLA_NOPE), sa, sa, zero(32)))
    cm = cat(cb, cb, one(MOBA_HEAD_DIM - MOBA_ROT_DIM), zero(LANES - MOBA_HEAD_DIM))
    sm = cat(sb, sb, zero(LANES - MOBA_ROT_DIM))
    moba = (cm * sc_b, sm * sc_b, cm, sm)
    return mla, moba


def _lane_row(v):
    return jnp.concatenate([v.astype(F32), jnp.zeros((LANES - v.shape[0],), F32)]).reshape(1, LANES)


def kernel(x, positions, ln_in_g, ln_in_b, w_in, mla_q_norm, mla_kv_norm, mla_w_uq, mla_w_ukv, gdn_conv_w, gdn_A_log, gdn_dt_bias, gdn_o_norm, gate_bias, w_branch, w_out, ln1_g, ln1_b, peer_w_q, peer_sub_keys, peer_u, peer_v, ln2_g, ln2_b):
    B, S, D = x.shape
    T = B * S
    assert S % ATT_TQ == 0 and S % GDN_KCHUNK == 0 and T % PEER_TT == 0 and ATT_TQ == MOBA_BLOCK
    mla_tabs, moba_tabs = _rope_tables(positions)
    h = _layer_norm(x.reshape(T, D), ln_in_g, ln_in_b)
    for l in range(DEPTH):
        w_main, w_small = _prep_w_in(w_in[l])
        proj = _mm(h, w_main, tm=512, tn=PROJ_TN, out_dtype=BF16)
        small = _mm(h, w_small, tm=512, tn=LANES)
        qa, ka, va = _mla_qkv(proj, small, mla_q_norm[l], mla_kv_norm[l], _prep_w_uq(mla_w_uq[l]),
                              _prep_w_ukv(mla_w_ukv[l]), mla_tabs)
        o_a = _attention(qa, ka, va, 0, B, S)
        qm, km, kmean = _moba_prep(proj, moba_tabs)
        o_b = _attention(qm, km, proj, C_MV, B, S, kmean=kmean.reshape(B, S // MOBA_BLOCK, HEADS * LANES))
        qkv = _gdn_conv(proj, gdn_conv_w[l], B, S)
        parts = _gdn_chunks(qkv, small, _lane_row(gdn_A_log[l]), _lane_row(gdn_dt_bias[l]), B, S)
        o_c = _gdn_scan(parts, proj, jnp.tile(gdn_o_norm[l], 2).reshape(1, LANES), B, S)
        h, ht = _merge_out_ln(proj, gate_bias[l], o_a, o_b, o_c, w_branch[l].astype(BF16), w_out[l].astype(BF16),
                              h, ln1_g[l], ln1_b[l])
        qt = _mm(peer_w_q[l].T.astype(BF16), ht, tm=1024, tn=min(1024, T), out_dtype=BF16)
        keys = peer_sub_keys[l].reshape(2 * HEADS, N_KEYS, PEER_HALF).astype(BF16)
        s2, e2, c, e1 = _peer_scores(qt, keys)
        h = _peer_experts(ht, peer_u[l].astype(BF16), peer_v[l].astype(BF16).T, s2, e2, c, e1, h, ln2_g[l], ln2_b[l])
    return h.reshape(B, S, D)
```

```python
import functools
import math

import numpy as np
import jax
import jax.numpy as jnp
from jax import lax
from jax.experimental import pallas as pl
from jax.experimental.pallas import tpu as pltpu

F32 = jnp.float32
BF16 = jnp.bfloat16
HIGHEST = lax.Precision.HIGHEST

DEPTH = 2
ROPE_THETA = 500000.0
NEG_INF = -1e30
LN_EPS = 1e-5
RMS_EPS = 1e-6
DEEPNORM_ALPHA = (2 * DEPTH) ** 0.25
HEADS = 8
MLA_NOPE, MLA_ROPE, MLA_V = 64, 32, 64
MLA_Q_RANK, MLA_KV_RANK = 768, 256
MOBA_HEAD_DIM, MOBA_ROT_DIM, MOBA_BLOCK, MOBA_TOPK = 64, 16, 256, 3
GDN_HEAD_DIM, GDN_CONV = 64, 4
BRANCH_WIDTH = 512
N_KEYS, PEER_TOPK, PEER_HALF = 128, 16, 128
N_EXPERTS = N_KEYS * N_KEYS

LANES = 128
SUBLANES = 8
VMEM_LIMIT = 56 * 1024 * 1024

GDN_KCHUNK = 128
GDN_HEADS_PER_STEP = 8
GDN_CHUNKS_PER_STEP = 2
GDN_SCAN_PAIRS = 2
ATT_TQ = 256
PEER_TT = 512
PEER_EB = 1024
PEER_SCORE_TT = 256
PEER_SCORE_HEADS_PER_TRIP = 4

C_GATE, C_CQ, C_CKV, C_MQ, C_MK, C_MV, C_GQKV, C_GZ, C_TOTAL = (
    0, 3072, 3840, 4096, 5120, 6144, 6656, 8192, 8704)
PROJ_TN = 4352


def _cp(*sem):
    return pltpu.CompilerParams(dimension_semantics=sem, vmem_limit_bytes=VMEM_LIMIT)


def _dot(a, b):
    return jnp.dot(a.astype(BF16), b.astype(BF16), preferred_element_type=F32)


def _dot_nt(a, b):
    return lax.dot_general(a.astype(BF16), b.astype(BF16), (((1,), (1,)), ((), ())),
                           preferred_element_type=F32)


def _ln_rows(y, g, b):
    mu = jnp.mean(y, axis=-1, keepdims=True)
    d = y - mu
    var = jnp.mean(d * d, axis=-1, keepdims=True)
    return d * lax.rsqrt(var + LN_EPS) * g + b


def _sigmoid(x):
    return 1.0 / (1.0 + jnp.exp(-x))


def _ln_kernel(x_ref, g_ref, b_ref, o_ref):
    o_ref[...] = _ln_rows(x_ref[...], g_ref[...], b_ref[...])


def _layer_norm(x, g, b, tm=512):
    T, D = x.shape
    return pl.pallas_call(
        _ln_kernel, grid=(T // tm,),
        in_specs=[pl.BlockSpec((tm, D), lambda i: (i, 0)),
                  pl.BlockSpec((1, D), lambda i: (0, 0)),
                  pl.BlockSpec((1, D), lambda i: (0, 0))],
        out_specs=pl.BlockSpec((tm, D), lambda i: (i, 0)),
        out_shape=jax.ShapeDtypeStruct((T, D), F32),
        compiler_params=_cp("parallel"))(x, g.reshape(1, D), b.reshape(1, D))


def _mm_kernel(x_ref, w_ref, o_ref):
    o_ref[...] = _dot(x_ref[...], w_ref[...]).astype(o_ref.dtype)


def _mm(x, w, tm, tn, out_dtype=F32):
    M, K = x.shape
    N = w.shape[1]
    return pl.pallas_call(
        _mm_kernel, grid=(N // tn, M // tm),
        in_specs=[pl.BlockSpec((tm, K), lambda j, i: (i, 0)),
                  pl.BlockSpec((K, tn), lambda j, i: (0, j))],
        out_specs=pl.BlockSpec((tm, tn), lambda j, i: (i, j)),
        out_shape=jax.ShapeDtypeStruct((M, N), out_dtype),
        compiler_params=_cp("parallel", "parallel"))(x, w)


def _rms_rows(x, g):
    return x * lax.rsqrt(jnp.mean(x * x, axis=-1, keepdims=True) + RMS_EPS) * g


def _head_slabs(x):
    return [x[:, h * LANES:(h + 1) * LANES] for h in range(HEADS)]


def _mla_qkv_kernel(cq_ref, ckv_ref, gq_ref, gkv_ref, wq_ref, wkv_ref, sm_ref, cq_tab, sq_tab, ck_tab, sk_tab,
                    qo_ref, ko_ref, vo_ref):
    W = HEADS * LANES
    q = _dot(_rms_rows(cq_ref[...].astype(F32), gq_ref[...]), wq_ref[...])
    kv = _dot(_rms_rows(ckv_ref[...].astype(F32), gkv_ref[...]), wkv_ref[...])
    vo_ref[...] = kv[:, W:].astype(BF16)
    cq, sq = cq_tab[...], sq_tab[...]
    sm = sm_ref[...]
    kr = sm * ck_tab[...] + pltpu.roll(sm, LANES - MLA_ROPE, 1) * sk_tab[...]
    for h in range(HEADS):
        sl = slice(h * LANES, (h + 1) * LANES)
        qs = q[:, sl]
        qo_ref[:, sl] = (qs * cq + pltpu.roll(qs, LANES - MLA_ROPE, 1) * sq).astype(BF16)
        ko_ref[:, sl] = (kv[:, sl] + kr).astype(BF16)


def _mla_qkv(proj, small, gq, gkv, wq, wkv, tabs, tm=512):
    T = proj.shape[0]
    W = HEADS * LANES
    row = lambda i: (i, 0)
    tab = pl.BlockSpec((tm, LANES), row)
    full = lambda a: pl.BlockSpec(a.shape, lambda i: (0, 0))
    gq, gkv = gq.reshape(1, -1), gkv.reshape(1, -1)
    return pl.pallas_call(
        _mla_qkv_kernel, grid=(T // tm,),
        in_specs=[pl.BlockSpec((tm, MLA_Q_RANK), lambda i: (i, C_CQ // MLA_Q_RANK)),
                  pl.BlockSpec((tm, MLA_KV_RANK), lambda i: (i, C_CKV // MLA_KV_RANK)),
                  full(gq), full(gkv), full(wq), full(wkv), tab, tab, tab, tab, tab],
        out_specs=[pl.BlockSpec((tm, W), row), pl.BlockSpec((tm, W), row), pl.BlockSpec((tm, BRANCH_WIDTH), row)],
        out_shape=[jax.ShapeDtypeStruct((T, W), BF16)] * 2 + [jax.ShapeDtypeStruct((T, BRANCH_WIDTH), BF16)],
        compiler_params=_cp("parallel"))(proj, proj, gq, gkv, wq, wkv, small, *tabs)


def _moba_prep_kernel(q_ref, k_ref, cq_ref, sq_ref, ck_ref, sk_ref, qo_ref, ko_ref, km_ref):
    cq, sq, ck, sk = cq_ref[...], sq_ref[...], ck_ref[...], sk_ref[...]
    q, k = q_ref[...].astype(F32), k_ref[...].astype(F32)
    for h, (qs, ks) in enumerate(zip(_head_slabs(q), _head_slabs(k))):
        sl = slice(h * LANES, (h + 1) * LANES)
        qo_ref[:, sl] = (qs * cq + pltpu.roll(qs, LANES // 2, 1) * sq).astype(BF16)
        kk = ks * ck + pltpu.roll(ks, LANES // 2, 1) * sk
        ko_ref[:, sl] = kk.astype(BF16)
        km_ref[0, :, sl] = jnp.mean(kk, axis=0, keepdims=True)


def _moba_prep(proj, tabs):
    T = proj.shape[0]
    W = HEADS * LANES
    tm = MOBA_BLOCK
    row = lambda i: (i, 0)
    tab = pl.BlockSpec((tm, LANES), row)
    return pl.pallas_call(
        _moba_prep_kernel, grid=(T // tm,),
        in_specs=[pl.BlockSpec((tm, W), lambda i: (i, C_MQ // W)),
                  pl.BlockSpec((tm, W), lambda i: (i, C_MK // W)), tab, tab, tab, tab],
        out_specs=[pl.BlockSpec((tm, W), row), pl.BlockSpec((tm, W), row),
                   pl.BlockSpec((1, 1, W), lambda i: (i, 0, 0))],
        out_shape=[jax.ShapeDtypeStruct((T, W), BF16), jax.ShapeDtypeStruct((T, W), BF16),
                   jax.ShapeDtypeStruct((T // tm, 1, W), F32)],
        compiler_params=_cp("parallel"))(proj, proj, *tabs)


def _attn_kernel(*refs, moba, nk):
    if moba:
        q_ref, k_ref, v_ref, km_ref, o_ref, vt_sc, m_sc, l_sc, al_sc, acc_sc, s_sc, p_sc, bias_sc = refs
    else:
        q_ref, k_ref, v_ref, o_ref, vt_sc, m_sc, l_sc, al_sc, acc_sc, s_sc, p_sc = refs
    tq = tk = ATT_TQ
    dv = BRANCH_WIDTH // HEADS
    qi = pl.program_id(1)
    hslab = [slice(h * LANES, (h + 1) * LANES) for h in range(HEADS)]

    @pl.when(qi == 0)
    def _():
        for j in range(nk):
            for c in range(tk // LANES):
                for g in range(BRANCH_WIDTH // LANES):
                    vt_sc[j, g * LANES:(g + 1) * LANES, c * LANES:(c + 1) * LANES] = (
                        v_ref[j * tk + c * LANES:j * tk + (c + 1) * LANES,
                              g * LANES:(g + 1) * LANES].astype(F32).T.astype(BF16))

    if moba:
        blk = lax.broadcasted_iota(jnp.int32, (SUBLANES, tq), 0)
        for h in range(HEADS):
            qf = q_ref[:, hslab[h]].astype(F32)
            km = km_ref[0, :, hslab[h]]
            if nk < SUBLANES:
                km = jnp.concatenate([km, jnp.zeros((SUBLANES - nk, LANES), F32)], axis=0)
            gate = lax.dot_general(km, qf, (((1,), (1,)), ((), ())), precision=HIGHEST,
                                   preferred_element_type=F32)
            gate = jnp.where(blk < qi, gate, -jnp.inf)
            for n in range(nk):
                gn = gate[n:n + 1, :]
                beats = jnp.where(gate > gn, 1.0, jnp.where((gate == gn) & (blk < n), 1.0, 0.0))
                cnt = jnp.sum(beats, axis=0, keepdims=True)
                bias_sc[h, n] = jnp.where((cnt < MOBA_TOPK) & (n < qi), 0.0, NEG_INF)
    m_sc[...] = jnp.full(m_sc.shape, NEG_INF, F32)
    l_sc[...] = jnp.zeros(l_sc.shape, F32)
    acc_sc[...] = jnp.zeros(acc_sc.shape, F32)

    def block(j, diag):
        rows = pl.ds(pl.multiple_of(j * tk, tk), tk)
        for h in range(HEADS):
            s = _dot_nt(k_ref[rows, hslab[h]], q_ref[:, hslab[h]])
            if diag:
                keyi = lax.broadcasted_iota(jnp.int32, (tk, tq), 0)
                qryi = lax.broadcasted_iota(jnp.int32, (tk, tq), 1)
                s = jnp.where(keyi <= qryi, s, NEG_INF)
            elif moba:
                s = s + bias_sc[h, j]
            s_sc[h] = s
        for h in range(HEADS):
            s = s_sc[h]
            m_prev = m_sc[h]
            m_new = jnp.maximum(m_prev, jnp.max(s, axis=0, keepdims=True))
            alpha = jnp.exp2(m_prev - m_new)
            p = jnp.exp2(s - m_new)
            l_sc[h] = alpha * l_sc[h] + jnp.sum(p, axis=0, keepdims=True)
            p_sc[h] = p.astype(BF16)
            al_sc[h] = alpha
            m_sc[h] = m_new
        for h in range(HEADS):
            acc_sc[h] = al_sc[h] * acc_sc[h] + jnp.dot(vt_sc[j, h * dv:(h + 1) * dv, :], p_sc[h],
                                                       preferred_element_type=F32)

    def past(j, carry):
        block(j, False)
        return carry
    lax.fori_loop(0, qi, past, 0)
    block(qi, True)
    ot = jnp.concatenate([acc_sc[h] * (1.0 / l_sc[h]) for h in range(HEADS)], axis=0)
    o_ref[...] = ot.T


def _attention(q, k, v, v_col, B, S, kmean=None):
    T = B * S
    tq = ATT_TQ
    nq = S // tq
    W = HEADS * LANES
    dv = BRANCH_WIDTH // HEADS
    moba = kmean is not None
    assert nq <= SUBLANES and v_col % BRANCH_WIDTH == 0
    in_specs = [pl.BlockSpec((tq, W), lambda b, i: (b * nq + i, 0)),
                pl.BlockSpec((S, W), lambda b, i: (b, 0)),
                pl.BlockSpec((S, BRANCH_WIDTH), lambda b, i: (b, v_col // BRANCH_WIDTH))]
    scratch = [pltpu.VMEM((nq, BRANCH_WIDTH, tq), BF16)]
    row = pltpu.VMEM((HEADS, 1, tq), F32)
    scratch += [row, row, row, pltpu.VMEM((HEADS, dv, tq), F32),
                pltpu.VMEM((HEADS, tq, tq), F32), pltpu.VMEM((HEADS, tq, tq), BF16)]
    args = [q, k, v]
    if moba:
        in_specs.append(pl.BlockSpec((1, nq, W), lambda b, i: (b, 0, 0)))
        scratch.append(pltpu.VMEM((HEADS, nq, 1, tq), F32))
        args.append(kmean)
    return pl.pallas_call(
        functools.partial(_attn_kernel, moba=moba, nk=nq),
        grid=(B, nq), in_specs=in_specs,
        out_specs=pl.BlockSpec((tq, BRANCH_WIDTH), lambda b, i: (b * nq + i, 0)),
        out_shape=jax.ShapeDtypeStruct((T, BRANCH_WIDTH), F32),
        scratch_shapes=scratch,
        compiler_params=_cp("parallel", "arbitrary"))(*args)


def _gdn_conv_kernel(x_ref, w_ref, o_ref):
    c = pl.program_id(1)
    x = x_ref[...].astype(F32)
    w = w_ref[...]
    S = x.shape[0]
    row = lax.broadcasted_iota(jnp.int32, (S, LANES), 0)
    lane = lax.broadcasted_iota(jnp.int32, (S, LANES), 1)
    y = x * w[GDN_CONV - 1:GDN_CONV, :]
    for d in range(1, GDN_CONV):
        xs = jnp.where(row >= d, pltpu.roll(x, d, 0), 0.0)
        y = y + xs * w[GDN_CONV - 1 - d:GDN_CONV - d, :]
    y = y * _sigmoid(y)
    sq = y * y
    lo = lane < GDN_HEAD_DIM
    ss0 = jnp.sum(jnp.where(lo, sq, 0.0), axis=1, keepdims=True)
    ss1 = jnp.sum(jnp.where(lo, 0.0, sq), axis=1, keepdims=True)
    inv = lax.rsqrt(jnp.where(lo, ss0, ss1) + RMS_EPS)
    nqb = BRANCH_WIDTH // LANES
    scale = jnp.where(c < nqb, GDN_HEAD_DIM ** -0.5, 1.0)
    o_ref[...] = jnp.where(c < 2 * nqb, y * inv * scale, y).astype(BF16)


def _gdn_conv(proj, conv_w, B, S):
    T = B * S
    nb = 3 * BRANCH_WIDTH // LANES
    return pl.pallas_call(
        _gdn_conv_kernel, grid=(B, nb),
        in_specs=[pl.BlockSpec((S, LANES), lambda b, c: (b, C_GQKV // LANES + c)),
                  pl.BlockSpec((GDN_CONV, LANES), lambda b, c: (0, c))],
        out_specs=pl.BlockSpec((S, LANES), lambda b, c: (b, c)),
        out_shape=jax.ShapeDtypeStruct((T, 3 * BRANCH_WIDTH), BF16),
        compiler_params=_cp("parallel", "parallel"))(proj, conv_w)


def _gdn_chunk_kernel(q_ref, k_ref, v_ref, sm_ref, alog_ref, dtb_ref,
                      u_ref, w_ref, qg_ref, qk_ref, kdt_ref, dl_ref):
    C = GDN_KCHUNK
    nh = GDN_HEADS_PER_STEP
    ncs = GDN_CHUNKS_PER_STEP
    h0 = pl.program_id(1) * nh
    lane = lax.broadcasted_iota(jnp.int32, (C, LANES), 1)
    row = lax.broadcasted_iota(jnp.int32, (C, LANES), 0)
    lo = lane < GDN_HEAD_DIM
    tril = row >= lane
    strict = row > lane
    eye = jnp.where(row == lane, 1.0, 0.0)
    ltri = tril.astype(BF16)

    def pick(ref, k, cc):
        x = ref[cc * C:(cc + 1) * C, (k // 2) * LANES:(k // 2 + 1) * LANES].astype(F32)
        return jnp.where(lo, pltpu.roll(x, LANES // 2, 1) if k % 2 else x, 0.0)

    garr, sig = [], []
    for cc in range(ncs):
        sm = sm_ref[cc * C:(cc + 1) * C, :]
        a = sm + dtb_ref[...]
        softplus = jnp.maximum(a, 0.0) + jnp.log1p(jnp.exp(-jnp.abs(a)))
        garr.append(-jnp.exp(alog_ref[...]) * softplus)
        sig.append(_sigmoid(sm))
    nc = ncs * nh
    kh, kb, gc, decay, A = [], [], [], [], []
    for i in range(nc):
        cc, k = divmod(i, nh)
        g_col = jnp.sum(jnp.where(lane == h0 + k, garr[cc], 0.0), axis=1, keepdims=True)
        beta = jnp.sum(jnp.where(lane == HEADS + h0 + k, sig[cc], 0.0), axis=1, keepdims=True)
        g1 = jnp.broadcast_to(g_col, (C, LANES))
        gsum = None
        for _ in range(3):
            gb = g1.astype(BF16)
            part = jnp.dot(ltri, gb, preferred_element_type=F32)
            gsum = part if gsum is None else gsum + part
            g1 = g1 - gb.astype(F32)
        gc.append(gsum)
        decay.append(jnp.where(tril, jnp.exp(jnp.where(tril, gsum - gsum.T, 0.0)), 0.0))
        kh.append(pick(k_ref, k, cc))
        kb.append(kh[i] * beta)
        vb = pick(v_ref, k, cc) * beta
        A.append(jnp.where(strict, _dot_nt(kb[i], kh[i]) * decay[i], 0.0))
        eg = jnp.exp(gsum)
        qh = pick(q_ref, k, cc)
        qk_ref[k, cc] = jnp.where(tril, _dot_nt(qh, kh[i]) * decay[i], 0.0).astype(BF16)
        qg_ref[k, cc] = (qh * eg).astype(BF16)
        glast = gsum[C - 1:C, :]
        kdt_ref[k, cc] = (kh[i] * jnp.exp(glast - gsum)).T.astype(BF16)
        dl_ref[k, cc] = jnp.broadcast_to(jnp.exp(glast), (SUBLANES, LANES))
        u_ref[k, cc] = vb
        w_ref[k, cc] = (kb[i] * eg).astype(BF16)
    def joiner(level):
        same = (row >> (level + 1)) == (lane >> (level + 1))
        return same & (((row >> level) & 1) == 1) & (((lane >> level) & 1) == 0)

    P = [eye - jnp.where(joiner(0), A[i], 0.0) for i in range(nc)]
    for level in range(1, int(math.log2(C))):
        msk = joiner(level)
        T1 = [_dot(P[i], jnp.where(msk, A[i], 0.0)) for i in range(nc)]
        P = [P[i] - _dot(T1[i], P[i]) for i in range(nc)]
    for i in range(nc):
        cc, k = divmod(i, nh)
        u_ref[k, cc] = _dot(P[i], u_ref[k, cc])
        w_ref[k, cc] = _dot(P[i], w_ref[k, cc]).astype(BF16)


def _gdn_chunks(qkv, small, alog_row, dtb_row, B, S):
    C = GDN_KCHUNK
    n = S // C
    nh = GDN_HEADS_PER_STEP
    ng = HEADS // nh
    wq = nh // 2 * LANES
    nqb = BRANCH_WIDTH // wq
    ncs = GDN_CHUNKS_PER_STEP
    ns = n // ncs
    big = lambda dt: jax.ShapeDtypeStruct((B * HEADS, n, C, LANES), dt)
    ospec = pl.BlockSpec((nh, ncs, C, LANES), lambda b, g, c: (b * ng + g, c, 0, 0))
    par = pl.BlockSpec((1, LANES), lambda b, g, c: (0, 0))
    outs = pl.pallas_call(
        _gdn_chunk_kernel, grid=(B, ng, ns),
        in_specs=[pl.BlockSpec((ncs * C, wq), lambda b, g, c: (b * ns + c, g)),
                  pl.BlockSpec((ncs * C, wq), lambda b, g, c: (b * ns + c, nqb + g)),
                  pl.BlockSpec((ncs * C, wq), lambda b, g, c: (b * ns + c, 2 * nqb + g)),
                  pl.BlockSpec((ncs * C, LANES), lambda b, g, c: (b * ns + c, 0)), par, par],
        out_specs=[ospec] * 5 + [pl.BlockSpec((nh, ncs, SUBLANES, LANES), lambda b, g, c: (b * ng + g, c, 0, 0))],
        out_shape=[big(F32)] + [big(BF16)] * 4 + [jax.ShapeDtypeStruct((B * HEADS, n, SUBLANES, LANES), F32)],
        compiler_params=_cp("parallel", "parallel", "parallel"))(qkv, qkv, qkv, small, alog_row, dtb_row)
    return [o.reshape(-1, LANES) for o in outs]


def _gdn_scan_kernel(u_ref, w_ref, qg_ref, qk_ref, kdt_ref, dl_ref, z_ref, g_ref, o_ref, o_sc, *, n):
    C = GDN_KCHUNK
    S = n * C
    npair = GDN_SCAN_PAIRS

    def step(c, states):
        out = []
        for k, state in enumerate(states):
            r = pl.ds(pl.multiple_of(k * S + c * C, C), C)
            sb = state.astype(BF16)
            v_new = u_ref[r, :] - _dot(w_ref[r, :], sb)
            vb = v_new.astype(BF16)
            o_sc[r, :] = _dot(qg_ref[r, :], sb) + _dot(qk_ref[r, :], vb)
            dl = dl_ref[pl.ds(pl.multiple_of((k * n + c) * SUBLANES, SUBLANES), 1), :]
            out.append(state * dl + _dot(kdt_ref[r, :], vb))
        return tuple(out)

    lax.fori_loop(0, n, step, tuple(jnp.zeros((LANES, LANES), F32) for _ in range(2 * npair)))
    lane = lax.broadcasted_iota(jnp.int32, (S, LANES), 1)

    def nrm(o):
        ms = jnp.sum(o * o, axis=1, keepdims=True) * (1.0 / GDN_HEAD_DIM)
        return o * lax.rsqrt(ms + RMS_EPS)

    for p in range(npair):
        ps = slice(p * LANES, (p + 1) * LANES)
        nn = jnp.where(lane < GDN_HEAD_DIM, nrm(o_sc[2 * p * S:(2 * p + 1) * S, :]),
                       pltpu.roll(nrm(o_sc[(2 * p + 1) * S:(2 * p + 2) * S, :]), LANES // 2, 1))
        z = z_ref[:, ps].astype(F32)
        o_ref[:, ps] = nn * g_ref[...] * (z * _sigmoid(z))


def _gdn_scan(parts, proj, g_row, B, S):
    n = S // GDN_KCHUNK
    nps = GDN_SCAN_PAIRS
    steps = HEADS // 2 // nps
    wz = nps * LANES
    seq = pl.BlockSpec((2 * nps * S, LANES), lambda i: (i, 0))
    return pl.pallas_call(
        functools.partial(_gdn_scan_kernel, n=n), grid=(B * steps,),
        in_specs=[seq] * 5 + [pl.BlockSpec((2 * nps * n * SUBLANES, LANES), lambda i: (i, 0)),
                              pl.BlockSpec((S, wz), lambda i: (i // steps, C_GZ // wz + i % steps)),
                              pl.BlockSpec((1, LANES), lambda i: (0, 0))],
        out_specs=pl.BlockSpec((S, wz), lambda i: (i // steps, i % steps)),
        out_shape=jax.ShapeDtypeStruct((B * S, BRANCH_WIDTH), F32),
        scratch_shapes=[pltpu.VMEM((2 * nps * S, LANES), F32)],
        compiler_params=_cp("parallel"))(*parts, proj, g_row)


def _merge_out_kernel(g0_ref, g1_ref, g2_ref, gb_ref, a_ref, b_ref, c_ref, wb_ref, wo_ref, h_ref, lg_ref, lb_ref,
                      o_ref, ot_ref):
    acc = None
    for n, (gl, br) in enumerate(((g0_ref, a_ref), (g1_ref, b_ref), (g2_ref, c_ref))):
        y = _sigmoid(gl[...].astype(F32) + gb_ref[n:n + 1, :]) * _dot(br[...], wb_ref[n])
        acc = y if acc is None else acc + y
    y = DEEPNORM_ALPHA * h_ref[...] + _dot(acc, wo_ref[...])
    y = _ln_rows(y, lg_ref[...], lb_ref[...])
    o_ref[...] = y
    ot_ref[...] = y.T.astype(BF16)


def _merge_out_ln(proj, gate_bias, o_a, o_b, o_c, w_branch, w_out, h, g, b, tm=512):
    T, D = h.shape
    row = lambda i: (i, 0)
    gspec = lambda n: pl.BlockSpec((tm, D), lambda i: (i, n))
    bspec = pl.BlockSpec((tm, BRANCH_WIDTH), row)
    vec = pl.BlockSpec((1, D), lambda i: (0, 0))
    return pl.pallas_call(
        _merge_out_kernel, grid=(T // tm,),
        in_specs=[gspec(0), gspec(1), gspec(2), pl.BlockSpec((3, D), lambda i: (0, 0)), bspec, bspec, bspec,
                  pl.BlockSpec((3, BRANCH_WIDTH, D), lambda i: (0, 0, 0)), pl.BlockSpec((D, D), lambda i: (0, 0)),
                  pl.BlockSpec((tm, D), row), vec, vec],
        out_specs=[pl.BlockSpec((tm, D), row), pl.BlockSpec((D, tm), lambda i: (0, i))],
        out_shape=[jax.ShapeDtypeStruct((T, D), F32), jax.ShapeDtypeStruct((D, T), BF16)],
        compiler_params=_cp("parallel"))(proj, proj, proj, gate_bias, o_a, o_b, o_c, w_branch, w_out, h,
                                         g.reshape(1, D), b.reshape(1, D))


def _sort_network(n):
    pairs, p = [], 1
    while p < n:
        k = p
        while k >= 1:
            for j in range(k % p, n - k, 2 * k):
                for i in range(min(k, n - j - k)):
                    if (i + j) // (2 * p) == (i + j + k) // (2 * p):
                        pairs.append((i + j, i + j + k))
            k //= 2
        p *= 2
    return pairs


def _top_of_sorted_lists(scores, n, on_max):
    lists = []
    for s in scores:
        nt = s.shape[0] // SUBLANES
        rows = [s[t * SUBLANES:(t + 1) * SUBLANES, :] for t in range(nt)]
        rows += [None] * ((1 << (nt - 1).bit_length()) - nt)
        for a, b in _sort_network(len(rows)):
            if rows[b] is None:
                continue
            if rows[a] is None:
                rows[a], rows[b] = rows[b], None
            else:
                rows[a], rows[b] = jnp.maximum(rows[a], rows[b]), jnp.minimum(rows[a], rows[b])
        lists.append(rows[:nt] + [None])
    for k in range(n):
        for i, rows in enumerate(lists):
            m = jnp.max(rows[0], axis=0, keepdims=True)
            on_max(i, k, m)
            hit = rows[0] >= m
            for t in range(min(n - 1 - k, len(rows) - 1)):
                nxt = -jnp.inf if rows[t + 1] is None else rows[t + 1]
                rows[t] = jnp.where(hit, nxt, rows[t])


def _peer_score_kernel(qt_ref, keys_ref, s2_ref, e2_ref, c_ref, e1_ref, top_sc, cand_sc):
    K = PEER_TOPK
    tt = qt_ref.shape[1]
    nh = PEER_SCORE_HEADS_PER_TRIP

    def heads(hp, carry):
        s = []
        for i in range(2 * nh):
            r = pl.multiple_of((hp * nh * 2 + i) * PEER_HALF, PEER_HALF)
            s.append(_dot(keys_ref[hp * nh * 2 + i], qt_ref[pl.ds(r, PEER_HALF), :]))

        def put(i, k, m):
            top_sc[i, k:k + 1, :] = m
        _top_of_sorted_lists(s, K + 1, put)
        r8 = lax.broadcasted_iota(jnp.int32, (SUBLANES, tt), 0)
        v1max, v2max = [], []
        for d in range(nh):
            v1a, v1x = top_sc[2 * d, 0:K, :], top_sc[2 * d, K:K + 1, :]
            v2a, v2x = top_sc[2 * d + 1, 0:K, :], top_sc[2 * d + 1, K:K + 1, :]
            v1max.append(v1a[0:1])
            v2max.append(v2a[0:1])
            cand_sc[d, 0:K, :] = v1a + v2a[0:1]
            for b in range(1, SUBLANES):
                cand_sc[d, K + SUBLANES * (b - 1):K + SUBLANES * b, :] = v1a[0:SUBLANES] + v2a[b:b + 1]
            base = K + SUBLANES * (SUBLANES - 1)
            cand_sc[d, base:base + SUBLANES, :] = v1a[0:1] + v2a[SUBLANES:K]
            cand_sc[d, base + SUBLANES:base + 2 * SUBLANES, :] = jnp.where(
                r8 == 0, v1x + v2a[0:1], jnp.where(r8 == 1, v1a[0:1] + v2x, -jnp.inf))
        st = [{"z": jnp.zeros((1, tt), F32)} for _ in range(nh)]

        def acc(d, k, m):
            if k == 0:
                st[d]["top"] = m
            if k < K:
                st[d]["z"] = st[d]["z"] + jnp.exp(m - st[d]["top"])
            if k == K - 1:
                st[d]["t16"] = m
            if k == K:
                st[d]["t17"] = m
        _top_of_sorted_lists([cand_sc[d] for d in range(nh)], K + 1, acc)
        for d in range(nh):
            tau = 0.5 * (st[d]["t16"] + st[d]["t17"])
            ro = pl.ds(pl.multiple_of((hp * nh + d) * N_KEYS, N_KEYS), N_KEYS)
            s2_ref[ro, :] = s[2 * d + 1]
            e2_ref[ro, :] = jnp.exp(s[2 * d + 1] - v2max[d]) / st[d]["z"]
            c_ref[ro, :] = tau - s[2 * d]
            e1_ref[ro, :] = jnp.exp(s[2 * d] - v1max[d])
        return carry

    lax.fori_loop(0, HEADS // nh, heads, 0)


def _peer_scores(qt, keys):
    T = qt.shape[1]
    tt = PEER_SCORE_TT
    R = HEADS * N_KEYS
    ncand = PEER_TOPK + SUBLANES * (SUBLANES + 1)
    ospec = pl.BlockSpec((R, tt), lambda i: (0, i))
    return pl.pallas_call(
        _peer_score_kernel, grid=(T // tt,),
        in_specs=[pl.BlockSpec((2 * R, tt), lambda i: (0, i)),
                  pl.BlockSpec((2 * HEADS, N_KEYS, PEER_HALF), lambda i: (0, 0, 0))],
        out_specs=[ospec] * 4,
        out_shape=[jax.ShapeDtypeStruct((R, T), F32)] * 4,
        scratch_shapes=[pltpu.VMEM((2 * PEER_SCORE_HEADS_PER_TRIP, 3 * SUBLANES, tt), F32),
                        pltpu.VMEM((PEER_SCORE_HEADS_PER_TRIP, ncand, tt), F32)],
        compiler_params=_cp("parallel"))(qt, keys)


def _peer_expert_kernel(ht_ref, u_ref, vtp_ref, vtl_ref, s2_ref, e2_ref, c_ref, e1_ref, h_ref, g_ref, b_ref,
                        o_ref, acc_sc, act_sc, hw_sc, bc_sc):
    j = pl.program_id(1)
    _, eb, tt = hw_sc.shape
    ng = eb // N_KEYS
    assert c_ref.shape == (HEADS, ng, tt)
    slot = j % 2

    @pl.when(j == 0)
    def _():
        acc_sc[...] = jnp.zeros(acc_sc.shape, F32)
        hw_sc[1] = jnp.zeros(hw_sc.shape[1:], BF16)

    mxu_w = 2 * LANES
    for half in range(tt // mxu_w):
        hl = slice(half * mxu_w, (half + 1) * mxu_w)
        a = jnp.dot(u_ref[...], ht_ref[:, hl], preferred_element_type=F32)
        act_sc[:, hl] = 0.5 * a * (1.0 + lax.erf(a * (2.0 ** -0.5)))
        acc_sc[:, hl] += jnp.dot(vtp_ref[...], hw_sc[1 - slot, :, hl], preferred_element_type=F32)
        for lc in range(half * (mxu_w // LANES), (half + 1) * (mxu_w // LANES)):
            ls = slice(lc * LANES, (lc + 1) * LANES)
            for h in range(HEADS):
                thr, e1 = c_ref[h, :, ls], e1_ref[h, :, ls]
                for g in range(ng):
                    bc_sc[0, h * ng + g] = jnp.broadcast_to(thr[g:g + 1], (SUBLANES, LANES))
                    bc_sc[1, h * ng + g] = jnp.broadcast_to(e1[g:g + 1], (SUBLANES, LANES))
            for g in range(ng):
                wsum = jnp.zeros((N_KEYS // SUBLANES, SUBLANES, LANES), F32)
                for h in range(HEADS):
                    hr = slice(h * N_KEYS, (h + 1) * N_KEYS)
                    s2 = s2_ref[hr, ls].reshape(N_KEYS // SUBLANES, SUBLANES, LANES)
                    e2 = e2_ref[hr, ls].reshape(N_KEYS // SUBLANES, SUBLANES, LANES)
                    wsum = wsum + jnp.where(s2 >= bc_sc[0, h * ng + g][None], e2 * bc_sc[1, h * ng + g][None], 0.0)
                gs = slice(g * N_KEYS, (g + 1) * N_KEYS)
                hw_sc[slot, gs, ls] = (wsum.reshape(N_KEYS, LANES) * act_sc[gs, ls]).astype(BF16)

    @pl.when(j == pl.num_programs(1) - 1)
    def _():
        acc = acc_sc[...] + jnp.dot(vtl_ref[...], hw_sc[slot], preferred_element_type=F32)
        y = DEEPNORM_ALPHA * h_ref[...] + acc.T
        o_ref[...] = _ln_rows(y, g_ref[...], b_ref[...])


def _peer_experts(ht, u, vt, s2, e2, c, e1, h, g, b):
    T, D = h.shape
    tt, eb = PEER_TT, PEER_EB
    R = HEADS * N_KEYS
    nb = N_EXPERTS // eb
    assert nb % 2 == 0
    ng = eb // N_KEYS
    sspec = pl.BlockSpec((R, tt), lambda i, j: (0, i))
    gspec = pl.BlockSpec((HEADS, ng, tt), lambda i, j: (0, j, i))
    vec = pl.BlockSpec((1, D), lambda i, j: (0, 0))
    return pl.pallas_call(
        _peer_expert_kernel, grid=(T // tt, nb),
        in_specs=[pl.BlockSpec((D, tt), lambda i, j: (0, i)),
                  pl.BlockSpec((eb, D), lambda i, j: (j, 0)),
                  pl.BlockSpec((D, eb), lambda i, j: (0, jnp.maximum(j - 1, 0))),
                  pl.BlockSpec((D, eb), lambda i, j: (0, nb - 1)),
                  sspec, sspec, gspec, gspec,
                  pl.BlockSpec((tt, D), lambda i, j: (i, 0)), vec, vec],
        out_specs=pl.BlockSpec((tt, D), lambda i, j: (i, 0)),
        out_shape=jax.ShapeDtypeStruct((T, D), F32),
        scratch_shapes=[pltpu.VMEM((D, tt), F32), pltpu.VMEM((eb, tt), F32), pltpu.VMEM((2, eb, tt), BF16),
                        pltpu.VMEM((2, HEADS * ng, SUBLANES, LANES), F32)],
        compiler_params=_cp("parallel", "arbitrary"))(ht, u, vt, vt, s2, e2, c.reshape(HEADS, N_KEYS, T),
                                                      e1.reshape(HEADS, N_KEYS, T), h, g.reshape(1, D),
                                                      b.reshape(1, D))


def _rot_half(x, r):
    return jnp.concatenate([-x[..., r:2 * r], x[..., :r]], axis=-1)


def _prep_w_in(w):
    D = w.shape[0]
    splits = (MLA_Q_RANK, MLA_KV_RANK, MLA_ROPE, 3 * BRANCH_WIDTH, 3 * BRANCH_WIDTH, BRANCH_WIDTH, HEADS, HEADS,
              3 * D)
    o = np.cumsum((0,) + splits)
    cq, ckv, kr, mqkv, gqkv, gz, ga, gb, gl = [w[:, o[i]:o[i + 1]] for i in range(9)]
    mq, mk, mv = jnp.split(mqkv, 3, axis=1)

    def slab(m):
        m = m.reshape(D, HEADS, MOBA_HEAD_DIM)
        pad = jnp.zeros((D, HEADS, LANES - MOBA_HEAD_DIM - MOBA_ROT_DIM), w.dtype)
        return jnp.concatenate([m, _rot_half(m, MOBA_ROT_DIM // 2), pad], axis=-1).reshape(D, HEADS * LANES)

    small = jnp.concatenate([ga, gb, jnp.zeros((D, LANES // 2 - 2 * HEADS), w.dtype), kr,
                             _rot_half(kr, MLA_ROPE // 2)], axis=1)
    out = jnp.concatenate([gl, cq, ckv, slab(mq), slab(mk), mv, gqkv, gz], axis=1)
    assert out.shape[1] == C_TOTAL and C_TOTAL % PROJ_TN == 0
    return out.astype(BF16), small.astype(BF16)


def _prep_w_uq(w):
    R = w.shape[0]
    w = w.reshape(R, HEADS, MLA_NOPE + MLA_ROPE)
    rope = w[..., MLA_NOPE:]
    return jnp.concatenate([w, _rot_half(rope, MLA_ROPE // 2)], axis=-1).reshape(R, HEADS * LANES).astype(BF16)


def _prep_w_ukv(w):
    R = w.shape[0]
    w = w.reshape(R, HEADS, MLA_NOPE + MLA_V)
    k = jnp.concatenate([w[..., :MLA_NOPE], jnp.zeros((R, HEADS, LANES - MLA_NOPE), w.dtype)], axis=-1)
    return jnp.concatenate([k.reshape(R, HEADS * LANES), w[..., MLA_NOPE:].reshape(R, HEADS * MLA_V)],
                           axis=1).astype(BF16)


def _rope_tables(positions):
    pos = positions.reshape(-1).astype(F32)[:, None]
    T = pos.shape[0]

    def cs(rot):
        inv = ROPE_THETA ** (-jnp.arange(0, rot, 2, dtype=F32) / rot)
        ang = pos * inv
        return jnp.cos(ang), jnp.sin(ang)

    ca, sa = cs(MLA_ROPE)
    cb, sb = cs(MOBA_ROT_DIM)
    one = lambda n: jnp.ones((T, n), F32)
    zero = lambda n: jnp.zeros((T, n), F32)
    sc_a = (MLA_NOPE + MLA_ROPE) ** -0.5 * math.log2(math.e)
    sc_b = MOBA_HEAD_DIM ** -0.5 * math.log2(math.e)
    cat = lambda *xs: jnp.concatenate(xs, axis=1)
    mla = (cat(one(MLA_NOPE), ca, ca, zero(32)) * sc_a, cat(zero(MLA_NOPE), sa, sa, zero(32)) * sc_a,
           cat(zero(MLA_NOPE), ca, ca, zero(32)), cat(zero(MLA_NOPE), sa, sa, zero(32)))
    cm = cat(cb, cb, one(MOBA_HEAD_DIM - MOBA_ROT_DIM), zero(LANES - MOBA_HEAD_DIM))
    sm = cat(sb, sb, zero(LANES - MOBA_ROT_DIM))
    moba = (cm * sc_b, sm * sc_b, cm, sm)
    return mla, moba


def _lane_row(v):
    return jnp.concatenate([v.astype(F32), jnp.zeros((LANES - v.shape[0],), F32)]).reshape(1, LANES)


def kernel(x, positions, ln_in_g, ln_in_b, w_in, mla_q_norm, mla_kv_norm, mla_w_uq, mla_w_ukv, gdn_conv_w, gdn_A_log, gdn_dt_bias, gdn_o_norm, gate_bias, w_branch, w_out, ln1_g, ln1_b, peer_w_q, peer_sub_keys, peer_u, peer_v, ln2_g, ln2_b):
    B, S, D = x.shape
    T = B * S
    assert S % ATT_TQ == 0 and S % GDN_KCHUNK == 0 and T % PEER_TT == 0 and ATT_TQ == MOBA_BLOCK
    mla_tabs, moba_tabs = _rope_tables(positions)
    h = _layer_norm(x.reshape(T, D), ln_in_g, ln_in_b)
    for l in range(DEPTH):
        w_main, w_small = _prep_w_in(w_in[l])
        proj = _mm(h, w_main, tm=512, tn=PROJ_TN, out_dtype=BF16)
        small = _mm(h, w_small, tm=512, tn=LANES)
        qa, ka, va = _mla_qkv(proj, small, mla_q_norm[l], mla_kv_norm[l], _prep_w_uq(mla_w_uq[l]),
                              _prep_w_ukv(mla_w_ukv[l]), mla_tabs)
        o_a = _attention(qa, ka, va, 0, B, S)
        qm, km, kmean = _moba_prep(proj, moba_tabs)
        o_b = _attention(qm, km, proj, C_MV, B, S, kmean=kmean.reshape(B, S // MOBA_BLOCK, HEADS * LANES))
        qkv = _gdn_conv(proj, gdn_conv_w[l], B, S)
        parts = _gdn_chunks(qkv, small, _lane_row(gdn_A_log[l]), _lane_row(gdn_dt_bias[l]), B, S)
        o_c = _gdn_scan(parts, proj, jnp.tile(gdn_o_norm[l], 2).reshape(1, LANES), B, S)
        h, ht = _merge_out_ln(proj, gate_bias[l], o_a, o_b, o_c, w_branch[l].astype(BF16), w_out[l].astype(BF16),
                              h, ln1_g[l], ln1_b[l])
        qt = _mm(peer_w_q[l].T.astype(BF16), ht, tm=1024, tn=min(1024, T), out_dtype=BF16)
        keys = peer_sub_keys[l].reshape(2 * HEADS, N_KEYS, PEER_HALF).astype(BF16)
        s2, e2, c, e1 = _peer_scores(qt, keys)
        h = _peer_experts(ht, peer_u[l].astype(BF16), peer_v[l].astype(BF16).T, s2, e2, c, e1, h, ln2_g[l], ln2_b[l])
    return h.reshape(B, S, D)
```

```python
import functools
import math

import numpy as np
import jax
import jax.numpy as jnp
from jax import lax
from jax.experimental import pallas as pl
from jax.experimental.pallas import tpu as pltpu

F32 = jnp.float32
BF16 = jnp.bfloat16
HIGHEST = lax.Precision.HIGHEST

DEPTH = 2
ROPE_THETA = 500000.0
NEG_INF = -1e30
LN_EPS = 1e-5
RMS_EPS = 1e-6
DEEPNORM_ALPHA = (2 * DEPTH) ** 0.25
HEADS = 8
MLA_NOPE, MLA_ROPE, MLA_V = 64, 32, 64
MLA_Q_RANK, MLA_KV_RANK = 768, 256
MOBA_HEAD_DIM, MOBA_ROT_DIM, MOBA_BLOCK, MOBA_TOPK = 64, 16, 256, 3
GDN_HEAD_DIM, GDN_CONV = 64, 4
BRANCH_WIDTH = 512
N_KEYS, PEER_TOPK, PEER_HALF = 128, 16, 128
N_EXPERTS = N_KEYS * N_KEYS
MLA_QSCALE = (MLA_NOPE + MLA_ROPE) ** -0.5 * math.log2(math.e)
MOBA_QSCALE = MOBA_HEAD_DIM ** -0.5 * math.log2(math.e)

LANES = 128
SUBLANES = 8
VMEM_LIMIT = 56 * 1024 * 1024

GDN_KCHUNK = 128
GDN_HEADS_PER_STEP = 8
GDN_CHUNKS_PER_STEP = 2
GDN_SCAN_PAIRS = 2
ATT_TQ = 256
PEER_TT = 512
PEER_EB = 1024
PEER_SCORE_TT = 256
PEER_SCORE_HEADS_PER_TRIP = 4

C_GATE, C_CQ, C_CKV, C_MQ, C_MK, C_MV, C_GQKV, C_GZ, C_TOTAL = (
    0, 3072, 3840, 4096, 5120, 6144, 6656, 8192, 8704)
PROJ_TN = 4352


def _cp(*sem):
    return pltpu.CompilerParams(dimension_semantics=sem, vmem_limit_bytes=VMEM_LIMIT)


def _dot(a, b):
    return jnp.dot(a.astype(BF16), b.astype(BF16), preferred_element_type=F32)


def _dot_nt(a, b):
    return lax.dot_general(a.astype(BF16), b.astype(BF16), (((1,), (1,)), ((), ())),
                           preferred_element_type=F32)


def _ln_rows(y, g, b):
    mu = jnp.mean(y, axis=-1, keepdims=True)
    d = y - mu
    var = jnp.mean(d * d, axis=-1, keepdims=True)
    return d * lax.rsqrt(var + LN_EPS) * g + b


def _sigmoid(x):
    return 1.0 / (1.0 + jnp.exp(-x))


def _ln_kernel(x_ref, g_ref, b_ref, o_ref):
    o_ref[...] = _ln_rows(x_ref[...], g_ref[...], b_ref[...])


def _layer_norm(x, g, b, tm=512):
    T, D = x.shape
    return pl.pallas_call(
        _ln_kernel, grid=(T // tm,),
        in_specs=[pl.BlockSpec((tm, D), lambda i: (i, 0)),
                  pl.BlockSpec((1, D), lambda i: (0, 0)),
                  pl.BlockSpec((1, D), lambda i: (0, 0))],
        out_specs=pl.BlockSpec((tm, D), lambda i: (i, 0)),
        out_shape=jax.ShapeDtypeStruct((T, D), F32),
        compiler_params=_cp("parallel"))(x, g.reshape(1, D), b.reshape(1, D))


def _mm_kernel(x_ref, w_ref, o_ref):
    o_ref[...] = _dot(x_ref[...], w_ref[...]).astype(o_ref.dtype)


def _mm(x, w, tm, tn, out_dtype=F32):
    M, K = x.shape
    N = w.shape[1]
    return pl.pallas_call(
        _mm_kernel, grid=(N // tn, M // tm),
        in_specs=[pl.BlockSpec((tm, K), lambda j, i: (i, 0)),
                  pl.BlockSpec((K, tn), lambda j, i: (0, j))],
        out_specs=pl.BlockSpec((tm, tn), lambda j, i: (i, j)),
        out_shape=jax.ShapeDtypeStruct((M, N), out_dtype),
        compiler_params=_cp("parallel", "parallel"))(x, w)


def _rms_rows(x, g):
    return x * lax.rsqrt(jnp.mean(x * x, axis=-1, keepdims=True) + RMS_EPS) * g


def _head_slabs(x):
    return [x[:, h * LANES:(h + 1) * LANES] for h in range(HEADS)]


def _mla_qkv_kernel(cq_ref, ckv_ref, gq_ref, gkv_ref, wq_ref, wkv_ref, sm_ref, ck_tab, sk_tab,
                    qo_ref, ko_ref, vo_ref):
    W = HEADS * LANES
    q = _dot(_rms_rows(cq_ref[...].astype(F32), gq_ref[...]), wq_ref[...])
    kv = _dot(_rms_rows(ckv_ref[...].astype(F32), gkv_ref[...]), wkv_ref[...])
    vo_ref[...] = kv[:, W:].astype(BF16)
    ck, sk = ck_tab[...], sk_tab[...]
    lane = lax.broadcasted_iota(jnp.int32, ck.shape, 1)
    cq = jnp.where(lane < MLA_NOPE, MLA_QSCALE, ck * MLA_QSCALE)
    sq = sk * MLA_QSCALE
    sm = sm_ref[...]
    kr = sm * ck + pltpu.roll(sm, LANES - MLA_ROPE, 1) * sk
    for h in range(HEADS):
        sl = slice(h * LANES, (h + 1) * LANES)
        qs = q[:, sl]
        qo_ref[:, sl] = (qs * cq + pltpu.roll(qs, LANES - MLA_ROPE, 1) * sq).astype(BF16)
        ko_ref[:, sl] = (kv[:, sl] + kr).astype(BF16)


def _mla_qkv(proj, small, gq, gkv, wq, wkv, tabs, tm=512):
    T = proj.shape[0]
    W = HEADS * LANES
    row = lambda i: (i, 0)
    tab = pl.BlockSpec((tm, LANES), row)
    full = lambda a: pl.BlockSpec(a.shape, lambda i: (0, 0))
    gq, gkv = gq.reshape(1, -1), gkv.reshape(1, -1)
    return pl.pallas_call(
        _mla_qkv_kernel, grid=(T // tm,),
        in_specs=[pl.BlockSpec((tm, MLA_Q_RANK), lambda i: (i, C_CQ // MLA_Q_RANK)),
                  pl.BlockSpec((tm, MLA_KV_RANK), lambda i: (i, C_CKV // MLA_KV_RANK)),
                  full(gq), full(gkv), full(wq), full(wkv), tab, tab, tab],
        out_specs=[pl.BlockSpec((tm, W), row), pl.BlockSpec((tm, W), row), pl.BlockSpec((tm, BRANCH_WIDTH), row)],
        out_shape=[jax.ShapeDtypeStruct((T, W), BF16)] * 2 + [jax.ShapeDtypeStruct((T, BRANCH_WIDTH), BF16)],
        compiler_params=_cp("parallel"))(proj, proj, gq, gkv, wq, wkv, small, *tabs)


def _moba_prep_kernel(q_ref, k_ref, ck_ref, sk_ref, qo_ref, ko_ref, km_ref):
    ck, sk = ck_ref[...], sk_ref[...]
    cq, sq = ck * MOBA_QSCALE, sk * MOBA_QSCALE
    q, k = q_ref[...].astype(F32), k_ref[...].astype(F32)
    for h, (qs, ks) in enumerate(zip(_head_slabs(q), _head_slabs(k))):
        sl = slice(h * LANES, (h + 1) * LANES)
        qo_ref[:, sl] = (qs * cq + pltpu.roll(qs, LANES // 2, 1) * sq).astype(BF16)
        kk = ks * ck + pltpu.roll(ks, LANES // 2, 1) * sk
        ko_ref[:, sl] = kk.astype(BF16)
        km_ref[0, :, sl] = jnp.mean(kk, axis=0, keepdims=True)


def _moba_prep(proj, tabs):
    T = proj.shape[0]
    W = HEADS * LANES
    tm = MOBA_BLOCK
    row = lambda i: (i, 0)
    tab = pl.BlockSpec((tm, LANES), row)
    return pl.pallas_call(
        _moba_prep_kernel, grid=(T // tm,),
        in_specs=[pl.BlockSpec((tm, W), lambda i: (i, C_MQ // W)),
                  pl.BlockSpec((tm, W), lambda i: (i, C_MK // W)), tab, tab],
        out_specs=[pl.BlockSpec((tm, W), row), pl.BlockSpec((tm, W), row),
                   pl.BlockSpec((1, 1, W), lambda i: (i, 0, 0))],
        out_shape=[jax.ShapeDtypeStruct((T, W), BF16), jax.ShapeDtypeStruct((T, W), BF16),
                   jax.ShapeDtypeStruct((T // tm, 1, W), F32)],
        compiler_params=_cp("parallel"))(proj, proj, *tabs)


def _attn_kernel(*refs, moba, nk):
    if moba:
        q_ref, k_ref, v_ref, km_ref, o_ref, vt_sc, m_sc, l_sc, al_sc, acc_sc, s_sc, p_sc, bias_sc = refs
    else:
        q_ref, k_ref, v_ref, o_ref, vt_sc, m_sc, l_sc, al_sc, acc_sc, s_sc, p_sc = refs
    tq = tk = ATT_TQ
    dv = BRANCH_WIDTH // HEADS
    qi = pl.program_id(1)
    hslab = [slice(h * LANES, (h + 1) * LANES) for h in range(HEADS)]

    @pl.when(qi == 0)
    def _():
        for j in range(nk):
            for c in range(tk // LANES):
                for g in range(BRANCH_WIDTH // LANES):
                    vt_sc[j, g * LANES:(g + 1) * LANES, c * LANES:(c + 1) * LANES] = (
                        v_ref[j * tk + c * LANES:j * tk + (c + 1) * LANES,
                              g * LANES:(g + 1) * LANES].astype(F32).T.astype(BF16))

    if moba:
        blk = lax.broadcasted_iota(jnp.int32, (SUBLANES, tq), 0)
        for h in range(HEADS):
            qf = q_ref[:, hslab[h]].astype(F32)
            km = km_ref[0, :, hslab[h]]
            if nk < SUBLANES:
                km = jnp.concatenate([km, jnp.zeros((SUBLANES - nk, LANES), F32)], axis=0)
            gate = lax.dot_general(km, qf, (((1,), (1,)), ((), ())), precision=HIGHEST,
                                   preferred_element_type=F32)
            gate = jnp.where(blk < qi, gate, -jnp.inf)
            for n in range(nk):
                gn = gate[n:n + 1, :]
                beats = jnp.where(gate > gn, 1.0, jnp.where((gate == gn) & (blk < n), 1.0, 0.0))
                cnt = jnp.sum(beats, axis=0, keepdims=True)
                bias_sc[h, n] = jnp.where((cnt < MOBA_TOPK) & (n < qi), 0.0, NEG_INF)
    m_sc[...] = jnp.full(m_sc.shape, NEG_INF, F32)
    l_sc[...] = jnp.zeros(l_sc.shape, F32)
    acc_sc[...] = jnp.zeros(acc_sc.shape, F32)

    def block(j, diag):
        rows = pl.ds(pl.multiple_of(j * tk, tk), tk)
        for h in range(HEADS):
            s = _dot_nt(k_ref[rows, hslab[h]], q_ref[:, hslab[h]])
            if diag:
                keyi = lax.broadcasted_iota(jnp.int32, (tk, tq), 0)
                qryi = lax.broadcasted_iota(jnp.int32, (tk, tq), 1)
                s = jnp.where(keyi <= qryi, s, NEG_INF)
            elif moba:
                s = s + bias_sc[h, j]
            s_sc[h] = s
        for h in range(HEADS):
            s = s_sc[h]
            m_prev = m_sc[h]
            m_new = jnp.maximum(m_prev, jnp.max(s, axis=0, keepdims=True))
            alpha = jnp.exp2(m_prev - m_new)
            p = jnp.exp2(s - m_new)
            l_sc[h] = alpha * l_sc[h] + jnp.sum(p, axis=0, keepdims=True)
            p_sc[h] = p.astype(BF16)
            al_sc[h] = alpha
            m_sc[h] = m_new
        for h in range(HEADS):
            acc_sc[h] = al_sc[h] * acc_sc[h] + jnp.dot(vt_sc[j, h * dv:(h + 1) * dv, :], p_sc[h],
                                                       preferred_element_type=F32)

    def past(j, carry):
        block(j, False)
        return carry
    lax.fori_loop(0, qi, past, 0)
    block(qi, True)
    ot = jnp.concatenate([acc_sc[h] * (1.0 / l_sc[h]) for h in range(HEADS)], axis=0)
    o_ref[...] = ot.T


def _attention(q, k, v, v_col, B, S, kmean=None):
    T = B * S
    tq = ATT_TQ
    nq = S // tq
    W = HEADS * LANES
    dv = BRANCH_WIDTH // HEADS
    moba = kmean is not None
    assert nq <= SUBLANES and v_col % BRANCH_WIDTH == 0
    in_specs = [pl.BlockSpec((tq, W), lambda b, i: (b * nq + i, 0)),
                pl.BlockSpec((S, W), lambda b, i: (b, 0)),
                pl.BlockSpec((S, BRANCH_WIDTH), lambda b, i: (b, v_col // BRANCH_WIDTH))]
    scratch = [pltpu.VMEM((nq, BRANCH_WIDTH, tq), BF16)]
    row = pltpu.VMEM((HEADS, 1, tq), F32)
    scratch += [row, row, row, pltpu.VMEM((HEADS, dv, tq), F32),
                pltpu.VMEM((HEADS, tq, tq), F32), pltpu.VMEM((HEADS, tq, tq), BF16)]
    args = [q, k, v]
    if moba:
        in_specs.append(pl.BlockSpec((1, nq, W), lambda b, i: (b, 0, 0)))
        scratch.append(pltpu.VMEM((HEADS, nq, 1, tq), F32))
        args.append(kmean)
    return pl.pallas_call(
        functools.partial(_attn_kernel, moba=moba, nk=nq),
        grid=(B, nq), in_specs=in_specs,
        out_specs=pl.BlockSpec((tq, BRANCH_WIDTH), lambda b, i: (b * nq + i, 0)),
        out_shape=jax.ShapeDtypeStruct((T, BRANCH_WIDTH), F32),
        scratch_shapes=scratch,
        compiler_params=_cp("parallel", "arbitrary"))(*args)


def _gdn_conv_kernel(x_ref, w_ref, o_ref):
    c = pl.program_id(1)
    x = x_ref[...].astype(F32)
    w = w_ref[...]
    S = x.shape[0]
    row = lax.broadcasted_iota(jnp.int32, (S, LANES), 0)
    lane = lax.broadcasted_iota(jnp.int32, (S, LANES), 1)
    y = x * w[GDN_CONV - 1:GDN_CONV, :]
    for d in range(1, GDN_CONV):
        xs = jnp.where(row >= d, pltpu.roll(x, d, 0), 0.0)
        y = y + xs * w[GDN_CONV - 1 - d:GDN_CONV - d, :]
    y = y * _sigmoid(y)
    sq = y * y
    lo = lane < GDN_HEAD_DIM
    ss0 = jnp.sum(jnp.where(lo, sq, 0.0), axis=1, keepdims=True)
    ss1 = jnp.sum(jnp.where(lo, 0.0, sq), axis=1, keepdims=True)
    inv = lax.rsqrt(jnp.where(lo, ss0, ss1) + RMS_EPS)
    nqb = BRANCH_WIDTH // LANES
    scale = jnp.where(c < nqb, GDN_HEAD_DIM ** -0.5, 1.0)
    o_ref[...] = jnp.where(c < 2 * nqb, y * inv * scale, y).astype(BF16)


def _gdn_conv(proj, conv_w, B, S):
    T = B * S
    nb = 3 * BRANCH_WIDTH // LANES
    return pl.pallas_call(
        _gdn_conv_kernel, grid=(B, nb),
        in_specs=[pl.BlockSpec((S, LANES), lambda b, c: (b, C_GQKV // LANES + c)),
                  pl.BlockSpec((GDN_CONV, LANES), lambda b, c: (0, c))],
        out_specs=pl.BlockSpec((S, LANES), lambda b, c: (b, c)),
        out_shape=jax.ShapeDtypeStruct((T, 3 * BRANCH_WIDTH), BF16),
        compiler_params=_cp("parallel", "parallel"))(proj, conv_w)


def _gdn_chunk_kernel(q_ref, k_ref, v_ref, sm_ref, alog_ref, dtb_ref,
                      u_ref, w_ref, qg_ref, qk_ref, kdt_ref, dl_ref):
    C = GDN_KCHUNK
    nh = GDN_HEADS_PER_STEP
    ncs = GDN_CHUNKS_PER_STEP
    h0 = pl.program_id(1) * nh
    lane = lax.broadcasted_iota(jnp.int32, (C, LANES), 1)
    row = lax.broadcasted_iota(jnp.int32, (C, LANES), 0)
    lo = lane < GDN_HEAD_DIM
    tril = row >= lane
    strict = row > lane
    eye = jnp.where(row == lane, 1.0, 0.0)
    ltri = tril.astype(BF16)

    def pick(ref, k, cc):
        x = ref[cc * C:(cc + 1) * C, (k // 2) * LANES:(k // 2 + 1) * LANES].astype(F32)
        return jnp.where(lo, pltpu.roll(x, LANES // 2, 1) if k % 2 else x, 0.0)

    garr, sig = [], []
    for cc in range(ncs):
        sm = sm_ref[cc * C:(cc + 1) * C, :]
        a = sm + dtb_ref[...]
        softplus = jnp.maximum(a, 0.0) + jnp.log1p(jnp.exp(-jnp.abs(a)))
        garr.append(-jnp.exp(alog_ref[...]) * softplus)
        sig.append(_sigmoid(sm))
    nc = ncs * nh
    kh, kb, gc, decay, A = [], [], [], [], []
    for i in range(nc):
        cc, k = divmod(i, nh)
        g_col = jnp.sum(jnp.where(lane == h0 + k, garr[cc], 0.0), axis=1, keepdims=True)
        beta = jnp.sum(jnp.where(lane == HEADS + h0 + k, sig[cc], 0.0), axis=1, keepdims=True)
        g1 = jnp.broadcast_to(g_col, (C, LANES))
        gsum = None
        for _ in range(3):
            gb = g1.astype(BF16)
            part = jnp.dot(ltri, gb, preferred_element_type=F32)
            gsum = part if gsum is None else gsum + part
            g1 = g1 - gb.astype(F32)
        gc.append(gsum)
        decay.append(jnp.where(tril, jnp.exp(jnp.where(tril, gsum - gsum.T, 0.0)), 0.0))
        kh.append(pick(k_ref, k, cc))
        kb.append(kh[i] * beta)
        vb = pick(v_ref, k, cc) * beta
        A.append(jnp.where(strict, _dot_nt(kb[i], kh[i]) * decay[i], 0.0))
        eg = jnp.exp(gsum)
        qh = pick(q_ref, k, cc)
        qk_ref[k, cc] = jnp.where(tril, _dot_nt(qh, kh[i]) * decay[i], 0.0).astype(BF16)
        qg_ref[k, cc] = (qh * eg).astype(BF16)
        glast = gsum[C - 1:C, :]
        kdt_ref[k, cc] = (kh[i] * jnp.exp(glast - gsum)).T.astype(BF16)
        dl_ref[k, cc] = jnp.broadcast_to(jnp.exp(glast), (SUBLANES, LANES))
        u_ref[k, cc] = vb
        w_ref[k, cc] = (kb[i] * eg).astype(BF16)
    def joiner(level):
        same = (row >> (level + 1)) == (lane >> (level + 1))
        return same & (((row >> level) & 1) == 1) & (((lane >> level) & 1) == 0)

    P = [eye - jnp.where(joiner(0), A[i], 0.0) for i in range(nc)]
    for level in range(1, int(math.log2(C))):
        msk = joiner(level)
        T1 = [_dot(P[i], jnp.where(msk, A[i], 0.0)) for i in range(nc)]
        P = [P[i] - _dot(T1[i], P[i]) for i in range(nc)]
    for i in range(nc):
        cc, k = divmod(i, nh)
        u_ref[k, cc] = _dot(P[i], u_ref[k, cc])
        w_ref[k, cc] = _dot(P[i], w_ref[k, cc]).astype(BF16)


def _gdn_chunks(qkv, small, alog_row, dtb_row, B, S):
    C = GDN_KCHUNK
    n = S // C
    nh = GDN_HEADS_PER_STEP
    ng = HEADS // nh
    wq = nh // 2 * LANES
    nqb = BRANCH_WIDTH // wq
    ncs = GDN_CHUNKS_PER_STEP
    ns = n // ncs
    big = lambda dt: jax.ShapeDtypeStruct((B * HEADS, n, C, LANES), dt)
    ospec = pl.BlockSpec((nh, ncs, C, LANES), lambda b, g, c: (b * ng + g, c, 0, 0))
    par = pl.BlockSpec((1, LANES), lambda b, g, c: (0, 0))
    outs = pl.pallas_call(
        _gdn_chunk_kernel, grid=(B, ng, ns),
        in_specs=[pl.BlockSpec((ncs * C, wq), lambda b, g, c: (b * ns + c, g)),
                  pl.BlockSpec((ncs * C, wq), lambda b, g, c: (b * ns + c, nqb + g)),
                  pl.BlockSpec((ncs * C, wq), lambda b, g, c: (b * ns + c, 2 * nqb + g)),
                  pl.BlockSpec((ncs * C, LANES), lambda b, g, c: (b * ns + c, 0)), par, par],
        out_specs=[ospec] * 5 + [pl.BlockSpec((nh, ncs, SUBLANES, LANES), lambda b, g, c: (b * ng + g, c, 0, 0))],
        out_shape=[big(F32)] + [big(BF16)] * 4 + [jax.ShapeDtypeStruct((B * HEADS, n, SUBLANES, LANES), F32)],
        compiler_params=_cp("parallel", "parallel", "parallel"))(qkv, qkv, qkv, small, alog_row, dtb_row)
    return [o.reshape(-1, LANES) for o in outs]


def _gdn_scan_kernel(u_ref, w_ref, qg_ref, qk_ref, kdt_ref, dl_ref, z_ref, g_ref, o_ref, o_sc, *, n):
    C = GDN_KCHUNK
    S = n * C
    npair = GDN_SCAN_PAIRS

    def step(c, states):
        out = []
        for k, state in enumerate(states):
            r = pl.ds(pl.multiple_of(k * S + c * C, C), C)
            sb = state.astype(BF16)
            v_new = u_ref[r, :] - _dot(w_ref[r, :], sb)
            vb = v_new.astype(BF16)
            o_sc[r, :] = _dot(qg_ref[r, :], sb) + _dot(qk_ref[r, :], vb)
            dl = dl_ref[pl.ds(pl.multiple_of((k * n + c) * SUBLANES, SUBLANES), 1), :]
            out.append(state * dl + _dot(kdt_ref[r, :], vb))
        return tuple(out)

    lax.fori_loop(0, n, step, tuple(jnp.zeros((LANES, LANES), F32) for _ in range(2 * npair)))
    lane = lax.broadcasted_iota(jnp.int32, (S, LANES), 1)

    def nrm(o):
        ms = jnp.sum(o * o, axis=1, keepdims=True) * (1.0 / GDN_HEAD_DIM)
        return o * lax.rsqrt(ms + RMS_EPS)

    for p in range(npair):
        ps = slice(p * LANES, (p + 1) * LANES)
        nn = jnp.where(lane < GDN_HEAD_DIM, nrm(o_sc[2 * p * S:(2 * p + 1) * S, :]),
                       pltpu.roll(nrm(o_sc[(2 * p + 1) * S:(2 * p + 2) * S, :]), LANES // 2, 1))
        z = z_ref[:, ps].astype(F32)
        o_ref[:, ps] = nn * g_ref[...] * (z * _sigmoid(z))


def _gdn_scan(parts, proj, g_row, B, S):
    n = S // GDN_KCHUNK
    nps = GDN_SCAN_PAIRS
    steps = HEADS // 2 // nps
    wz = nps * LANES
    seq = pl.BlockSpec((2 * nps * S, LANES), lambda i: (i, 0))
    return pl.pallas_call(
        functools.partial(_gdn_scan_kernel, n=n), grid=(B * steps,),
        in_specs=[seq] * 5 + [pl.BlockSpec((2 * nps * n * SUBLANES, LANES), lambda i: (i, 0)),
                              pl.BlockSpec((S, wz), lambda i: (i // steps, C_GZ // wz + i % steps)),
                              pl.BlockSpec((1, LANES), lambda i: (0, 0))],
        out_specs=pl.BlockSpec((S, wz), lambda i: (i // steps, i % steps)),
        out_shape=jax.ShapeDtypeStruct((B * S, BRANCH_WIDTH), F32),
        scratch_shapes=[pltpu.VMEM((2 * nps * S, LANES), F32)],
        compiler_params=_cp("parallel"))(*parts, proj, g_row)


def _merge_out_kernel(g0_ref, g1_ref, g2_ref, gb_ref, a_ref, b_ref, c_ref, wb_ref, wo_ref, h_ref, lg_ref, lb_ref,
                      o_ref, ot_ref):
    acc = None
    for n, (gl, br) in enumerate(((g0_ref, a_ref), (g1_ref, b_ref), (g2_ref, c_ref))):
        y = _sigmoid(gl[...].astype(F32) + gb_ref[n:n + 1, :]) * _dot(br[...], wb_ref[n])
        acc = y if acc is None else acc + y
    y = DEEPNORM_ALPHA * h_ref[...] + _dot(acc, wo_ref[...])
    y = _ln_rows(y, lg_ref[...], lb_ref[...])
    o_ref[...] = y
    ot_ref[...] = y.T.astype(BF16)


def _merge_out_ln(proj, gate_bias, o_a, o_b, o_c, w_branch, w_out, h, g, b, tm=512):
    T, D = h.shape
    row = lambda i: (i, 0)
    gspec = lambda n: pl.BlockSpec((tm, D), lambda i: (i, n))
    bspec = pl.BlockSpec((tm, BRANCH_WIDTH), row)
    vec = pl.BlockSpec((1, D), lambda i: (0, 0))
    return pl.pallas_call(
        _merge_out_kernel, grid=(T // tm,),
        in_specs=[gspec(0), gspec(1), gspec(2), pl.BlockSpec((3, D), lambda i: (0, 0)), bspec, bspec, bspec,
                  pl.BlockSpec((3, BRANCH_WIDTH, D), lambda i: (0, 0, 0)), pl.BlockSpec((D, D), lambda i: (0, 0)),
                  pl.BlockSpec((tm, D), row), vec, vec],
        out_specs=[pl.BlockSpec((tm, D), row), pl.BlockSpec((D, tm), lambda i: (0, i))],
        out_shape=[jax.ShapeDtypeStruct((T, D), F32), jax.ShapeDtypeStruct((D, T), BF16)],
        compiler_params=_cp("parallel"))(proj, proj, proj, gate_bias, o_a, o_b, o_c, w_branch, w_out, h,
                                         g.reshape(1, D), b.reshape(1, D))


def _sort_network(n):
    pairs, p = [], 1
    while p < n:
        k = p
        while k >= 1:
            for j in range(k % p, n - k, 2 * k):
                for i in range(min(k, n - j - k)):
                    if (i + j) // (2 * p) == (i + j + k) // (2 * p):
                        pairs.append((i + j, i + j + k))
            k //= 2
        p *= 2
    return pairs


def _top_of_sorted_lists(scores, n, on_max):
    lists = []
    for s in scores:
        nt = s.shape[0] // SUBLANES
        rows = [s[t * SUBLANES:(t + 1) * SUBLANES, :] for t in range(nt)]
        rows += [None] * ((1 << (nt - 1).bit_length()) - nt)
        for a, b in _sort_network(len(rows)):
            if rows[b] is None:
                continue
            if rows[a] is None:
                rows[a], rows[b] = rows[b], None
            else:
                rows[a], rows[b] = jnp.maximum(rows[a], rows[b]), jnp.minimum(rows[a], rows[b])
        lists.append(rows[:nt] + [None])
    for k in range(n):
        for i, rows in enumerate(lists):
            m = jnp.max(rows[0], axis=0, keepdims=True)
            on_max(i, k, m)
            hit = rows[0] >= m
            for t in range(min(n - 1 - k, len(rows) - 1)):
                nxt = -jnp.inf if rows[t + 1] is None else rows[t + 1]
                rows[t] = jnp.where(hit, nxt, rows[t])


def _peer_score_kernel(qt_ref, keys_ref, s2_ref, e2_ref, c_ref, e1_ref, top_sc, cand_sc):
    K = PEER_TOPK
    tt = qt_ref.shape[1]
    nh = PEER_SCORE_HEADS_PER_TRIP

    def heads(hp, carry):
        s = []
        for i in range(2 * nh):
            r = pl.multiple_of((hp * nh * 2 + i) * PEER_HALF, PEER_HALF)
            s.append(_dot(keys_ref[hp * nh * 2 + i], qt_ref[pl.ds(r, PEER_HALF), :]))

        def put(i, k, m):
            top_sc[i, k:k + 1, :] = m
        _top_of_sorted_lists(s, K + 1, put)
        r8 = lax.broadcasted_iota(jnp.int32, (SUBLANES, tt), 0)
        v1max, v2max = [], []
        for d in range(nh):
            v1a, v1x = top_sc[2 * d, 0:K, :], top_sc[2 * d, K:K + 1, :]
            v2a, v2x = top_sc[2 * d + 1, 0:K, :], top_sc[2 * d + 1, K:K + 1, :]
            v1max.append(v1a[0:1])
            v2max.append(v2a[0:1])
            cand_sc[d, 0:K, :] = v1a + v2a[0:1]
            for b in range(1, SUBLANES):
                cand_sc[d, K + SUBLANES * (b - 1):K + SUBLANES * b, :] = v1a[0:SUBLANES] + v2a[b:b + 1]
            base = K + SUBLANES * (SUBLANES - 1)
            cand_sc[d, base:base + SUBLANES, :] = v1a[0:1] + v2a[SUBLANES:K]
            cand_sc[d, base + SUBLANES:base + 2 * SUBLANES, :] = jnp.where(
                r8 == 0, v1x + v2a[0:1], jnp.where(r8 == 1, v1a[0:1] + v2x, -jnp.inf))
        st = [{"z": jnp.zeros((1, tt), F32)} for _ in range(nh)]

        def acc(d, k, m):
            if k == 0:
                st[d]["top"] = m
            if k < K:
                st[d]["z"] = st[d]["z"] + jnp.exp(m - st[d]["top"])
            if k == K - 1:
                st[d]["t16"] = m
            if k == K:
                st[d]["t17"] = m
        _top_of_sorted_lists([cand_sc[d] for d in range(nh)], K + 1, acc)
        for d in range(nh):
            tau = 0.5 * (st[d]["t16"] + st[d]["t17"])
            ro = pl.ds(pl.multiple_of((hp * nh + d) * N_KEYS, N_KEYS), N_KEYS)
            s2_ref[ro, :] = s[2 * d + 1]
            e2_ref[ro, :] = jnp.exp(s[2 * d + 1] - v2max[d]) / st[d]["z"]
            c_ref[ro, :] = tau - s[2 * d]
            e1_ref[ro, :] = jnp.exp(s[2 * d] - v1max[d])
        return carry

    lax.fori_loop(0, HEADS // nh, heads, 0)


def _peer_scores(qt, keys):
    T = qt.shape[1]
    tt = PEER_SCORE_TT
    R = HEADS * N_KEYS
    ncand = PEER_TOPK + SUBLANES * (SUBLANES + 1)
    ospec = pl.BlockSpec((R, tt), lambda i: (0, i))
    return pl.pallas_call(
        _peer_score_kernel, grid=(T // tt,),
        in_specs=[pl.BlockSpec((2 * R, tt), lambda i: (0, i)),
                  pl.BlockSpec((2 * HEADS, N_KEYS, PEER_HALF), lambda i: (0, 0, 0))],
        out_specs=[ospec] * 4,
        out_shape=[jax.ShapeDtypeStruct((R, T), F32)] * 4,
        scratch_shapes=[pltpu.VMEM((2 * PEER_SCORE_HEADS_PER_TRIP, 3 * SUBLANES, tt), F32),
                        pltpu.VMEM((PEER_SCORE_HEADS_PER_TRIP, ncand, tt), F32)],
        compiler_params=_cp("parallel"))(qt, keys)


def _peer_expert_kernel(ht_ref, u_ref, vtp_ref, vtl_ref, s2_ref, e2_ref, c_ref, e1_ref, h_ref, g_ref, b_ref,
                        o_ref, acc_sc, act_sc, hw_sc, bc_sc):
    j = pl.program_id(1)
    _, eb, tt = hw_sc.shape
    ng = eb // N_KEYS
    assert c_ref.shape == (HEADS, ng, tt)
    slot = j % 2

    @pl.when(j == 0)
    def _():
        acc_sc[...] = jnp.zeros(acc_sc.shape, F32)
        hw_sc[1] = jnp.zeros(hw_sc.shape[1:], BF16)

    mxu_w = 2 * LANES
    for half in range(tt // mxu_w):
        hl = slice(half * mxu_w, (half + 1) * mxu_w)
        a = jnp.dot(u_ref[...], ht_ref[:, hl], preferred_element_type=F32)
        act_sc[:, hl] = 0.5 * a * (1.0 + lax.erf(a * (2.0 ** -0.5)))
        acc_sc[:, hl] += jnp.dot(vtp_ref[...], hw_sc[1 - slot, :, hl], preferred_element_type=F32)
        for lc in range(half * (mxu_w // LANES), (half + 1) * (mxu_w // LANES)):
            ls = slice(lc * LANES, (lc + 1) * LANES)
            for h in range(HEADS):
                thr, e1 = c_ref[h, :, ls], e1_ref[h, :, ls]
                for g in range(ng):
                    bc_sc[0, h * ng + g] = jnp.broadcast_to(thr[g:g + 1], (SUBLANES, LANES))
                    bc_sc[1, h * ng + g] = jnp.broadcast_to(e1[g:g + 1], (SUBLANES, LANES))
            for g in range(ng):
                wsum = jnp.zeros((N_KEYS // SUBLANES, SUBLANES, LANES), F32)
                for h in range(HEADS):
                    hr = slice(h * N_KEYS, (h + 1) * N_KEYS)
                    s2 = s2_ref[hr, ls].reshape(N_KEYS // SUBLANES, SUBLANES, LANES)
                    e2 = e2_ref[hr, ls].reshape(N_KEYS // SUBLANES, SUBLANES, LANES)
                    wsum = wsum + jnp.where(s2 >= bc_sc[0, h * ng + g][None], e2 * bc_sc[1, h * ng + g][None], 0.0)
                gs = slice(g * N_KEYS, (g + 1) * N_KEYS)
                hw_sc[slot, gs, ls] = (wsum.reshape(N_KEYS, LANES) * act_sc[gs, ls]).astype(BF16)

    @pl.when(j == pl.num_programs(1) - 1)
    def _():
        acc = acc_sc[...] + jnp.dot(vtl_ref[...], hw_sc[slot], preferred_element_type=F32)
        y = DEEPNORM_ALPHA * h_ref[...] + acc.T
        o_ref[...] = _ln_rows(y, g_ref[...], b_ref[...])


def _peer_experts(ht, u, vt, s2, e2, c, e1, h, g, b):
    T, D = h.shape
    tt, eb = PEER_TT, PEER_EB
    R = HEADS * N_KEYS
    nb = N_EXPERTS // eb
    assert nb % 2 == 0
    ng = eb // N_KEYS
    sspec = pl.BlockSpec((R, tt), lambda i, j: (0, i))
    gspec = pl.BlockSpec((HEADS, ng, tt), lambda i, j: (0, j, i))
    vec = pl.BlockSpec((1, D), lambda i, j: (0, 0))
    return pl.pallas_call(
        _peer_expert_kernel, grid=(T // tt, nb),
        in_specs=[pl.BlockSpec((D, tt), lambda i, j: (0, i)),
                  pl.BlockSpec((eb, D), lambda i, j: (j, 0)),
                  pl.BlockSpec((D, eb), lambda i, j: (0, jnp.maximum(j - 1, 0))),
                  pl.BlockSpec((D, eb), lambda i, j: (0, nb - 1)),
                  sspec, sspec, gspec, gspec,
                  pl.BlockSpec((tt, D), lambda i, j: (i, 0)), vec, vec],
        out_specs=pl.BlockSpec((tt, D), lambda i, j: (i, 0)),
        out_shape=jax.ShapeDtypeStruct((T, D), F32),
        scratch_shapes=[pltpu.VMEM((D, tt), F32), pltpu.VMEM((eb, tt), F32), pltpu.VMEM((2, eb, tt), BF16),
                        pltpu.VMEM((2, HEADS * ng, SUBLANES, LANES), F32)],
        compiler_params=_cp("parallel", "arbitrary"))(ht, u, vt, vt, s2, e2, c.reshape(HEADS, N_KEYS, T),
                                                      e1.reshape(HEADS, N_KEYS, T), h, g.reshape(1, D),
                                                      b.reshape(1, D))


def _rot_half(x, r):
    return jnp.concatenate([-x[..., r:2 * r], x[..., :r]], axis=-1)


def _prep_w_in(w):
    D = w.shape[0]
    splits = (MLA_Q_RANK, MLA_KV_RANK, MLA_ROPE, 3 * BRANCH_WIDTH, 3 * BRANCH_WIDTH, BRANCH_WIDTH, HEADS, HEADS,
              3 * D)
    o = np.cumsum((0,) + splits)
    cq, ckv, kr, mqkv, gqkv, gz, ga, gb, gl = [w[:, o[i]:o[i + 1]] for i in range(9)]
    mq, mk, mv = jnp.split(mqkv, 3, axis=1)

    def slab(m):
        m = m.reshape(D, HEADS, MOBA_HEAD_DIM)
        pad = jnp.zeros((D, HEADS, LANES - MOBA_HEAD_DIM - MOBA_ROT_DIM), w.dtype)
        return jnp.concatenate([m, _rot_half(m, MOBA_ROT_DIM // 2), pad], axis=-1).reshape(D, HEADS * LANES)

    small = jnp.concatenate([ga, gb, jnp.zeros((D, LANES // 2 - 2 * HEADS), w.dtype), kr,
                             _rot_half(kr, MLA_ROPE // 2)], axis=1)
    out = jnp.concatenate([gl, cq, ckv, slab(mq), slab(mk), mv, gqkv, gz], axis=1)
    assert out.shape[1] == C_TOTAL and C_TOTAL % PROJ_TN == 0
    return out.astype(BF16), small.astype(BF16)


def _prep_w_uq(w):
    R = w.shape[0]
    w = w.reshape(R, HEADS, MLA_NOPE + MLA_ROPE)
    rope = w[..., MLA_NOPE:]
    return jnp.concatenate([w, _rot_half(rope, MLA_ROPE // 2)], axis=-1).reshape(R, HEADS * LANES).astype(BF16)


def _prep_w_ukv(w):
    R = w.shape[0]
    w = w.reshape(R, HEADS, MLA_NOPE + MLA_V)
    k = jnp.concatenate([w[..., :MLA_NOPE], jnp.zeros((R, HEADS, LANES - MLA_NOPE), w.dtype)], axis=-1)
    return jnp.concatenate([k.reshape(R, HEADS * LANES), w[..., MLA_NOPE:].reshape(R, HEADS * MLA_V)],
                           axis=1).astype(BF16)


def _rope_tables(positions):
    pos = positions.reshape(-1).astype(F32)[:, None]
    T = pos.shape[0]

    def cs(rot):
        inv = ROPE_THETA ** (-jnp.arange(0, rot, 2, dtype=F32) / rot)
        ang = pos * inv
        return jnp.cos(ang), jnp.sin(ang)

    ca, sa = cs(MLA_ROPE)
    cb, sb = cs(MOBA_ROT_DIM)
    one = lambda n: jnp.ones((T, n), F32)
    zero = lambda n: jnp.zeros((T, n), F32)
    cat = lambda *xs: jnp.concatenate(xs, axis=1)
    pad_a = LANES - MLA_NOPE - MLA_ROPE
    mla = (cat(zero(MLA_NOPE), ca, ca, zero(pad_a)), cat(zero(MLA_NOPE), sa, sa, zero(pad_a)))
    moba = (cat(cb, cb, one(MOBA_HEAD_DIM - MOBA_ROT_DIM), zero(LANES - MOBA_HEAD_DIM)),
            cat(sb, sb, zero(LANES - MOBA_ROT_DIM)))
    return mla, moba


def _lane_row(v):
    return jnp.concatenate([v.astype(F32), jnp.zeros((LANES - v.shape[0],), F32)]).reshape(1, LANES)


def kernel(x, positions, ln_in_g, ln_in_b, w_in, mla_q_norm, mla_kv_norm, mla_w_uq, mla_w_ukv, gdn_conv_w, gdn_A_log, gdn_dt_bias, gdn_o_norm, gate_bias, w_branch, w_out, ln1_g, ln1_b, peer_w_q, peer_sub_keys, peer_u, peer_v, ln2_g, ln2_b):
    B, S, D = x.shape
    T = B * S
    assert S % ATT_TQ == 0 and S % GDN_KCHUNK == 0 and T % PEER_TT == 0 and ATT_TQ == MOBA_BLOCK
    mla_tabs, moba_tabs = _rope_tables(positions)
    h = _layer_norm(x.reshape(T, D), ln_in_g, ln_in_b)
    for l in range(DEPTH):
        w_main, w_small = _prep_w_in(w_in[l])
        proj = _mm(h, w_main, tm=512, tn=PROJ_TN, out_dtype=BF16)
        small = _mm(h, w_small, tm=512, tn=LANES)
        qa, ka, va = _mla_qkv(proj, small, mla_q_norm[l], mla_kv_norm[l], _prep_w_uq(mla_w_uq[l]),
                              _prep_w_ukv(mla_w_ukv[l]), mla_tabs)
        o_a = _attention(qa, ka, va, 0, B, S)
        qm, km, kmean = _moba_prep(proj, moba_tabs)
        o_b = _attention(qm, km, proj, C_MV, B, S, kmean=kmean.reshape(B, S // MOBA_BLOCK, HEADS * LANES))
        qkv = _gdn_conv(proj, gdn_conv_w[l], B, S)
        parts = _gdn_chunks(qkv, small, _lane_row(gdn_A_log[l]), _lane_row(gdn_dt_bias[l]), B, S)
        o_c = _gdn_scan(parts, proj, jnp.tile(gdn_o_norm[l], 2).reshape(1, LANES), B, S)
        h, ht = _merge_out_ln(proj, gate_bias[l], o_a, o_b, o_c, w_branch[l].astype(BF16), w_out[l].astype(BF16),
                              h, ln1_g[l], ln1_b[l])
        qt = _mm(peer_w_q[l].T.astype(BF16), ht, tm=1024, tn=min(1024, T), out_dtype=BF16)
        keys = peer_sub_keys[l].reshape(2 * HEADS, N_KEYS, PEER_HALF).astype(BF16)
        s2, e2, c, e1 = _peer_scores(qt, keys)
        h = _peer_experts(ht, peer_u[l].astype(BF16), peer_v[l].astype(BF16).T, s2, e2, c, e1, h, ln2_g[l], ln2_b[l])
    return h.reshape(B, S, D)
```

```python
import functools
import math

import numpy as np
import jax
import jax.numpy as jnp
from jax import lax
from jax.experimental import pallas as pl
from jax.experimental.pallas import tpu as pltpu

F32 = jnp.float32
BF16 = jnp.bfloat16
HIGHEST = lax.Precision.HIGHEST

DEPTH = 2
ROPE_THETA = 500000.0
NEG_INF = -1e30
LN_EPS = 1e-5
RMS_EPS = 1e-6
DEEPNORM_ALPHA = (2 * DEPTH) ** 0.25
HEADS = 8
MLA_NOPE, MLA_ROPE, MLA_V = 64, 32, 64
MLA_Q_RANK, MLA_KV_RANK = 768, 256
MOBA_HEAD_DIM, MOBA_ROT_DIM, MOBA_BLOCK, MOBA_TOPK = 64, 16, 256, 3
GDN_HEAD_DIM, GDN_CONV = 64, 4
BRANCH_WIDTH = 512
N_KEYS, PEER_TOPK, PEER_HALF = 128, 16, 128
N_EXPERTS = N_KEYS * N_KEYS
MLA_QSCALE = (MLA_NOPE + MLA_ROPE) ** -0.5 * math.log2(math.e)
MOBA_QSCALE = MOBA_HEAD_DIM ** -0.5 * math.log2(math.e)

LANES = 128
SUBLANES = 8
VMEM_LIMIT = 56 * 1024 * 1024

GDN_KCHUNK = 128
GDN_HEADS_PER_STEP = 8
GDN_CHUNKS_PER_STEP = 2
GDN_SCAN_PAIRS = 2
ATT_TQ = 256
PEER_TT = 512
PEER_EB = 1024
PEER_SCORE_TT = 256
PEER_SCORE_HEADS_PER_TRIP = 4

C_GATE, C_CQ, C_CKV, C_MQ, C_MK, C_MV, C_GQKV, C_GZ, C_TOTAL = (
    0, 3072, 3840, 4096, 5120, 6144, 6656, 8192, 8704)
PROJ_TN = 4352


def _cp(*sem):
    return pltpu.CompilerParams(dimension_semantics=sem, vmem_limit_bytes=VMEM_LIMIT)


def _dot(a, b):
    return jnp.dot(a.astype(BF16), b.astype(BF16), preferred_element_type=F32)


def _dot_nt(a, b):
    return lax.dot_general(a.astype(BF16), b.astype(BF16), (((1,), (1,)), ((), ())),
                           preferred_element_type=F32)


def _ln_rows(y, g, b):
    mu = jnp.mean(y, axis=-1, keepdims=True)
    d = y - mu
    var = jnp.mean(d * d, axis=-1, keepdims=True)
    return d * lax.rsqrt(var + LN_EPS) * g + b


def _sigmoid(x):
    return 1.0 / (1.0 + jnp.exp(-x))


def _ln_kernel(x_ref, g_ref, b_ref, o_ref):
    o_ref[...] = _ln_rows(x_ref[...], g_ref[...], b_ref[...])


def _layer_norm(x, g, b, tm=512):
    T, D = x.shape
    return pl.pallas_call(
        _ln_kernel, grid=(T // tm,),
        in_specs=[pl.BlockSpec((tm, D), lambda i: (i, 0)),
                  pl.BlockSpec((1, D), lambda i: (0, 0)),
                  pl.BlockSpec((1, D), lambda i: (0, 0))],
        out_specs=pl.BlockSpec((tm, D), lambda i: (i, 0)),
        out_shape=jax.ShapeDtypeStruct((T, D), F32),
        compiler_params=_cp("parallel"))(x, g.reshape(1, D), b.reshape(1, D))


def _mm_kernel(x_ref, w_ref, o_ref):
    o_ref[...] = _dot(x_ref[...], w_ref[...]).astype(o_ref.dtype)


def _mm(x, w, tm, tn, out_dtype=F32):
    M, K = x.shape
    N = w.shape[1]
    return pl.pallas_call(
        _mm_kernel, grid=(N // tn, M // tm),
        in_specs=[pl.BlockSpec((tm, K), lambda j, i: (i, 0)),
                  pl.BlockSpec((K, tn), lambda j, i: (0, j))],
        out_specs=pl.BlockSpec((tm, tn), lambda j, i: (i, j)),
        out_shape=jax.ShapeDtypeStruct((M, N), out_dtype),
        compiler_params=_cp("parallel", "parallel"))(x, w)


def _rms_rows(x, g):
    return x * lax.rsqrt(jnp.mean(x * x, axis=-1, keepdims=True) + RMS_EPS) * g


def _head_slabs(x):
    return [x[:, h * LANES:(h + 1) * LANES] for h in range(HEADS)]


def _mla_qkv_kernel(cq_ref, ckv_ref, gq_ref, gkv_ref, wq_ref, wkv_ref, sm_ref, ck_tab, sk_tab,
                    qo_ref, ko_ref, vo_ref):
    W = HEADS * LANES
    q = _dot(_rms_rows(cq_ref[...].astype(F32), gq_ref[...]), wq_ref[...])
    kv = _dot(_rms_rows(ckv_ref[...].astype(F32), gkv_ref[...]), wkv_ref[...])
    vo_ref[...] = kv[:, W:].astype(BF16)
    ck, sk = ck_tab[...], sk_tab[...]
    lane = lax.broadcasted_iota(jnp.int32, ck.shape, 1)
    cq = jnp.where(lane < MLA_NOPE, MLA_QSCALE, ck * MLA_QSCALE)
    sq = sk * MLA_QSCALE
    sm = sm_ref[...]
    kr = sm * ck + pltpu.roll(sm, LANES - MLA_ROPE, 1) * sk
    for h in range(HEADS):
        sl = slice(h * LANES, (h + 1) * LANES)
        qs = q[:, sl]
        qo_ref[:, sl] = (qs * cq + pltpu.roll(qs, LANES - MLA_ROPE, 1) * sq).astype(BF16)
        ko_ref[:, sl] = (kv[:, sl] + kr).astype(BF16)


def _mla_qkv(proj, small, gq, gkv, wq, wkv, tabs, tm=512):
    T = proj.shape[0]
    W = HEADS * LANES
    row = lambda i: (i, 0)
    tab = pl.BlockSpec((tm, LANES), row)
    full = lambda a: pl.BlockSpec(a.shape, lambda i: (0, 0))
    gq, gkv = gq.reshape(1, -1), gkv.reshape(1, -1)
    return pl.pallas_call(
        _mla_qkv_kernel, grid=(T // tm,),
        in_specs=[pl.BlockSpec((tm, MLA_Q_RANK), lambda i: (i, C_CQ // MLA_Q_RANK)),
                  pl.BlockSpec((tm, MLA_KV_RANK), lambda i: (i, C_CKV // MLA_KV_RANK)),
                  full(gq), full(gkv), full(wq), full(wkv), tab, tab, tab],
        out_specs=[pl.BlockSpec((tm, W), row), pl.BlockSpec((tm, W), row), pl.BlockSpec((tm, BRANCH_WIDTH), row)],
        out_shape=[jax.ShapeDtypeStruct((T, W), BF16)] * 2 + [jax.ShapeDtypeStruct((T, BRANCH_WIDTH), BF16)],
        compiler_params=_cp("parallel"))(proj, proj, gq, gkv, wq, wkv, small, *tabs)


def _moba_prep_kernel(q_ref, k_ref, ck_ref, sk_ref, qo_ref, ko_ref, km_ref):
    ck, sk = ck_ref[...], sk_ref[...]
    cq, sq = ck * MOBA_QSCALE, sk * MOBA_QSCALE
    q, k = q_ref[...].astype(F32), k_ref[...].astype(F32)
    for h, (qs, ks) in enumerate(zip(_head_slabs(q), _head_slabs(k))):
        sl = slice(h * LANES, (h + 1) * LANES)
        qo_ref[:, sl] = (qs * cq + pltpu.roll(qs, LANES // 2, 1) * sq).astype(BF16)
        kk = ks * ck + pltpu.roll(ks, LANES // 2, 1) * sk
        ko_ref[:, sl] = kk.astype(BF16)
        km_ref[0, :, sl] = jnp.mean(kk, axis=0, keepdims=True)


def _moba_prep(proj, tabs):
    T = proj.shape[0]
    W = HEADS * LANES
    tm = MOBA_BLOCK
    row = lambda i: (i, 0)
    tab = pl.BlockSpec((tm, LANES), row)
    return pl.pallas_call(
        _moba_prep_kernel, grid=(T // tm,),
        in_specs=[pl.BlockSpec((tm, W), lambda i: (i, C_MQ // W)),
                  pl.BlockSpec((tm, W), lambda i: (i, C_MK // W)), tab, tab],
        out_specs=[pl.BlockSpec((tm, W), row), pl.BlockSpec((tm, W), row),
                   pl.BlockSpec((1, 1, W), lambda i: (i, 0, 0))],
        out_shape=[jax.ShapeDtypeStruct((T, W), BF16), jax.ShapeDtypeStruct((T, W), BF16),
                   jax.ShapeDtypeStruct((T // tm, 1, W), F32)],
        compiler_params=_cp("parallel"))(proj, proj, *tabs)


def _attn_kernel(*refs, moba, nk):
    if moba:
        q_ref, k_ref, v_ref, km_ref, o_ref, vt_sc, qt_sc, m_sc, l_sc, al_sc, acc_sc, s_sc, p_sc, bias_sc = refs
    else:
        q_ref, k_ref, v_ref, o_ref, vt_sc, qt_sc, m_sc, l_sc, al_sc, acc_sc, s_sc, p_sc = refs
    tq = tk = ATT_TQ
    dv = BRANCH_WIDTH // HEADS
    qi = pl.program_id(1)
    hslab = [slice(h * LANES, (h + 1) * LANES) for h in range(HEADS)]

    @pl.when(qi == 0)
    def _():
        for j in range(nk):
            for c in range(tk // LANES):
                for g in range(BRANCH_WIDTH // LANES):
                    vt_sc[j, g * LANES:(g + 1) * LANES, c * LANES:(c + 1) * LANES] = (
                        v_ref[j * tk + c * LANES:j * tk + (c + 1) * LANES,
                              g * LANES:(g + 1) * LANES].astype(F32).T.astype(BF16))

    if moba:
        blk = lax.broadcasted_iota(jnp.int32, (SUBLANES, tq), 0)
        for h in range(HEADS):
            qf = q_ref[:, hslab[h]].astype(F32)
            km = km_ref[0, :, hslab[h]]
            if nk < SUBLANES:
                km = jnp.concatenate([km, jnp.zeros((SUBLANES - nk, LANES), F32)], axis=0)
            gate = lax.dot_general(km, qf, (((1,), (1,)), ((), ())), precision=HIGHEST,
                                   preferred_element_type=F32)
            gate = jnp.where(blk < qi, gate, -jnp.inf)
            for n in range(nk):
                gn = gate[n:n + 1, :]
                beats = jnp.where(gate > gn, 1.0, jnp.where((gate == gn) & (blk < n), 1.0, 0.0))
                cnt = jnp.sum(beats, axis=0, keepdims=True)
                bias_sc[h, n] = jnp.where((cnt < MOBA_TOPK) & (n < qi), 0.0, NEG_INF)
    for h in range(HEADS):
        qt_sc[h] = q_ref[:, hslab[h]].astype(F32).T.astype(BF16)
    m_sc[...] = jnp.full(m_sc.shape, NEG_INF, F32)
    l_sc[...] = jnp.zeros(l_sc.shape, F32)
    acc_sc[...] = jnp.zeros(acc_sc.shape, F32)

    def block(j, diag):
        rows = pl.ds(pl.multiple_of(j * tk, tk), tk)
        for h in range(HEADS):
            s = jnp.dot(k_ref[rows, hslab[h]], qt_sc[h], preferred_element_type=F32)
            if diag:
                keyi = lax.broadcasted_iota(jnp.int32, (tk, tq), 0)
                qryi = lax.broadcasted_iota(jnp.int32, (tk, tq), 1)
                s = jnp.where(keyi <= qryi, s, NEG_INF)
            elif moba:
                s = s + bias_sc[h, j]
            s_sc[h] = s
        for h in range(HEADS):
            s = s_sc[h]
            m_prev = m_sc[h]
            m_new = jnp.maximum(m_prev, jnp.max(s, axis=0, keepdims=True))
            alpha = jnp.exp2(m_prev - m_new)
            p = jnp.exp2(s - m_new)
            l_sc[h] = alpha * l_sc[h] + jnp.sum(p, axis=0, keepdims=True)
            p_sc[h] = p.astype(BF16)
            al_sc[h] = alpha
            m_sc[h] = m_new
        for h in range(HEADS):
            acc_sc[h] = al_sc[h] * acc_sc[h] + jnp.dot(vt_sc[j, h * dv:(h + 1) * dv, :], p_sc[h],
                                                       preferred_element_type=F32)

    def past(j, carry):
        block(j, False)
        return carry
    lax.fori_loop(0, qi, past, 0)
    block(qi, True)
    ot = jnp.concatenate([acc_sc[h] * (1.0 / l_sc[h]) for h in range(HEADS)], axis=0)
    o_ref[...] = ot.T


def _attention(q, k, v, v_col, B, S, kmean=None):
    T = B * S
    tq = ATT_TQ
    nq = S // tq
    W = HEADS * LANES
    dv = BRANCH_WIDTH // HEADS
    moba = kmean is not None
    assert nq <= SUBLANES and v_col % BRANCH_WIDTH == 0
    in_specs = [pl.BlockSpec((tq, W), lambda b, i: (b * nq + i, 0)),
                pl.BlockSpec((S, W), lambda b, i: (b, 0)),
                pl.BlockSpec((S, BRANCH_WIDTH), lambda b, i: (b, v_col // BRANCH_WIDTH))]
    scratch = [pltpu.VMEM((nq, BRANCH_WIDTH, tq), BF16), pltpu.VMEM((HEADS, LANES, tq), BF16)]
    row = pltpu.VMEM((HEADS, 1, tq), F32)
    scratch += [row, row, row, pltpu.VMEM((HEADS, dv, tq), F32),
                pltpu.VMEM((HEADS, tq, tq), F32), pltpu.VMEM((HEADS, tq, tq), BF16)]
    args = [q, k, v]
    if moba:
        in_specs.append(pl.BlockSpec((1, nq, W), lambda b, i: (b, 0, 0)))
        scratch.append(pltpu.VMEM((HEADS, nq, 1, tq), F32))
        args.append(kmean)
    return pl.pallas_call(
        functools.partial(_attn_kernel, moba=moba, nk=nq),
        grid=(B, nq), in_specs=in_specs,
        out_specs=pl.BlockSpec((tq, BRANCH_WIDTH), lambda b, i: (b * nq + i, 0)),
        out_shape=jax.ShapeDtypeStruct((T, BRANCH_WIDTH), F32),
        scratch_shapes=scratch,
        compiler_params=_cp("parallel", "arbitrary"))(*args)


def _gdn_conv_kernel(x_ref, w_ref, o_ref):
    c = pl.program_id(1)
    x = x_ref[...].astype(F32)
    w = w_ref[...]
    S = x.shape[0]
    row = lax.broadcasted_iota(jnp.int32, (S, LANES), 0)
    lane = lax.broadcasted_iota(jnp.int32, (S, LANES), 1)
    y = x * w[GDN_CONV - 1:GDN_CONV, :]
    for d in range(1, GDN_CONV):
        xs = jnp.where(row >= d, pltpu.roll(x, d, 0), 0.0)
        y = y + xs * w[GDN_CONV - 1 - d:GDN_CONV - d, :]
    y = y * _sigmoid(y)
    sq = y * y
    lo = lane < GDN_HEAD_DIM
    ss0 = jnp.sum(jnp.where(lo, sq, 0.0), axis=1, keepdims=True)
    ss1 = jnp.sum(jnp.where(lo, 0.0, sq), axis=1, keepdims=True)
    inv = lax.rsqrt(jnp.where(lo, ss0, ss1) + RMS_EPS)
    nqb = BRANCH_WIDTH // LANES
    scale = jnp.where(c < nqb, GDN_HEAD_DIM ** -0.5, 1.0)
    o_ref[...] = jnp.where(c < 2 * nqb, y * inv * scale, y).astype(BF16)


def _gdn_conv(proj, conv_w, B, S):
    T = B * S
    nb = 3 * BRANCH_WIDTH // LANES
    return pl.pallas_call(
        _gdn_conv_kernel, grid=(B, nb),
        in_specs=[pl.BlockSpec((S, LANES), lambda b, c: (b, C_GQKV // LANES + c)),
                  pl.BlockSpec((GDN_CONV, LANES), lambda b, c: (0, c))],
        out_specs=pl.BlockSpec((S, LANES), lambda b, c: (b, c)),
        out_shape=jax.ShapeDtypeStruct((T, 3 * BRANCH_WIDTH), BF16),
        compiler_params=_cp("parallel", "parallel"))(proj, conv_w)


def _gdn_chunk_kernel(q_ref, k_ref, v_ref, sm_ref, alog_ref, dtb_ref,
                      u_ref, w_ref, qg_ref, qk_ref, kdt_ref, dl_ref):
    C = GDN_KCHUNK
    nh = GDN_HEADS_PER_STEP
    ncs = GDN_CHUNKS_PER_STEP
    h0 = pl.program_id(1) * nh
    lane = lax.broadcasted_iota(jnp.int32, (C, LANES), 1)
    row = lax.broadcasted_iota(jnp.int32, (C, LANES), 0)
    lo = lane < GDN_HEAD_DIM
    tril = row >= lane
    strict = row > lane
    eye = jnp.where(row == lane, 1.0, 0.0)
    ltri = tril.astype(BF16)

    def pick(ref, k, cc):
        x = ref[cc * C:(cc + 1) * C, (k // 2) * LANES:(k // 2 + 1) * LANES].astype(F32)
        return jnp.where(lo, pltpu.roll(x, LANES // 2, 1) if k % 2 else x, 0.0)

    garr, sig = [], []
    for cc in range(ncs):
        sm = sm_ref[cc * C:(cc + 1) * C, :]
        a = sm + dtb_ref[...]
        softplus = jnp.maximum(a, 0.0) + jnp.log1p(jnp.exp(-jnp.abs(a)))
        garr.append(-jnp.exp(alog_ref[...]) * softplus)
        sig.append(_sigmoid(sm))
    nc = ncs * nh
    kh, kb, gc, decay, A = [], [], [], [], []
    for i in range(nc):
        cc, k = divmod(i, nh)
        g_col = jnp.sum(jnp.where(lane == h0 + k, garr[cc], 0.0), axis=1, keepdims=True)
        beta = jnp.sum(jnp.where(lane == HEADS + h0 + k, sig[cc], 0.0), axis=1, keepdims=True)
        g1 = jnp.broadcast_to(g_col, (C, LANES))
        gsum = None
        for _ in range(3):
            gb = g1.astype(BF16)
            part = jnp.dot(ltri, gb, preferred_element_type=F32)
            gsum = part if gsum is None else gsum + part
            g1 = g1 - gb.astype(F32)
        gc.append(gsum)
        decay.append(jnp.where(tril, jnp.exp(jnp.where(tril, gsum - gsum.T, 0.0)), 0.0))
        kh.append(pick(k_ref, k, cc))
        kb.append(kh[i] * beta)
        vb = pick(v_ref, k, cc) * beta
        A.append(jnp.where(strict, _dot_nt(kb[i], kh[i]) * decay[i], 0.0))
        eg = jnp.exp(gsum)
        qh = pick(q_ref, k, cc)
        qk_ref[k, cc] = jnp.where(tril, _dot_nt(qh, kh[i]) * decay[i], 0.0).astype(BF16)
        qg_ref[k, cc] = (qh * eg).astype(BF16)
        glast = gsum[C - 1:C, :]
        kdt_ref[k, cc] = (kh[i] * jnp.exp(glast - gsum)).T.astype(BF16)
        dl_ref[k, cc] = jnp.broadcast_to(jnp.exp(glast), (SUBLANES, LANES))
        u_ref[k, cc] = vb
        w_ref[k, cc] = (kb[i] * eg).astype(BF16)
    def joiner(level):
        same = (row >> (level + 1)) == (lane >> (level + 1))
        return same & (((row >> level) & 1) == 1) & (((lane >> level) & 1) == 0)

    P = [eye - jnp.where(joiner(0), A[i], 0.0) for i in range(nc)]
    for level in range(1, int(math.log2(C))):
        msk = joiner(level)
        T1 = [_dot(P[i], jnp.where(msk, A[i], 0.0)) for i in range(nc)]
        P = [P[i] - _dot(T1[i], P[i]) for i in range(nc)]
    for i in range(nc):
        cc, k = divmod(i, nh)
        u_ref[k, cc] = _dot(P[i], u_ref[k, cc])
        w_ref[k, cc] = _dot(P[i], w_ref[k, cc]).astype(BF16)


def _gdn_chunks(qkv, small, alog_row, dtb_row, B, S):
    C = GDN_KCHUNK
    n = S // C
    nh = GDN_HEADS_PER_STEP
    ng = HEADS // nh
    wq = nh // 2 * LANES
    nqb = BRANCH_WIDTH // wq
    ncs = GDN_CHUNKS_PER_STEP
    ns = n // ncs
    big = lambda dt: jax.ShapeDtypeStruct((B * HEADS, n, C, LANES), dt)
    ospec = pl.BlockSpec((nh, ncs, C, LANES), lambda b, g, c: (b * ng + g, c, 0, 0))
    par = pl.BlockSpec((1, LANES), lambda b, g, c: (0, 0))
    outs = pl.pallas_call(
        _gdn_chunk_kernel, grid=(B, ng, ns),
        in_specs=[pl.BlockSpec((ncs * C, wq), lambda b, g, c: (b * ns + c, g)),
                  pl.BlockSpec((ncs * C, wq), lambda b, g, c: (b * ns + c, nqb + g)),
                  pl.BlockSpec((ncs * C, wq), lambda b, g, c: (b * ns + c, 2 * nqb + g)),
                  pl.BlockSpec((ncs * C, LANES), lambda b, g, c: (b * ns + c, 0)), par, par],
        out_specs=[ospec] * 5 + [pl.BlockSpec((nh, ncs, SUBLANES, LANES), lambda b, g, c: (b * ng + g, c, 0, 0))],
        out_shape=[big(F32)] + [big(BF16)] * 4 + [jax.ShapeDtypeStruct((B * HEADS, n, SUBLANES, LANES), F32)],
        compiler_params=_cp("parallel", "parallel", "parallel"))(qkv, qkv, qkv, small, alog_row, dtb_row)
    return [o.reshape(-1, LANES) for o in outs]


def _gdn_scan_kernel(u_ref, w_ref, qg_ref, qk_ref, kdt_ref, dl_ref, z_ref, g_ref, o_ref, o_sc, *, n):
    C = GDN_KCHUNK
    S = n * C
    npair = GDN_SCAN_PAIRS

    def step(c, states):
        out = []
        for k, state in enumerate(states):
            r = pl.ds(pl.multiple_of(k * S + c * C, C), C)
            sb = state.astype(BF16)
            v_new = u_ref[r, :] - _dot(w_ref[r, :], sb)
            vb = v_new.astype(BF16)
            o_sc[r, :] = _dot(qg_ref[r, :], sb) + _dot(qk_ref[r, :], vb)
            dl = dl_ref[pl.ds(pl.multiple_of((k * n + c) * SUBLANES, SUBLANES), 1), :]
            out.append(state * dl + _dot(kdt_ref[r, :], vb))
        return tuple(out)

    lax.fori_loop(0, n, step, tuple(jnp.zeros((LANES, LANES), F32) for _ in range(2 * npair)))
    lane = lax.broadcasted_iota(jnp.int32, (S, LANES), 1)

    def nrm(o):
        ms = jnp.sum(o * o, axis=1, keepdims=True) * (1.0 / GDN_HEAD_DIM)
        return o * lax.rsqrt(ms + RMS_EPS)

    for p in range(npair):
        ps = slice(p * LANES, (p + 1) * LANES)
        nn = jnp.where(lane < GDN_HEAD_DIM, nrm(o_sc[2 * p * S:(2 * p + 1) * S, :]),
                       pltpu.roll(nrm(o_sc[(2 * p + 1) * S:(2 * p + 2) * S, :]), LANES // 2, 1))
        z = z_ref[:, ps].astype(F32)
        o_ref[:, ps] = nn * g_ref[...] * (z * _sigmoid(z))


def _gdn_scan(parts, proj, g_row, B, S):
    n = S // GDN_KCHUNK
    nps = GDN_SCAN_PAIRS
    steps = HEADS // 2 // nps
    wz = nps * LANES
    seq = pl.BlockSpec((2 * nps * S, LANES), lambda i: (i, 0))
    return pl.pallas_call(
        functools.partial(_gdn_scan_kernel, n=n), grid=(B * steps,),
        in_specs=[seq] * 5 + [pl.BlockSpec((2 * nps * n * SUBLANES, LANES), lambda i: (i, 0)),
                              pl.BlockSpec((S, wz), lambda i: (i // steps, C_GZ // wz + i % steps)),
                              pl.BlockSpec((1, LANES), lambda i: (0, 0))],
        out_specs=pl.BlockSpec((S, wz), lambda i: (i // steps, i % steps)),
        out_shape=jax.ShapeDtypeStruct((B * S, BRANCH_WIDTH), F32),
        scratch_shapes=[pltpu.VMEM((2 * nps * S, LANES), F32)],
        compiler_params=_cp("parallel"))(*parts, proj, g_row)


def _merge_out_kernel(g0_ref, g1_ref, g2_ref, gb_ref, a_ref, b_ref, c_ref, wb_ref, wo_ref, h_ref, lg_ref, lb_ref,
                      o_ref, ot_ref):
    acc = None
    for n, (gl, br) in enumerate(((g0_ref, a_ref), (g1_ref, b_ref), (g2_ref, c_ref))):
        y = _sigmoid(gl[...].astype(F32) + gb_ref[n:n + 1, :]) * _dot(br[...], wb_ref[n])
        acc = y if acc is None else acc + y
    y = DEEPNORM_ALPHA * h_ref[...] + _dot(acc, wo_ref[...])
    y = _ln_rows(y, lg_ref[...], lb_ref[...])
    o_ref[...] = y
    ot_ref[...] = y.T.astype(BF16)


def _merge_out_ln(proj, gate_bias, o_a, o_b, o_c, w_branch, w_out, h, g, b, tm=512):
    T, D = h.shape
    row = lambda i: (i, 0)
    gspec = lambda n: pl.BlockSpec((tm, D), lambda i: (i, n))
    bspec = pl.BlockSpec((tm, BRANCH_WIDTH), row)
    vec = pl.BlockSpec((1, D), lambda i: (0, 0))
    return pl.pallas_call(
        _merge_out_kernel, grid=(T // tm,),
        in_specs=[gspec(0), gspec(1), gspec(2), pl.BlockSpec((3, D), lambda i: (0, 0)), bspec, bspec, bspec,
                  pl.BlockSpec((3, BRANCH_WIDTH, D), lambda i: (0, 0, 0)), pl.BlockSpec((D, D), lambda i: (0, 0)),
                  pl.BlockSpec((tm, D), row), vec, vec],
        out_specs=[pl.BlockSpec((tm, D), row), pl.BlockSpec((D, tm), lambda i: (0, i))],
        out_shape=[jax.ShapeDtypeStruct((T, D), F32), jax.ShapeDtypeStruct((D, T), BF16)],
        compiler_params=_cp("parallel"))(proj, proj, proj, gate_bias, o_a, o_b, o_c, w_branch, w_out, h,
                                         g.reshape(1, D), b.reshape(1, D))


def _sort_network(n):
    pairs, p = [], 1
    while p < n:
        k = p
        while k >= 1:
            for j in range(k % p, n - k, 2 * k):
                for i in range(min(k, n - j - k)):
                    if (i + j) // (2 * p) == (i + j + k) // (2 * p):
                        pairs.append((i + j, i + j + k))
            k //= 2
        p *= 2
    return pairs


def _top_of_sorted_lists(scores, n, on_max):
    lists = []
    for s in scores:
        nt = s.shape[0] // SUBLANES
        rows = [s[t * SUBLANES:(t + 1) * SUBLANES, :] for t in range(nt)]
        rows += [None] * ((1 << (nt - 1).bit_length()) - nt)
        for a, b in _sort_network(len(rows)):
            if rows[b] is None:
                continue
            if rows[a] is None:
                rows[a], rows[b] = rows[b], None
            else:
                rows[a], rows[b] = jnp.maximum(rows[a], rows[b]), jnp.minimum(rows[a], rows[b])
        lists.append(rows[:nt] + [None])
    for k in range(n):
        for i, rows in enumerate(lists):
            m = jnp.max(rows[0], axis=0, keepdims=True)
            on_max(i, k, m)
            hit = rows[0] >= m
            for t in range(min(n - 1 - k, len(rows) - 1)):
                nxt = -jnp.inf if rows[t + 1] is None else rows[t + 1]
                rows[t] = jnp.where(hit, nxt, rows[t])


def _peer_score_kernel(qt_ref, keys_ref, s2_ref, e2_ref, c_ref, e1_ref, top_sc, cand_sc):
    K = PEER_TOPK
    tt = qt_ref.shape[1]
    nh = PEER_SCORE_HEADS_PER_TRIP

    def heads(hp, carry):
        s = []
        for i in range(2 * nh):
            r = pl.multiple_of((hp * nh * 2 + i) * PEER_HALF, PEER_HALF)
            s.append(_dot(keys_ref[hp * nh * 2 + i], qt_ref[pl.ds(r, PEER_HALF), :]))

        def put(i, k, m):
            top_sc[i, k:k + 1, :] = m
        _top_of_sorted_lists(s, K + 1, put)
        r8 = lax.broadcasted_iota(jnp.int32, (SUBLANES, tt), 0)
        v1max, v2max = [], []
        for d in range(nh):
            v1a, v1x = top_sc[2 * d, 0:K, :], top_sc[2 * d, K:K + 1, :]
            v2a, v2x = top_sc[2 * d + 1, 0:K, :], top_sc[2 * d + 1, K:K + 1, :]
            v1max.append(v1a[0:1])
            v2max.append(v2a[0:1])
            cand_sc[d, 0:K, :] = v1a + v2a[0:1]
            for b in range(1, SUBLANES):
                cand_sc[d, K + SUBLANES * (b - 1):K + SUBLANES * b, :] = v1a[0:SUBLANES] + v2a[b:b + 1]
            base = K + SUBLANES * (SUBLANES - 1)
            cand_sc[d, base:base + SUBLANES, :] = v1a[0:1] + v2a[SUBLANES:K]
            cand_sc[d, base + SUBLANES:base + 2 * SUBLANES, :] = jnp.where(
                r8 == 0, v1x + v2a[0:1], jnp.where(r8 == 1, v1a[0:1] + v2x, -jnp.inf))
        st = [{"z": jnp.zeros((1, tt), F32)} for _ in range(nh)]

        def acc(d, k, m):
            if k == 0:
                st[d]["top"] = m
            if k < K:
                st[d]["z"] = st[d]["z"] + jnp.exp(m - st[d]["top"])
            if k == K - 1:
                st[d]["t16"] = m
            if k == K:
                st[d]["t17"] = m
        _top_of_sorted_lists([cand_sc[d] for d in range(nh)], K + 1, acc)
        for d in range(nh):
            tau = 0.5 * (st[d]["t16"] + st[d]["t17"])
            ro = pl.ds(pl.multiple_of((hp * nh + d) * N_KEYS, N_KEYS), N_KEYS)
            s2_ref[ro, :] = s[2 * d + 1]
            e2_ref[ro, :] = jnp.exp(s[2 * d + 1] - v2max[d]) / st[d]["z"]
            c_ref[ro, :] = tau - s[2 * d]
            e1_ref[ro, :] = jnp.exp(s[2 * d] - v1max[d])
        return carry

    lax.fori_loop(0, HEADS // nh, heads, 0)


def _peer_scores(qt, keys):
    T = qt.shape[1]
    tt = PEER_SCORE_TT
    R = HEADS * N_KEYS
    ncand = PEER_TOPK + SUBLANES * (SUBLANES + 1)
    ospec = pl.BlockSpec((R, tt), lambda i: (0, i))
    return pl.pallas_call(
        _peer_score_kernel, grid=(T // tt,),
        in_specs=[pl.BlockSpec((2 * R, tt), lambda i: (0, i)),
                  pl.BlockSpec((2 * HEADS, N_KEYS, PEER_HALF), lambda i: (0, 0, 0))],
        out_specs=[ospec] * 4,
        out_shape=[jax.ShapeDtypeStruct((R, T), F32)] * 4,
        scratch_shapes=[pltpu.VMEM((2 * PEER_SCORE_HEADS_PER_TRIP, 3 * SUBLANES, tt), F32),
                        pltpu.VMEM((PEER_SCORE_HEADS_PER_TRIP, ncand, tt), F32)],
        compiler_params=_cp("parallel"))(qt, keys)


def _peer_expert_kernel(ht_ref, u_ref, vtp_ref, vtl_ref, s2_ref, e2_ref, c_ref, e1_ref, h_ref, g_ref, b_ref,
                        o_ref, acc_sc, act_sc, hw_sc, bc_sc):
    j = pl.program_id(1)
    _, eb, tt = hw_sc.shape
    ng = eb // N_KEYS
    assert c_ref.shape == (HEADS, ng, tt)
    slot = j % 2

    @pl.when(j == 0)
    def _():
        acc_sc[...] = jnp.zeros(acc_sc.shape, F32)
        hw_sc[1] = jnp.zeros(hw_sc.shape[1:], BF16)

    mxu_w = 2 * LANES
    for half in range(tt // mxu_w):
        hl = slice(half * mxu_w, (half + 1) * mxu_w)
        a = jnp.dot(u_ref[...], ht_ref[:, hl], preferred_element_type=F32)
        act_sc[:, hl] = 0.5 * a * (1.0 + lax.erf(a * (2.0 ** -0.5)))
        acc_sc[:, hl] += jnp.dot(vtp_ref[...], hw_sc[1 - slot, :, hl], preferred_element_type=F32)
        for lc in range(half * (mxu_w // LANES), (half + 1) * (mxu_w // LANES)):
            ls = slice(lc * LANES, (lc + 1) * LANES)
            for h in range(HEADS):
                thr, e1 = c_ref[h, :, ls], e1_ref[h, :, ls]
                for g in range(ng):
                    bc_sc[0, h * ng + g] = jnp.broadcast_to(thr[g:g + 1], (SUBLANES, LANES))
                    bc_sc[1, h * ng + g] = jnp.broadcast_to(e1[g:g + 1], (SUBLANES, LANES))
            for g in range(ng):
                wsum = jnp.zeros((N_KEYS // SUBLANES, SUBLANES, LANES), F32)
                for h in range(HEADS):
                    hr = slice(h * N_KEYS, (h + 1) * N_KEYS)
                    s2 = s2_ref[hr, ls].reshape(N_KEYS // SUBLANES, SUBLANES, LANES)
                    e2 = e2_ref[hr, ls].reshape(N_KEYS // SUBLANES, SUBLANES, LANES)
                    wsum = wsum + jnp.where(s2 >= bc_sc[0, h * ng + g][None], e2 * bc_sc[1, h * ng + g][None], 0.0)
                gs = slice(g * N_KEYS, (g + 1) * N_KEYS)
                hw_sc[slot, gs, ls] = (wsum.reshape(N_KEYS, LANES) * act_sc[gs, ls]).astype(BF16)

    @pl.when(j == pl.num_programs(1) - 1)
    def _():
        acc = acc_sc[...] + jnp.dot(vtl_ref[...], hw_sc[slot], preferred_element_type=F32)
        y = DEEPNORM_ALPHA * h_ref[...] + acc.T
        o_ref[...] = _ln_rows(y, g_ref[...], b_ref[...])


def _peer_experts(ht, u, vt, s2, e2, c, e1, h, g, b):
    T, D = h.shape
    tt, eb = PEER_TT, PEER_EB
    R = HEADS * N_KEYS
    nb = N_EXPERTS // eb
    assert nb % 2 == 0
    ng = eb // N_KEYS
    sspec = pl.BlockSpec((R, tt), lambda i, j: (0, i))
    gspec = pl.BlockSpec((HEADS, ng, tt), lambda i, j: (0, j, i))
    vec = pl.BlockSpec((1, D), lambda i, j: (0, 0))
    return pl.pallas_call(
        _peer_expert_kernel, grid=(T // tt, nb),
        in_specs=[pl.BlockSpec((D, tt), lambda i, j: (0, i)),
                  pl.BlockSpec((eb, D), lambda i, j: (j, 0)),
                  pl.BlockSpec((D, eb), lambda i, j: (0, jnp.maximum(j - 1, 0))),
                  pl.BlockSpec((D, eb), lambda i, j: (0, nb - 1)),
                  sspec, sspec, gspec, gspec,
                  pl.BlockSpec((tt, D), lambda i, j: (i, 0)), vec, vec],
        out_specs=pl.BlockSpec((tt, D), lambda i, j: (i, 0)),
        out_shape=jax.ShapeDtypeStruct((T, D), F32),
        scratch_shapes=[pltpu.VMEM((D, tt), F32), pltpu.VMEM((eb, tt), F32), pltpu.VMEM((2, eb, tt), BF16),
                        pltpu.VMEM((2, HEADS * ng, SUBLANES, LANES), F32)],
        compiler_params=_cp("parallel", "arbitrary"))(ht, u, vt, vt, s2, e2, c.reshape(HEADS, N_KEYS, T),
                                                      e1.reshape(HEADS, N_KEYS, T), h, g.reshape(1, D),
                                                      b.reshape(1, D))


def _rot_half(x, r):
    return jnp.concatenate([-x[..., r:2 * r], x[..., :r]], axis=-1)


def _prep_w_in(w):
    D = w.shape[0]
    splits = (MLA_Q_RANK, MLA_KV_RANK, MLA_ROPE, 3 * BRANCH_WIDTH, 3 * BRANCH_WIDTH, BRANCH_WIDTH, HEADS, HEADS,
              3 * D)
    o = np.cumsum((0,) + splits)
    cq, ckv, kr, mqkv, gqkv, gz, ga, gb, gl = [w[:, o[i]:o[i + 1]] for i in range(9)]
    mq, mk, mv = jnp.split(mqkv, 3, axis=1)

    def slab(m):
        m = m.reshape(D, HEADS, MOBA_HEAD_DIM)
        pad = jnp.zeros((D, HEADS, LANES - MOBA_HEAD_DIM - MOBA_ROT_DIM), w.dtype)
        return jnp.concatenate([m, _rot_half(m, MOBA_ROT_DIM // 2), pad], axis=-1).reshape(D, HEADS * LANES)

    small = jnp.concatenate([ga, gb, jnp.zeros((D, LANES // 2 - 2 * HEADS), w.dtype), kr,
                             _rot_half(kr, MLA_ROPE // 2)], axis=1)
    out = jnp.concatenate([gl, cq, ckv, slab(mq), slab(mk), mv, gqkv, gz], axis=1)
    assert out.shape[1] == C_TOTAL and C_TOTAL % PROJ_TN == 0
    return out.astype(BF16), small.astype(BF16)


def _prep_w_uq(w):
    R = w.shape[0]
    w = w.reshape(R, HEADS, MLA_NOPE + MLA_ROPE)
    rope = w[..., MLA_NOPE:]
    return jnp.concatenate([w, _rot_half(rope, MLA_ROPE // 2)], axis=-1).reshape(R, HEADS * LANES).astype(BF16)


def _prep_w_ukv(w):
    R = w.shape[0]
    w = w.reshape(R, HEADS, MLA_NOPE + MLA_V)
    k = jnp.concatenate([w[..., :MLA_NOPE], jnp.zeros((R, HEADS, LANES - MLA_NOPE), w.dtype)], axis=-1)
    return jnp.concatenate([k.reshape(R, HEADS * LANES), w[..., MLA_NOPE:].reshape(R, HEADS * MLA_V)],
                           axis=1).astype(BF16)


def _rope_tables(positions):
    pos = positions.reshape(-1).astype(F32)[:, None]
    T = pos.shape[0]

    def cs(rot):
        inv = ROPE_THETA ** (-jnp.arange(0, rot, 2, dtype=F32) / rot)
        ang = pos * inv
        return jnp.cos(ang), jnp.sin(ang)

    ca, sa = cs(MLA_ROPE)
    cb, sb = cs(MOBA_ROT_DIM)
    one = lambda n: jnp.ones((T, n), F32)
    zero = lambda n: jnp.zeros((T, n), F32)
    cat = lambda *xs: jnp.concatenate(xs, axis=1)
    pad_a = LANES - MLA_NOPE - MLA_ROPE
    mla = (cat(zero(MLA_NOPE), ca, ca, zero(pad_a)), cat(zero(MLA_NOPE), sa, sa, zero(pad_a)))
    moba = (cat(cb, cb, one(MOBA_HEAD_DIM - MOBA_ROT_DIM), zero(LANES - MOBA_HEAD_DIM)),
            cat(sb, sb, zero(LANES - MOBA_ROT_DIM)))
    return mla, moba


def _lane_row(v):
    return jnp.concatenate([v.astype(F32), jnp.zeros((LANES - v.shape[0],), F32)]).reshape(1, LANES)


def kernel(x, positions, ln_in_g, ln_in_b, w_in, mla_q_norm, mla_kv_norm, mla_w_uq, mla_w_ukv, gdn_conv_w, gdn_A_log, gdn_dt_bias, gdn_o_norm, gate_bias, w_branch, w_out, ln1_g, ln1_b, peer_w_q, peer_sub_keys, peer_u, peer_v, ln2_g, ln2_b):
    B, S, D = x.shape
    T = B * S
    assert S % ATT_TQ == 0 and S % GDN_KCHUNK == 0 and T % PEER_TT == 0 and ATT_TQ == MOBA_BLOCK
    mla_tabs, moba_tabs = _rope_tables(positions)
    h = _layer_norm(x.reshape(T, D), ln_in_g, ln_in_b)
    for l in range(DEPTH):
        w_main, w_small = _prep_w_in(w_in[l])
        proj = _mm(h, w_main, tm=512, tn=PROJ_TN, out_dtype=BF16)
        small = _mm(h, w_small, tm=512, tn=LANES)
        qa, ka, va = _mla_qkv(proj, small, mla_q_norm[l], mla_kv_norm[l], _prep_w_uq(mla_w_uq[l]),
                              _prep_w_ukv(mla_w_ukv[l]), mla_tabs)
        o_a = _attention(qa, ka, va, 0, B, S)
        qm, km, kmean = _moba_prep(proj, moba_tabs)
        o_b = _attention(qm, km, proj, C_MV, B, S, kmean=kmean.reshape(B, S // MOBA_BLOCK, HEADS * LANES))
        qkv = _gdn_conv(proj, gdn_conv_w[l], B, S)
        parts = _gdn_chunks(qkv, small, _lane_row(gdn_A_log[l]), _lane_row(gdn_dt_bias[l]), B, S)
        o_c = _gdn_scan(parts, proj, jnp.tile(gdn_o_norm[l], 2).reshape(1, LANES), B, S)
        h, ht = _merge_out_ln(proj, gate_bias[l], o_a, o_b, o_c, w_branch[l].astype(BF16), w_out[l].astype(BF16),
                              h, ln1_g[l], ln1_b[l])
        qt = _mm(peer_w_q[l].T.astype(BF16), ht, tm=1024, tn=min(1024, T), out_dtype=BF16)
        keys = peer_sub_keys[l].reshape(2 * HEADS, N_KEYS, PEER_HALF).astype(BF16)
        s2, e2, c, e1 = _peer_scores(qt, keys)
        h = _peer_experts(ht, peer_u[l].astype(BF16), peer_v[l].astype(BF16).T, s2, e2, c, e1, h, ln2_g[l], ln2_b[l])
    return h.reshape(B, S, D)
```

```python
import functools
import math

import numpy as np
import jax
import jax.numpy as jnp
from jax import lax
from jax.experimental import pallas as pl
from jax.experimental.pallas import tpu as pltpu

F32 = jnp.float32
BF16 = jnp.bfloat16
HIGHEST = lax.Precision.HIGHEST

DEPTH = 2
ROPE_THETA = 500000.0
NEG_INF = -1e30
LN_EPS = 1e-5
RMS_EPS = 1e-6
DEEPNORM_ALPHA = (2 * DEPTH) ** 0.25
HEADS = 8
MLA_NOPE, MLA_ROPE, MLA_V = 64, 32, 64
MLA_Q_RANK, MLA_KV_RANK = 768, 256
MOBA_HEAD_DIM, MOBA_ROT_DIM, MOBA_BLOCK, MOBA_TOPK = 64, 16, 256, 3
GDN_HEAD_DIM, GDN_CONV = 64, 4
BRANCH_WIDTH = 512
N_KEYS, PEER_TOPK, PEER_HALF = 128, 16, 128
N_EXPERTS = N_KEYS * N_KEYS
MLA_QSCALE = (MLA_NOPE + MLA_ROPE) ** -0.5 * math.log2(math.e)
MOBA_QSCALE = MOBA_HEAD_DIM ** -0.5 * math.log2(math.e)

LANES = 128
SUBLANES = 8
VMEM_LIMIT = 56 * 1024 * 1024

GDN_KCHUNK = 128
GDN_HEADS_PER_STEP = 8
GDN_CHUNKS_PER_STEP = 2
GDN_SCAN_PAIRS = 2
ATT_TQ = 256
PEER_TT = 512
PEER_EB = 1024
PEER_SCORE_TT = 256
PEER_SCORE_HEADS_PER_TRIP = 8

C_GATE, C_CQ, C_CKV, C_MQ, C_MK, C_MV, C_GQKV, C_GZ, C_TOTAL = (
    0, 3072, 3840, 4096, 5120, 6144, 6656, 8192, 8704)
PROJ_TN = 4352


def _cp(*sem):
    return pltpu.CompilerParams(dimension_semantics=sem, vmem_limit_bytes=VMEM_LIMIT)


def _dot(a, b):
    return jnp.dot(a.astype(BF16), b.astype(BF16), preferred_element_type=F32)


def _dot_nt(a, b):
    return lax.dot_general(a.astype(BF16), b.astype(BF16), (((1,), (1,)), ((), ())),
                           preferred_element_type=F32)


def _ln_rows(y, g, b):
    mu = jnp.mean(y, axis=-1, keepdims=True)
    d = y - mu
    var = jnp.mean(d * d, axis=-1, keepdims=True)
    return d * lax.rsqrt(var + LN_EPS) * g + b


def _sigmoid(x):
    return 1.0 / (1.0 + jnp.exp(-x))


def _ln_kernel(x_ref, g_ref, b_ref, o_ref):
    o_ref[...] = _ln_rows(x_ref[...], g_ref[...], b_ref[...])


def _layer_norm(x, g, b, tm=512):
    T, D = x.shape
    return pl.pallas_call(
        _ln_kernel, grid=(T // tm,),
        in_specs=[pl.BlockSpec((tm, D), lambda i: (i, 0)),
                  pl.BlockSpec((1, D), lambda i: (0, 0)),
                  pl.BlockSpec((1, D), lambda i: (0, 0))],
        out_specs=pl.BlockSpec((tm, D), lambda i: (i, 0)),
        out_shape=jax.ShapeDtypeStruct((T, D), F32),
        compiler_params=_cp("parallel"))(x, g.reshape(1, D), b.reshape(1, D))


def _mm_kernel(x_ref, w_ref, o_ref):
    o_ref[...] = _dot(x_ref[...], w_ref[...]).astype(o_ref.dtype)


def _mm(x, w, tm, tn, out_dtype=F32):
    M, K = x.shape
    N = w.shape[1]
    return pl.pallas_call(
        _mm_kernel, grid=(N // tn, M // tm),
        in_specs=[pl.BlockSpec((tm, K), lambda j, i: (i, 0)),
                  pl.BlockSpec((K, tn), lambda j, i: (0, j))],
        out_specs=pl.BlockSpec((tm, tn), lambda j, i: (i, j)),
        out_shape=jax.ShapeDtypeStruct((M, N), out_dtype),
        compiler_params=_cp("parallel", "parallel"))(x, w)


def _rms_rows(x, g):
    return x * lax.rsqrt(jnp.mean(x * x, axis=-1, keepdims=True) + RMS_EPS) * g


def _head_slabs(x):
    return [x[:, h * LANES:(h + 1) * LANES] for h in range(HEADS)]


def _mla_qkv_kernel(cq_ref, ckv_ref, gq_ref, gkv_ref, wq_ref, wkv_ref, sm_ref, ck_tab, sk_tab,
                    qo_ref, ko_ref, vo_ref):
    W = HEADS * LANES
    q = _dot(_rms_rows(cq_ref[...].astype(F32), gq_ref[...]), wq_ref[...])
    kv = _dot(_rms_rows(ckv_ref[...].astype(F32), gkv_ref[...]), wkv_ref[...])
    vo_ref[...] = kv[:, W:].astype(BF16)
    ck, sk = ck_tab[...], sk_tab[...]
    lane = lax.broadcasted_iota(jnp.int32, ck.shape, 1)
    cq = jnp.where(lane < MLA_NOPE, MLA_QSCALE, ck * MLA_QSCALE)
    sq = sk * MLA_QSCALE
    sm = sm_ref[...]
    kr = sm * ck + pltpu.roll(sm, LANES - MLA_ROPE, 1) * sk
    for h in range(HEADS):
        sl = slice(h * LANES, (h + 1) * LANES)
        qs = q[:, sl]
        qo_ref[:, sl] = (qs * cq + pltpu.roll(qs, LANES - MLA_ROPE, 1) * sq).astype(BF16)
        ko_ref[:, sl] = (kv[:, sl] + kr).astype(BF16)


def _mla_qkv(proj, small, gq, gkv, wq, wkv, tabs, tm=512):
    T = proj.shape[0]
    W = HEADS * LANES
    row = lambda i: (i, 0)
    tab = pl.BlockSpec((tm, LANES), row)
    full = lambda a: pl.BlockSpec(a.shape, lambda i: (0, 0))
    gq, gkv = gq.reshape(1, -1), gkv.reshape(1, -1)
    return pl.pallas_call(
        _mla_qkv_kernel, grid=(T // tm,),
        in_specs=[pl.BlockSpec((tm, MLA_Q_RANK), lambda i: (i, C_CQ // MLA_Q_RANK)),
                  pl.BlockSpec((tm, MLA_KV_RANK), lambda i: (i, C_CKV // MLA_KV_RANK)),
                  full(gq), full(gkv), full(wq), full(wkv), tab, tab, tab],
        out_specs=[pl.BlockSpec((tm, W), row), pl.BlockSpec((tm, W), row), pl.BlockSpec((tm, BRANCH_WIDTH), row)],
        out_shape=[jax.ShapeDtypeStruct((T, W), BF16)] * 2 + [jax.ShapeDtypeStruct((T, BRANCH_WIDTH), BF16)],
        compiler_params=_cp("parallel"))(proj, proj, gq, gkv, wq, wkv, small, *tabs)


def _moba_prep_kernel(q_ref, k_ref, ck_ref, sk_ref, qo_ref, ko_ref, km_ref):
    ck, sk = ck_ref[...], sk_ref[...]
    cq, sq = ck * MOBA_QSCALE, sk * MOBA_QSCALE
    q, k = q_ref[...].astype(F32), k_ref[...].astype(F32)
    for h, (qs, ks) in enumerate(zip(_head_slabs(q), _head_slabs(k))):
        sl = slice(h * LANES, (h + 1) * LANES)
        qo_ref[:, sl] = (qs * cq + pltpu.roll(qs, LANES // 2, 1) * sq).astype(BF16)
        kk = ks * ck + pltpu.roll(ks, LANES // 2, 1) * sk
        ko_ref[:, sl] = kk.astype(BF16)
        km_ref[0, :, sl] = jnp.mean(kk, axis=0, keepdims=True)


def _moba_prep(proj, tabs):
    T = proj.shape[0]
    W = HEADS * LANES
    tm = MOBA_BLOCK
    row = lambda i: (i, 0)
    tab = pl.BlockSpec((tm, LANES), row)
    return pl.pallas_call(
        _moba_prep_kernel, grid=(T // tm,),
        in_specs=[pl.BlockSpec((tm, W), lambda i: (i, C_MQ // W)),
                  pl.BlockSpec((tm, W), lambda i: (i, C_MK // W)), tab, tab],
        out_specs=[pl.BlockSpec((tm, W), row), pl.BlockSpec((tm, W), row),
                   pl.BlockSpec((1, 1, W), lambda i: (i, 0, 0))],
        out_shape=[jax.ShapeDtypeStruct((T, W), BF16), jax.ShapeDtypeStruct((T, W), BF16),
                   jax.ShapeDtypeStruct((T // tm, 1, W), F32)],
        compiler_params=_cp("parallel"))(proj, proj, *tabs)


def _attn_kernel(*refs, moba, nk):
    if moba:
        q_ref, k_ref, v_ref, km_ref, o_ref, vt_sc, m_sc, l_sc, al_sc, acc_sc, s_sc, p_sc, bias_sc = refs
    else:
        q_ref, k_ref, v_ref, o_ref, vt_sc, m_sc, l_sc, al_sc, acc_sc, s_sc, p_sc = refs
    tq = tk = ATT_TQ
    dv = BRANCH_WIDTH // HEADS
    qi = pl.program_id(1)
    hslab = [slice(h * LANES, (h + 1) * LANES) for h in range(HEADS)]

    @pl.when(qi == 0)
    def _():
        for j in range(nk):
            for c in range(tk // LANES):
                for g in range(BRANCH_WIDTH // LANES):
                    vt_sc[j, g * LANES:(g + 1) * LANES, c * LANES:(c + 1) * LANES] = (
                        v_ref[j * tk + c * LANES:j * tk + (c + 1) * LANES,
                              g * LANES:(g + 1) * LANES].astype(F32).T.astype(BF16))

    if moba:
        blk = lax.broadcasted_iota(jnp.int32, (SUBLANES, tq), 0)
        for h in range(HEADS):
            qf = q_ref[:, hslab[h]].astype(F32)
            km = km_ref[0, :, hslab[h]]
            if nk < SUBLANES:
                km = jnp.concatenate([km, jnp.zeros((SUBLANES - nk, LANES), F32)], axis=0)
            gate = lax.dot_general(km, qf, (((1,), (1,)), ((), ())), precision=HIGHEST,
                                   preferred_element_type=F32)
            gate = jnp.where(blk < qi, gate, -jnp.inf)
            for n in range(nk):
                gn = gate[n:n + 1, :]
                beats = jnp.where(gate > gn, 1.0, jnp.where((gate == gn) & (blk < n), 1.0, 0.0))
                cnt = jnp.sum(beats, axis=0, keepdims=True)
                bias_sc[h, n] = jnp.where((cnt < MOBA_TOPK) & (n < qi), 0.0, NEG_INF)
    m_sc[...] = jnp.full(m_sc.shape, NEG_INF, F32)
    l_sc[...] = jnp.zeros(l_sc.shape, F32)
    acc_sc[...] = jnp.zeros(acc_sc.shape, F32)

    def block(j, diag):
        rows = pl.ds(pl.multiple_of(j * tk, tk), tk)
        for h in range(HEADS):
            s = _dot_nt(k_ref[rows, hslab[h]], q_ref[:, hslab[h]])
            if diag:
                keyi = lax.broadcasted_iota(jnp.int32, (tk, tq), 0)
                qryi = lax.broadcasted_iota(jnp.int32, (tk, tq), 1)
                s = jnp.where(keyi <= qryi, s, NEG_INF)
            elif moba:
                s = s + bias_sc[h, j]
            s_sc[h] = s
        for h in range(HEADS):
            s = s_sc[h]
            m_prev = m_sc[h]
            m_new = jnp.maximum(m_prev, jnp.max(s, axis=0, keepdims=True))
            alpha = jnp.exp2(m_prev - m_new)
            p = jnp.exp2(s - m_new)
            l_sc[h] = alpha * l_sc[h] + jnp.sum(p, axis=0, keepdims=True)
            p_sc[h] = p.astype(BF16)
            al_sc[h] = alpha
            m_sc[h] = m_new
        for h in range(HEADS):
            acc_sc[h] = al_sc[h] * acc_sc[h] + jnp.dot(vt_sc[j, h * dv:(h + 1) * dv, :], p_sc[h],
                                                       preferred_element_type=F32)

    def past(j, carry):
        block(j, False)
        return carry
    lax.fori_loop(0, qi, past, 0)
    block(qi, True)
    ot = jnp.concatenate([acc_sc[h] * (1.0 / l_sc[h]) for h in range(HEADS)], axis=0)
    o_ref[...] = ot.T


def _attention(q, k, v, v_col, B, S, kmean=None):
    T = B * S
    tq = ATT_TQ
    nq = S // tq
    W = HEADS * LANES
    dv = BRANCH_WIDTH // HEADS
    moba = kmean is not None
    assert nq <= SUBLANES and v_col % BRANCH_WIDTH == 0
    in_specs = [pl.BlockSpec((tq, W), lambda b, i: (b * nq + i, 0)),
                pl.BlockSpec((S, W), lambda b, i: (b, 0)),
                pl.BlockSpec((S, BRANCH_WIDTH), lambda b, i: (b, v_col // BRANCH_WIDTH))]
    scratch = [pltpu.VMEM((nq, BRANCH_WIDTH, tq), BF16)]
    row = pltpu.VMEM((HEADS, 1, tq), F32)
    scratch += [row, row, row, pltpu.VMEM((HEADS, dv, tq), F32),
                pltpu.VMEM((HEADS, tq, tq), F32), pltpu.VMEM((HEADS, tq, tq), BF16)]
    args = [q, k, v]
    if moba:
        in_specs.append(pl.BlockSpec((1, nq, W), lambda b, i: (b, 0, 0)))
        scratch.append(pltpu.VMEM((HEADS, nq, 1, tq), F32))
        args.append(kmean)
    return pl.pallas_call(
        functools.partial(_attn_kernel, moba=moba, nk=nq),
        grid=(B, nq), in_specs=in_specs,
        out_specs=pl.BlockSpec((tq, BRANCH_WIDTH), lambda b, i: (b * nq + i, 0)),
        out_shape=jax.ShapeDtypeStruct((T, BRANCH_WIDTH), F32),
        scratch_shapes=scratch,
        compiler_params=_cp("parallel", "arbitrary"))(*args)


def _gdn_conv_kernel(x_ref, w_ref, o_ref):
    c = pl.program_id(1)
    x = x_ref[...].astype(F32)
    w = w_ref[...]
    S = x.shape[0]
    row = lax.broadcasted_iota(jnp.int32, (S, LANES), 0)
    lane = lax.broadcasted_iota(jnp.int32, (S, LANES), 1)
    y = x * w[GDN_CONV - 1:GDN_CONV, :]
    for d in range(1, GDN_CONV):
        xs = jnp.where(row >= d, pltpu.roll(x, d, 0), 0.0)
        y = y + xs * w[GDN_CONV - 1 - d:GDN_CONV - d, :]
    y = y * _sigmoid(y)
    sq = y * y
    lo = lane < GDN_HEAD_DIM
    ss0 = jnp.sum(jnp.where(lo, sq, 0.0), axis=1, keepdims=True)
    ss1 = jnp.sum(jnp.where(lo, 0.0, sq), axis=1, keepdims=True)
    inv = lax.rsqrt(jnp.where(lo, ss0, ss1) + RMS_EPS)
    nqb = BRANCH_WIDTH // LANES
    scale = jnp.where(c < nqb, GDN_HEAD_DIM ** -0.5, 1.0)
    o_ref[...] = jnp.where(c < 2 * nqb, y * inv * scale, y).astype(BF16)


def _gdn_conv(proj, conv_w, B, S):
    T = B * S
    nb = 3 * BRANCH_WIDTH // LANES
    return pl.pallas_call(
        _gdn_conv_kernel, grid=(B, nb),
        in_specs=[pl.BlockSpec((S, LANES), lambda b, c: (b, C_GQKV // LANES + c)),
                  pl.BlockSpec((GDN_CONV, LANES), lambda b, c: (0, c))],
        out_specs=pl.BlockSpec((S, LANES), lambda b, c: (b, c)),
        out_shape=jax.ShapeDtypeStruct((T, 3 * BRANCH_WIDTH), BF16),
        compiler_params=_cp("parallel", "parallel"))(proj, conv_w)


def _gdn_chunk_kernel(q_ref, k_ref, v_ref, sm_ref, alog_ref, dtb_ref,
                      u_ref, w_ref, qg_ref, qk_ref, kdt_ref, dl_ref):
    C = GDN_KCHUNK
    nh = GDN_HEADS_PER_STEP
    ncs = GDN_CHUNKS_PER_STEP
    h0 = pl.program_id(1) * nh
    lane = lax.broadcasted_iota(jnp.int32, (C, LANES), 1)
    row = lax.broadcasted_iota(jnp.int32, (C, LANES), 0)
    lo = lane < GDN_HEAD_DIM
    tril = row >= lane
    strict = row > lane
    eye = jnp.where(row == lane, 1.0, 0.0)
    ltri = tril.astype(BF16)

    def pick(ref, k, cc):
        x = ref[cc * C:(cc + 1) * C, (k // 2) * LANES:(k // 2 + 1) * LANES].astype(F32)
        return jnp.where(lo, pltpu.roll(x, LANES // 2, 1) if k % 2 else x, 0.0)

    garr, sig = [], []
    for cc in range(ncs):
        sm = sm_ref[cc * C:(cc + 1) * C, :]
        a = sm + dtb_ref[...]
        softplus = jnp.maximum(a, 0.0) + jnp.log1p(jnp.exp(-jnp.abs(a)))
        garr.append(-jnp.exp(alog_ref[...]) * softplus)
        sig.append(_sigmoid(sm))
    nc = ncs * nh
    kh, kb, gc, decay, A = [], [], [], [], []
    for i in range(nc):
        cc, k = divmod(i, nh)
        g_col = jnp.sum(jnp.where(lane == h0 + k, garr[cc], 0.0), axis=1, keepdims=True)
        beta = jnp.sum(jnp.where(lane == HEADS + h0 + k, sig[cc], 0.0), axis=1, keepdims=True)
        g1 = jnp.broadcast_to(g_col, (C, LANES))
        gsum = None
        for _ in range(3):
            gb = g1.astype(BF16)
            part = jnp.dot(ltri, gb, preferred_element_type=F32)
            gsum = part if gsum is None else gsum + part
            g1 = g1 - gb.astype(F32)
        gc.append(gsum)
        decay.append(jnp.where(tril, jnp.exp(jnp.where(tril, gsum - gsum.T, 0.0)), 0.0))
        kh.append(pick(k_ref, k, cc))
        kb.append(kh[i] * beta)
        vb = pick(v_ref, k, cc) * beta
        A.append(jnp.where(strict, _dot_nt(kb[i], kh[i]) * decay[i], 0.0))
        eg = jnp.exp(gsum)
        qh = pick(q_ref, k, cc)
        qk_ref[k, cc] = jnp.where(tril, _dot_nt(qh, kh[i]) * decay[i], 0.0).astype(BF16)
        qg_ref[k, cc] = (qh * eg).astype(BF16)
        glast = gsum[C - 1:C, :]
        kdt_ref[k, cc] = (kh[i] * jnp.exp(glast - gsum)).T.astype(BF16)
        dl_ref[k, cc] = jnp.broadcast_to(jnp.exp(glast), (SUBLANES, LANES))
        u_ref[k, cc] = vb
        w_ref[k, cc] = (kb[i] * eg).astype(BF16)
    def joiner(level):
        same = (row >> (level + 1)) == (lane >> (level + 1))
        return same & (((row >> level) & 1) == 1) & (((lane >> level) & 1) == 0)

    P = [eye - jnp.where(joiner(0), A[i], 0.0) for i in range(nc)]
    for level in range(1, int(math.log2(C))):
        msk = joiner(level)
        T1 = [_dot(P[i], jnp.where(msk, A[i], 0.0)) for i in range(nc)]
        P = [P[i] - _dot(T1[i], P[i]) for i in range(nc)]
    for i in range(nc):
        cc, k = divmod(i, nh)
        u_ref[k, cc] = _dot(P[i], u_ref[k, cc])
        w_ref[k, cc] = _dot(P[i], w_ref[k, cc]).astype(BF16)


def _gdn_chunks(qkv, small, alog_row, dtb_row, B, S):
    C = GDN_KCHUNK
    n = S // C
    nh = GDN_HEADS_PER_STEP
    ng = HEADS // nh
    wq = nh // 2 * LANES
    nqb = BRANCH_WIDTH // wq
    ncs = GDN_CHUNKS_PER_STEP
    ns = n // ncs
    big = lambda dt: jax.ShapeDtypeStruct((B * HEADS, n, C, LANES), dt)
    ospec = pl.BlockSpec((nh, ncs, C, LANES), lambda b, g, c: (b * ng + g, c, 0, 0))
    par = pl.BlockSpec((1, LANES), lambda b, g, c: (0, 0))
    outs = pl.pallas_call(
        _gdn_chunk_kernel, grid=(B, ng, ns),
        in_specs=[pl.BlockSpec((ncs * C, wq), lambda b, g, c: (b * ns + c, g)),
                  pl.BlockSpec((ncs * C, wq), lambda b, g, c: (b * ns + c, nqb + g)),
                  pl.BlockSpec((ncs * C, wq), lambda b, g, c: (b * ns + c, 2 * nqb + g)),
                  pl.BlockSpec((ncs * C, LANES), lambda b, g, c: (b * ns + c, 0)), par, par],
        out_specs=[ospec] * 5 + [pl.BlockSpec((nh, ncs, SUBLANES, LANES), lambda b, g, c: (b * ng + g, c, 0, 0))],
        out_shape=[big(F32)] + [big(BF16)] * 4 + [jax.ShapeDtypeStruct((B * HEADS, n, SUBLANES, LANES), F32)],
        compiler_params=_cp("parallel", "parallel", "parallel"))(qkv, qkv, qkv, small, alog_row, dtb_row)
    return [o.reshape(-1, LANES) for o in outs]


def _gdn_scan_kernel(u_ref, w_ref, qg_ref, qk_ref, kdt_ref, dl_ref, z_ref, g_ref, o_ref, o_sc, *, n):
    C = GDN_KCHUNK
    S = n * C
    npair = GDN_SCAN_PAIRS

    def step(c, states):
        out = []
        for k, state in enumerate(states):
            r = pl.ds(pl.multiple_of(k * S + c * C, C), C)
            sb = state.astype(BF16)
            v_new = u_ref[r, :] - _dot(w_ref[r, :], sb)
            vb = v_new.astype(BF16)
            o_sc[r, :] = _dot(qg_ref[r, :], sb) + _dot(qk_ref[r, :], vb)
            dl = dl_ref[pl.ds(pl.multiple_of((k * n + c) * SUBLANES, SUBLANES), 1), :]
            out.append(state * dl + _dot(kdt_ref[r, :], vb))
        return tuple(out)

    lax.fori_loop(0, n, step, tuple(jnp.zeros((LANES, LANES), F32) for _ in range(2 * npair)))
    lane = lax.broadcasted_iota(jnp.int32, (S, LANES), 1)

    def nrm(o):
        ms = jnp.sum(o * o, axis=1, keepdims=True) * (1.0 / GDN_HEAD_DIM)
        return o * lax.rsqrt(ms + RMS_EPS)

    for p in range(npair):
        ps = slice(p * LANES, (p + 1) * LANES)
        nn = jnp.where(lane < GDN_HEAD_DIM, nrm(o_sc[2 * p * S:(2 * p + 1) * S, :]),
                       pltpu.roll(nrm(o_sc[(2 * p + 1) * S:(2 * p + 2) * S, :]), LANES // 2, 1))
        z = z_ref[:, ps].astype(F32)
        o_ref[:, ps] = nn * g_ref[...] * (z * _sigmoid(z))


def _gdn_scan(parts, proj, g_row, B, S):
    n = S // GDN_KCHUNK
    nps = GDN_SCAN_PAIRS
    steps = HEADS // 2 // nps
    wz = nps * LANES
    seq = pl.BlockSpec((2 * nps * S, LANES), lambda i: (i, 0))
    return pl.pallas_call(
        functools.partial(_gdn_scan_kernel, n=n), grid=(B * steps,),
        in_specs=[seq] * 5 + [pl.BlockSpec((2 * nps * n * SUBLANES, LANES), lambda i: (i, 0)),
                              pl.BlockSpec((S, wz), lambda i: (i // steps, C_GZ // wz + i % steps)),
                              pl.BlockSpec((1, LANES), lambda i: (0, 0))],
        out_specs=pl.BlockSpec((S, wz), lambda i: (i // steps, i % steps)),
        out_shape=jax.ShapeDtypeStruct((B * S, BRANCH_WIDTH), F32),
        scratch_shapes=[pltpu.VMEM((2 * nps * S, LANES), F32)],
        compiler_params=_cp("parallel"))(*parts, proj, g_row)


def _merge_out_kernel(g0_ref, g1_ref, g2_ref, gb_ref, a_ref, b_ref, c_ref, wb_ref, wo_ref, h_ref, lg_ref, lb_ref,
                      o_ref, ot_ref):
    acc = None
    for n, (gl, br) in enumerate(((g0_ref, a_ref), (g1_ref, b_ref), (g2_ref, c_ref))):
        y = _sigmoid(gl[...].astype(F32) + gb_ref[n:n + 1, :]) * _dot(br[...], wb_ref[n])
        acc = y if acc is None else acc + y
    y = DEEPNORM_ALPHA * h_ref[...] + _dot(acc, wo_ref[...])
    y = _ln_rows(y, lg_ref[...], lb_ref[...])
    o_ref[...] = y
    ot_ref[...] = y.T.astype(BF16)


def _merge_out_ln(proj, gate_bias, o_a, o_b, o_c, w_branch, w_out, h, g, b, tm=512):
    T, D = h.shape
    row = lambda i: (i, 0)
    gspec = lambda n: pl.BlockSpec((tm, D), lambda i: (i, n))
    bspec = pl.BlockSpec((tm, BRANCH_WIDTH), row)
    vec = pl.BlockSpec((1, D), lambda i: (0, 0))
    return pl.pallas_call(
        _merge_out_kernel, grid=(T // tm,),
        in_specs=[gspec(0), gspec(1), gspec(2), pl.BlockSpec((3, D), lambda i: (0, 0)), bspec, bspec, bspec,
                  pl.BlockSpec((3, BRANCH_WIDTH, D), lambda i: (0, 0, 0)), pl.BlockSpec((D, D), lambda i: (0, 0)),
                  pl.BlockSpec((tm, D), row), vec, vec],
        out_specs=[pl.BlockSpec((tm, D), row), pl.BlockSpec((D, tm), lambda i: (0, i))],
        out_shape=[jax.ShapeDtypeStruct((T, D), F32), jax.ShapeDtypeStruct((D, T), BF16)],
        compiler_params=_cp("parallel"))(proj, proj, proj, gate_bias, o_a, o_b, o_c, w_branch, w_out, h,
                                         g.reshape(1, D), b.reshape(1, D))


def _sort_network(n):
    pairs, p = [], 1
    while p < n:
        k = p
        while k >= 1:
            for j in range(k % p, n - k, 2 * k):
                for i in range(min(k, n - j - k)):
                    if (i + j) // (2 * p) == (i + j + k) // (2 * p):
                        pairs.append((i + j, i + j + k))
            k //= 2
        p *= 2
    return pairs


def _top_of_sorted_lists(scores, n, on_max):
    lists = []
    for s in scores:
        nt = s.shape[0] // SUBLANES
        rows = [s[t * SUBLANES:(t + 1) * SUBLANES, :] for t in range(nt)]
        rows += [None] * ((1 << (nt - 1).bit_length()) - nt)
        for a, b in _sort_network(len(rows)):
            if rows[b] is None:
                continue
            if rows[a] is None:
                rows[a], rows[b] = rows[b], None
            else:
                rows[a], rows[b] = jnp.maximum(rows[a], rows[b]), jnp.minimum(rows[a], rows[b])
        lists.append(rows[:nt] + [None])
    for k in range(n):
        for i, rows in enumerate(lists):
            m = jnp.max(rows[0], axis=0, keepdims=True)
            on_max(i, k, m)
            hit = rows[0] >= m
            for t in range(min(n - 1 - k, len(rows) - 1)):
                nxt = -jnp.inf if rows[t + 1] is None else rows[t + 1]
                rows[t] = jnp.where(hit, nxt, rows[t])


def _peer_score_kernel(qt_ref, keys_ref, s2_ref, e2_ref, c_ref, e1_ref, top_sc, cand_sc):
    K = PEER_TOPK
    tt = qt_ref.shape[1]
    nh = PEER_SCORE_HEADS_PER_TRIP

    def heads(hp, carry):
        s = []
        for i in range(2 * nh):
            r = pl.multiple_of((hp * nh * 2 + i) * PEER_HALF, PEER_HALF)
            s.append(_dot(keys_ref[hp * nh * 2 + i], qt_ref[pl.ds(r, PEER_HALF), :]))

        def put(i, k, m):
            top_sc[i, k:k + 1, :] = m
        _top_of_sorted_lists(s, K + 1, put)
        r8 = lax.broadcasted_iota(jnp.int32, (SUBLANES, tt), 0)
        v1max, v2max = [], []
        for d in range(nh):
            v1a, v1x = top_sc[2 * d, 0:K, :], top_sc[2 * d, K:K + 1, :]
            v2a, v2x = top_sc[2 * d + 1, 0:K, :], top_sc[2 * d + 1, K:K + 1, :]
            v1max.append(v1a[0:1])
            v2max.append(v2a[0:1])
            cand_sc[d, 0:K, :] = v1a + v2a[0:1]
            for b in range(1, SUBLANES):
                cand_sc[d, K + SUBLANES * (b - 1):K + SUBLANES * b, :] = v1a[0:SUBLANES] + v2a[b:b + 1]
            base = K + SUBLANES * (SUBLANES - 1)
            cand_sc[d, base:base + SUBLANES, :] = v1a[0:1] + v2a[SUBLANES:K]
            cand_sc[d, base + SUBLANES:base + 2 * SUBLANES, :] = jnp.where(
                r8 == 0, v1x + v2a[0:1], jnp.where(r8 == 1, v1a[0:1] + v2x, -jnp.inf))
        st = [{"z": jnp.zeros((1, tt), F32)} for _ in range(nh)]

        def acc(d, k, m):
            if k == 0:
                st[d]["top"] = m
            if k < K:
                st[d]["z"] = st[d]["z"] + jnp.exp(m - st[d]["top"])
            if k == K - 1:
                st[d]["t16"] = m
            if k == K:
                st[d]["t17"] = m
        _top_of_sorted_lists([cand_sc[d] for d in range(nh)], K + 1, acc)
        for d in range(nh):
            tau = 0.5 * (st[d]["t16"] + st[d]["t17"])
            ro = pl.ds(pl.multiple_of((hp * nh + d) * N_KEYS, N_KEYS), N_KEYS)
            s2_ref[ro, :] = s[2 * d + 1]
            e2_ref[ro, :] = jnp.exp(s[2 * d + 1] - v2max[d]) / st[d]["z"]
            c_ref[ro, :] = tau - s[2 * d]
            e1_ref[ro, :] = jnp.exp(s[2 * d] - v1max[d])
        return carry

    lax.fori_loop(0, HEADS // nh, heads, 0)


def _peer_scores(qt, keys):
    T = qt.shape[1]
    tt = PEER_SCORE_TT
    R = HEADS * N_KEYS
    ncand = PEER_TOPK + SUBLANES * (SUBLANES + 1)
    ospec = pl.BlockSpec((R, tt), lambda i: (0, i))
    return pl.pallas_call(
        _peer_score_kernel, grid=(T // tt,),
        in_specs=[pl.BlockSpec((2 * R, tt), lambda i: (0, i)),
                  pl.BlockSpec((2 * HEADS, N_KEYS, PEER_HALF), lambda i: (0, 0, 0))],
        out_specs=[ospec] * 4,
        out_shape=[jax.ShapeDtypeStruct((R, T), F32)] * 4,
        scratch_shapes=[pltpu.VMEM((2 * PEER_SCORE_HEADS_PER_TRIP, 3 * SUBLANES, tt), F32),
                        pltpu.VMEM((PEER_SCORE_HEADS_PER_TRIP, ncand, tt), F32)],
        compiler_params=_cp("parallel"))(qt, keys)


def _peer_expert_kernel(ht_ref, u_ref, vtp_ref, vtl_ref, s2_ref, e2_ref, c_ref, e1_ref, h_ref, g_ref, b_ref,
                        o_ref, acc_sc, act_sc, hw_sc, bc_sc):
    j = pl.program_id(1)
    _, eb, tt = hw_sc.shape
    ng = eb // N_KEYS
    assert c_ref.shape == (HEADS, ng, tt)
    slot = j % 2

    @pl.when(j == 0)
    def _():
        acc_sc[...] = jnp.zeros(acc_sc.shape, F32)
        hw_sc[1] = jnp.zeros(hw_sc.shape[1:], BF16)

    mxu_w = 2 * LANES
    for half in range(tt // mxu_w):
        hl = slice(half * mxu_w, (half + 1) * mxu_w)
        a = jnp.dot(u_ref[...], ht_ref[:, hl], preferred_element_type=F32)
        act_sc[:, hl] = 0.5 * a * (1.0 + lax.erf(a * (2.0 ** -0.5)))
        acc_sc[:, hl] += jnp.dot(vtp_ref[...], hw_sc[1 - slot, :, hl], preferred_element_type=F32)
        for lc in range(half * (mxu_w // LANES), (half + 1) * (mxu_w // LANES)):
            ls = slice(lc * LANES, (lc + 1) * LANES)
            for h in range(HEADS):
                thr, e1 = c_ref[h, :, ls], e1_ref[h, :, ls]
                for g in range(ng):
                    bc_sc[0, h * ng + g] = jnp.broadcast_to(thr[g:g + 1], (SUBLANES, LANES))
                    bc_sc[1, h * ng + g] = jnp.broadcast_to(e1[g:g + 1], (SUBLANES, LANES))
            for g in range(ng):
                wsum = jnp.zeros((N_KEYS // SUBLANES, SUBLANES, LANES), F32)
                for h in range(HEADS):
                    hr = slice(h * N_KEYS, (h + 1) * N_KEYS)
                    s2 = s2_ref[hr, ls].reshape(N_KEYS // SUBLANES, SUBLANES, LANES)
                    e2 = e2_ref[hr, ls].reshape(N_KEYS // SUBLANES, SUBLANES, LANES)
                    wsum = wsum + jnp.where(s2 >= bc_sc[0, h * ng + g][None], e2 * bc_sc[1, h * ng + g][None], 0.0)
                gs = slice(g * N_KEYS, (g + 1) * N_KEYS)
                hw_sc[slot, gs, ls] = (wsum.reshape(N_KEYS, LANES) * act_sc[gs, ls]).astype(BF16)

    @pl.when(j == pl.num_programs(1) - 1)
    def _():
        acc = acc_sc[...] + jnp.dot(vtl_ref[...], hw_sc[slot], preferred_element_type=F32)
        y = DEEPNORM_ALPHA * h_ref[...] + acc.T
        o_ref[...] = _ln_rows(y, g_ref[...], b_ref[...])


def _peer_experts(ht, u, vt, s2, e2, c, e1, h, g, b):
    T, D = h.shape
    tt, eb = PEER_TT, PEER_EB
    R = HEADS * N_KEYS
    nb = N_EXPERTS // eb
    assert nb % 2 == 0
    ng = eb // N_KEYS
    sspec = pl.BlockSpec((R, tt), lambda i, j: (0, i))
    gspec = pl.BlockSpec((HEADS, ng, tt), lambda i, j: (0, j, i))
    vec = pl.BlockSpec((1, D), lambda i, j: (0, 0))
    return pl.pallas_call(
        _peer_expert_kernel, grid=(T // tt, nb),
        in_specs=[pl.BlockSpec((D, tt), lambda i, j: (0, i)),
                  pl.BlockSpec((eb, D), lambda i, j: (j, 0)),
                  pl.BlockSpec((D, eb), lambda i, j: (0, jnp.maximum(j - 1, 0))),
                  pl.BlockSpec((D, eb), lambda i, j: (0, nb - 1)),
                  sspec, sspec, gspec, gspec,
                  pl.BlockSpec((tt, D), lambda i, j: (i, 0)), vec, vec],
        out_specs=pl.BlockSpec((tt, D), lambda i, j: (i, 0)),
        out_shape=jax.ShapeDtypeStruct((T, D), F32),
        scratch_shapes=[pltpu.VMEM((D, tt), F32), pltpu.VMEM((eb, tt), F32), pltpu.VMEM((2, eb, tt), BF16),
                        pltpu.VMEM((2, HEADS * ng, SUBLANES, LANES), F32)],
        compiler_params=_cp("parallel", "arbitrary"))(ht, u, vt, vt, s2, e2, c.reshape(HEADS, N_KEYS, T),
                                                      e1.reshape(HEADS, N_KEYS, T), h, g.reshape(1, D),
                                                      b.reshape(1, D))


def _rot_half(x, r):
    return jnp.concatenate([-x[..., r:2 * r], x[..., :r]], axis=-1)


def _prep_w_in(w):
    D = w.shape[0]
    splits = (MLA_Q_RANK, MLA_KV_RANK, MLA_ROPE, 3 * BRANCH_WIDTH, 3 * BRANCH_WIDTH, BRANCH_WIDTH, HEADS, HEADS,
              3 * D)
    o = np.cumsum((0,) + splits)
    cq, ckv, kr, mqkv, gqkv, gz, ga, gb, gl = [w[:, o[i]:o[i + 1]] for i in range(9)]
    mq, mk, mv = jnp.split(mqkv, 3, axis=1)

    def slab(m):
        m = m.reshape(D, HEADS, MOBA_HEAD_DIM)
        pad = jnp.zeros((D, HEADS, LANES - MOBA_HEAD_DIM - MOBA_ROT_DIM), w.dtype)
        return jnp.concatenate([m, _rot_half(m, MOBA_ROT_DIM // 2), pad], axis=-1).reshape(D, HEADS * LANES)

    small = jnp.concatenate([ga, gb, jnp.zeros((D, LANES // 2 - 2 * HEADS), w.dtype), kr,
                             _rot_half(kr, MLA_ROPE // 2)], axis=1)
    out = jnp.concatenate([gl, cq, ckv, slab(mq), slab(mk), mv, gqkv, gz], axis=1)
    assert out.shape[1] == C_TOTAL and C_TOTAL % PROJ_TN == 0
    return out.astype(BF16), small.astype(BF16)


def _prep_w_uq(w):
    R = w.shape[0]
    w = w.reshape(R, HEADS, MLA_NOPE + MLA_ROPE)
    rope = w[..., MLA_NOPE:]
    return jnp.concatenate([w, _rot_half(rope, MLA_ROPE // 2)], axis=-1).reshape(R, HEADS * LANES).astype(BF16)


def _prep_w_ukv(w):
    R = w.shape[0]
    w = w.reshape(R, HEADS, MLA_NOPE + MLA_V)
    k = jnp.concatenate([w[..., :MLA_NOPE], jnp.zeros((R, HEADS, LANES - MLA_NOPE), w.dtype)], axis=-1)
    return jnp.concatenate([k.reshape(R, HEADS * LANES), w[..., MLA_NOPE:].reshape(R, HEADS * MLA_V)],
                           axis=1).astype(BF16)


def _rope_tables(positions):
    pos = positions.reshape(-1).astype(F32)[:, None]
    T = pos.shape[0]

    def cs(rot):
        inv = ROPE_THETA ** (-jnp.arange(0, rot, 2, dtype=F32) / rot)
        ang = pos * inv
        return jnp.cos(ang), jnp.sin(ang)

    ca, sa = cs(MLA_ROPE)
    cb, sb = cs(MOBA_ROT_DIM)
    one = lambda n: jnp.ones((T, n), F32)
    zero = lambda n: jnp.zeros((T, n), F32)
    cat = lambda *xs: jnp.concatenate(xs, axis=1)
    pad_a = LANES - MLA_NOPE - MLA_ROPE
    mla = (cat(zero(MLA_NOPE), ca, ca, zero(pad_a)), cat(zero(MLA_NOPE), sa, sa, zero(pad_a)))
    moba = (cat(cb, cb, one(MOBA_HEAD_DIM - MOBA_ROT_DIM), zero(LANES - MOBA_HEAD_DIM)),
            cat(sb, sb, zero(LANES - MOBA_ROT_DIM)))
    return mla, moba


def _lane_row(v):
    return jnp.concatenate([v.astype(F32), jnp.zeros((LANES - v.shape[0],), F32)]).reshape(1, LANES)


def kernel(x, positions, ln_in_g, ln_in_b, w_in, mla_q_norm, mla_kv_norm, mla_w_uq, mla_w_ukv, gdn_conv_w, gdn_A_log, gdn_dt_bias, gdn_o_norm, gate_bias, w_branch, w_out, ln1_g, ln1_b, peer_w_q, peer_sub_keys, peer_u, peer_v, ln2_g, ln2_b):
    B, S, D = x.shape
    T = B * S
    assert S % ATT_TQ == 0 and S % GDN_KCHUNK == 0 and T % PEER_TT == 0 and ATT_TQ == MOBA_BLOCK
    mla_tabs, moba_tabs = _rope_tables(positions)
    h = _layer_norm(x.reshape(T, D), ln_in_g, ln_in_b)
    for l in range(DEPTH):
        w_main, w_small = _prep_w_in(w_in[l])
        proj = _mm(h, w_main, tm=512, tn=PROJ_TN, out_dtype=BF16)
        small = _mm(h, w_small, tm=512, tn=LANES)
        qa, ka, va = _mla_qkv(proj, small, mla_q_norm[l], mla_kv_norm[l], _prep_w_uq(mla_w_uq[l]),
                              _prep_w_ukv(mla_w_ukv[l]), mla_tabs)
        o_a = _attention(qa, ka, va, 0, B, S)
        qm, km, kmean = _moba_prep(proj, moba_tabs)
        o_b = _attention(qm, km, proj, C_MV, B, S, kmean=kmean.reshape(B, S // MOBA_BLOCK, HEADS * LANES))
        qkv = _gdn_conv(proj, gdn_conv_w[l], B, S)
        parts = _gdn_chunks(qkv, small, _lane_row(gdn_A_log[l]), _lane_row(gdn_dt_bias[l]), B, S)
        o_c = _gdn_scan(parts, proj, jnp.tile(gdn_o_norm[l], 2).reshape(1, LANES), B, S)
        h, ht = _merge_out_ln(proj, gate_bias[l], o_a, o_b, o_c, w_branch[l].astype(BF16), w_out[l].astype(BF16),
                              h, ln1_g[l], ln1_b[l])
        qt = _mm(peer_w_q[l].T.astype(BF16), ht, tm=1024, tn=min(1024, T), out_dtype=BF16)
        keys = peer_sub_keys[l].reshape(2 * HEADS, N_KEYS, PEER_HALF).astype(BF16)
        s2, e2, c, e1 = _peer_scores(qt, keys)
        h = _peer_experts(ht, peer_u[l].astype(BF16), peer_v[l].astype(BF16).T, s2, e2, c, e1, h, ln2_g[l], ln2_b[l])
    return h.reshape(B, S, D)
```
